```python
import math
import jax, jax.numpy as jnp
from jax import lax
import numpy as np

D_MODEL = 1024
BATCH = 8
SEQ = 2048
DEPTH = 2

BLOCK = 128
RET_HEADS = 4
RET_DK = 64
RET_DV = 128
RET_QK = RET_HEADS * RET_DK
RET_WIDTH = RET_HEADS * RET_DV
FOX_HEADS = 8
FOX_DH = 64
FOX_WIDTH = FOX_HEADS * FOX_DH
GM_GROUPS = 4
GM_DG = 128
GM_WIDTH = GM_GROUPS * GM_DG
BRANCH_WIDTH = 512
N_BRANCH = 3
COL_SIZES = (RET_QK, RET_QK, RET_WIDTH, RET_WIDTH,
             FOX_WIDTH, FOX_WIDTH, FOX_WIDTH, FOX_HEADS,
             GM_WIDTH, GM_WIDTH, N_BRANCH * D_MODEL)
N_IN = sum(COL_SIZES)
D_FF_DENSE = 2816
N_EXPERTS = 8
TOP_K = 2
D_FF_EXPERT = 3584
N_DENSE = (DEPTH + 1) // 2
N_MOE = DEPTH // 2
ROPE_BASE = 10000.0
LN_EPS = 1e-5
GN_EPS = 1e-6
ALPHA = (2 * DEPTH) ** 0.25
BETA = (8 * DEPTH) ** -0.25

kernel_name = "hybrid_retention_fox_gmlp_moe_deepnorm"


def layer_norm(x, g, b, eps=LN_EPS):
    xf = x.astype(jnp.float32)
    mu = xf.mean(-1, keepdims=True)
    var = jnp.square(xf - mu).mean(-1, keepdims=True)
    return ((xf - mu) * lax.rsqrt(var + eps)).astype(x.dtype) * g + b


def head_group_norm(y):
    yf = y.astype(jnp.float32)
    mu = yf.mean(-1, keepdims=True)
    var = jnp.square(yf - mu).mean(-1, keepdims=True)
    return ((yf - mu) * lax.rsqrt(var + GN_EPS)).astype(y.dtype)


def rotary(x, pos):
    half = x.shape[-1] // 2
    inv_freq = ROPE_BASE ** (-jnp.arange(half, dtype=jnp.float32) / half)
    ang = pos.astype(jnp.float32)[:, None] * inv_freq[None, :]
    cos = jnp.cos(ang)[None, :, None, :].astype(x.dtype)
    sin = jnp.sin(ang)[None, :, None, :].astype(x.dtype)
    x1, x2 = x[..., :half], x[..., half:]
    return jnp.concatenate([x1 * cos - x2 * sin, x1 * sin + x2 * cos], axis=-1)


def retention_chunkwise(q, k, v):
    bsz, s, h, dk = q.shape
    dv = v.shape[-1]
    n = s // BLOCK
    dt = q.dtype
    log_gamma = jnp.log1p(-jnp.exp2(-5.0 - jnp.arange(h, dtype=jnp.float32)))
    idx = jnp.arange(BLOCK, dtype=jnp.float32)
    rel = idx[:, None] - idx[None, :]
    causal = rel >= 0
    decay_in = jnp.where(causal[None], jnp.exp(log_gamma[:, None, None] * jnp.where(causal, rel, 0.0)[None]), 0.0).astype(dt)
    q_dec = jnp.exp(log_gamma[:, None] * (idx + 1.0)).astype(dt)
    k_dec = jnp.exp(log_gamma[:, None] * (BLOCK - 1.0 - idx)).astype(dt)
    chunk_dec = jnp.exp(log_gamma * BLOCK).astype(dt)

    def to_chunks(t):
        return t.reshape(bsz, n, BLOCK, h, t.shape[-1]).transpose(1, 0, 3, 2, 4)

    qc, kc, vc = to_chunks(q), to_chunks(k), to_chunks(v)

    def step(state, inp):
        qi, ki, vi = inp
        scores = jnp.einsum('bhid,bhjd->bhij', qi, ki) * decay_in
        inner = jnp.einsum('bhij,bhjv->bhiv', scores, vi)
        cross = jnp.einsum('bhid,bhdv->bhiv', qi * q_dec[:, :, None], state)
        new_state = state * chunk_dec[:, None, None] + jnp.einsum('bhjd,bhjv->bhdv', ki * k_dec[:, :, None], vi)
        return new_state, inner + cross

    state0 = jnp.zeros((bsz, h, dk, dv), dt)
    _, out = lax.scan(step, state0, (qc, kc, vc))
    return out.transpose(1, 0, 3, 2, 4).reshape(bsz, s, h, dv)


def forgetting_attention(q, k, v, log_f):
    s = q.shape[1]
    scale = q.shape[-1] ** -0.5
    c = jnp.cumsum(log_f, axis=1).transpose(0, 2, 1)
    outs = []
    for i in range(s // BLOCK):
        lo, hi = i * BLOCK, (i + 1) * BLOCK
        qb = q[:, lo:hi]
        kb, vb = k[:, :hi], v[:, :hi]
        logits = jnp.einsum('bqhd,bkhd->bhqk', qb, kb).astype(jnp.float32) * scale
        bias = c[:, :, lo:hi, None] - c[:, :, None, :hi]
        mask = jnp.arange(lo, hi)[:, None] >= jnp.arange(hi)[None, :]
        logits = jnp.where(mask, logits + bias, -jnp.inf)
        p = jax.nn.softmax(logits, axis=-1).astype(v.dtype)
        outs.append(jnp.einsum('bhqk,bkhd->bqhd', p, vb))
    return jnp.concatenate(outs, axis=1)


def chunked_spatial_gating(u, v, w_s, b_s):
    bsz, s, _ = u.shape
    n = s // BLOCK
    causal = jnp.tril(jnp.ones((BLOCK, BLOCK), dtype=bool))
    w = jnp.where(causal[None], w_s, 0.0)
    vc = v.reshape(bsz, n, BLOCK, GM_GROUPS, GM_DG)
    sp = jnp.einsum('gij,bnjgc->bnigc', w, vc) + b_s.T[None, None, :, :, None]
    return u * sp.reshape(bsz, s, GM_WIDTH)


def split_columns(proj):
    outs, start = [], 0
    for size in COL_SIZES:
        outs.append(proj[..., start:start + size])
        start += size
    return outs


def hybrid_mixer(x, pos, w_in, fox_b_f, gate_b, gm_w_s, gm_b_s, gm_ln_g, gm_ln_b, w_branch, w_out):
    bsz, s, _ = x.shape
    proj = x @ w_in
    (rq, rk, rv, rg, fq, fk, fv, fl, gu, gv, gl) = split_columns(proj)
    q = rotary(rq.reshape(bsz, s, RET_HEADS, RET_DK), pos)
    k = rotary(rk.reshape(bsz, s, RET_HEADS, RET_DK), pos) * (RET_DK ** -0.5)
    ret = retention_chunkwise(q, k, rv.reshape(bsz, s, RET_HEADS, RET_DV))
    y_a = jax.nn.silu(rg) * head_group_norm(ret).reshape(bsz, s, RET_WIDTH)
    log_f = jax.nn.log_sigmoid(fl.astype(jnp.float32) + fox_b_f.astype(jnp.float32))
    y_b = forgetting_attention(fq.reshape(bsz, s, FOX_HEADS, FOX_DH),
                               fk.reshape(bsz, s, FOX_HEADS, FOX_DH),
                               fv.reshape(bsz, s, FOX_HEADS, FOX_DH), log_f).reshape(bsz, s, FOX_WIDTH)
    u = jax.nn.gelu(gu)
    v = layer_norm(jax.nn.gelu(gv), gm_ln_g, gm_ln_b)
    y_c = chunked_spatial_gating(u, v, gm_w_s, gm_b_s)
    gates = jax.nn.sigmoid(gl + gate_b).reshape(bsz, s, N_BRANCH, D_MODEL)
    yb = jnp.stack([y_a, y_b, y_c], axis=2)
    branch = jnp.einsum('bsnw,nwd->bsnd', yb, w_branch)
    merged = jnp.einsum('bsnd,bsnd->bsd', gates, branch)
    return merged @ w_out


def swiglu(x, w_up, w_down):
    a, b = jnp.split(x @ w_up, 2, axis=-1)
    return (jax.nn.silu(a) * b) @ w_down


def moe_swiglu(x, w_router, w_up, w_down):
    bsz, s, d = x.shape
    t = x.reshape(-1, d)
    logits = (t @ w_router).astype(jnp.float32)
    top_val, top_idx = lax.top_k(logits, TOP_K)
    top_w = jax.nn.softmax(top_val, axis=-1)
    gate = jnp.sum(jax.nn.one_hot(top_idx, N_EXPERTS, dtype=jnp.float32) * top_w[..., None], axis=1).astype(x.dtype)
    out = jnp.zeros_like(t)
    for e in range(N_EXPERTS):
        out = out + gate[:, e:e + 1] * swiglu(t, w_up[e], w_down[e])
    return out.reshape(bsz, s, d)


def setup_inputs(seed: int = 0) -> dict:
    key = jax.random.key(seed)
    ks = jax.random.split(key, 20)
    f32 = jnp.float32
    nrm = lambda k, shape, sc: jax.random.normal(k, shape, f32) * sc
    return {
        "x": nrm(ks[0], (BATCH, SEQ, D_MODEL), 1.0),
        "w_in": nrm(ks[1], (DEPTH, D_MODEL, N_IN), D_MODEL ** -0.5),
        "fox_b_f": jax.random.uniform(ks[2], (DEPTH, FOX_HEADS), f32, 1.0, 4.0),
        "gate_b": nrm(ks[3], (DEPTH, N_BRANCH * D_MODEL), 0.02),
        "gm_w_s": nrm(ks[4], (DEPTH, GM_GROUPS, BLOCK, BLOCK), BLOCK ** -0.5),
        "gm_b_s": 1.0 + nrm(ks[5], (DEPTH, GM_GROUPS, BLOCK), 0.02),
        "gm_ln_g": 1.0 + nrm(ks[6], (DEPTH, GM_WIDTH), 0.02),
        "gm_ln_b": nrm(ks[7], (DEPTH, GM_WIDTH), 0.02),
        "w_branch": nrm(ks[8], (DEPTH, N_BRANCH, BRANCH_WIDTH, D_MODEL), BRANCH_WIDTH ** -0.5),
        "w_out": nrm(ks[9], (DEPTH, D_MODEL, D_MODEL), BETA * D_MODEL ** -0.5),
        "ln_g": 1.0 + nrm(ks[10], (DEPTH, 2, D_MODEL), 0.02),
        "ln_b": nrm(ks[11], (DEPTH, 2, D_MODEL), 0.02),
        "dense_w_up": nrm(ks[12], (N_DENSE, D_MODEL, 2 * D_FF_DENSE), D_MODEL ** -0.5),
        "dense_w_down": nrm(ks[13], (N_DENSE, D_FF_DENSE, D_MODEL), BETA * D_FF_DENSE ** -0.5),
        "moe_router": nrm(ks[14], (N_MOE, D_MODEL, N_EXPERTS), D_MODEL ** -0.5),
        "moe_w_up": nrm(ks[15], (N_MOE, N_EXPERTS, D_MODEL, 2 * D_FF_EXPERT), D_MODEL ** -0.5),
        "moe_w_down": nrm(ks[16], (N_MOE, N_EXPERTS, D_FF_EXPERT, D_MODEL), BETA * D_FF_EXPERT ** -0.5),
    }


def reference(x, w_in, fox_b_f, gate_b, gm_w_s, gm_b_s, gm_ln_g, gm_ln_b, w_branch, w_out,
              ln_g, ln_b, dense_w_up, dense_w_down, moe_router, moe_w_up, moe_w_down):
    pos = jnp.arange(x.shape[1], dtype=jnp.int32)
    for l in range(DEPTH):
        h = hybrid_mixer(x, pos, w_in[l], fox_b_f[l], gate_b[l], gm_w_s[l], gm_b_s[l],
                         gm_ln_g[l], gm_ln_b[l], w_branch[l], w_out[l])
        x = layer_norm(ALPHA * x + h, ln_g[l, 0], ln_b[l, 0])
        if l % 2 == 0:
            f = swiglu(x, dense_w_up[l // 2], dense_w_down[l // 2])
        else:
            f = moe_swiglu(x, moe_router[l // 2], moe_w_up[l // 2], moe_w_down[l // 2])
        x = layer_norm(ALPHA * x + f, ln_g[l, 1], ln_b[l, 1])
    return x
```

```python
import functools
import math

import jax
import jax.numpy as jnp
from jax import lax
from jax.experimental import pallas as pl
from jax.experimental.pallas import tpu as pltpu

_BF16 = jnp.bfloat16
_F32 = jnp.float32

_BLOCK = 128
_RET_HEADS, _RET_DK, _RET_DV = 4, 64, 128
_FOX_HEADS, _FOX_DH = 8, 64
_GM_GROUPS, _GM_DG = 4, 128
_N_EXPERTS = 8
_ROPE_BASE = 10000.0
_LN_EPS = 1e-5
_GN_EPS = 1e-6

_LANES = 128
_VMEM_PHYSICAL_BYTES = 64 * 1024 * 1024
_VMEM_LIMIT_BYTES = (_VMEM_PHYSICAL_BYTES * 7) // 8

_TM = 512
_TQ = 256


def _params(n_axes):
    return pltpu.CompilerParams(
        dimension_semantics=("arbitrary",) * n_axes,
        vmem_limit_bytes=_VMEM_LIMIT_BYTES,
    )


def _layer_norm(x, g, b, eps):
    mu = jnp.mean(x, axis=-1, keepdims=True)
    xc = x - mu
    var = jnp.mean(xc * xc, axis=-1, keepdims=True)
    return xc * lax.rsqrt(var + eps) * g + b


def _silu(x):
    return x * (1.0 / (1.0 + jnp.exp(-x)))


def _gelu_tanh(x):
    c = math.sqrt(2.0 / math.pi)
    return x * (0.5 * (1.0 + jnp.tanh(c * (x + 0.044715 * (x * x * x)))))


def _dot(a, b):
    return jnp.dot(a, b, preferred_element_type=_F32)


def _dot_nt(a, b):
    return lax.dot_general(a, b, (((1,), (1,)), ((), ())), preferred_element_type=_F32)


def _dot_tn(a, b):
    return lax.dot_general(a, b, (((0,), (0,)), ((), ())), preferred_element_type=_F32)


def _matmul_kernel(x_ref, w_ref, o_ref):
    o_ref[...] = _dot(x_ref[...], w_ref[...]).astype(o_ref.dtype)


def _matmul(x, w, *, tm, tn, out_dtype, name):
    m, k = x.shape
    n = w.shape[1]
    return pl.pallas_call(
        _matmul_kernel,
        out_shape=jax.ShapeDtypeStruct((m, n), out_dtype),
        grid=(n // tn, m // tm),
        in_specs=[pl.BlockSpec((tm, k), lambda j, i: (i, 0)),
                  pl.BlockSpec((k, tn), lambda j, i: (0, j))],
        out_specs=pl.BlockSpec((tm, tn), lambda j, i: (i, j)),
        compiler_params=_params(2),
        name=name,
    )(x, w)


def _retention_kernel(q_ref, k_ref, v_ref, g_ref, cos_ref, sin_ref, qdec_ref, kdec_ref, din_ref,
                      y_ref, state_sc, *, chunk_dec):
    c = pl.program_id(1)

    @pl.when(c == 0)
    def _():
        state_sc[...] = jnp.zeros_like(state_sc)

    width = _RET_HEADS * _RET_DK
    lane = lax.broadcasted_iota(jnp.int32, (_BLOCK, width), 1)
    first_half = (lane % _RET_DK) < (_RET_DK // 2)

    def rotary(x):
        partner = jnp.where(first_half,
                            pltpu.roll(x, width - _RET_DK // 2, 1),
                            pltpu.roll(x, _RET_DK // 2, 1))
        return x * cos_ref[...] + partner * sin_ref[...]

    q = rotary(q_ref[...])
    k = rotary(k_ref[...]) * (_RET_DK ** -0.5)
    k_bf = k.astype(_BF16)
    q_cross = q * qdec_ref[...]
    k_state = k * kdec_ref[...]
    for h in range(_RET_HEADS):
        in_head = (lane // _RET_DK) == h
        vh = v_ref[:, h * _RET_DV:(h + 1) * _RET_DV].astype(_BF16)
        qh = jnp.where(in_head, q, 0.0).astype(_BF16)
        scores = _dot_nt(qh, k_bf) * din_ref[h]
        inner = _dot(scores.astype(_BF16), vh)
        state = state_sc[h]
        cross = _dot(jnp.where(in_head, q_cross, 0.0).astype(_BF16), state.astype(_BF16))
        state_sc[h] = state * chunk_dec[h] + _dot_tn(
            jnp.where(in_head, k_state, 0.0).astype(_BF16), vh)
        out = inner + cross
        mu = jnp.mean(out, axis=-1, keepdims=True)
        oc = out - mu
        var = jnp.mean(oc * oc, axis=-1, keepdims=True)
        normed = oc * lax.rsqrt(var + _GN_EPS)
        gate = g_ref[:, h * _RET_DV:(h + 1) * _RET_DV]
        y_ref[:, h * _RET_DV:(h + 1) * _RET_DV] = (_silu(gate) * normed).astype(y_ref.dtype)


def _retention_tables(seq):
    half = _RET_DK // 2
    inv_freq = _ROPE_BASE ** (-jnp.arange(half, dtype=_F32) / half)
    ang = jnp.arange(seq, dtype=jnp.int32).astype(_F32)[:, None] * inv_freq[None, :]
    cos, sin = jnp.cos(ang), jnp.sin(ang)
    cos_t = jnp.tile(jnp.concatenate([cos, cos], axis=1), (1, _RET_HEADS))
    sin_t = jnp.tile(jnp.concatenate([-sin, sin], axis=1), (1, _RET_HEADS))
    log_gamma = jnp.log1p(-jnp.exp2(-5.0 - jnp.arange(_RET_HEADS, dtype=_F32)))
    idx = jnp.arange(_BLOCK, dtype=_F32)
    rel = idx[:, None] - idx[None, :]
    causal = rel >= 0
    decay_in = jnp.where(causal[None],
                         jnp.exp(log_gamma[:, None, None] * jnp.where(causal, rel, 0.0)[None]), 0.0)
    q_dec = jnp.exp(log_gamma[:, None] * (idx + 1.0))
    k_dec = jnp.exp(log_gamma[:, None] * (_BLOCK - 1.0 - idx))
    q_dec_t = jnp.repeat(q_dec.T, _RET_DK, axis=1)
    k_dec_t = jnp.repeat(k_dec.T, _RET_DK, axis=1)
    return cos_t, sin_t, q_dec_t, k_dec_t, decay_in


def _retention(proj_a, bsz, seq):
    tokens = bsz * seq
    nchunk = seq // _BLOCK
    cos_t, sin_t, q_dec_t, k_dec_t, decay_in = _retention_tables(seq)
    chunk_dec = tuple(float((1.0 - 2.0 ** (-5.0 - h)) ** _BLOCK) for h in range(_RET_HEADS))
    qk = _RET_HEADS * _RET_DK
    wv = _RET_HEADS * _RET_DV
    row = lambda b, c: b * nchunk + c
    return pl.pallas_call(
        functools.partial(_retention_kernel, chunk_dec=chunk_dec),
        out_shape=jax.ShapeDtypeStruct((tokens, wv), _BF16),
        grid=(bsz, nchunk),
        in_specs=[
            pl.BlockSpec((_BLOCK, qk), lambda b, c: (row(b, c), 0)),
            pl.BlockSpec((_BLOCK, qk), lambda b, c: (row(b, c), 1)),
            pl.BlockSpec((_BLOCK, wv), lambda b, c: (row(b, c), 1)),
            pl.BlockSpec((_BLOCK, wv), lambda b, c: (row(b, c), 2)),
            pl.BlockSpec((_BLOCK, qk), lambda b, c: (c, 0)),
            pl.BlockSpec((_BLOCK, qk), lambda b, c: (c, 0)),
            pl.BlockSpec((_BLOCK, qk), lambda b, c: (0, 0)),
            pl.BlockSpec((_BLOCK, qk), lambda b, c: (0, 0)),
            pl.BlockSpec((_RET_HEADS, _BLOCK, _BLOCK), lambda b, c: (0, 0, 0)),
        ],
        out_specs=pl.BlockSpec((_BLOCK, wv), lambda b, c: (row(b, c), 0)),
        scratch_shapes=[pltpu.VMEM((_RET_HEADS, qk, _RET_DV), _F32)],
        compiler_params=_params(2),
        name="retention",
    )(proj_a, proj_a, proj_a, proj_a, cos_t, sin_t, q_dec_t, k_dec_t, decay_in)


def _fox_prep_kernel(fl_ref, b_ref, ccol_ref, crow_ref, *, nblk):
    r = lax.broadcasted_iota(jnp.int32, (_BLOCK, _BLOCK), 0)
    c = lax.broadcasted_iota(jnp.int32, (_BLOCK, _BLOCK), 1)
    tri = (r >= c).astype(_F32)
    carry = jnp.zeros((1, _LANES), _F32)
    for blk in range(nblk):
        rows = slice(blk * _BLOCK, (blk + 1) * _BLOCK)
        z = fl_ref[rows, :] + b_ref[...]
        log_f = jnp.minimum(z, 0.0) - jnp.log1p(jnp.exp(-jnp.abs(z)))
        cum = jnp.dot(tri, log_f, preferred_element_type=_F32,
                      precision=lax.Precision.HIGHEST) + carry
        ccol_ref[rows, :] = cum
        crow_ref[:, rows] = cum.T[0:_FOX_HEADS, :]
        carry = cum[_BLOCK - 1:_BLOCK, :]


def _fox_prep(fl, bias, bsz, seq):
    return pl.pallas_call(
        functools.partial(_fox_prep_kernel, nblk=seq // _BLOCK),
        out_shape=(jax.ShapeDtypeStruct((bsz * seq, _LANES), _F32),
                   jax.ShapeDtypeStruct((bsz, _FOX_HEADS, seq), _F32)),
        grid=(bsz,),
        in_specs=[pl.BlockSpec((seq, _LANES), lambda b: (b, 0)),
                  pl.BlockSpec((1, _LANES), lambda b: (0, 0))],
        out_specs=(pl.BlockSpec((seq, _LANES), lambda b: (b, 0)),
                   pl.BlockSpec((None, _FOX_HEADS, seq), lambda b: (b, 0, 0))),
        compiler_params=_params(1),
        name="fox_prep",
    )(fl, bias)


def _fox_kernel(q_ref, k_ref, v_ref, ccol_ref, crow_ref, o_ref, m_sc, l_sc, acc_sc, *, tq):
    p = pl.program_id(1)
    i = pl.program_id(2)
    lane = lax.broadcasted_iota(jnp.int32, (tq, _LANES), 1)
    low = lane < _FOX_DH
    q = q_ref[...] * (_FOX_DH ** -0.5)
    q2 = jnp.concatenate([jnp.where(low, q, 0.0), jnp.where(low, 0.0, q)], axis=0).astype(_BF16)
    cc = ccol_ref[...]
    c_t = [jnp.sum(jnp.where(lane == 2 * p + s, cc, 0.0), axis=1, keepdims=True) for s in range(2)]

    m_sc[...] = jnp.full_like(m_sc, -jnp.inf)
    l_sc[...] = jnp.zeros_like(l_sc)
    acc_sc[...] = jnp.zeros_like(acc_sc)

    row_id = lax.broadcasted_iota(jnp.int32, (tq, tq), 0)
    col_id = lax.broadcasted_iota(jnp.int32, (tq, tq), 1)
    keep = row_id >= col_id

    def block(j, masked):
        start = pl.multiple_of(j * tq, tq)
        kb = k_ref[pl.ds(start, tq), :].astype(_BF16)
        vb = v_ref[pl.ds(start, tq), :].astype(_BF16)
        s2 = _dot_nt(q2, kb)
        for s in range(2):
            c_s = crow_ref[s:s + 1, pl.ds(start, tq)]
            logits = s2[s * tq:(s + 1) * tq] + (c_t[s] - c_s)
            if masked:
                logits = jnp.where(keep, logits, -jnp.inf)
            m_old = m_sc[s]
            m_new = jnp.maximum(m_old, jnp.max(logits, axis=1, keepdims=True))
            scale = jnp.exp(m_old - m_new)
            pexp = jnp.exp(logits - m_new)
            l_sc[s] = scale * l_sc[s] + jnp.sum(pexp, axis=1, keepdims=True)
            acc_sc[s] = scale * acc_sc[s] + _dot(pexp.astype(_BF16), vb)
            m_sc[s] = m_new

    def body(j, carry):
        block(j, False)
        return carry

    lax.fori_loop(0, i, body, 0)
    block(i, True)
    out = jnp.where(low, acc_sc[0] / l_sc[0], acc_sc[1] / l_sc[1])
    o_ref[...] = out.astype(o_ref.dtype)


def _fox(proj_a, ccol, crow, bsz, seq):
    tokens = bsz * seq
    tq = min(_TQ, seq)
    nq = seq // tq
    npair = _FOX_HEADS // 2
    q0 = (_RET_HEADS * _RET_DK * 2 + _RET_HEADS * _RET_DV * 2) // _LANES
    k0 = q0 + _FOX_HEADS * _FOX_DH // _LANES
    v0 = k0 + _FOX_HEADS * _FOX_DH // _LANES
    crow4 = crow.reshape(bsz, npair, 2, seq)
    return pl.pallas_call(
        functools.partial(_fox_kernel, tq=tq),
        out_shape=jax.ShapeDtypeStruct((tokens, _FOX_HEADS * _FOX_DH), _BF16),
        grid=(bsz, npair, nq),
        in_specs=[
            pl.BlockSpec((tq, _LANES), lambda b, p, i: (b * nq + i, q0 + p)),
            pl.BlockSpec((seq, _LANES), lambda b, p, i: (b, k0 + p)),
            pl.BlockSpec((seq, _LANES), lambda b, p, i: (b, v0 + p)),
            pl.BlockSpec((tq, _LANES), lambda b, p, i: (b * nq + i, 0)),
            pl.BlockSpec((None, None, 2, seq), lambda b, p, i: (b, p, 0, 0)),
        ],
        out_specs=pl.BlockSpec((tq, _LANES), lambda b, p, i: (b * nq + i, p)),
        scratch_shapes=[pltpu.VMEM((2, tq, 1), _F32), pltpu.VMEM((2, tq, 1), _F32),
                        pltpu.VMEM((2, tq, _LANES), _F32)],
        compiler_params=_params(3),
        name="fox_attention",
    )(proj_a, proj_a, proj_a, ccol, crow4)


def _gmlp_kernel(gu_ref, gv_ref, w_ref, bst_ref, lng_ref, lnb_ref, y_ref, *, nchunk):
    u = _gelu_tanh(gu_ref[...])
    v = _layer_norm(_gelu_tanh(gv_ref[...]), lng_ref[...], lnb_ref[...], _LN_EPS).astype(_BF16)
    r = lax.broadcasted_iota(jnp.int32, (_BLOCK, _BLOCK), 0)
    c = lax.broadcasted_iota(jnp.int32, (_BLOCK, _BLOCK), 1)
    causal = r >= c
    for g in range(_GM_GROUPS):
        cols = slice(g * _GM_DG, (g + 1) * _GM_DG)
        wg = jnp.where(causal, w_ref[g], 0.0).astype(_BF16)
        bias = bst_ref[:, g:g + 1]
        for ch in range(nchunk):
            rows = slice(ch * _BLOCK, (ch + 1) * _BLOCK)
            sp = _dot(wg, v[rows, cols]) + bias
            y_ref[rows, cols] = (u[rows, cols] * sp).astype(y_ref.dtype)


def _gmlp(proj_c, w_s, b_s, ln_g, ln_b, *, tr):
    tokens = proj_c.shape[0]
    width = _GM_GROUPS * _GM_DG
    return pl.pallas_call(
        functools.partial(_gmlp_kernel, nchunk=tr // _BLOCK),
        out_shape=jax.ShapeDtypeStruct((tokens, width), _BF16),
        grid=(tokens // tr,),
        in_specs=[
            pl.BlockSpec((tr, width), lambda i: (i, 0)),
            pl.BlockSpec((tr, width), lambda i: (i, 1)),
            pl.BlockSpec((_GM_GROUPS, _BLOCK, _BLOCK), lambda i: (0, 0, 0)),
            pl.BlockSpec((_BLOCK, _GM_GROUPS), lambda i: (0, 0)),
            pl.BlockSpec((1, width), lambda i: (0, 0)),
            pl.BlockSpec((1, width), lambda i: (0, 0)),
        ],
        out_specs=pl.BlockSpec((tr, width), lambda i: (i, 0)),
        compiler_params=_params(1),
        name="gmlp",
    )(proj_c, proj_c, w_s, b_s.T, ln_g.reshape(1, width), ln_b.reshape(1, width))


def _merge_kernel(ya_ref, yb_ref, yc_ref, ga_ref, gb_ref, gc_ref, gbias_ref, wbr_ref, wout_ref,
                  x_ref, lng_ref, lnb_ref, xo_ref, xo16_ref, *, alpha, d_model):
    merged = None
    for n, (y_ref, gl_ref) in enumerate(((ya_ref, ga_ref), (yb_ref, gb_ref), (yc_ref, gc_ref))):
        z = gl_ref[...] + gbias_ref[:, n * d_model:(n + 1) * d_model]
        gate = 1.0 / (1.0 + jnp.exp(-z))
        term = gate * _dot(y_ref[...], wbr_ref[n])
        merged = term if merged is None else merged + term
    h = _dot(merged.astype(_BF16), wout_ref[...])
    out = _layer_norm(alpha * x_ref[...] + h, lng_ref[...], lnb_ref[...], _LN_EPS)
    xo_ref[...] = out
    xo16_ref[...] = out.astype(_BF16)


def _merge(y_a, y_b, y_c, proj_c, gate_b, w_branch, w_out, x, ln_g, ln_b, *, alpha, tm):
    tokens, d_model = x.shape
    bw = y_a.shape[1]
    gl0 = (_GM_GROUPS * _GM_DG * 2) // d_model
    row = lambda i: (i, 0)
    const2 = lambda i: (0, 0)
    return pl.pallas_call(
        functools.partial(_merge_kernel, alpha=alpha, d_model=d_model),
        out_shape=(jax.ShapeDtypeStruct((tokens, d_model), _F32),
                   jax.ShapeDtypeStruct((tokens, d_model), _BF16)),
        grid=(tokens // tm,),
        in_specs=[
            pl.BlockSpec((tm, bw), row), pl.BlockSpec((tm, bw), row), pl.BlockSpec((tm, bw), row),
            pl.BlockSpec((tm, d_model), lambda i: (i, gl0)),
            pl.BlockSpec((tm, d_model), lambda i: (i, gl0 + 1)),
            pl.BlockSpec((tm, d_model), lambda i: (i, gl0 + 2)),
            pl.BlockSpec((1, 3 * d_model), const2),
            pl.BlockSpec((3, bw, d_model), lambda i: (0, 0, 0)),
            pl.BlockSpec((d_model, d_model), const2),
            pl.BlockSpec((tm, d_model), row),
            pl.BlockSpec((1, d_model), const2), pl.BlockSpec((1, d_model), const2),
        ],
        out_specs=(pl.BlockSpec((tm, d_model), row), pl.BlockSpec((tm, d_model), row)),
        compiler_params=_params(1),
        name="merge",
    )(y_a, y_b, y_c, proj_c, proj_c, proj_c, gate_b.reshape(1, -1), w_branch.astype(_BF16),
      w_out.astype(_BF16), x, ln_g.reshape(1, -1), ln_b.reshape(1, -1))


def _weights_changed(te_ref, i):
    return jnp.logical_or(i == 0, te_ref[i] != te_ref[jnp.maximum(i - 1, 0)])


def _up_kernel(te_ref, nused_ref, x_ref, wa_ref, wb_ref, h_ref, wa_sc, wb_sc):
    i = pl.program_id(1)

    @pl.when(_weights_changed(te_ref, i))
    def _():
        wa_sc[...] = wa_ref[...].astype(_BF16)
        wb_sc[...] = wb_ref[...].astype(_BF16)

    @pl.when(i < nused_ref[0])
    def _():
        x = x_ref[...]
        a = _dot(x, wa_sc[...])
        b = _dot(x, wb_sc[...])
        h_ref[...] = (_silu(a) * b).astype(h_ref.dtype)

    @pl.when(i >= nused_ref[0])
    def _():
        h_ref[...] = jnp.zeros_like(h_ref)


def _swiglu_up(x16, w_up, te, nused, *, tm, tn):
    rows, k = x16.shape
    d_ff = w_up.shape[2] // 2
    nb = d_ff // tn
    grid_spec = pltpu.PrefetchScalarGridSpec(
        num_scalar_prefetch=2,
        grid=(nb, rows // tm),
        in_specs=[
            pl.BlockSpec((tm, k), lambda j, i, te, nu: (i, 0)),
            pl.BlockSpec((None, k, tn), lambda j, i, te, nu: (te[i], 0, j)),
            pl.BlockSpec((None, k, tn), lambda j, i, te, nu: (te[i], 0, j + nb)),
        ],
        out_specs=pl.BlockSpec((tm, tn), lambda j, i, te, nu: (i, j)),
        scratch_shapes=[pltpu.VMEM((k, tn), _BF16), pltpu.VMEM((k, tn), _BF16)],
    )
    return pl.pallas_call(
        _up_kernel,
        out_shape=jax.ShapeDtypeStruct((rows, d_ff), _BF16),
        grid_spec=grid_spec,
        compiler_params=_params(2),
        name="swiglu_up",
    )(te, nused, x16, w_up, w_up)


def _down_kernel(te_ref, nused_ref, h_ref, w_ref, y_ref, w_sc):
    i = pl.program_id(1)

    @pl.when(_weights_changed(te_ref, i))
    def _():
        w_sc[...] = w_ref[...].astype(_BF16)

    @pl.when(i < nused_ref[0])
    def _():
        y_ref[...] = _dot(h_ref[...], w_sc[...])

    @pl.when(i >= nused_ref[0])
    def _():
        y_ref[...] = jnp.zeros_like(y_ref)


def _expert_down(h16, w_down, te, nused, *, tm, tn):
    rows, k = h16.shape
    n = w_down.shape[2]
    grid_spec = pltpu.PrefetchScalarGridSpec(
        num_scalar_prefetch=2,
        grid=(n // tn, rows // tm),
        in_specs=[
            pl.BlockSpec((tm, k), lambda j, i, te, nu: (i, 0)),
            pl.BlockSpec((None, k, tn), lambda j, i, te, nu: (te[i], 0, j)),
        ],
        out_specs=pl.BlockSpec((tm, tn), lambda j, i, te, nu: (i, j)),
        scratch_shapes=[pltpu.VMEM((k, tn), _BF16)],
    )
    return pl.pallas_call(
        _down_kernel,
        out_shape=jax.ShapeDtypeStruct((rows, n), _F32),
        grid_spec=grid_spec,
        compiler_params=_params(2),
        name="expert_down",
    )(te, nused, h16, w_down)


def _down_ln_kernel(h_ref, w_ref, x_ref, lng_ref, lnb_ref, xo_ref, xo16_ref, *, alpha):
    f = _dot(h_ref[...], w_ref[...])
    out = _layer_norm(alpha * x_ref[...] + f, lng_ref[...], lnb_ref[...], _LN_EPS)
    xo_ref[...] = out
    xo16_ref[...] = out.astype(_BF16)


def _dense_down_ln(h16, w_down16, x, ln_g, ln_b, *, alpha, tm):
    tokens, d_model = x.shape
    k = h16.shape[1]
    row = lambda i: (i, 0)
    const2 = lambda i: (0, 0)
    return pl.pallas_call(
        functools.partial(_down_ln_kernel, alpha=alpha),
        out_shape=(jax.ShapeDtypeStruct((tokens, d_model), _F32),
                   jax.ShapeDtypeStruct((tokens, d_model), _BF16)),
        grid=(tokens // tm,),
        in_specs=[pl.BlockSpec((tm, k), row), pl.BlockSpec((k, d_model), const2),
                  pl.BlockSpec((tm, d_model), row),
                  pl.BlockSpec((1, d_model), const2), pl.BlockSpec((1, d_model), const2)],
        out_specs=(pl.BlockSpec((tm, d_model), row), pl.BlockSpec((tm, d_model), row)),
        compiler_params=_params(1),
        name="dense_down_ln",
    )(h16, w_down16, x, ln_g.reshape(1, -1), ln_b.reshape(1, -1))


def _router_kernel(x_ref, wr_ref, o_ref):
    logits = jnp.dot(x_ref[...], wr_ref[...], preferred_element_type=_F32,
                     precision=lax.Precision.HIGHEST)
    lane = lax.broadcasted_iota(jnp.int32, logits.shape, 1).astype(_F32)
    lg = jnp.where(lane < _N_EXPERTS, logits, -jnp.inf)
    m1 = jnp.max(lg, axis=1, keepdims=True)
    i1 = jnp.min(jnp.where(lg == m1, lane, float(_LANES)), axis=1, keepdims=True)
    lg2 = jnp.where(lane == i1, -jnp.inf, lg)
    m2 = jnp.max(lg2, axis=1, keepdims=True)
    i2 = jnp.min(jnp.where(lg2 == m2, lane, float(_LANES)), axis=1, keepdims=True)
    e2 = jnp.exp(m2 - m1)
    den = 1.0 + e2
    w1 = 1.0 / den
    w2 = e2 / den
    o_ref[...] = jnp.where(lane == 0, i1, jnp.where(lane == 1, i2,
                           jnp.where(lane == 2, w1, jnp.where(lane == 3, w2, 0.0))))


def _router(x, w_router, *, tm):
    tokens, d_model = x.shape
    wr = jnp.zeros((d_model, _LANES), _F32).at[:, :_N_EXPERTS].set(w_router)
    return pl.pallas_call(
        _router_kernel,
        out_shape=jax.ShapeDtypeStruct((tokens, _LANES), _F32),
        grid=(tokens // tm,),
        in_specs=[pl.BlockSpec((tm, d_model), lambda i: (i, 0)),
                  pl.BlockSpec((d_model, _LANES), lambda i: (0, 0))],
        out_specs=pl.BlockSpec((tm, _LANES), lambda i: (i, 0)),
        compiler_params=_params(1),
        name="router",
    )(x, wr)


def _gather_kernel(src_ref, x_hbm, o_ref, buf, sem, *, tm):
    def row_copy(r, tok):
        return pltpu.make_async_copy(x_hbm.at[pl.ds(tok, 1), :], buf.at[pl.ds(r, 1), :], sem)

    def issue(r, carry):
        row_copy(r, src_ref[0, 0, r]).start()
        return carry

    def wait(r, carry):
        row_copy(r, 0).wait()
        return carry

    lax.fori_loop(0, tm, issue, 0)
    lax.fori_loop(0, tm, wait, 0)
    o_ref[...] = buf[...].astype(o_ref.dtype)


def _gather_rows(x, src, *, tm):
    d_model = x.shape[1]
    ntiles = src.shape[0]
    return pl.pallas_call(
        functools.partial(_gather_kernel, tm=tm),
        out_shape=jax.ShapeDtypeStruct((ntiles * tm, d_model), _BF16),
        grid=(ntiles,),
        in_specs=[pl.BlockSpec((1, 1, tm), lambda i: (i, 0, 0), memory_space=pltpu.SMEM),
                  pl.BlockSpec(memory_space=pl.ANY)],
        out_specs=pl.BlockSpec((tm, d_model), lambda i: (i, 0)),
        scratch_shapes=[pltpu.VMEM((tm, d_model), _F32), pltpu.SemaphoreType.DMA],
        compiler_params=_params(1),
        name="moe_gather",
    )(src, x)


def _combine_kernel(d0_ref, d1_ref, y_hbm, route_ref, x_ref, lng_ref, lnb_ref, xo_ref, xo16_ref,
                    buf, sem, *, alpha, tm):
    def row_copy(s, r, slot):
        return pltpu.make_async_copy(y_hbm.at[pl.ds(slot, 1), :], buf.at[s, pl.ds(r, 1), :], sem)

    def issue(r, carry):
        row_copy(0, r, d0_ref[0, 0, r]).start()
        row_copy(1, r, d1_ref[0, 0, r]).start()
        return carry

    def wait(r, carry):
        row_copy(0, r, 0).wait()
        row_copy(1, r, 0).wait()
        return carry

    lax.fori_loop(0, tm, issue, 0)
    lax.fori_loop(0, tm, wait, 0)
    route = route_ref[...]
    f = route[:, 2:3] * buf[0] + route[:, 3:4] * buf[1]
    out = _layer_norm(alpha * x_ref[...] + f, lng_ref[...], lnb_ref[...], _LN_EPS)
    xo_ref[...] = out
    xo16_ref[...] = out.astype(_BF16)


def _combine_ln(ys, d0, d1, route, x, ln_g, ln_b, *, alpha, tm):
    tokens, d_model = x.shape
    row = lambda i: (i, 0)
    const2 = lambda i: (0, 0)
    smem = lambda: pl.BlockSpec((1, 1, tm), lambda i: (i, 0, 0), memory_space=pltpu.SMEM)
    return pl.pallas_call(
        functools.partial(_combine_kernel, alpha=alpha, tm=tm),
        out_shape=(jax.ShapeDtypeStruct((tokens, d_model), _F32),
                   jax.ShapeDtypeStruct((tokens, d_model), _BF16)),
        grid=(tokens // tm,),
        in_specs=[smem(), smem(), pl.BlockSpec(memory_space=pl.ANY),
                  pl.BlockSpec((tm, _LANES), row), pl.BlockSpec((tm, d_model), row),
                  pl.BlockSpec((1, d_model), const2), pl.BlockSpec((1, d_model), const2)],
        out_specs=(pl.BlockSpec((tm, d_model), row), pl.BlockSpec((tm, d_model), row)),
        scratch_shapes=[pltpu.VMEM((2, tm, d_model), _F32), pltpu.SemaphoreType.DMA],
        compiler_params=_params(1),
        name="moe_combine_ln",
    )(d0, d1, ys, route, x, ln_g.reshape(1, -1), ln_b.reshape(1, -1))


def _routing_plan(route, *, tm):
    tokens = route.shape[0]
    pairs = 2 * tokens
    ntiles = pairs // tm + _N_EXPERTS
    flat_e = route[:, 0:2].astype(jnp.int32).reshape(pairs)
    onehot = (flat_e[:, None] == jnp.arange(_N_EXPERTS, dtype=jnp.int32)[None, :]).astype(jnp.int32)
    csum = jnp.cumsum(onehot, axis=0)
    rank = jnp.take_along_axis(csum, flat_e[:, None], axis=1)[:, 0] - 1
    counts = csum[-1]
    padded = ((counts + tm - 1) // tm) * tm
    ends = jnp.cumsum(padded)
    dest = (ends - padded)[flat_e] + rank
    src = jnp.zeros((ntiles * tm,), jnp.int32).at[dest].set(jnp.arange(pairs, dtype=jnp.int32) // 2)
    tile_start = jnp.arange(ntiles, dtype=jnp.int32) * tm
    te = jnp.minimum(jnp.sum((ends[None, :] <= tile_start[:, None]).astype(jnp.int32), axis=1),
                     _N_EXPERTS - 1)
    nused = (ends[-1] // tm).reshape(1)
    dest2 = dest.reshape(tokens, 2)
    return src.reshape(ntiles, 1, tm), te, nused, dest2[:, 0], dest2[:, 1]


def _moe_ffn_ln(x, x16, w_router, w_up, w_down, ln_g, ln_b, *, alpha, tm):
    del x16
    tokens = x.shape[0]
    tc = min(256, tokens)
    route = _router(x, w_router, tm=tm)
    src, te, nused, d0, d1 = _routing_plan(route, tm=tm)
    xs = _gather_rows(x, src, tm=tm)
    h = _swiglu_up(xs, w_up, te, nused, tm=tm, tn=512)
    ys = _expert_down(h, w_down, te, nused, tm=tm, tn=512)
    return _combine_ln(ys, d0.reshape(tokens // tc, 1, tc), d1.reshape(tokens // tc, 1, tc),
                       route, x, ln_g, ln_b, alpha=alpha, tm=tc)


def _dense_ffn_ln(x, x16, w_up, w_down, ln_g, ln_b, *, alpha, tm):
    tokens = x.shape[0]
    d_ff = w_up.shape[1] // 2
    te = jnp.zeros((tokens // tm,), jnp.int32)
    nused = jnp.full((1,), tokens // tm, jnp.int32)
    tn = d_ff // 2 if (d_ff // 2) % _LANES == 0 else d_ff
    h = _swiglu_up(x16, w_up[None], te, nused, tm=tm, tn=tn)
    return _dense_down_ln(h, w_down.astype(_BF16), x, ln_g, ln_b, alpha=alpha, tm=tm)


def _mixer_ln(x, x16, w_in, fox_b_f, gate_b, gm_w_s, gm_b_s, gm_ln_g, gm_ln_b, w_branch, w_out,
              ln_g, ln_b, *, bsz, seq, alpha, tm):
    d_model = x.shape[1]
    n_a = 2 * _RET_HEADS * _RET_DK + 2 * _RET_HEADS * _RET_DV + 3 * _FOX_HEADS * _FOX_DH
    n_fl = _FOX_HEADS
    w_a = w_in[:, :n_a].astype(_BF16)
    w_fl = jnp.zeros((d_model, _LANES), _BF16).at[:, :n_fl].set(
        w_in[:, n_a:n_a + n_fl].astype(_BF16))
    w_c = w_in[:, n_a + n_fl:].astype(_BF16)
    proj_a = _matmul(x16, w_a, tm=tm, tn=1024, out_dtype=_F32, name="in_proj_a")
    fl = _matmul(x16, w_fl, tm=tm, tn=_LANES, out_dtype=_F32, name="in_proj_fl")
    proj_c = _matmul(x16, w_c, tm=tm, tn=1024, out_dtype=_F32, name="in_proj_c")

    y_a = _retention(proj_a, bsz, seq)
    bias = jnp.zeros((1, _LANES), _F32).at[0, :n_fl].set(fox_b_f)
    ccol, crow = _fox_prep(fl, bias, bsz, seq)
    y_b = _fox(proj_a, ccol, crow, bsz, seq)
    y_c = _gmlp(proj_c, gm_w_s, gm_b_s, gm_ln_g, gm_ln_b, tr=tm)
    return _merge(y_a, y_b, y_c, proj_c, gate_b, w_branch, w_out, x, ln_g, ln_b, alpha=alpha, tm=tm)


def kernel(x, w_in, fox_b_f, gate_b, gm_w_s, gm_b_s, gm_ln_g, gm_ln_b, w_branch, w_out, ln_g, ln_b,
           dense_w_up, dense_w_down, moe_router, moe_w_up, moe_w_down):
    bsz, seq, d_model = x.shape
    depth = w_in.shape[0]
    alpha = (2 * depth) ** 0.25
    tokens = bsz * seq
    tm = min(_TM, tokens)
    xf = x.reshape(tokens, d_model)
    x16 = xf.astype(_BF16)
    for l in range(depth):
        xf, x16 = _mixer_ln(xf, x16, w_in[l], fox_b_f[l], gate_b[l], gm_w_s[l], gm_b_s[l],
                            gm_ln_g[l], gm_ln_b[l], w_branch[l], w_out[l], ln_g[l, 0], ln_b[l, 0],
                            bsz=bsz, seq=seq, alpha=alpha, tm=tm)
        if l % 2 == 0:
            xf, x16 = _dense_ffn_ln(xf, x16, dense_w_up[l // 2], dense_w_down[l // 2],
                                    ln_g[l, 1], ln_b[l, 1], alpha=alpha, tm=tm)
        else:
            xf, x16 = _moe_ffn_ln(xf, x16, moe_router[l // 2], moe_w_up[l // 2], moe_w_down[l // 2],
                                  ln_g[l, 1], ln_b[l, 1], alpha=alpha, tm=tm)
    return xf.reshape(bsz, seq, d_model)
```

```python
import functools
import math

import jax
import jax.numpy as jnp
from jax import lax
from jax.experimental import pallas as pl
from jax.experimental.pallas import tpu as pltpu

_BF16 = jnp.bfloat16
_F32 = jnp.float32

_BLOCK = 128
_RET_HEADS, _RET_DK, _RET_DV = 4, 64, 128
_FOX_HEADS, _FOX_DH = 8, 64
_GM_GROUPS, _GM_DG = 4, 128
_N_EXPERTS = 8
_ROPE_BASE = 10000.0
_LN_EPS = 1e-5
_GN_EPS = 1e-6

_LANES = 128
_VMEM_PHYSICAL_BYTES = 64 * 1024 * 1024
_VMEM_LIMIT_BYTES = (_VMEM_PHYSICAL_BYTES * 7) // 8

_TM = 512
_TQ = 512


def _params(n_axes):
    return pltpu.CompilerParams(
        dimension_semantics=("arbitrary",) * n_axes,
        vmem_limit_bytes=_VMEM_LIMIT_BYTES,
    )


def _layer_norm(x, g, b, eps):
    mu = jnp.mean(x, axis=-1, keepdims=True)
    xc = x - mu
    var = jnp.mean(xc * xc, axis=-1, keepdims=True)
    return xc * lax.rsqrt(var + eps) * g + b


def _silu(x):
    return x * (1.0 / (1.0 + jnp.exp(-x)))


def _gelu_tanh(x):
    c = math.sqrt(2.0 / math.pi)
    return x * (0.5 * (1.0 + jnp.tanh(c * (x + 0.044715 * (x * x * x)))))


def _dot(a, b):
    return jnp.dot(a, b, preferred_element_type=_F32)


def _dot_nt(a, b):
    return lax.dot_general(a, b, (((1,), (1,)), ((), ())), preferred_element_type=_F32)


def _dot_tn(a, b):
    return lax.dot_general(a, b, (((0,), (0,)), ((), ())), preferred_element_type=_F32)


def _matmul_kernel(x_ref, w_ref, o_ref):
    o_ref[...] = _dot(x_ref[...], w_ref[...]).astype(o_ref.dtype)


def _matmul(x, w, *, tm, tn, out_dtype, name):
    m, k = x.shape
    n = w.shape[1]
    return pl.pallas_call(
        _matmul_kernel,
        out_shape=jax.ShapeDtypeStruct((m, n), out_dtype),
        grid=(n // tn, m // tm),
        in_specs=[pl.BlockSpec((tm, k), lambda j, i: (i, 0)),
                  pl.BlockSpec((k, tn), lambda j, i: (0, j))],
        out_specs=pl.BlockSpec((tm, tn), lambda j, i: (i, j)),
        compiler_params=_params(2),
        name=name,
    )(x, w)


def _retention_kernel(q_ref, k_ref, v_ref, g_ref, cos_ref, sin_ref, qdec_ref, kdec_ref, din_ref,
                      y_ref, state_sc, *, chunk_dec):
    c = pl.program_id(1)

    @pl.when(c == 0)
    def _():
        state_sc[...] = jnp.zeros_like(state_sc)

    width = _RET_HEADS * _RET_DK
    lane = lax.broadcasted_iota(jnp.int32, (_BLOCK, width), 1)
    first_half = (lane % _RET_DK) < (_RET_DK // 2)

    def rotary(x):
        partner = jnp.where(first_half,
                            pltpu.roll(x, width - _RET_DK // 2, 1),
                            pltpu.roll(x, _RET_DK // 2, 1))
        return x * cos_ref[...] + partner * sin_ref[...]

    q = rotary(q_ref[...])
    k = rotary(k_ref[...]) * (_RET_DK ** -0.5)
    k_bf = k.astype(_BF16)
    q_cross = q * qdec_ref[...]
    k_state = k * kdec_ref[...]
    for h in range(_RET_HEADS):
        in_head = (lane // _RET_DK) == h
        vh = v_ref[:, h * _RET_DV:(h + 1) * _RET_DV].astype(_BF16)
        qh = jnp.where(in_head, q, 0.0).astype(_BF16)
        scores = _dot_nt(qh, k_bf) * din_ref[h]
        inner = _dot(scores.astype(_BF16), vh)
        state = state_sc[h]
        cross = _dot(jnp.where(in_head, q_cross, 0.0).astype(_BF16), state.astype(_BF16))
        state_sc[h] = state * chunk_dec[h] + _dot_tn(
            jnp.where(in_head, k_state, 0.0).astype(_BF16), vh)
        out = inner + cross
        mu = jnp.mean(out, axis=-1, keepdims=True)
        oc = out - mu
        var = jnp.mean(oc * oc, axis=-1, keepdims=True)
        normed = oc * lax.rsqrt(var + _GN_EPS)
        gate = g_ref[:, h * _RET_DV:(h + 1) * _RET_DV]
        y_ref[:, h * _RET_DV:(h + 1) * _RET_DV] = (_silu(gate) * normed).astype(y_ref.dtype)


def _retention_tables(seq):
    half = _RET_DK // 2
    inv_freq = _ROPE_BASE ** (-jnp.arange(half, dtype=_F32) / half)
    ang = jnp.arange(seq, dtype=jnp.int32).astype(_F32)[:, None] * inv_freq[None, :]
    cos, sin = jnp.cos(ang), jnp.sin(ang)
    cos_t = jnp.tile(jnp.concatenate([cos, cos], axis=1), (1, _RET_HEADS))
    sin_t = jnp.tile(jnp.concatenate([-sin, sin], axis=1), (1, _RET_HEADS))
    log_gamma = jnp.log1p(-jnp.exp2(-5.0 - jnp.arange(_RET_HEADS, dtype=_F32)))
    idx = jnp.arange(_BLOCK, dtype=_F32)
    rel = idx[:, None] - idx[None, :]
    causal = rel >= 0
    decay_in = jnp.where(causal[None],
                         jnp.exp(log_gamma[:, None, None] * jnp.where(causal, rel, 0.0)[None]), 0.0)
    q_dec = jnp.exp(log_gamma[:, None] * (idx + 1.0))
    k_dec = jnp.exp(log_gamma[:, None] * (_BLOCK - 1.0 - idx))
    q_dec_t = jnp.repeat(q_dec.T, _RET_DK, axis=1)
    k_dec_t = jnp.repeat(k_dec.T, _RET_DK, axis=1)
    return cos_t, sin_t, q_dec_t, k_dec_t, decay_in


def _retention(proj_a, bsz, seq):
    tokens = bsz * seq
    nchunk = seq // _BLOCK
    cos_t, sin_t, q_dec_t, k_dec_t, decay_in = _retention_tables(seq)
    chunk_dec = tuple(float((1.0 - 2.0 ** (-5.0 - h)) ** _BLOCK) for h in range(_RET_HEADS))
    qk = _RET_HEADS * _RET_DK
    wv = _RET_HEADS * _RET_DV
    row = lambda b, c: b * nchunk + c
    return pl.pallas_call(
        functools.partial(_retention_kernel, chunk_dec=chunk_dec),
        out_shape=jax.ShapeDtypeStruct((tokens, wv), _BF16),
        grid=(bsz, nchunk),
        in_specs=[
            pl.BlockSpec((_BLOCK, qk), lambda b, c: (row(b, c), 0)),
            pl.BlockSpec((_BLOCK, qk), lambda b, c: (row(b, c), 1)),
            pl.BlockSpec((_BLOCK, wv), lambda b, c: (row(b, c), 1)),
            pl.BlockSpec((_BLOCK, wv), lambda b, c: (row(b, c), 2)),
            pl.BlockSpec((_BLOCK, qk), lambda b, c: (c, 0)),
            pl.BlockSpec((_BLOCK, qk), lambda b, c: (c, 0)),
            pl.BlockSpec((_BLOCK, qk), lambda b, c: (0, 0)),
            pl.BlockSpec((_BLOCK, qk), lambda b, c: (0, 0)),
            pl.BlockSpec((_RET_HEADS, _BLOCK, _BLOCK), lambda b, c: (0, 0, 0)),
        ],
        out_specs=pl.BlockSpec((_BLOCK, wv), lambda b, c: (row(b, c), 0)),
        scratch_shapes=[pltpu.VMEM((_RET_HEADS, qk, _RET_DV), _F32)],
        compiler_params=_params(2),
        name="retention",
    )(proj_a, proj_a, proj_a, proj_a, cos_t, sin_t, q_dec_t, k_dec_t, decay_in)


_C_PARTS = 3
_TS_PREP = 512


def _split_bf16(c):
    parts = []
    rest = c
    for _ in range(_C_PARTS):
        piece = rest.astype(_BF16).astype(_F32)
        parts.append(piece)
        rest = rest - piece
    return parts


def _fox_prep_kernel(fq_ref, fk_ref, fv_ref, fl_ref, b_ref, qa_ref, ka_ref, va_ref, carry_sc, *, nblk):
    @pl.when(pl.program_id(1) == 0)
    def _():
        carry_sc[...] = jnp.zeros_like(carry_sc)

    r = lax.broadcasted_iota(jnp.int32, (_BLOCK, _LANES), 0)
    lane = lax.broadcasted_iota(jnp.int32, (_BLOCK, _LANES), 1)
    tri = (r >= lane).astype(_F32)
    half = _LANES // 2
    for blk in range(nblk):
        rows = slice(blk * _BLOCK, (blk + 1) * _BLOCK)
        z = fl_ref[rows, :] + b_ref[...]
        log_f = jnp.minimum(z, 0.0) - jnp.log1p(jnp.exp(-jnp.abs(z)))
        cum = jnp.dot(tri, log_f, preferred_element_type=_F32,
                      precision=lax.Precision.HIGHEST) + carry_sc[...]
        carry_sc[...] = cum[_BLOCK - 1:_BLOCK, :]
        for h in range(_FOX_HEADS):
            pair = slice((h // 2) * _LANES, (h // 2 + 1) * _LANES)
            in_head = (lane < half) if h % 2 == 0 else (lane >= half)
            base = half if h % 2 == 0 else 0
            c_h = jnp.sum(jnp.where(lane == h, cum, 0.0), axis=1, keepdims=True)
            qa = jnp.where(in_head, fq_ref[rows, pair].astype(_F32) * (_FOX_DH ** -0.5), 0.0)
            ka = jnp.where(in_head, fk_ref[rows, pair].astype(_F32), 0.0)
            for n, piece in enumerate(_split_bf16(c_h)):
                qa = jnp.where(lane == base + n, piece, qa)
                ka = jnp.where(lane == base + n, 1.0, ka)
                qa = jnp.where(lane == base + _C_PARTS + n, 1.0, qa)
                ka = jnp.where(lane == base + _C_PARTS + n, -piece, ka)
            qa_ref[h, rows, :] = qa.astype(_BF16)
            ka_ref[h, rows, :] = ka.astype(_BF16)
            va_ref[h, rows, :] = jnp.where(in_head, fv_ref[rows, pair].astype(_F32), 1.0).astype(_BF16)


def _fox_prep(proj_f, fl, bias, bsz, seq):
    ts = min(_TS_PREP, seq)
    ns = seq // ts
    width = _FOX_HEADS * _FOX_DH
    aug = jax.ShapeDtypeStruct((bsz, _FOX_HEADS, seq, _LANES), _BF16)
    aug_spec = pl.BlockSpec((None, _FOX_HEADS, ts, _LANES), lambda b, s: (b, 0, s, 0))
    return pl.pallas_call(
        functools.partial(_fox_prep_kernel, nblk=ts // _BLOCK),
        out_shape=(aug, aug, aug),
        grid=(bsz, ns),
        in_specs=[pl.BlockSpec((ts, width), lambda b, s: (b * ns + s, 0)),
                  pl.BlockSpec((ts, width), lambda b, s: (b * ns + s, 1)),
                  pl.BlockSpec((ts, width), lambda b, s: (b * ns + s, 2)),
                  pl.BlockSpec((ts, _LANES), lambda b, s: (b * ns + s, 0)),
                  pl.BlockSpec((1, _LANES), lambda b, s: (0, 0))],
        out_specs=(aug_spec, aug_spec, aug_spec),
        scratch_shapes=[pltpu.VMEM((1, _LANES), _F32)],
        compiler_params=_params(2),
        name="fox_prep",
    )(proj_f, proj_f, proj_f, fl, bias)


def _fox_kernel(qa_ref, ka_ref, va_ref, o_ref, m0_sc, m1_sc, acc0_sc, acc1_sc, *, tq):
    i = pl.program_id(2)
    m_scs = (m0_sc, m1_sc)
    acc_scs = (acc0_sc, acc1_sc)
    for s in range(2):
        m_scs[s][...] = jnp.full_like(m_scs[s], -jnp.inf)
        acc_scs[s][...] = jnp.zeros_like(acc_scs[s])

    row_id = lax.broadcasted_iota(jnp.int32, (tq, tq), 0)
    col_id = lax.broadcasted_iota(jnp.int32, (tq, tq), 1)
    keep = row_id >= col_id

    def block(j, masked):
        start = pl.multiple_of(j * tq, tq)
        logits = [_dot_nt(qa_ref[s], ka_ref[s, pl.ds(start, tq), :]) for s in range(2)]
        pexp, scale, m_new = [], [], []
        for s in range(2):
            lg = jnp.where(keep, logits[s], -jnp.inf) if masked else logits[s]
            m_old = m_scs[s][...]
            m_new.append(jnp.maximum(m_old, jnp.max(lg, axis=1, keepdims=True)))
            pexp.append(jnp.exp(lg - jnp.concatenate([m_new[s]] * (tq // _LANES), axis=1)))
            scale.append(jnp.exp(m_old - m_new[s]))
        pv = [_dot(pexp[s].astype(_BF16), va_ref[s, pl.ds(start, tq), :]) for s in range(2)]
        for s in range(2):
            acc_scs[s][...] = scale[s] * acc_scs[s][...] + pv[s]
            m_scs[s][...] = m_new[s]

    def body(j, carry):
        block(j, False)
        return carry

    lax.fori_loop(0, i, body, 0)
    block(i, True)
    half = _LANES // 2
    acc0 = acc0_sc[...]
    acc1 = acc1_sc[...]
    low = lax.broadcasted_iota(jnp.int32, (tq, _LANES), 1) < half
    out = jnp.where(low, acc0 / acc0[:, half:half + 1], acc1 / acc1[:, 0:1])
    o_ref[...] = out.astype(o_ref.dtype)


def _fox(qa, ka, va, bsz, seq):
    tq = min(_TQ, seq)
    nq = seq // tq
    npair = _FOX_HEADS // 2
    return pl.pallas_call(
        functools.partial(_fox_kernel, tq=tq),
        out_shape=jax.ShapeDtypeStruct((bsz * seq, _FOX_HEADS * _FOX_DH), _BF16),
        grid=(bsz, npair, nq),
        in_specs=[
            pl.BlockSpec((None, 2, tq, _LANES), lambda b, p, i: (b, p, i, 0)),
            pl.BlockSpec((None, 2, seq, _LANES), lambda b, p, i: (b, p, 0, 0)),
            pl.BlockSpec((None, 2, seq, _LANES), lambda b, p, i: (b, p, 0, 0)),
        ],
        out_specs=pl.BlockSpec((tq, _LANES), lambda b, p, i: (b * nq + i, p)),
        scratch_shapes=[pltpu.VMEM((tq, _LANES), _F32), pltpu.VMEM((tq, _LANES), _F32),
                        pltpu.VMEM((tq, _LANES), _F32), pltpu.VMEM((tq, _LANES), _F32)],
        compiler_params=_params(3),
        name="fox_attention",
    )(qa, ka, va)


def _gmlp_kernel(gu_ref, gv_ref, w_ref, bst_ref, lng_ref, lnb_ref, y_ref, *, nchunk):
    u = _gelu_tanh(gu_ref[...])
    v = _layer_norm(_gelu_tanh(gv_ref[...]), lng_ref[...], lnb_ref[...], _LN_EPS).astype(_BF16)
    r = lax.broadcasted_iota(jnp.int32, (_BLOCK, _BLOCK), 0)
    c = lax.broadcasted_iota(jnp.int32, (_BLOCK, _BLOCK), 1)
    causal = r >= c
    for g in range(_GM_GROUPS):
        cols = slice(g * _GM_DG, (g + 1) * _GM_DG)
        wg = jnp.where(causal, w_ref[g], 0.0).astype(_BF16)
        bias = bst_ref[:, g:g + 1]
        for ch in range(nchunk):
            rows = slice(ch * _BLOCK, (ch + 1) * _BLOCK)
            sp = _dot(wg, v[rows, cols]) + bias
            y_ref[rows, cols] = (u[rows, cols] * sp).astype(y_ref.dtype)


def _gmlp(proj_c, w_s, b_s, ln_g, ln_b, *, tr):
    tokens = proj_c.shape[0]
    width = _GM_GROUPS * _GM_DG
    return pl.pallas_call(
        functools.partial(_gmlp_kernel, nchunk=tr // _BLOCK),
        out_shape=jax.ShapeDtypeStruct((tokens, width), _BF16),
        grid=(tokens // tr,),
        in_specs=[
            pl.BlockSpec((tr, width), lambda i: (i, 0)),
            pl.BlockSpec((tr, width), lambda i: (i, 1)),
            pl.BlockSpec((_GM_GROUPS, _BLOCK, _BLOCK), lambda i: (0, 0, 0)),
            pl.BlockSpec((_BLOCK, _GM_GROUPS), lambda i: (0, 0)),
            pl.BlockSpec((1, width), lambda i: (0, 0)),
            pl.BlockSpec((1, width), lambda i: (0, 0)),
        ],
        out_specs=pl.BlockSpec((tr, width), lambda i: (i, 0)),
        compiler_params=_params(1),
        name="gmlp",
    )(proj_c, proj_c, w_s, b_s.T, ln_g.reshape(1, width), ln_b.reshape(1, width))


def _merge_kernel(ya_ref, yb_ref, yc_ref, ga_ref, gb_ref, gc_ref, gbias_ref, wbr_ref, wout_ref,
                  x_ref, lng_ref, lnb_ref, xo_ref, xo16_ref, *, alpha, d_model):
    merged = None
    for n, (y_ref, gl_ref) in enumerate(((ya_ref, ga_ref), (yb_ref, gb_ref), (yc_ref, gc_ref))):
        z = gl_ref[...] + gbias_ref[:, n * d_model:(n + 1) * d_model]
        gate = 1.0 / (1.0 + jnp.exp(-z))
        term = gate * _dot(y_ref[...], wbr_ref[n])
        merged = term if merged is None else merged + term
    h = _dot(merged.astype(_BF16), wout_ref[...])
    out = _layer_norm(alpha * x_ref[...] + h, lng_ref[...], lnb_ref[...], _LN_EPS)
    xo_ref[...] = out
    xo16_ref[...] = out.astype(_BF16)


def _merge(y_a, y_b, y_c, proj_c, gate_b, w_branch, w_out, x, ln_g, ln_b, *, alpha, tm):
    tokens, d_model = x.shape
    bw = y_a.shape[1]
    gl0 = (_GM_GROUPS * _GM_DG * 2) // d_model
    row = lambda i: (i, 0)
    const2 = lambda i: (0, 0)
    return pl.pallas_call(
        functools.partial(_merge_kernel, alpha=alpha, d_model=d_model),
        out_shape=(jax.ShapeDtypeStruct((tokens, d_model), _F32),
                   jax.ShapeDtypeStruct((tokens, d_model), _BF16)),
        grid=(tokens // tm,),
        in_specs=[
            pl.BlockSpec((tm, bw), row), pl.BlockSpec((tm, bw), row), pl.BlockSpec((tm, bw), row),
            pl.BlockSpec((tm, d_model), lambda i: (i, gl0)),
            pl.BlockSpec((tm, d_model), lambda i: (i, gl0 + 1)),
            pl.BlockSpec((tm, d_model), lambda i: (i, gl0 + 2)),
            pl.BlockSpec((1, 3 * d_model), const2),
            pl.BlockSpec((3, bw, d_model), lambda i: (0, 0, 0)),
            pl.BlockSpec((d_model, d_model), const2),
            pl.BlockSpec((tm, d_model), row),
            pl.BlockSpec((1, d_model), const2), pl.BlockSpec((1, d_model), const2),
        ],
        out_specs=(pl.BlockSpec((tm, d_model), row), pl.BlockSpec((tm, d_model), row)),
        compiler_params=_params(1),
        name="merge",
    )(y_a, y_b, y_c, proj_c, proj_c, proj_c, gate_b.reshape(1, -1), w_branch.astype(_BF16),
      w_out.astype(_BF16), x, ln_g.reshape(1, -1), ln_b.reshape(1, -1))


def _weights_changed(te_ref, i):
    return jnp.logical_or(i == 0, te_ref[i] != te_ref[jnp.maximum(i - 1, 0)])


def _up_kernel(te_ref, nused_ref, x_ref, wa_ref, wb_ref, h_ref, wa_sc, wb_sc):
    i = pl.program_id(1)

    @pl.when(_weights_changed(te_ref, i))
    def _():
        wa_sc[...] = wa_ref[...].astype(_BF16)
        wb_sc[...] = wb_ref[...].astype(_BF16)

    @pl.when(i < nused_ref[0])
    def _():
        x = x_ref[...].astype(_BF16)
        a = _dot(x, wa_sc[...])
        b = _dot(x, wb_sc[...])
        h_ref[...] = (_silu(a) * b).astype(h_ref.dtype)

    @pl.when(i >= nused_ref[0])
    def _():
        h_ref[...] = jnp.zeros_like(h_ref)


def _swiglu_up(x16, w_up, te, nused, *, tm, tn):
    rows, k = x16.shape
    d_ff = w_up.shape[2] // 2
    nb = d_ff // tn
    grid_spec = pltpu.PrefetchScalarGridSpec(
        num_scalar_prefetch=2,
        grid=(nb, rows // tm),
        in_specs=[
            pl.BlockSpec((tm, k), lambda j, i, te, nu: (i, 0)),
            pl.BlockSpec((None, k, tn), lambda j, i, te, nu: (te[i], 0, j)),
            pl.BlockSpec((None, k, tn), lambda j, i, te, nu: (te[i], 0, j + nb)),
        ],
        out_specs=pl.BlockSpec((tm, tn), lambda j, i, te, nu: (i, j)),
        scratch_shapes=[pltpu.VMEM((k, tn), _BF16), pltpu.VMEM((k, tn), _BF16)],
    )
    return pl.pallas_call(
        _up_kernel,
        out_shape=jax.ShapeDtypeStruct((rows, d_ff), _BF16),
        grid_spec=grid_spec,
        compiler_params=_params(2),
        name="swiglu_up",
    )(te, nused, x16, w_up, w_up)


def _down_kernel(te_ref, nused_ref, h_ref, w_ref, y_ref, w_sc):
    i = pl.program_id(1)

    @pl.when(_weights_changed(te_ref, i))
    def _():
        w_sc[...] = w_ref[...].astype(_BF16)

    @pl.when(i < nused_ref[0])
    def _():
        y_ref[...] = _dot(h_ref[...], w_sc[...])

    @pl.when(i >= nused_ref[0])
    def _():
        y_ref[...] = jnp.zeros_like(y_ref)


def _expert_down(h16, w_down, te, nused, *, tm, tn):
    rows, k = h16.shape
    n = w_down.shape[2]
    grid_spec = pltpu.PrefetchScalarGridSpec(
        num_scalar_prefetch=2,
        grid=(n // tn, rows // tm),
        in_specs=[
            pl.BlockSpec((tm, k), lambda j, i, te, nu: (i, 0)),
            pl.BlockSpec((None, k, tn), lambda j, i, te, nu: (te[i], 0, j)),
        ],
        out_specs=pl.BlockSpec((tm, tn), lambda j, i, te, nu: (i, j)),
        scratch_shapes=[pltpu.VMEM((k, tn), _BF16)],
    )
    return pl.pallas_call(
        _down_kernel,
        out_shape=jax.ShapeDtypeStruct((rows, n), _F32),
        grid_spec=grid_spec,
        compiler_params=_params(2),
        name="expert_down",
    )(te, nused, h16, w_down)


def _down_ln_kernel(h_ref, w_ref, x_ref, lng_ref, lnb_ref, xo_ref, xo16_ref, *, alpha):
    f = _dot(h_ref[...], w_ref[...])
    out = _layer_norm(alpha * x_ref[...] + f, lng_ref[...], lnb_ref[...], _LN_EPS)
    xo_ref[...] = out
    xo16_ref[...] = out.astype(_BF16)


def _dense_down_ln(h16, w_down16, x, ln_g, ln_b, *, alpha, tm):
    tokens, d_model = x.shape
    k = h16.shape[1]
    row = lambda i: (i, 0)
    const2 = lambda i: (0, 0)
    return pl.pallas_call(
        functools.partial(_down_ln_kernel, alpha=alpha),
        out_shape=(jax.ShapeDtypeStruct((tokens, d_model), _F32),
                   jax.ShapeDtypeStruct((tokens, d_model), _BF16)),
        grid=(tokens // tm,),
        in_specs=[pl.BlockSpec((tm, k), row), pl.BlockSpec((k, d_model), const2),
                  pl.BlockSpec((tm, d_model), row),
                  pl.BlockSpec((1, d_model), const2), pl.BlockSpec((1, d_model), const2)],
        out_specs=(pl.BlockSpec((tm, d_model), row), pl.BlockSpec((tm, d_model), row)),
        compiler_params=_params(1),
        name="dense_down_ln",
    )(h16, w_down16, x, ln_g.reshape(1, -1), ln_b.reshape(1, -1))


def _router_kernel(x_ref, wr_ref, o_ref):
    logits = jnp.dot(x_ref[...], wr_ref[...], preferred_element_type=_F32,
                     precision=lax.Precision.HIGHEST)
    lane = lax.broadcasted_iota(jnp.int32, logits.shape, 1).astype(_F32)
    lg = jnp.where(lane < _N_EXPERTS, logits, -jnp.inf)
    m1 = jnp.max(lg, axis=1, keepdims=True)
    i1 = jnp.min(jnp.where(lg == m1, lane, float(_LANES)), axis=1, keepdims=True)
    lg2 = jnp.where(lane == i1, -jnp.inf, lg)
    m2 = jnp.max(lg2, axis=1, keepdims=True)
    i2 = jnp.min(jnp.where(lg2 == m2, lane, float(_LANES)), axis=1, keepdims=True)
    e2 = jnp.exp(m2 - m1)
    den = 1.0 + e2
    w1 = 1.0 / den
    w2 = e2 / den
    o_ref[...] = jnp.where(lane == 0, i1, jnp.where(lane == 1, i2,
                           jnp.where(lane == 2, w1, jnp.where(lane == 3, w2, 0.0))))


def _router(x, w_router, *, tm):
    tokens, d_model = x.shape
    wr = jnp.zeros((d_model, _LANES), _F32).at[:, :_N_EXPERTS].set(w_router)
    return pl.pallas_call(
        _router_kernel,
        out_shape=jax.ShapeDtypeStruct((tokens, _LANES), _F32),
        grid=(tokens // tm,),
        in_specs=[pl.BlockSpec((tm, d_model), lambda i: (i, 0)),
                  pl.BlockSpec((d_model, _LANES), lambda i: (0, 0))],
        out_specs=pl.BlockSpec((tm, _LANES), lambda i: (i, 0)),
        compiler_params=_params(1),
        name="router",
    )(x, wr)


_ROW_UNROLL = 8


def _dispatch_kernel(zstart_ref, zvalid_ref, d0_ref, d1_ref, x_ref, xs_hbm, zero_sc, sem, zsem,
                     *, tt, tm):
    @pl.when(pl.program_id(0) == 0)
    def _():
        zero_sc[...] = jnp.zeros_like(zero_sc)

        def zero_copy(n):
            start = pl.multiple_of(zstart_ref[n], tm)
            return pltpu.make_async_copy(zero_sc, xs_hbm.at[pl.ds(start, tm), :], zsem)

        for n in range(2 * _N_EXPERTS):
            @pl.when(zvalid_ref[n] > 0)
            def _():
                zero_copy(n).start()
        for n in range(2 * _N_EXPERTS):
            @pl.when(zvalid_ref[n] > 0)
            def _():
                zero_copy(n).wait()

    def issue(g, carry):
        for u in range(_ROW_UNROLL):
            r = g * _ROW_UNROLL + u
            for d_ref in (d0_ref, d1_ref):
                pltpu.make_async_copy(x_ref.at[pl.ds(r, 1), :],
                                      xs_hbm.at[pl.ds(d_ref[0, 0, r], 1), :], sem).start()
        return carry

    lax.fori_loop(0, tt // _ROW_UNROLL, issue, 0)
    for _ in range(2):
        pltpu.make_async_copy(x_ref, xs_hbm.at[pl.ds(0, tt), :], sem).wait()


def _dispatch_rows(x, d0, d1, zstart, zvalid, *, rows, tt, tm):
    tokens, d_model = x.shape
    smem = lambda: pl.BlockSpec((1, 1, tt), lambda i, zs, zv: (i, 0, 0), memory_space=pltpu.SMEM)
    grid_spec = pltpu.PrefetchScalarGridSpec(
        num_scalar_prefetch=2,
        grid=(tokens // tt,),
        in_specs=[smem(), smem(), pl.BlockSpec((tt, d_model), lambda i, zs, zv: (i, 0))],
        out_specs=pl.BlockSpec(memory_space=pl.ANY),
        scratch_shapes=[pltpu.VMEM((tm, d_model), _F32), pltpu.SemaphoreType.DMA,
                        pltpu.SemaphoreType.DMA],
    )
    return pl.pallas_call(
        functools.partial(_dispatch_kernel, tt=tt, tm=tm),
        out_shape=jax.ShapeDtypeStruct((rows, d_model), _F32),
        grid_spec=grid_spec,
        compiler_params=_params(1),
        name="moe_dispatch",
    )(zstart, zvalid, d0, d1, x)


def _combine_kernel(d0_ref, d1_ref, y_hbm, route_ref, x_ref, lng_ref, lnb_ref, xo_ref, xo16_ref,
                    buf, sem, *, alpha, tm):
    def issue(g, carry):
        for u in range(_ROW_UNROLL):
            r = g * _ROW_UNROLL + u
            for s, d_ref in enumerate((d0_ref, d1_ref)):
                pltpu.make_async_copy(y_hbm.at[pl.ds(d_ref[0, 0, r], 1), :],
                                      buf.at[s, pl.ds(r, 1), :], sem).start()
        return carry

    lax.fori_loop(0, tm // _ROW_UNROLL, issue, 0)
    for s in range(2):
        pltpu.make_async_copy(y_hbm.at[pl.ds(0, tm), :], buf.at[s], sem).wait()
    route = route_ref[...]
    f = route[:, 2:3] * buf[0] + route[:, 3:4] * buf[1]
    out = _layer_norm(alpha * x_ref[...] + f, lng_ref[...], lnb_ref[...], _LN_EPS)
    xo_ref[...] = out
    xo16_ref[...] = out.astype(_BF16)


def _combine_ln(ys, d0, d1, route, x, ln_g, ln_b, *, alpha, tm):
    tokens, d_model = x.shape
    row = lambda i: (i, 0)
    const2 = lambda i: (0, 0)
    smem = lambda: pl.BlockSpec((1, 1, tm), lambda i: (i, 0, 0), memory_space=pltpu.SMEM)
    return pl.pallas_call(
        functools.partial(_combine_kernel, alpha=alpha, tm=tm),
        out_shape=(jax.ShapeDtypeStruct((tokens, d_model), _F32),
                   jax.ShapeDtypeStruct((tokens, d_model), _BF16)),
        grid=(tokens // tm,),
        in_specs=[smem(), smem(), pl.BlockSpec(memory_space=pl.ANY),
                  pl.BlockSpec((tm, _LANES), row), pl.BlockSpec((tm, d_model), row),
                  pl.BlockSpec((1, d_model), const2), pl.BlockSpec((1, d_model), const2)],
        out_specs=(pl.BlockSpec((tm, d_model), row), pl.BlockSpec((tm, d_model), row)),
        scratch_shapes=[pltpu.VMEM((2, tm, d_model), _F32), pltpu.SemaphoreType.DMA],
        compiler_params=_params(1),
        name="moe_combine_ln",
    )(d0, d1, ys, route, x, ln_g.reshape(1, -1), ln_b.reshape(1, -1))


def _routing_plan(route, *, tm):
    tokens = route.shape[0]
    pairs = 2 * tokens
    ntiles = pairs // tm + _N_EXPERTS
    flat_e = route[:, 0:2].astype(jnp.int32).reshape(pairs)
    onehot = (flat_e[:, None] == jnp.arange(_N_EXPERTS, dtype=jnp.int32)[None, :]).astype(jnp.int32)
    csum = jnp.cumsum(onehot, axis=0)
    counts = csum[-1]
    padded = ((counts + tm - 1) // tm) * tm
    ends = jnp.cumsum(padded)
    starts = ends - padded
    dest = jnp.sum(onehot * (starts[None, :] + csum - 1), axis=1)
    tile_start = jnp.arange(ntiles, dtype=jnp.int32) * tm
    te = jnp.minimum(jnp.sum((ends[None, :] <= tile_start[:, None]).astype(jnp.int32), axis=1),
                     _N_EXPERTS - 1)
    nused = (ends[-1] // tm).reshape(1)
    tail = ends[-1] + jnp.arange(_N_EXPERTS, dtype=jnp.int32) * tm
    zstart = jnp.concatenate([jnp.maximum(ends - tm, 0), jnp.minimum(tail, (ntiles - 1) * tm)])
    zvalid = jnp.concatenate([padded > 0, tail < ntiles * tm]).astype(jnp.int32)
    dest2 = dest.reshape(tokens, 2)
    return ntiles * tm, te, nused, dest2[:, 0], dest2[:, 1], zstart, zvalid


def _moe_ffn_ln(x, x16, w_router, w_up, w_down, ln_g, ln_b, *, alpha, tm):
    del x16
    tokens = x.shape[0]
    tc = min(256, tokens)
    route = _router(x, w_router, tm=tm)
    rows, te, nused, d0, d1, zstart, zvalid = _routing_plan(route, tm=tm)
    d0 = d0.reshape(tokens // tc, 1, tc)
    d1 = d1.reshape(tokens // tc, 1, tc)
    xs = _dispatch_rows(x, d0, d1, zstart, zvalid, rows=rows, tt=tc, tm=tm)
    h = _swiglu_up(xs, w_up, te, nused, tm=tm, tn=512)
    ys = _expert_down(h, w_down, te, nused, tm=tm, tn=512)
    return _combine_ln(ys, d0, d1, route, x, ln_g, ln_b, alpha=alpha, tm=tc)


def _dense_ffn_ln(x, x16, w_up, w_down, ln_g, ln_b, *, alpha, tm):
    tokens = x.shape[0]
    d_ff = w_up.shape[1] // 2
    te = jnp.zeros((tokens // tm,), jnp.int32)
    nused = jnp.full((1,), tokens // tm, jnp.int32)
    tn = d_ff // 2 if (d_ff // 2) % _LANES == 0 else d_ff
    h = _swiglu_up(x16, w_up[None], te, nused, tm=tm, tn=tn)
    return _dense_down_ln(h, w_down.astype(_BF16), x, ln_g, ln_b, alpha=alpha, tm=tm)


def _mixer_ln(x, x16, w_in, fox_b_f, gate_b, gm_w_s, gm_b_s, gm_ln_g, gm_ln_b, w_branch, w_out,
              ln_g, ln_b, *, bsz, seq, alpha, tm):
    d_model = x.shape[1]
    n_r = 2 * _RET_HEADS * _RET_DK + 2 * _RET_HEADS * _RET_DV
    n_f = 3 * _FOX_HEADS * _FOX_DH
    n_fl = _FOX_HEADS
    w_r = w_in[:, :n_r].astype(_BF16)
    w_f = w_in[:, n_r:n_r + n_f].astype(_BF16)
    w_fl = jnp.zeros((d_model, _LANES), _BF16).at[:, :n_fl].set(
        w_in[:, n_r + n_f:n_r + n_f + n_fl].astype(_BF16))
    w_c = w_in[:, n_r + n_f + n_fl:].astype(_BF16)
    proj_r = _matmul(x16, w_r, tm=tm, tn=n_r, out_dtype=_F32, name="in_proj_r")
    proj_f = _matmul(x16, w_f, tm=tm, tn=n_f, out_dtype=_BF16, name="in_proj_f")
    fl = _matmul(x16, w_fl, tm=tm, tn=_LANES, out_dtype=_F32, name="in_proj_fl")
    proj_c = _matmul(x16, w_c, tm=tm, tn=2048, out_dtype=_F32, name="in_proj_c")

    y_a = _retention(proj_r, bsz, seq)
    bias = jnp.zeros((1, _LANES), _F32).at[0, :n_fl].set(fox_b_f)
    qa, ka, va = _fox_prep(proj_f, fl, bias, bsz, seq)
    y_b = _fox(qa, ka, va, bsz, seq)
    y_c = _gmlp(proj_c, gm_w_s, gm_b_s, gm_ln_g, gm_ln_b, tr=tm)
    return _merge(y_a, y_b, y_c, proj_c, gate_b, w_branch, w_out, x, ln_g, ln_b, alpha=alpha, tm=tm)


def kernel(x, w_in, fox_b_f, gate_b, gm_w_s, gm_b_s, gm_ln_g, gm_ln_b, w_branch, w_out, ln_g, ln_b,
           dense_w_up, dense_w_down, moe_router, moe_w_up, moe_w_down):
    bsz, seq, d_model = x.shape
    depth = w_in.shape[0]
    alpha = (2 * depth) ** 0.25
    tokens = bsz * seq
    tm = min(_TM, tokens)
    xf = x.reshape(tokens, d_model)
    x16 = xf.astype(_BF16)
    for l in range(depth):
        xf, x16 = _mixer_ln(xf, x16, w_in[l], fox_b_f[l], gate_b[l], gm_w_s[l], gm_b_s[l],
                            gm_ln_g[l], gm_ln_b[l], w_branch[l], w_out[l], ln_g[l, 0], ln_b[l, 0],
                            bsz=bsz, seq=seq, alpha=alpha, tm=tm)
        if l % 2 == 0:
            xf, x16 = _dense_ffn_ln(xf, x16, dense_w_up[l // 2], dense_w_down[l // 2],
                                    ln_g[l, 1], ln_b[l, 1], alpha=alpha, tm=tm)
        else:
            xf, x16 = _moe_ffn_ln(xf, x16, moe_router[l // 2], moe_w_up[l // 2], moe_w_down[l // 2],
                                  ln_g[l, 1], ln_b[l, 1], alpha=alpha, tm=tm)
    return xf.reshape(bsz, seq, d_model)
```

```python
import functools
import math

import jax
import jax.numpy as jnp
import numpy as np
from jax import lax
from jax.experimental import pallas as pl
from jax.experimental.pallas import tpu as pltpu

_BF16 = jnp.bfloat16
_F32 = jnp.float32

_BLOCK = 128
_RET_HEADS, _RET_DK, _RET_DV = 4, 64, 128
_FOX_HEADS, _FOX_DH = 8, 64
_GM_GROUPS, _GM_DG = 4, 128
_N_EXPERTS = 8
_ROPE_BASE = 10000.0
_LN_EPS = 1e-5
_GN_EPS = 1e-6

_LANES = 128
_VMEM_PHYSICAL_BYTES = 64 * 1024 * 1024
_VMEM_LIMIT_BYTES = (_VMEM_PHYSICAL_BYTES * 7) // 8

_TM = 512
_TQ = 512
_TR_RET = 512


def _params(n_axes):
    return pltpu.CompilerParams(
        dimension_semantics=("arbitrary",) * n_axes,
        vmem_limit_bytes=_VMEM_LIMIT_BYTES,
    )


def _layer_norm(x, g, b, eps):
    mu = jnp.mean(x, axis=-1, keepdims=True)
    xc = x - mu
    var = jnp.mean(xc * xc, axis=-1, keepdims=True)
    return xc * lax.rsqrt(var + eps) * g + b


def _silu(x):
    return x * (1.0 / (1.0 + jnp.exp(-x)))


def _gelu_tanh(x):
    c = math.sqrt(2.0 / math.pi)
    return x * (0.5 * (1.0 + jnp.tanh(c * (x + 0.044715 * (x * x * x)))))


def _dot(a, b):
    return jnp.dot(a, b, preferred_element_type=_F32)


def _dot_nt(a, b):
    return lax.dot_general(a, b, (((1,), (1,)), ((), ())), preferred_element_type=_F32)


def _dot_tn(a, b):
    return lax.dot_general(a, b, (((0,), (0,)), ((), ())), preferred_element_type=_F32)


def _matmul_kernel(x_ref, w_ref, o_ref):
    o_ref[...] = _dot(x_ref[...], w_ref[...]).astype(o_ref.dtype)


def _matmul(x, w, *, tm, tn, out_dtype, name):
    m, k = x.shape
    n = w.shape[1]
    return pl.pallas_call(
        _matmul_kernel,
        out_shape=jax.ShapeDtypeStruct((m, n), out_dtype),
        grid=(n // tn, m // tm),
        in_specs=[pl.BlockSpec((tm, k), lambda j, i: (i, 0)),
                  pl.BlockSpec((k, tn), lambda j, i: (0, j))],
        out_specs=pl.BlockSpec((tm, tn), lambda j, i: (i, j)),
        compiler_params=_params(2),
        name=name,
    )(x, w)


def _retention_kernel(q_ref, k_ref, v_ref, g_ref, cos_ref, sin_ref, qdec_ref, kdec_ref, din_ref,
                      y_ref, state_sc, *, chunk_dec):
    c = pl.program_id(1)

    @pl.when(c == 0)
    def _():
        state_sc[...] = jnp.zeros_like(state_sc)

    width = _RET_HEADS * _RET_DK
    rows_total = q_ref.shape[0]
    lane_all = lax.broadcasted_iota(jnp.int32, (rows_total, width), 1)
    first_half = (lane_all % _RET_DK) < (_RET_DK // 2)

    def rotary(x):
        partner = jnp.where(first_half,
                            pltpu.roll(x, width - _RET_DK // 2, 1),
                            pltpu.roll(x, _RET_DK // 2, 1))
        return x * cos_ref[...] + partner * sin_ref[...]

    q_all = rotary(q_ref[...])
    k_all = rotary(k_ref[...]) * (_RET_DK ** -0.5)
    lane = lax.broadcasted_iota(jnp.int32, (_BLOCK, width), 1)
    for cc in range(rows_total // _BLOCK):
        rows = slice(cc * _BLOCK, (cc + 1) * _BLOCK)
        q = q_all[rows]
        k = k_all[rows]
        k_bf = k.astype(_BF16)
        q_cross = q * qdec_ref[...]
        k_state = k * kdec_ref[...]
        for h in range(_RET_HEADS):
            cols = slice(h * _RET_DV, (h + 1) * _RET_DV)
            in_head = (lane // _RET_DK) == h
            vh = v_ref[rows, cols].astype(_BF16)
            qh = jnp.where(in_head, q, 0.0).astype(_BF16)
            scores = _dot_nt(qh, k_bf) * din_ref[h]
            inner = _dot(scores.astype(_BF16), vh)
            state = state_sc[h]
            cross = _dot(jnp.where(in_head, q_cross, 0.0).astype(_BF16), state.astype(_BF16))
            state_sc[h] = state * chunk_dec[h] + _dot_tn(
                jnp.where(in_head, k_state, 0.0).astype(_BF16), vh)
            out = inner + cross
            mu = jnp.mean(out, axis=-1, keepdims=True)
            oc = out - mu
            var = jnp.mean(oc * oc, axis=-1, keepdims=True)
            normed = oc * lax.rsqrt(var + _GN_EPS)
            y_ref[rows, cols] = (_silu(g_ref[rows, cols]) * normed).astype(y_ref.dtype)


def _retention_tables(seq):
    half = _RET_DK // 2
    inv_freq = _ROPE_BASE ** (-jnp.arange(half, dtype=_F32) / half)
    ang = jnp.arange(seq, dtype=jnp.int32).astype(_F32)[:, None] * inv_freq[None, :]
    cos, sin = jnp.cos(ang), jnp.sin(ang)
    cos_t = jnp.tile(jnp.concatenate([cos, cos], axis=1), (1, _RET_HEADS))
    sin_t = jnp.tile(jnp.concatenate([-sin, sin], axis=1), (1, _RET_HEADS))
    log_gamma = jnp.log1p(-jnp.exp2(-5.0 - jnp.arange(_RET_HEADS, dtype=_F32)))
    idx = jnp.arange(_BLOCK, dtype=_F32)
    rel = idx[:, None] - idx[None, :]
    causal = rel >= 0
    decay_in = jnp.where(causal[None],
                         jnp.exp(log_gamma[:, None, None] * jnp.where(causal, rel, 0.0)[None]), 0.0)
    q_dec = jnp.exp(log_gamma[:, None] * (idx + 1.0))
    k_dec = jnp.exp(log_gamma[:, None] * (_BLOCK - 1.0 - idx))
    q_dec_t = jnp.repeat(q_dec.T, _RET_DK, axis=1)
    k_dec_t = jnp.repeat(k_dec.T, _RET_DK, axis=1)
    return cos_t, sin_t, q_dec_t, k_dec_t, decay_in


def _retention(proj_a, bsz, seq):
    tokens = bsz * seq
    tr = min(_TR_RET, seq)
    nstep = seq // tr
    cos_t, sin_t, q_dec_t, k_dec_t, decay_in = _retention_tables(seq)
    chunk_dec = tuple(float((1.0 - 2.0 ** (-5.0 - h)) ** _BLOCK) for h in range(_RET_HEADS))
    qk = _RET_HEADS * _RET_DK
    wv = _RET_HEADS * _RET_DV
    row = lambda b, c: b * nstep + c
    return pl.pallas_call(
        functools.partial(_retention_kernel, chunk_dec=chunk_dec),
        out_shape=jax.ShapeDtypeStruct((tokens, wv), _BF16),
        grid=(bsz, nstep),
        in_specs=[
            pl.BlockSpec((tr, qk), lambda b, c: (row(b, c), 0)),
            pl.BlockSpec((tr, qk), lambda b, c: (row(b, c), 1)),
            pl.BlockSpec((tr, wv), lambda b, c: (row(b, c), 1)),
            pl.BlockSpec((tr, wv), lambda b, c: (row(b, c), 2)),
            pl.BlockSpec((tr, qk), lambda b, c: (c, 0)),
            pl.BlockSpec((tr, qk), lambda b, c: (c, 0)),
            pl.BlockSpec((_BLOCK, qk), lambda b, c: (0, 0)),
            pl.BlockSpec((_BLOCK, qk), lambda b, c: (0, 0)),
            pl.BlockSpec((_RET_HEADS, _BLOCK, _BLOCK), lambda b, c: (0, 0, 0)),
        ],
        out_specs=pl.BlockSpec((tr, wv), lambda b, c: (row(b, c), 0)),
        scratch_shapes=[pltpu.VMEM((_RET_HEADS, qk, _RET_DV), _F32)],
        compiler_params=_params(2),
        name="retention",
    )(proj_a, proj_a, proj_a, proj_a, cos_t, sin_t, q_dec_t, k_dec_t, decay_in)


_C_PARTS = 3
_TS_PREP = 512


def _fox_selectors():
    half = _LANES // 2
    one_row = _C_PARTS * _FOX_HEADS
    sel = np.zeros((_FOX_HEADS, _LANES, 2 * _LANES), np.float32)
    for h in range(_FOX_HEADS):
        base = half if h % 2 == 0 else 0
        for n in range(_C_PARTS):
            sel[h, n * _FOX_HEADS + h, base + n] = 1.0
            sel[h, one_row, base + _C_PARTS + n] = 1.0
            sel[h, one_row, _LANES + base + n] = 1.0
            sel[h, n * _FOX_HEADS + h, _LANES + base + _C_PARTS + n] = -1.0
    return jnp.asarray(sel, _BF16)


def _fox_prep_kernel(fq_ref, fk_ref, fv_ref, fl_ref, b_ref, sel_ref, qa_ref, ka_ref, va_ref, carry_sc,
                     *, nblk):
    @pl.when(pl.program_id(1) == 0)
    def _():
        carry_sc[...] = jnp.zeros_like(carry_sc)

    r = lax.broadcasted_iota(jnp.int32, (_BLOCK, _LANES), 0)
    lane = lax.broadcasted_iota(jnp.int32, (_BLOCK, _LANES), 1)
    tri = (r >= lane).astype(_F32)
    half = _LANES // 2
    q_scale = jnp.asarray(_FOX_DH ** -0.5, _BF16)
    carry = carry_sc[...]
    packed = []
    for blk in range(nblk):
        rows = slice(blk * _BLOCK, (blk + 1) * _BLOCK)
        z = fl_ref[rows, :] + b_ref[...]
        log_f = jnp.minimum(z, 0.0) - jnp.log1p(jnp.exp(-jnp.abs(z)))
        cum = jnp.dot(tri, log_f, preferred_element_type=_F32,
                      precision=lax.Precision.HIGHEST) + carry
        carry = cum[_BLOCK - 1:_BLOCK, :]
        row = jnp.where(lane == _C_PARTS * _FOX_HEADS, 1.0, 0.0)
        rest = cum
        for n in range(_C_PARTS):
            piece = rest.astype(_BF16).astype(_F32)
            rest = rest - piece
            shifted = piece if n == 0 else pltpu.roll(piece, n * _FOX_HEADS, 1)
            row = jnp.where((lane >= n * _FOX_HEADS) & (lane < (n + 1) * _FOX_HEADS), shifted, row)
        packed.append(row.astype(_BF16))
    carry_sc[...] = carry
    packed = jnp.concatenate(packed, axis=0)
    lane_all = lax.broadcasted_iota(jnp.int32, (nblk * _BLOCK, _LANES), 1)
    for h in range(_FOX_HEADS):
        pair = slice((h // 2) * _LANES, (h // 2 + 1) * _LANES)
        in_head = (lane_all < half) if h % 2 == 0 else (lane_all >= half)
        extra = _dot(packed, sel_ref[h]).astype(_BF16)
        qa_ref[h] = jnp.where(in_head, fq_ref[:, pair] * q_scale, extra[:, :_LANES])
        ka_ref[h] = jnp.where(in_head, fk_ref[:, pair], extra[:, _LANES:])
        va_ref[h] = jnp.where(in_head, fv_ref[:, pair], jnp.ones((), _BF16))


def _fox_prep(proj_f, proj_r, fl_block, bias, bsz, seq):
    ts = min(_TS_PREP, seq)
    ns = seq // ts
    width = _FOX_HEADS * _FOX_DH
    sel = _fox_selectors()
    aug = jax.ShapeDtypeStruct((bsz, _FOX_HEADS, seq, _LANES), _BF16)
    aug_spec = pl.BlockSpec((None, _FOX_HEADS, ts, _LANES), lambda b, s: (b, 0, s, 0))
    return pl.pallas_call(
        functools.partial(_fox_prep_kernel, nblk=ts // _BLOCK),
        out_shape=(aug, aug, aug),
        grid=(bsz, ns),
        in_specs=[pl.BlockSpec((ts, width), lambda b, s: (b * ns + s, 0)),
                  pl.BlockSpec((ts, width), lambda b, s: (b * ns + s, 1)),
                  pl.BlockSpec((ts, width), lambda b, s: (b * ns + s, 2)),
                  pl.BlockSpec((ts, _LANES), lambda b, s: (b * ns + s, fl_block)),
                  pl.BlockSpec((1, _LANES), lambda b, s: (0, 0)),
                  pl.BlockSpec(sel.shape, lambda b, s: (0, 0, 0))],
        out_specs=(aug_spec, aug_spec, aug_spec),
        scratch_shapes=[pltpu.VMEM((1, _LANES), _F32)],
        compiler_params=_params(2),
        name="fox_prep",
    )(proj_f, proj_f, proj_f, proj_r, bias, sel)


def _fox_kernel(qa_ref, ka_ref, va_ref, o_ref, m0_sc, m1_sc, acc0_sc, acc1_sc, *, tq):
    i = pl.program_id(2)
    m_scs = (m0_sc, m1_sc)
    acc_scs = (acc0_sc, acc1_sc)
    for s in range(2):
        m_scs[s][...] = jnp.full_like(m_scs[s], -jnp.inf)
        acc_scs[s][...] = jnp.zeros_like(acc_scs[s])

    row_id = lax.broadcasted_iota(jnp.int32, (tq, tq), 0)
    col_id = lax.broadcasted_iota(jnp.int32, (tq, tq), 1)
    keep = row_id >= col_id

    def block(j, masked):
        start = pl.multiple_of(j * tq, tq)
        logits = [_dot_nt(qa_ref[s], ka_ref[s, pl.ds(start, tq), :]) for s in range(2)]
        pexp, scale, m_new = [], [], []
        for s in range(2):
            lg = jnp.where(keep, logits[s], -jnp.inf) if masked else logits[s]
            m_old = m_scs[s][...]
            m_new.append(jnp.maximum(m_old, jnp.max(lg, axis=1, keepdims=True)))
            pexp.append(jnp.exp(lg - jnp.concatenate([m_new[s]] * (tq // _LANES), axis=1)))
            scale.append(jnp.exp(m_old - m_new[s]))
        pv = [_dot(pexp[s].astype(_BF16), va_ref[s, pl.ds(start, tq), :]) for s in range(2)]
        for s in range(2):
            acc_scs[s][...] = scale[s] * acc_scs[s][...] + pv[s]
            m_scs[s][...] = m_new[s]

    def body(j, carry):
        block(j, False)
        return carry

    lax.fori_loop(0, i, body, 0)
    block(i, True)
    half = _LANES // 2
    acc0 = acc0_sc[...]
    acc1 = acc1_sc[...]
    low = lax.broadcasted_iota(jnp.int32, (tq, _LANES), 1) < half
    out = jnp.where(low, acc0 / acc0[:, half:half + 1], acc1 / acc1[:, 0:1])
    o_ref[...] = out.astype(o_ref.dtype)


def _fox(qa, ka, va, bsz, seq):
    tq = min(_TQ, seq)
    nq = seq // tq
    npair = _FOX_HEADS // 2
    return pl.pallas_call(
        functools.partial(_fox_kernel, tq=tq),
        out_shape=jax.ShapeDtypeStruct((bsz * seq, _FOX_HEADS * _FOX_DH), _BF16),
        grid=(bsz, npair, nq),
        in_specs=[
            pl.BlockSpec((None, 2, tq, _LANES), lambda b, p, i: (b, p, i, 0)),
            pl.BlockSpec((None, 2, seq, _LANES), lambda b, p, i: (b, p, 0, 0)),
            pl.BlockSpec((None, 2, seq, _LANES), lambda b, p, i: (b, p, 0, 0)),
        ],
        out_specs=pl.BlockSpec((tq, _LANES), lambda b, p, i: (b * nq + i, p)),
        scratch_shapes=[pltpu.VMEM((tq, _LANES), _F32), pltpu.VMEM((tq, _LANES), _F32),
                        pltpu.VMEM((tq, _LANES), _F32), pltpu.VMEM((tq, _LANES), _F32)],
        compiler_params=_params(3),
        name="fox_attention",
    )(qa, ka, va)


def _gmlp_kernel(gu_ref, gv_ref, w_ref, bst_ref, lng_ref, lnb_ref, y_ref, *, nchunk):
    u = _gelu_tanh(gu_ref[...])
    v = _layer_norm(_gelu_tanh(gv_ref[...]), lng_ref[...], lnb_ref[...], _LN_EPS).astype(_BF16)
    r = lax.broadcasted_iota(jnp.int32, (_BLOCK, _BLOCK), 0)
    c = lax.broadcasted_iota(jnp.int32, (_BLOCK, _BLOCK), 1)
    causal = r >= c
    for g in range(_GM_GROUPS):
        cols = slice(g * _GM_DG, (g + 1) * _GM_DG)
        wg = jnp.where(causal, w_ref[g], 0.0).astype(_BF16)
        bias = bst_ref[:, g:g + 1]
        for ch in range(nchunk):
            rows = slice(ch * _BLOCK, (ch + 1) * _BLOCK)
            sp = _dot(wg, v[rows, cols]) + bias
            y_ref[rows, cols] = (u[rows, cols] * sp).astype(y_ref.dtype)


def _gmlp(proj_c, w_s, b_s, ln_g, ln_b, *, tr):
    tokens = proj_c.shape[0]
    width = _GM_GROUPS * _GM_DG
    return pl.pallas_call(
        functools.partial(_gmlp_kernel, nchunk=tr // _BLOCK),
        out_shape=jax.ShapeDtypeStruct((tokens, width), _BF16),
        grid=(tokens // tr,),
        in_specs=[
            pl.BlockSpec((tr, width), lambda i: (i, 0)),
            pl.BlockSpec((tr, width), lambda i: (i, 1)),
            pl.BlockSpec((_GM_GROUPS, _BLOCK, _BLOCK), lambda i: (0, 0, 0)),
            pl.BlockSpec((_BLOCK, _GM_GROUPS), lambda i: (0, 0)),
            pl.BlockSpec((1, width), lambda i: (0, 0)),
            pl.BlockSpec((1, width), lambda i: (0, 0)),
        ],
        out_specs=pl.BlockSpec((tr, width), lambda i: (i, 0)),
        compiler_params=_params(1),
        name="gmlp",
    )(proj_c, proj_c, w_s, b_s.T, ln_g.reshape(1, width), ln_b.reshape(1, width))


def _merge_kernel(ya_ref, yb_ref, yc_ref, ga_ref, gb_ref, gc_ref, gbias_ref, wbr_ref, wout_ref,
                  x_ref, lng_ref, lnb_ref, xo_ref, xo16_ref, *, alpha, d_model):
    merged = None
    for n, (y_ref, gl_ref) in enumerate(((ya_ref, ga_ref), (yb_ref, gb_ref), (yc_ref, gc_ref))):
        z = gl_ref[...] + gbias_ref[:, n * d_model:(n + 1) * d_model]
        gate = 1.0 / (1.0 + jnp.exp(-z))
        term = gate * _dot(y_ref[...], wbr_ref[n])
        merged = term if merged is None else merged + term
    h = _dot(merged.astype(_BF16), wout_ref[...])
    out = _layer_norm(alpha * x_ref[...] + h, lng_ref[...], lnb_ref[...], _LN_EPS)
    xo_ref[...] = out
    xo16_ref[...] = out.astype(_BF16)


def _merge(y_a, y_b, y_c, proj_c, gate_b, w_branch, w_out, x, ln_g, ln_b, *, alpha, tm):
    tokens, d_model = x.shape
    bw = y_a.shape[1]
    gl0 = (_GM_GROUPS * _GM_DG * 2) // d_model
    row = lambda i: (i, 0)
    const2 = lambda i: (0, 0)
    return pl.pallas_call(
        functools.partial(_merge_kernel, alpha=alpha, d_model=d_model),
        out_shape=(jax.ShapeDtypeStruct((tokens, d_model), _F32),
                   jax.ShapeDtypeStruct((tokens, d_model), _BF16)),
        grid=(tokens // tm,),
        in_specs=[
            pl.BlockSpec((tm, bw), row), pl.BlockSpec((tm, bw), row), pl.BlockSpec((tm, bw), row),
            pl.BlockSpec((tm, d_model), lambda i: (i, gl0)),
            pl.BlockSpec((tm, d_model), lambda i: (i, gl0 + 1)),
            pl.BlockSpec((tm, d_model), lambda i: (i, gl0 + 2)),
            pl.BlockSpec((1, 3 * d_model), const2),
            pl.BlockSpec((3, bw, d_model), lambda i: (0, 0, 0)),
            pl.BlockSpec((d_model, d_model), const2),
            pl.BlockSpec((tm, d_model), row),
            pl.BlockSpec((1, d_model), const2), pl.BlockSpec((1, d_model), const2),
        ],
        out_specs=(pl.BlockSpec((tm, d_model), row), pl.BlockSpec((tm, d_model), row)),
        compiler_params=_params(1),
        name="merge",
    )(y_a, y_b, y_c, proj_c, proj_c, proj_c, gate_b.reshape(1, -1), w_branch.astype(_BF16),
      w_out.astype(_BF16), x, ln_g.reshape(1, -1), ln_b.reshape(1, -1))


def _weights_changed(te_ref, i):
    return jnp.logical_or(i == 0, te_ref[i] != te_ref[jnp.maximum(i - 1, 0)])


def _up_kernel(te_ref, nused_ref, x_ref, wa_ref, wb_ref, h_ref, wa_sc, wb_sc):
    i = pl.program_id(1)

    @pl.when(_weights_changed(te_ref, i))
    def _():
        wa_sc[...] = wa_ref[...].astype(_BF16)
        wb_sc[...] = wb_ref[...].astype(_BF16)

    @pl.when(i < nused_ref[0])
    def _():
        x = x_ref[...].astype(_BF16)
        a = _dot(x, wa_sc[...])
        b = _dot(x, wb_sc[...])
        h_ref[...] = (_silu(a) * b).astype(h_ref.dtype)

    @pl.when(i >= nused_ref[0])
    def _():
        h_ref[...] = jnp.zeros_like(h_ref)


def _swiglu_up(x16, w_up, te, nused, *, tm, tn):
    rows, k = x16.shape
    d_ff = w_up.shape[2] // 2
    nb = d_ff // tn
    grid_spec = pltpu.PrefetchScalarGridSpec(
        num_scalar_prefetch=2,
        grid=(nb, rows // tm),
        in_specs=[
            pl.BlockSpec((tm, k), lambda j, i, te, nu: (i, 0)),
            pl.BlockSpec((None, k, tn), lambda j, i, te, nu: (te[i], 0, j)),
            pl.BlockSpec((None, k, tn), lambda j, i, te, nu: (te[i], 0, j + nb)),
        ],
        out_specs=pl.BlockSpec((tm, tn), lambda j, i, te, nu: (i, j)),
        scratch_shapes=[pltpu.VMEM((k, tn), _BF16), pltpu.VMEM((k, tn), _BF16)],
    )
    return pl.pallas_call(
        _up_kernel,
        out_shape=jax.ShapeDtypeStruct((rows, d_ff), _BF16),
        grid_spec=grid_spec,
        compiler_params=_params(2),
        name="swiglu_up",
    )(te, nused, x16, w_up, w_up)


def _down_kernel(te_ref, nused_ref, h_ref, w_ref, y_ref, w_sc):
    i = pl.program_id(1)

    @pl.when(_weights_changed(te_ref, i))
    def _():
        w_sc[...] = w_ref[...].astype(_BF16)

    @pl.when(i < nused_ref[0])
    def _():
        y_ref[...] = _dot(h_ref[...], w_sc[...])

    @pl.when(i >= nused_ref[0])
    def _():
        y_ref[...] = jnp.zeros_like(y_ref)


def _expert_down(h16, w_down, te, nused, *, tm, tn):
    rows, k = h16.shape
    n = w_down.shape[2]
    grid_spec = pltpu.PrefetchScalarGridSpec(
        num_scalar_prefetch=2,
        grid=(n // tn, rows // tm),
        in_specs=[
            pl.BlockSpec((tm, k), lambda j, i, te, nu: (i, 0)),
            pl.BlockSpec((None, k, tn), lambda j, i, te, nu: (te[i], 0, j)),
        ],
        out_specs=pl.BlockSpec((tm, tn), lambda j, i, te, nu: (i, j)),
        scratch_shapes=[pltpu.VMEM((k, tn), _BF16)],
    )
    return pl.pallas_call(
        _down_kernel,
        out_shape=jax.ShapeDtypeStruct((rows, n), _F32),
        grid_spec=grid_spec,
        compiler_params=_params(2),
        name="expert_down",
    )(te, nused, h16, w_down)


def _down_ln_kernel(h_ref, w_ref, x_ref, lng_ref, lnb_ref, xo_ref, xo16_ref, *, alpha):
    f = _dot(h_ref[...], w_ref[...])
    out = _layer_norm(alpha * x_ref[...] + f, lng_ref[...], lnb_ref[...], _LN_EPS)
    xo_ref[...] = out
    xo16_ref[...] = out.astype(_BF16)


def _dense_down_ln(h16, w_down16, x, ln_g, ln_b, *, alpha, tm):
    tokens, d_model = x.shape
    k = h16.shape[1]
    row = lambda i: (i, 0)
    const2 = lambda i: (0, 0)
    return pl.pallas_call(
        functools.partial(_down_ln_kernel, alpha=alpha),
        out_shape=(jax.ShapeDtypeStruct((tokens, d_model), _F32),
                   jax.ShapeDtypeStruct((tokens, d_model), _BF16)),
        grid=(tokens // tm,),
        in_specs=[pl.BlockSpec((tm, k), row), pl.BlockSpec((k, d_model), const2),
                  pl.BlockSpec((tm, d_model), row),
                  pl.BlockSpec((1, d_model), const2), pl.BlockSpec((1, d_model), const2)],
        out_specs=(pl.BlockSpec((tm, d_model), row), pl.BlockSpec((tm, d_model), row)),
        compiler_params=_params(1),
        name="dense_down_ln",
    )(h16, w_down16, x, ln_g.reshape(1, -1), ln_b.reshape(1, -1))


def _router_kernel(x_ref, wr_ref, o_ref):
    logits = jnp.dot(x_ref[...], wr_ref[...], preferred_element_type=_F32,
                     precision=lax.Precision.HIGHEST)
    lane = lax.broadcasted_iota(jnp.int32, logits.shape, 1).astype(_F32)
    lg = jnp.where(lane < _N_EXPERTS, logits, -jnp.inf)
    m1 = jnp.max(lg, axis=1, keepdims=True)
    i1 = jnp.min(jnp.where(lg == m1, lane, float(_LANES)), axis=1, keepdims=True)
    lg2 = jnp.where(lane == i1, -jnp.inf, lg)
    m2 = jnp.max(lg2, axis=1, keepdims=True)
    i2 = jnp.min(jnp.where(lg2 == m2, lane, float(_LANES)), axis=1, keepdims=True)
    e2 = jnp.exp(m2 - m1)
    den = 1.0 + e2
    w1 = 1.0 / den
    w2 = e2 / den
    o_ref[...] = jnp.where(lane == 0, i1, jnp.where(lane == 1, i2,
                           jnp.where(lane == 2, w1, jnp.where(lane == 3, w2, 0.0))))


def _router(x, w_router, *, tm):
    tokens, d_model = x.shape
    wr = jnp.zeros((d_model, _LANES), _F32).at[:, :_N_EXPERTS].set(w_router)
    return pl.pallas_call(
        _router_kernel,
        out_shape=jax.ShapeDtypeStruct((tokens, _LANES), _F32),
        grid=(tokens // tm,),
        in_specs=[pl.BlockSpec((tm, d_model), lambda i: (i, 0)),
                  pl.BlockSpec((d_model, _LANES), lambda i: (0, 0))],
        out_specs=pl.BlockSpec((tm, _LANES), lambda i: (i, 0)),
        compiler_params=_params(1),
        name="router",
    )(x, wr)


_ROW_UNROLL = 8


def _dispatch_kernel(zstart_ref, zvalid_ref, d0_ref, d1_ref, x_ref, xs_hbm, zero_sc, stage_sc, sem, zsem,
                     *, tt, tm, nsteps):
    @pl.when(pl.program_id(0) == 0)
    def _():
        zero_sc[...] = jnp.zeros_like(zero_sc)

        def zero_copy(n):
            start = pl.multiple_of(zstart_ref[n], tm)
            return pltpu.make_async_copy(zero_sc, xs_hbm.at[pl.ds(start, tm), :], zsem)

        for n in range(2 * _N_EXPERTS):
            @pl.when(zvalid_ref[n] > 0)
            def _():
                zero_copy(n).start()
        for n in range(2 * _N_EXPERTS):
            @pl.when(zvalid_ref[n] > 0)
            def _():
                zero_copy(n).wait()

    i = pl.program_id(0)
    slot = i % 2
    stage_sc[slot] = x_ref[...]

    def issue(g, carry):
        for u in range(_ROW_UNROLL):
            r = g * _ROW_UNROLL + u
            for d_ref in (d0_ref, d1_ref):
                pltpu.make_async_copy(stage_sc.at[slot, pl.ds(r, 1), :],
                                      xs_hbm.at[pl.ds(d_ref[0, 0, r], 1), :],
                                      sem.at[slot]).start(priority=u % 2)
        return carry

    def drain(s):
        for _ in range(2):
            pltpu.make_async_copy(stage_sc.at[s], xs_hbm.at[pl.ds(0, tt), :], sem.at[s]).wait()

    lax.fori_loop(0, tt // _ROW_UNROLL, issue, 0)

    @pl.when(i > 0)
    def _():
        drain(1 - slot)

    @pl.when(i == nsteps - 1)
    def _():
        drain(slot)


def _dispatch_rows(x, d0, d1, zstart, zvalid, *, rows, tt, tm):
    tokens, d_model = x.shape
    smem = lambda: pl.BlockSpec((1, 1, tt), lambda i, zs, zv: (i, 0, 0), memory_space=pltpu.SMEM)
    grid_spec = pltpu.PrefetchScalarGridSpec(
        num_scalar_prefetch=2,
        grid=(tokens // tt,),
        in_specs=[smem(), smem(), pl.BlockSpec((tt, d_model), lambda i, zs, zv: (i, 0))],
        out_specs=pl.BlockSpec(memory_space=pl.ANY),
        scratch_shapes=[pltpu.VMEM((tm, d_model), _F32), pltpu.VMEM((2, tt, d_model), _F32),
                        pltpu.SemaphoreType.DMA((2,)), pltpu.SemaphoreType.DMA],
    )
    return pl.pallas_call(
        functools.partial(_dispatch_kernel, tt=tt, tm=tm, nsteps=tokens // tt),
        out_shape=jax.ShapeDtypeStruct((rows, d_model), _F32),
        grid_spec=grid_spec,
        compiler_params=_params(1),
        name="moe_dispatch",
    )(zstart, zvalid, d0, d1, x)


def _combine_kernel(d0_ref, d1_ref, n0_ref, n1_ref, y_hbm, route_ref, x_ref, lng_ref, lnb_ref,
                    xo_ref, xo16_ref, buf, sem, *, alpha, tm, nsteps):
    i = pl.program_id(0)
    slot = i % 2

    def gather(dst_slot, a_ref, b_ref):
        def issue(g, carry):
            for u in range(_ROW_UNROLL):
                r = g * _ROW_UNROLL + u
                for s, d_ref in enumerate((a_ref, b_ref)):
                    pltpu.make_async_copy(y_hbm.at[pl.ds(d_ref[0, 0, r], 1), :],
                                          buf.at[dst_slot, s, pl.ds(r, 1), :],
                                          sem.at[dst_slot]).start(priority=u % 2)
            return carry

        lax.fori_loop(0, tm // _ROW_UNROLL, issue, 0)

    @pl.when(i == 0)
    def _():
        gather(0, d0_ref, d1_ref)

    @pl.when(i + 1 < nsteps)
    def _():
        gather(1 - slot, n0_ref, n1_ref)

    for s in range(2):
        pltpu.make_async_copy(y_hbm.at[pl.ds(0, tm), :], buf.at[slot, s], sem.at[slot]).wait()
    route = route_ref[...]
    f = route[:, 2:3] * buf[slot, 0] + route[:, 3:4] * buf[slot, 1]
    out = _layer_norm(alpha * x_ref[...] + f, lng_ref[...], lnb_ref[...], _LN_EPS)
    xo_ref[...] = out
    xo16_ref[...] = out.astype(_BF16)


def _combine_ln(ys, d0, d1, route, x, ln_g, ln_b, *, alpha, tm):
    tokens, d_model = x.shape
    nsteps = tokens // tm
    row = lambda i: (i, 0)
    const2 = lambda i: (0, 0)
    cur = lambda: pl.BlockSpec((1, 1, tm), lambda i: (i, 0, 0), memory_space=pltpu.SMEM)
    nxt = lambda: pl.BlockSpec((1, 1, tm), lambda i: (jnp.minimum(i + 1, nsteps - 1), 0, 0),
                               memory_space=pltpu.SMEM)
    return pl.pallas_call(
        functools.partial(_combine_kernel, alpha=alpha, tm=tm, nsteps=nsteps),
        out_shape=(jax.ShapeDtypeStruct((tokens, d_model), _F32),
                   jax.ShapeDtypeStruct((tokens, d_model), _BF16)),
        grid=(nsteps,),
        in_specs=[cur(), cur(), nxt(), nxt(), pl.BlockSpec(memory_space=pl.ANY),
                  pl.BlockSpec((tm, _LANES), row), pl.BlockSpec((tm, d_model), row),
                  pl.BlockSpec((1, d_model), const2), pl.BlockSpec((1, d_model), const2)],
        out_specs=(pl.BlockSpec((tm, d_model), row), pl.BlockSpec((tm, d_model), row)),
        scratch_shapes=[pltpu.VMEM((2, 2, tm, d_model), _F32), pltpu.SemaphoreType.DMA((2,))],
        compiler_params=_params(1),
        name="moe_combine_ln",
    )(d0, d1, d0, d1, ys, route, x, ln_g.reshape(1, -1), ln_b.reshape(1, -1))


def _routing_plan(route, *, tm):
    tokens = route.shape[0]
    pairs = 2 * tokens
    ntiles = pairs // tm + _N_EXPERTS
    flat_e = route[:, 0:2].astype(jnp.int32).reshape(pairs)
    onehot = (flat_e[:, None] == jnp.arange(_N_EXPERTS, dtype=jnp.int32)[None, :]).astype(jnp.int32)
    csum = jnp.cumsum(onehot, axis=0)
    counts = csum[-1]
    padded = ((counts + tm - 1) // tm) * tm
    ends = jnp.cumsum(padded)
    starts = ends - padded
    dest = jnp.sum(onehot * (starts[None, :] + csum - 1), axis=1)
    tile_start = jnp.arange(ntiles, dtype=jnp.int32) * tm
    te = jnp.minimum(jnp.sum((ends[None, :] <= tile_start[:, None]).astype(jnp.int32), axis=1),
                     _N_EXPERTS - 1)
    nused = (ends[-1] // tm).reshape(1)
    tail = ends[-1] + jnp.arange(_N_EXPERTS, dtype=jnp.int32) * tm
    zstart = jnp.concatenate([jnp.maximum(ends - tm, 0), jnp.minimum(tail, (ntiles - 1) * tm)])
    zvalid = jnp.concatenate([padded > 0, tail < ntiles * tm]).astype(jnp.int32)
    dest2 = dest.reshape(tokens, 2)
    return ntiles * tm, te, nused, dest2[:, 0], dest2[:, 1], zstart, zvalid


def _moe_ffn_ln(x, x16, w_router, w_up, w_down, ln_g, ln_b, *, alpha, tm):
    del x16
    tokens = x.shape[0]
    tc = min(256, tokens)
    route = _router(x, w_router, tm=tm)
    rows, te, nused, d0, d1, zstart, zvalid = _routing_plan(route, tm=tm)
    d0 = d0.reshape(tokens // tc, 1, tc)
    d1 = d1.reshape(tokens // tc, 1, tc)
    xs = _dispatch_rows(x, d0, d1, zstart, zvalid, rows=rows, tt=tc, tm=tm)
    h = _swiglu_up(xs, w_up, te, nused, tm=tm, tn=w_up.shape[2] // 4)
    ys = _expert_down(h, w_down, te, nused, tm=tm, tn=w_down.shape[2])
    return _combine_ln(ys, d0, d1, route, x, ln_g, ln_b, alpha=alpha, tm=tc)


def _dense_ffn_ln(x, x16, w_up, w_down, ln_g, ln_b, *, alpha, tm):
    tokens = x.shape[0]
    d_ff = w_up.shape[1] // 2
    te = jnp.zeros((tokens // tm,), jnp.int32)
    nused = jnp.full((1,), tokens // tm, jnp.int32)
    tn = d_ff // 2 if (d_ff // 2) % _LANES == 0 else d_ff
    h = _swiglu_up(x16, w_up[None], te, nused, tm=tm, tn=tn)
    return _dense_down_ln(h, w_down.astype(_BF16), x, ln_g, ln_b, alpha=alpha, tm=tm)


def _mixer_ln(x, x16, w_in, fox_b_f, gate_b, gm_w_s, gm_b_s, gm_ln_g, gm_ln_b, w_branch, w_out,
              ln_g, ln_b, *, bsz, seq, alpha, tm):
    d_model = x.shape[1]
    n_r = 2 * _RET_HEADS * _RET_DK + 2 * _RET_HEADS * _RET_DV
    n_f = 3 * _FOX_HEADS * _FOX_DH
    n_fl = _FOX_HEADS
    w_r = jnp.concatenate(
        [w_in[:, :n_r], w_in[:, n_r + n_f:n_r + n_f + n_fl],
         jnp.zeros((d_model, _LANES - n_fl), w_in.dtype)], axis=1).astype(_BF16)
    w_f = w_in[:, n_r:n_r + n_f].astype(_BF16)
    w_c = w_in[:, n_r + n_f + n_fl:].astype(_BF16)
    proj_r = _matmul(x16, w_r, tm=tm, tn=n_r + _LANES, out_dtype=_F32, name="in_proj_r")
    proj_f = _matmul(x16, w_f, tm=tm, tn=n_f, out_dtype=_BF16, name="in_proj_f")
    proj_c = _matmul(x16, w_c, tm=tm, tn=2048, out_dtype=_F32, name="in_proj_c")

    y_a = _retention(proj_r, bsz, seq)
    bias = jnp.zeros((1, _LANES), _F32).at[0, :n_fl].set(fox_b_f)
    qa, ka, va = _fox_prep(proj_f, proj_r, n_r // _LANES, bias, bsz, seq)
    y_b = _fox(qa, ka, va, bsz, seq)
    y_c = _gmlp(proj_c, gm_w_s, gm_b_s, gm_ln_g, gm_ln_b, tr=tm)
    return _merge(y_a, y_b, y_c, proj_c, gate_b, w_branch, w_out, x, ln_g, ln_b, alpha=alpha, tm=tm)


def kernel(x, w_in, fox_b_f, gate_b, gm_w_s, gm_b_s, gm_ln_g, gm_ln_b, w_branch, w_out, ln_g, ln_b,
           dense_w_up, dense_w_down, moe_router, moe_w_up, moe_w_down):
    bsz, seq, d_model = x.shape
    depth = w_in.shape[0]
    alpha = (2 * depth) ** 0.25
    tokens = bsz * seq
    tm = min(_TM, tokens)
    xf = x.reshape(tokens, d_model)
    x16 = xf.astype(_BF16)
    for l in range(depth):
        xf, x16 = _mixer_ln(xf, x16, w_in[l], fox_b_f[l], gate_b[l], gm_w_s[l], gm_b_s[l],
                            gm_ln_g[l], gm_ln_b[l], w_branch[l], w_out[l], ln_g[l, 0], ln_b[l, 0],
                            bsz=bsz, seq=seq, alpha=alpha, tm=tm)
        if l % 2 == 0:
            xf, x16 = _dense_ffn_ln(xf, x16, dense_w_up[l // 2], dense_w_down[l // 2],
                                    ln_g[l, 1], ln_b[l, 1], alpha=alpha, tm=tm)
        else:
            xf, x16 = _moe_ffn_ln(xf, x16, moe_router[l // 2], moe_w_up[l // 2], moe_w_down[l // 2],
                                  ln_g[l, 1], ln_b[l, 1], alpha=alpha, tm=tm)
    return xf.reshape(bsz, seq, d_model)
```

```python
import functools
import math

import jax
import jax.numpy as jnp
import numpy as np
from jax import lax
from jax.experimental import pallas as pl
from jax.experimental.pallas import tpu as pltpu

_BF16 = jnp.bfloat16
_F32 = jnp.float32

_BLOCK = 128
_RET_HEADS, _RET_DK, _RET_DV = 4, 64, 128
_FOX_HEADS, _FOX_DH = 8, 64
_GM_GROUPS, _GM_DG = 4, 128
_N_EXPERTS = 8
_ROPE_BASE = 10000.0
_LN_EPS = 1e-5
_GN_EPS = 1e-6

_LANES = 128
_VMEM_PHYSICAL_BYTES = 64 * 1024 * 1024
_VMEM_LIMIT_BYTES = (_VMEM_PHYSICAL_BYTES * 7) // 8

_TM = 512
_TQ = 512
_TR_RET = 512


def _params(n_axes):
    return pltpu.CompilerParams(
        dimension_semantics=("arbitrary",) * n_axes,
        vmem_limit_bytes=_VMEM_LIMIT_BYTES,
    )


def _layer_norm(x, g, b, eps):
    mu = jnp.mean(x, axis=-1, keepdims=True)
    xc = x - mu
    var = jnp.mean(xc * xc, axis=-1, keepdims=True)
    return xc * lax.rsqrt(var + eps) * g + b


def _silu(x):
    return x * (1.0 / (1.0 + jnp.exp(-x)))


def _gelu_tanh(x):
    c = math.sqrt(2.0 / math.pi)
    return x * (0.5 * (1.0 + jnp.tanh(c * (x + 0.044715 * (x * x * x)))))


def _dot(a, b):
    return jnp.dot(a, b, preferred_element_type=_F32)


def _dot_nt(a, b):
    return lax.dot_general(a, b, (((1,), (1,)), ((), ())), preferred_element_type=_F32)


def _dot_tn(a, b):
    return lax.dot_general(a, b, (((0,), (0,)), ((), ())), preferred_element_type=_F32)


def _matmul_kernel(x_ref, w_ref, o_ref):
    o_ref[...] = _dot(x_ref[...], w_ref[...]).astype(o_ref.dtype)


def _matmul(x, w, *, tm, tn, out_dtype, name):
    m, k = x.shape
    n = w.shape[1]
    return pl.pallas_call(
        _matmul_kernel,
        out_shape=jax.ShapeDtypeStruct((m, n), out_dtype),
        grid=(n // tn, m // tm),
        in_specs=[pl.BlockSpec((tm, k), lambda j, i: (i, 0)),
                  pl.BlockSpec((k, tn), lambda j, i: (0, j))],
        out_specs=pl.BlockSpec((tm, tn), lambda j, i: (i, j)),
        compiler_params=_params(2),
        name=name,
    )(x, w)


def _retention_kernel(x16_ref, w_ref, cos_ref, sin_ref, qdec_ref, kdec_ref, din_ref,
                      y_ref, v_sc, g_sc, state_sc, *, chunk_dec):
    c = pl.program_id(1)

    @pl.when(c == 0)
    def _():
        state_sc[...] = jnp.zeros_like(state_sc)

    width = _RET_HEADS * _RET_DK
    wv = _RET_HEADS * _RET_DV
    rows_total = x16_ref.shape[0]
    lane_all = lax.broadcasted_iota(jnp.int32, (rows_total, width), 1)
    first_half = (lane_all % _RET_DK) < (_RET_DK // 2)

    def rotary(x):
        partner = jnp.where(first_half,
                            pltpu.roll(x, width - _RET_DK // 2, 1),
                            pltpu.roll(x, _RET_DK // 2, 1))
        return x * cos_ref[...] + partner * sin_ref[...]

    x16 = x16_ref[...]
    q_all = rotary(_dot(x16, w_ref[:, 0:width]))
    k_all = rotary(_dot(x16, w_ref[:, width:2 * width])) * (_RET_DK ** -0.5)
    v_sc[...] = _dot(x16, w_ref[:, 2 * width:2 * width + wv]).astype(_BF16)
    g_sc[...] = _dot(x16, w_ref[:, 2 * width + wv:2 * width + 2 * wv])
    v_ref, g_ref = v_sc, g_sc
    lane = lax.broadcasted_iota(jnp.int32, (_BLOCK, width), 1)
    for cc in range(rows_total // _BLOCK):
        rows = slice(cc * _BLOCK, (cc + 1) * _BLOCK)
        q = q_all[rows]
        k = k_all[rows]
        k_bf = k.astype(_BF16)
        q_cross = q * qdec_ref[...]
        k_state = k * kdec_ref[...]
        for h in range(_RET_HEADS):
            cols = slice(h * _RET_DV, (h + 1) * _RET_DV)
            in_head = (lane // _RET_DK) == h
            vh = v_ref[rows, cols].astype(_BF16)
            qh = jnp.where(in_head, q, 0.0).astype(_BF16)
            scores = _dot_nt(qh, k_bf) * din_ref[h]
            inner = _dot(scores.astype(_BF16), vh)
            state = state_sc[h]
            cross = _dot(jnp.where(in_head, q_cross, 0.0).astype(_BF16), state.astype(_BF16))
            state_sc[h] = state * chunk_dec[h] + _dot_tn(
                jnp.where(in_head, k_state, 0.0).astype(_BF16), vh)
            out = inner + cross
            mu = jnp.mean(out, axis=-1, keepdims=True)
            oc = out - mu
            var = jnp.mean(oc * oc, axis=-1, keepdims=True)
            normed = oc * lax.rsqrt(var + _GN_EPS)
            y_ref[rows, cols] = (_silu(g_ref[rows, cols]) * normed).astype(y_ref.dtype)


def _retention_tables(seq):
    half = _RET_DK // 2
    inv_freq = _ROPE_BASE ** (-jnp.arange(half, dtype=_F32) / half)
    ang = jnp.arange(seq, dtype=jnp.int32).astype(_F32)[:, None] * inv_freq[None, :]
    cos, sin = jnp.cos(ang), jnp.sin(ang)
    cos_t = jnp.tile(jnp.concatenate([cos, cos], axis=1), (1, _RET_HEADS))
    sin_t = jnp.tile(jnp.concatenate([-sin, sin], axis=1), (1, _RET_HEADS))
    log_gamma = jnp.log1p(-jnp.exp2(-5.0 - jnp.arange(_RET_HEADS, dtype=_F32)))
    idx = jnp.arange(_BLOCK, dtype=_F32)
    rel = idx[:, None] - idx[None, :]
    causal = rel >= 0
    decay_in = jnp.where(causal[None],
                         jnp.exp(log_gamma[:, None, None] * jnp.where(causal, rel, 0.0)[None]), 0.0)
    q_dec = jnp.exp(log_gamma[:, None] * (idx + 1.0))
    k_dec = jnp.exp(log_gamma[:, None] * (_BLOCK - 1.0 - idx))
    q_dec_t = jnp.repeat(q_dec.T, _RET_DK, axis=1)
    k_dec_t = jnp.repeat(k_dec.T, _RET_DK, axis=1)
    return cos_t, sin_t, q_dec_t, k_dec_t, decay_in


def _retention(x16, w_r, bsz, seq):
    tokens, d_model = x16.shape
    tr = min(_TR_RET, seq)
    nstep = seq // tr
    cos_t, sin_t, q_dec_t, k_dec_t, decay_in = _retention_tables(seq)
    chunk_dec = tuple(float((1.0 - 2.0 ** (-5.0 - h)) ** _BLOCK) for h in range(_RET_HEADS))
    qk = _RET_HEADS * _RET_DK
    wv = _RET_HEADS * _RET_DV
    row = lambda b, c: b * nstep + c
    return pl.pallas_call(
        functools.partial(_retention_kernel, chunk_dec=chunk_dec),
        out_shape=jax.ShapeDtypeStruct((tokens, wv), _BF16),
        grid=(bsz, nstep),
        in_specs=[
            pl.BlockSpec((tr, d_model), lambda b, c: (row(b, c), 0)),
            pl.BlockSpec(w_r.shape, lambda b, c: (0, 0)),
            pl.BlockSpec((tr, qk), lambda b, c: (c, 0)),
            pl.BlockSpec((tr, qk), lambda b, c: (c, 0)),
            pl.BlockSpec((_BLOCK, qk), lambda b, c: (0, 0)),
            pl.BlockSpec((_BLOCK, qk), lambda b, c: (0, 0)),
            pl.BlockSpec((_RET_HEADS, _BLOCK, _BLOCK), lambda b, c: (0, 0, 0)),
        ],
        out_specs=pl.BlockSpec((tr, wv), lambda b, c: (row(b, c), 0)),
        scratch_shapes=[pltpu.VMEM((tr, wv), _BF16), pltpu.VMEM((tr, wv), _F32),
                        pltpu.VMEM((_RET_HEADS, qk, _RET_DV), _F32)],
        compiler_params=_params(2),
        name="retention",
    )(x16, w_r, cos_t, sin_t, q_dec_t, k_dec_t, decay_in)


_C_PARTS = 3
_TS_PREP = 512


def _fox_selectors():
    half = _LANES // 2
    one_row = _C_PARTS * _FOX_HEADS
    sel = np.zeros((_FOX_HEADS, _LANES, 2 * _LANES), np.float32)
    for h in range(_FOX_HEADS):
        base = half if h % 2 == 0 else 0
        for n in range(_C_PARTS):
            sel[h, n * _FOX_HEADS + h, base + n] = 1.0
            sel[h, one_row, base + _C_PARTS + n] = 1.0
            sel[h, one_row, _LANES + base + n] = 1.0
            sel[h, n * _FOX_HEADS + h, _LANES + base + _C_PARTS + n] = -1.0
    return jnp.asarray(sel, _BF16)


def _fox_prep_kernel(x16_ref, w_ref, b_ref, sel_ref, qa_ref, ka_ref, va_ref, fq_ref, fk_ref, fv_ref,
                     fl_ref, carry_sc, *, nblk):
    @pl.when(pl.program_id(1) == 0)
    def _():
        carry_sc[...] = jnp.zeros_like(carry_sc)

    x16 = x16_ref[...]
    width = _FOX_HEADS * _FOX_DH
    for n, dst in enumerate((fq_ref, fk_ref, fv_ref)):
        dst[...] = _dot(x16, w_ref[:, n * width:(n + 1) * width]).astype(_BF16)
    fl_ref[...] = _dot(x16, w_ref[:, 3 * width:3 * width + _LANES])

    r = lax.broadcasted_iota(jnp.int32, (_BLOCK, _LANES), 0)
    lane = lax.broadcasted_iota(jnp.int32, (_BLOCK, _LANES), 1)
    tri = (r >= lane).astype(_F32)
    half = _LANES // 2
    q_scale = jnp.asarray(_FOX_DH ** -0.5, _BF16)
    carry = carry_sc[...]
    packed = []
    for blk in range(nblk):
        rows = slice(blk * _BLOCK, (blk + 1) * _BLOCK)
        z = fl_ref[rows, :] + b_ref[...]
        log_f = jnp.minimum(z, 0.0) - jnp.log1p(jnp.exp(-jnp.abs(z)))
        cum = jnp.dot(tri, log_f, preferred_element_type=_F32,
                      precision=lax.Precision.HIGHEST) + carry
        carry = cum[_BLOCK - 1:_BLOCK, :]
        row = jnp.where(lane == _C_PARTS * _FOX_HEADS, 1.0, 0.0)
        rest = cum
        for n in range(_C_PARTS):
            piece = rest.astype(_BF16).astype(_F32)
            rest = rest - piece
            shifted = piece if n == 0 else pltpu.roll(piece, n * _FOX_HEADS, 1)
            row = jnp.where((lane >= n * _FOX_HEADS) & (lane < (n + 1) * _FOX_HEADS), shifted, row)
        packed.append(row.astype(_BF16))
    carry_sc[...] = carry
    packed = jnp.concatenate(packed, axis=0)
    lane_all = lax.broadcasted_iota(jnp.int32, (nblk * _BLOCK, _LANES), 1)
    for h in range(_FOX_HEADS):
        pair = slice((h // 2) * _LANES, (h // 2 + 1) * _LANES)
        in_head = (lane_all < half) if h % 2 == 0 else (lane_all >= half)
        extra = _dot(packed, sel_ref[h]).astype(_BF16)
        qa_ref[h] = jnp.where(in_head, fq_ref[:, pair] * q_scale, extra[:, :_LANES])
        ka_ref[h] = jnp.where(in_head, fk_ref[:, pair], extra[:, _LANES:])
        va_ref[h] = jnp.where(in_head, fv_ref[:, pair], jnp.ones((), _BF16))


def _fox_prep(x16, w_f, bias, bsz, seq):
    d_model = x16.shape[1]
    ts = min(_TS_PREP, seq)
    ns = seq // ts
    width = _FOX_HEADS * _FOX_DH
    sel = _fox_selectors()
    aug = jax.ShapeDtypeStruct((bsz, _FOX_HEADS, seq, _LANES), _BF16)
    aug_spec = pl.BlockSpec((None, _FOX_HEADS, ts, _LANES), lambda b, s: (b, 0, s, 0))
    return pl.pallas_call(
        functools.partial(_fox_prep_kernel, nblk=ts // _BLOCK),
        out_shape=(aug, aug, aug),
        grid=(bsz, ns),
        in_specs=[pl.BlockSpec((ts, d_model), lambda b, s: (b * ns + s, 0)),
                  pl.BlockSpec(w_f.shape, lambda b, s: (0, 0)),
                  pl.BlockSpec((1, _LANES), lambda b, s: (0, 0)),
                  pl.BlockSpec(sel.shape, lambda b, s: (0, 0, 0))],
        out_specs=(aug_spec, aug_spec, aug_spec),
        scratch_shapes=[pltpu.VMEM((ts, width), _BF16), pltpu.VMEM((ts, width), _BF16),
                        pltpu.VMEM((ts, width), _BF16), pltpu.VMEM((ts, _LANES), _F32),
                        pltpu.VMEM((1, _LANES), _F32)],
        compiler_params=_params(2),
        name="fox_prep",
    )(x16, w_f, bias, sel)


def _fox_kernel(qa_ref, ka_ref, va_ref, o_ref, m0_sc, m1_sc, acc0_sc, acc1_sc, *, tq):
    i = pl.program_id(2)
    m_scs = (m0_sc, m1_sc)
    acc_scs = (acc0_sc, acc1_sc)
    for s in range(2):
        m_scs[s][...] = jnp.full_like(m_scs[s], -jnp.inf)
        acc_scs[s][...] = jnp.zeros_like(acc_scs[s])

    row_id = lax.broadcasted_iota(jnp.int32, (tq, tq), 0)
    col_id = lax.broadcasted_iota(jnp.int32, (tq, tq), 1)
    keep = row_id >= col_id

    def block(j, masked):
        start = pl.multiple_of(j * tq, tq)
        logits = [_dot_nt(qa_ref[s], ka_ref[s, pl.ds(start, tq), :]) for s in range(2)]
        pexp, scale, m_new = [], [], []
        for s in range(2):
            lg = jnp.where(keep, logits[s], -jnp.inf) if masked else logits[s]
            m_old = m_scs[s][...]
            m_new.append(jnp.maximum(m_old, jnp.max(lg, axis=1, keepdims=True)))
            pexp.append(jnp.exp(lg - jnp.concatenate([m_new[s]] * (tq // _LANES), axis=1)))
            scale.append(jnp.exp(m_old - m_new[s]))
        pv = [_dot(pexp[s].astype(_BF16), va_ref[s, pl.ds(start, tq), :]) for s in range(2)]
        for s in range(2):
            acc_scs[s][...] = scale[s] * acc_scs[s][...] + pv[s]
            m_scs[s][...] = m_new[s]

    def body(j, carry):
        block(j, False)
        return carry

    lax.fori_loop(0, i, body, 0)
    block(i, True)
    half = _LANES // 2
    acc0 = acc0_sc[...]
    acc1 = acc1_sc[...]
    low = lax.broadcasted_iota(jnp.int32, (tq, _LANES), 1) < half
    out = jnp.where(low, acc0 / acc0[:, half:half + 1], acc1 / acc1[:, 0:1])
    o_ref[...] = out.astype(o_ref.dtype)


def _fox(qa, ka, va, bsz, seq):
    tq = min(_TQ, seq)
    nq = seq // tq
    npair = _FOX_HEADS // 2
    return pl.pallas_call(
        functools.partial(_fox_kernel, tq=tq),
        out_shape=jax.ShapeDtypeStruct((bsz * seq, _FOX_HEADS * _FOX_DH), _BF16),
        grid=(bsz, npair, nq),
        in_specs=[
            pl.BlockSpec((None, 2, tq, _LANES), lambda b, p, i: (b, p, i, 0)),
            pl.BlockSpec((None, 2, seq, _LANES), lambda b, p, i: (b, p, 0, 0)),
            pl.BlockSpec((None, 2, seq, _LANES), lambda b, p, i: (b, p, 0, 0)),
        ],
        out_specs=pl.BlockSpec((tq, _LANES), lambda b, p, i: (b * nq + i, p)),
        scratch_shapes=[pltpu.VMEM((tq, _LANES), _F32), pltpu.VMEM((tq, _LANES), _F32),
                        pltpu.VMEM((tq, _LANES), _F32), pltpu.VMEM((tq, _LANES), _F32)],
        compiler_params=_params(3),
        name="fox_attention",
    )(qa, ka, va)


def _gmlp_tile(gu, gv, w_ref, bst_ref, lng_ref, lnb_ref, yc_sc):
    u = _gelu_tanh(gu)
    v = _layer_norm(_gelu_tanh(gv), lng_ref[...], lnb_ref[...], _LN_EPS).astype(_BF16)
    r = lax.broadcasted_iota(jnp.int32, (_BLOCK, _BLOCK), 0)
    c = lax.broadcasted_iota(jnp.int32, (_BLOCK, _BLOCK), 1)
    causal = r >= c
    for g in range(_GM_GROUPS):
        cols = slice(g * _GM_DG, (g + 1) * _GM_DG)
        wg = jnp.where(causal, w_ref[g], 0.0).astype(_BF16)
        bias = bst_ref[:, g:g + 1]
        for ch in range(gu.shape[0] // _BLOCK):
            rows = slice(ch * _BLOCK, (ch + 1) * _BLOCK)
            sp = _dot(wg, v[rows, cols]) + bias
            yc_sc[rows, cols] = (u[rows, cols] * sp).astype(yc_sc.dtype)


def _mixer_out_kernel(x16_ref, ya_ref, yb_ref, wc_ref, gmw_ref, bst_ref, gmg_ref, gmb_ref, gbias_ref,
                      wbr_ref, wout_ref, x_ref, lng_ref, lnb_ref, xo_ref, xo16_ref, yc_sc,
                      *, alpha, d_model):
    x16 = x16_ref[...]
    gw = _GM_GROUPS * _GM_DG
    gu = _dot(x16, wc_ref[:, 0:gw])
    gv = _dot(x16, wc_ref[:, gw:2 * gw])
    _gmlp_tile(gu, gv, gmw_ref, bst_ref, gmg_ref, gmb_ref, yc_sc)
    merged = None
    for n, y_ref in enumerate((ya_ref, yb_ref, yc_sc)):
        cols = slice(2 * gw + n * d_model, 2 * gw + (n + 1) * d_model)
        z = _dot(x16, wc_ref[:, cols]) + gbias_ref[:, n * d_model:(n + 1) * d_model]
        gate = 1.0 / (1.0 + jnp.exp(-z))
        term = gate * _dot(y_ref[...], wbr_ref[n])
        merged = term if merged is None else merged + term
    h = _dot(merged.astype(_BF16), wout_ref[...])
    out = _layer_norm(alpha * x_ref[...] + h, lng_ref[...], lnb_ref[...], _LN_EPS)
    xo_ref[...] = out
    xo16_ref[...] = out.astype(_BF16)


def _mixer_out(x16, y_a, y_b, w_c, gm_w_s, gm_b_s, gm_ln_g, gm_ln_b, gate_b, w_branch, w_out, x,
               ln_g, ln_b, *, alpha, tm):
    tokens, d_model = x.shape
    bw = y_a.shape[1]
    gw = _GM_GROUPS * _GM_DG
    row = lambda i: (i, 0)
    const2 = lambda i: (0, 0)
    const3 = lambda i: (0, 0, 0)
    return pl.pallas_call(
        functools.partial(_mixer_out_kernel, alpha=alpha, d_model=d_model),
        out_shape=(jax.ShapeDtypeStruct((tokens, d_model), _F32),
                   jax.ShapeDtypeStruct((tokens, d_model), _BF16)),
        grid=(tokens // tm,),
        in_specs=[
            pl.BlockSpec((tm, d_model), row),
            pl.BlockSpec((tm, bw), row), pl.BlockSpec((tm, bw), row),
            pl.BlockSpec(w_c.shape, const2),
            pl.BlockSpec((_GM_GROUPS, _BLOCK, _BLOCK), const3),
            pl.BlockSpec((_BLOCK, _GM_GROUPS), const2),
            pl.BlockSpec((1, gw), const2), pl.BlockSpec((1, gw), const2),
            pl.BlockSpec((1, 3 * d_model), const2),
            pl.BlockSpec((3, bw, d_model), const3),
            pl.BlockSpec((d_model, d_model), const2),
            pl.BlockSpec((tm, d_model), row),
            pl.BlockSpec((1, d_model), const2), pl.BlockSpec((1, d_model), const2),
        ],
        out_specs=(pl.BlockSpec((tm, d_model), row), pl.BlockSpec((tm, d_model), row)),
        scratch_shapes=[pltpu.VMEM((tm, gw), _BF16)],
        compiler_params=_params(1),
        name="mixer_out",
    )(x16, y_a, y_b, w_c, gm_w_s, gm_b_s.T, gm_ln_g.reshape(1, gw), gm_ln_b.reshape(1, gw),
      gate_b.reshape(1, -1), w_branch.astype(_BF16), w_out.astype(_BF16), x,
      ln_g.reshape(1, -1), ln_b.reshape(1, -1))


def _weights_changed(te_ref, i):
    return jnp.logical_or(i == 0, te_ref[i] != te_ref[jnp.maximum(i - 1, 0)])


def _up_kernel(te_ref, nused_ref, x_ref, wa_ref, wb_ref, h_ref, wa_sc, wb_sc):
    i = pl.program_id(1)

    @pl.when(_weights_changed(te_ref, i))
    def _():
        wa_sc[...] = wa_ref[...].astype(_BF16)
        wb_sc[...] = wb_ref[...].astype(_BF16)

    @pl.when(i < nused_ref[0])
    def _():
        x = x_ref[...].astype(_BF16)
        a = _dot(x, wa_sc[...])
        b = _dot(x, wb_sc[...])
        h_ref[...] = (_silu(a) * b).astype(h_ref.dtype)

    @pl.when(i >= nused_ref[0])
    def _():
        h_ref[...] = jnp.zeros_like(h_ref)


def _swiglu_up(x16, w_up, te, nused, *, tm, tn):
    rows, k = x16.shape
    d_ff = w_up.shape[2] // 2
    nb = d_ff // tn
    grid_spec = pltpu.PrefetchScalarGridSpec(
        num_scalar_prefetch=2,
        grid=(nb, rows // tm),
        in_specs=[
            pl.BlockSpec((tm, k), lambda j, i, te, nu: (i, 0)),
            pl.BlockSpec((None, k, tn), lambda j, i, te, nu: (te[i], 0, j)),
            pl.BlockSpec((None, k, tn), lambda j, i, te, nu: (te[i], 0, j + nb)),
        ],
        out_specs=pl.BlockSpec((tm, tn), lambda j, i, te, nu: (i, j)),
        scratch_shapes=[pltpu.VMEM((k, tn), _BF16), pltpu.VMEM((k, tn), _BF16)],
    )
    return pl.pallas_call(
        _up_kernel,
        out_shape=jax.ShapeDtypeStruct((rows, d_ff), _BF16),
        grid_spec=grid_spec,
        compiler_params=_params(2),
        name="swiglu_up",
    )(te, nused, x16, w_up, w_up)


def _down_kernel(te_ref, nused_ref, h_ref, w_ref, y_ref, w_sc):
    i = pl.program_id(1)

    @pl.when(_weights_changed(te_ref, i))
    def _():
        w_sc[...] = w_ref[...].astype(_BF16)

    @pl.when(i < nused_ref[0])
    def _():
        y_ref[...] = _dot(h_ref[...], w_sc[...])

    @pl.when(i >= nused_ref[0])
    def _():
        y_ref[...] = jnp.zeros_like(y_ref)


def _expert_down(h16, w_down, te, nused, *, tm, tn):
    rows, k = h16.shape
    n = w_down.shape[2]
    grid_spec = pltpu.PrefetchScalarGridSpec(
        num_scalar_prefetch=2,
        grid=(n // tn, rows // tm),
        in_specs=[
            pl.BlockSpec((tm, k), lambda j, i, te, nu: (i, 0)),
            pl.BlockSpec((None, k, tn), lambda j, i, te, nu: (te[i], 0, j)),
        ],
        out_specs=pl.BlockSpec((tm, tn), lambda j, i, te, nu: (i, j)),
        scratch_shapes=[pltpu.VMEM((k, tn), _BF16)],
    )
    return pl.pallas_call(
        _down_kernel,
        out_shape=jax.ShapeDtypeStruct((rows, n), _F32),
        grid_spec=grid_spec,
        compiler_params=_params(2),
        name="expert_down",
    )(te, nused, h16, w_down)


def _down_ln_kernel(h_ref, w_ref, x_ref, lng_ref, lnb_ref, xo_ref, xo16_ref, *, alpha):
    f = _dot(h_ref[...], w_ref[...])
    out = _layer_norm(alpha * x_ref[...] + f, lng_ref[...], lnb_ref[...], _LN_EPS)
    xo_ref[...] = out
    xo16_ref[...] = out.astype(_BF16)


def _dense_down_ln(h16, w_down16, x, ln_g, ln_b, *, alpha, tm):
    tokens, d_model = x.shape
    k = h16.shape[1]
    row = lambda i: (i, 0)
    const2 = lambda i: (0, 0)
    return pl.pallas_call(
        functools.partial(_down_ln_kernel, alpha=alpha),
        out_shape=(jax.ShapeDtypeStruct((tokens, d_model), _F32),
                   jax.ShapeDtypeStruct((tokens, d_model), _BF16)),
        grid=(tokens // tm,),
        in_specs=[pl.BlockSpec((tm, k), row), pl.BlockSpec((k, d_model), const2),
                  pl.BlockSpec((tm, d_model), row),
                  pl.BlockSpec((1, d_model), const2), pl.BlockSpec((1, d_model), const2)],
        out_specs=(pl.BlockSpec((tm, d_model), row), pl.BlockSpec((tm, d_model), row)),
        compiler_params=_params(1),
        name="dense_down_ln",
    )(h16, w_down16, x, ln_g.reshape(1, -1), ln_b.reshape(1, -1))


def _router_kernel(x_ref, wr_ref, o_ref):
    logits = jnp.dot(x_ref[...], wr_ref[...], preferred_element_type=_F32,
                     precision=lax.Precision.HIGHEST)
    lane = lax.broadcasted_iota(jnp.int32, logits.shape, 1).astype(_F32)
    lg = jnp.where(lane < _N_EXPERTS, logits, -jnp.inf)
    m1 = jnp.max(lg, axis=1, keepdims=True)
    i1 = jnp.min(jnp.where(lg == m1, lane, float(_LANES)), axis=1, keepdims=True)
    lg2 = jnp.where(lane == i1, -jnp.inf, lg)
    m2 = jnp.max(lg2, axis=1, keepdims=True)
    i2 = jnp.min(jnp.where(lg2 == m2, lane, float(_LANES)), axis=1, keepdims=True)
    e2 = jnp.exp(m2 - m1)
    den = 1.0 + e2
    w1 = 1.0 / den
    w2 = e2 / den
    o_ref[...] = jnp.where(lane == 0, i1, jnp.where(lane == 1, i2,
                           jnp.where(lane == 2, w1, jnp.where(lane == 3, w2, 0.0))))


def _router(x, w_router, *, tm):
    tokens, d_model = x.shape
    wr = jnp.zeros((d_model, _LANES), _F32).at[:, :_N_EXPERTS].set(w_router)
    return pl.pallas_call(
        _router_kernel,
        out_shape=jax.ShapeDtypeStruct((tokens, _LANES), _F32),
        grid=(tokens // tm,),
        in_specs=[pl.BlockSpec((tm, d_model), lambda i: (i, 0)),
                  pl.BlockSpec((d_model, _LANES), lambda i: (0, 0))],
        out_specs=pl.BlockSpec((tm, _LANES), lambda i: (i, 0)),
        compiler_params=_params(1),
        name="router",
    )(x, wr)


_ROW_UNROLL = 8


def _dispatch_kernel(zstart_ref, zvalid_ref, d0_ref, d1_ref, x_ref, xs_hbm, zero_sc, stage_sc, sem, zsem,
                     *, tt, tm, nsteps):
    @pl.when(pl.program_id(0) == 0)
    def _():
        zero_sc[...] = jnp.zeros_like(zero_sc)

        def zero_copy(n):
            start = pl.multiple_of(zstart_ref[n], tm)
            return pltpu.make_async_copy(zero_sc, xs_hbm.at[pl.ds(start, tm), :], zsem)

        for n in range(2 * _N_EXPERTS):
            @pl.when(zvalid_ref[n] > 0)
            def _():
                zero_copy(n).start()
        for n in range(2 * _N_EXPERTS):
            @pl.when(zvalid_ref[n] > 0)
            def _():
                zero_copy(n).wait()

    i = pl.program_id(0)
    slot = i % 2
    stage_sc[slot] = x_ref[...]

    def issue(g, carry):
        for u in range(_ROW_UNROLL):
            r = g * _ROW_UNROLL + u
            for d_ref in (d0_ref, d1_ref):
                pltpu.make_async_copy(stage_sc.at[slot, pl.ds(r, 1), :],
                                      xs_hbm.at[pl.ds(d_ref[0, 0, r], 1), :],
                                      sem.at[slot]).start(priority=u % 2)
        return carry

    def drain(s):
        for _ in range(2):
            pltpu.make_async_copy(stage_sc.at[s], xs_hbm.at[pl.ds(0, tt), :], sem.at[s]).wait()

    lax.fori_loop(0, tt // _ROW_UNROLL, issue, 0)

    @pl.when(i > 0)
    def _():
        drain(1 - slot)

    @pl.when(i == nsteps - 1)
    def _():
        drain(slot)


def _dispatch_rows(x, d0, d1, zstart, zvalid, *, rows, tt, tm):
    tokens, d_model = x.shape
    smem = lambda: pl.BlockSpec((1, 1, tt), lambda i, zs, zv: (i, 0, 0), memory_space=pltpu.SMEM)
    grid_spec = pltpu.PrefetchScalarGridSpec(
        num_scalar_prefetch=2,
        grid=(tokens // tt,),
        in_specs=[smem(), smem(), pl.BlockSpec((tt, d_model), lambda i, zs, zv: (i, 0))],
        out_specs=pl.BlockSpec(memory_space=pl.ANY),
        scratch_shapes=[pltpu.VMEM((tm, d_model), _F32), pltpu.VMEM((2, tt, d_model), _F32),
                        pltpu.SemaphoreType.DMA((2,)), pltpu.SemaphoreType.DMA],
    )
    return pl.pallas_call(
        functools.partial(_dispatch_kernel, tt=tt, tm=tm, nsteps=tokens // tt),
        out_shape=jax.ShapeDtypeStruct((rows, d_model), _F32),
        grid_spec=grid_spec,
        compiler_params=_params(1),
        name="moe_dispatch",
    )(zstart, zvalid, d0, d1, x)


def _combine_kernel(d0_ref, d1_ref, n0_ref, n1_ref, y_hbm, route_ref, x_ref, lng_ref, lnb_ref,
                    xo_ref, xo16_ref, buf, sem, *, alpha, tm, nsteps):
    i = pl.program_id(0)
    slot = i % 2

    def gather(dst_slot, a_ref, b_ref):
        def issue(g, carry):
            for u in range(_ROW_UNROLL):
                r = g * _ROW_UNROLL + u
                for s, d_ref in enumerate((a_ref, b_ref)):
                    pltpu.make_async_copy(y_hbm.at[pl.ds(d_ref[0, 0, r], 1), :],
                                          buf.at[dst_slot, s, pl.ds(r, 1), :],
                                          sem.at[dst_slot]).start(priority=u % 2)
            return carry

        lax.fori_loop(0, tm // _ROW_UNROLL, issue, 0)

    @pl.when(i == 0)
    def _():
        gather(0, d0_ref, d1_ref)

    @pl.when(i + 1 < nsteps)
    def _():
        gather(1 - slot, n0_ref, n1_ref)

    for s in range(2):
        pltpu.make_async_copy(y_hbm.at[pl.ds(0, tm), :], buf.at[slot, s], sem.at[slot]).wait()
    route = route_ref[...]
    f = route[:, 2:3] * buf[slot, 0] + route[:, 3:4] * buf[slot, 1]
    out = _layer_norm(alpha * x_ref[...] + f, lng_ref[...], lnb_ref[...], _LN_EPS)
    xo_ref[...] = out
    xo16_ref[...] = out.astype(_BF16)


def _combine_ln(ys, d0, d1, route, x, ln_g, ln_b, *, alpha, tm):
    tokens, d_model = x.shape
    nsteps = tokens // tm
    row = lambda i: (i, 0)
    const2 = lambda i: (0, 0)
    cur = lambda: pl.BlockSpec((1, 1, tm), lambda i: (i, 0, 0), memory_space=pltpu.SMEM)
    nxt = lambda: pl.BlockSpec((1, 1, tm), lambda i: (jnp.minimum(i + 1, nsteps - 1), 0, 0),
                               memory_space=pltpu.SMEM)
    return pl.pallas_call(
        functools.partial(_combine_kernel, alpha=alpha, tm=tm, nsteps=nsteps),
        out_shape=(jax.ShapeDtypeStruct((tokens, d_model), _F32),
                   jax.ShapeDtypeStruct((tokens, d_model), _BF16)),
        grid=(nsteps,),
        in_specs=[cur(), cur(), nxt(), nxt(), pl.BlockSpec(memory_space=pl.ANY),
                  pl.BlockSpec((tm, _LANES), row), pl.BlockSpec((tm, d_model), row),
                  pl.BlockSpec((1, d_model), const2), pl.BlockSpec((1, d_model), const2)],
        out_specs=(pl.BlockSpec((tm, d_model), row), pl.BlockSpec((tm, d_model), row)),
        scratch_shapes=[pltpu.VMEM((2, 2, tm, d_model), _F32), pltpu.SemaphoreType.DMA((2,))],
        compiler_params=_params(1),
        name="moe_combine_ln",
    )(d0, d1, d0, d1, ys, route, x, ln_g.reshape(1, -1), ln_b.reshape(1, -1))


def _routing_plan(route, *, tm):
    tokens = route.shape[0]
    pairs = 2 * tokens
    ntiles = pairs // tm + _N_EXPERTS
    flat_e = route[:, 0:2].astype(jnp.int32).reshape(pairs)
    onehot = (flat_e[:, None] == jnp.arange(_N_EXPERTS, dtype=jnp.int32)[None, :]).astype(jnp.int32)
    csum = jnp.cumsum(onehot, axis=0)
    counts = csum[-1]
    padded = ((counts + tm - 1) // tm) * tm
    ends = jnp.cumsum(padded)
    starts = ends - padded
    dest = jnp.sum(onehot * (starts[None, :] + csum - 1), axis=1)
    tile_start = jnp.arange(ntiles, dtype=jnp.int32) * tm
    te = jnp.minimum(jnp.sum((ends[None, :] <= tile_start[:, None]).astype(jnp.int32), axis=1),
                     _N_EXPERTS - 1)
    nused = (ends[-1] // tm).reshape(1)
    tail = ends[-1] + jnp.arange(_N_EXPERTS, dtype=jnp.int32) * tm
    zstart = jnp.concatenate([jnp.maximum(ends - tm, 0), jnp.minimum(tail, (ntiles - 1) * tm)])
    zvalid = jnp.concatenate([padded > 0, tail < ntiles * tm]).astype(jnp.int32)
    dest2 = dest.reshape(tokens, 2)
    return ntiles * tm, te, nused, dest2[:, 0], dest2[:, 1], zstart, zvalid


def _moe_ffn_ln(x, x16, w_router, w_up, w_down, ln_g, ln_b, *, alpha, tm):
    del x16
    tokens = x.shape[0]
    tc = min(256, tokens)
    route = _router(x, w_router, tm=tm)
    rows, te, nused, d0, d1, zstart, zvalid = _routing_plan(route, tm=tm)
    d0 = d0.reshape(tokens // tc, 1, tc)
    d1 = d1.reshape(tokens // tc, 1, tc)
    xs = _dispatch_rows(x, d0, d1, zstart, zvalid, rows=rows, tt=tc, tm=tm)
    h = _swiglu_up(xs, w_up, te, nused, tm=tm, tn=w_up.shape[2] // 4)
    ys = _expert_down(h, w_down, te, nused, tm=tm, tn=w_down.shape[2])
    return _combine_ln(ys, d0, d1, route, x, ln_g, ln_b, alpha=alpha, tm=tc)


def _dense_ffn_ln(x, x16, w_up, w_down, ln_g, ln_b, *, alpha, tm):
    tokens = x.shape[0]
    d_ff = w_up.shape[1] // 2
    te = jnp.zeros((tokens // tm,), jnp.int32)
    nused = jnp.full((1,), tokens // tm, jnp.int32)
    tn = d_ff // 2 if (d_ff // 2) % _LANES == 0 else d_ff
    h = _swiglu_up(x16, w_up[None], te, nused, tm=tm, tn=tn)
    return _dense_down_ln(h, w_down.astype(_BF16), x, ln_g, ln_b, alpha=alpha, tm=tm)


def _mixer_ln(x, x16, w_in, fox_b_f, gate_b, gm_w_s, gm_b_s, gm_ln_g, gm_ln_b, w_branch, w_out,
              ln_g, ln_b, *, bsz, seq, alpha, tm):
    d_model = x.shape[1]
    n_r = 2 * _RET_HEADS * _RET_DK + 2 * _RET_HEADS * _RET_DV
    n_f = 3 * _FOX_HEADS * _FOX_DH
    n_fl = _FOX_HEADS
    w_r = w_in[:, :n_r].astype(_BF16)
    w_f = jnp.concatenate([w_in[:, n_r:n_r + n_f + n_fl],
                           jnp.zeros((d_model, _LANES - n_fl), w_in.dtype)], axis=1).astype(_BF16)
    w_c = w_in[:, n_r + n_f + n_fl:].astype(_BF16)

    y_a = _retention(x16, w_r, bsz, seq)
    bias = jnp.zeros((1, _LANES), _F32).at[0, :n_fl].set(fox_b_f)
    qa, ka, va = _fox_prep(x16, w_f, bias, bsz, seq)
    y_b = _fox(qa, ka, va, bsz, seq)
    return _mixer_out(x16, y_a, y_b, w_c, gm_w_s, gm_b_s, gm_ln_g, gm_ln_b, gate_b, w_branch, w_out,
                      x, ln_g, ln_b, alpha=alpha, tm=tm)


def kernel(x, w_in, fox_b_f, gate_b, gm_w_s, gm_b_s, gm_ln_g, gm_ln_b, w_branch, w_out, ln_g, ln_b,
           dense_w_up, dense_w_down, moe_router, moe_w_up, moe_w_down):
    bsz, seq, d_model = x.shape
    depth = w_in.shape[0]
    alpha = (2 * depth) ** 0.25
    tokens = bsz * seq
    tm = min(_TM, tokens)
    xf = x.reshape(tokens, d_model)
    x16 = xf.astype(_BF16)
    for l in range(depth):
        xf, x16 = _mixer_ln(xf, x16, w_in[l], fox_b_f[l], gate_b[l], gm_w_s[l], gm_b_s[l],
                            gm_ln_g[l], gm_ln_b[l], w_branch[l], w_out[l], ln_g[l, 0], ln_b[l, 0],
                            bsz=bsz, seq=seq, alpha=alpha, tm=tm)
        if l % 2 == 0:
            xf, x16 = _dense_ffn_ln(xf, x16, dense_w_up[l // 2], dense_w_down[l // 2],
                                    ln_g[l, 1], ln_b[l, 1], alpha=alpha, tm=tm)
        else:
            xf, x16 = _moe_ffn_ln(xf, x16, moe_router[l // 2], moe_w_up[l // 2], moe_w_down[l // 2],
                                  ln_g[l, 1], ln_b[l, 1], alpha=alpha, tm=tm)
    return xf.reshape(bsz, seq, d_model)
```

```python
import functools
import math

import jax
import jax.numpy as jnp
import numpy as np
from jax import lax
from jax.experimental import pallas as pl
from jax.experimental.pallas import tpu as pltpu

_BF16 = jnp.bfloat16
_F32 = jnp.float32

_BLOCK = 128
_RET_HEADS, _RET_DK, _RET_DV = 4, 64, 128
_FOX_HEADS, _FOX_DH = 8, 64
_GM_GROUPS, _GM_DG = 4, 128
_N_EXPERTS = 8
_ROPE_BASE = 10000.0
_LN_EPS = 1e-5
_GN_EPS = 1e-6

_LANES = 128
_MXU_COLS = 256
_VMEM_PHYSICAL_BYTES = 64 * 1024 * 1024
_VMEM_LIMIT_BYTES = (_VMEM_PHYSICAL_BYTES * 7) // 8

_TM = 512
_TQ = 512
_TR_RET = 512


def _params(n_axes):
    return pltpu.CompilerParams(
        dimension_semantics=("arbitrary",) * n_axes,
        vmem_limit_bytes=_VMEM_LIMIT_BYTES,
    )


def _layer_norm(x, g, b, eps):
    mu = jnp.mean(x, axis=-1, keepdims=True)
    xc = x - mu
    var = jnp.mean(xc * xc, axis=-1, keepdims=True)
    return xc * lax.rsqrt(var + eps) * g + b


def _silu(x):
    return x * (1.0 / (1.0 + jnp.exp(-x)))


def _gelu_tanh(x):
    c = math.sqrt(2.0 / math.pi)
    return x * (0.5 * (1.0 + jnp.tanh(c * (x + 0.044715 * (x * x * x)))))


def _dot(a, b):
    return jnp.dot(a, b, preferred_element_type=_F32)


def _dot_nt(a, b):
    return lax.dot_general(a, b, (((1,), (1,)), ((), ())), preferred_element_type=_F32)


def _dot_tn(a, b):
    return lax.dot_general(a, b, (((0,), (0,)), ((), ())), preferred_element_type=_F32)


def _matmul_kernel(x_ref, w_ref, o_ref):
    o_ref[...] = _dot(x_ref[...], w_ref[...]).astype(o_ref.dtype)


def _matmul(x, w, *, tm, tn, out_dtype, name):
    m, k = x.shape
    n = w.shape[1]
    return pl.pallas_call(
        _matmul_kernel,
        out_shape=jax.ShapeDtypeStruct((m, n), out_dtype),
        grid=(n // tn, m // tm),
        in_specs=[pl.BlockSpec((tm, k), lambda j, i: (i, 0)),
                  pl.BlockSpec((k, tn), lambda j, i: (0, j))],
        out_specs=pl.BlockSpec((tm, tn), lambda j, i: (i, j)),
        compiler_params=_params(2),
        name=name,
    )(x, w)


def _retention_kernel(x16_ref, w_ref, cos_ref, sin_ref, qdec_ref, kdec_ref, din_ref,
                      y_ref, v_sc, g_sc, state_sc, *, chunk_dec):
    c = pl.program_id(1)

    @pl.when(c == 0)
    def _():
        state_sc[...] = jnp.zeros_like(state_sc)

    width = _RET_HEADS * _RET_DK
    wv = _RET_HEADS * _RET_DV
    rows_total = x16_ref.shape[0]
    lane_all = lax.broadcasted_iota(jnp.int32, (rows_total, width), 1)
    first_half = (lane_all % _RET_DK) < (_RET_DK // 2)

    def rotary(x):
        partner = jnp.where(first_half,
                            pltpu.roll(x, width - _RET_DK // 2, 1),
                            pltpu.roll(x, _RET_DK // 2, 1))
        return x * cos_ref[...] + partner * sin_ref[...]

    x16 = x16_ref[...]
    q_all = rotary(_dot(x16, w_ref[:, 0:width]))
    k_all = rotary(_dot(x16, w_ref[:, width:2 * width])) * (_RET_DK ** -0.5)
    v_sc[...] = _dot(x16, w_ref[:, 2 * width:2 * width + wv]).astype(_BF16)
    g_sc[...] = _dot(x16, w_ref[:, 2 * width + wv:2 * width + 2 * wv])
    v_ref, g_ref = v_sc, g_sc
    lane = lax.broadcasted_iota(jnp.int32, (_BLOCK, width), 1)
    for cc in range(rows_total // _BLOCK):
        rows = slice(cc * _BLOCK, (cc + 1) * _BLOCK)
        q = q_all[rows]
        k = k_all[rows]
        k_bf = k.astype(_BF16)
        q_cross = q * qdec_ref[...]
        k_state = k * kdec_ref[...]
        for h in range(_RET_HEADS):
            cols = slice(h * _RET_DV, (h + 1) * _RET_DV)
            in_head = (lane // _RET_DK) == h
            vh = v_ref[rows, cols].astype(_BF16)
            qh = jnp.where(in_head, q, 0.0).astype(_BF16)
            scores = _dot_nt(qh, k_bf) * din_ref[h]
            inner = _dot(scores.astype(_BF16), vh)
            state = state_sc[h]
            cross = _dot(jnp.where(in_head, q_cross, 0.0).astype(_BF16), state.astype(_BF16))
            state_sc[h] = state * chunk_dec[h] + _dot_tn(
                jnp.where(in_head, k_state, 0.0).astype(_BF16), vh)
            out = inner + cross
            mu = jnp.mean(out, axis=-1, keepdims=True)
            oc = out - mu
            var = jnp.mean(oc * oc, axis=-1, keepdims=True)
            normed = oc * lax.rsqrt(var + _GN_EPS)
            y_ref[rows, cols] = (_silu(g_ref[rows, cols]) * normed).astype(y_ref.dtype)


def _retention_tables(seq):
    half = _RET_DK // 2
    inv_freq = _ROPE_BASE ** (-jnp.arange(half, dtype=_F32) / half)
    ang = jnp.arange(seq, dtype=jnp.int32).astype(_F32)[:, None] * inv_freq[None, :]
    cos, sin = jnp.cos(ang), jnp.sin(ang)
    cos_t = jnp.tile(jnp.concatenate([cos, cos], axis=1), (1, _RET_HEADS))
    sin_t = jnp.tile(jnp.concatenate([-sin, sin], axis=1), (1, _RET_HEADS))
    log_gamma = jnp.log1p(-jnp.exp2(-5.0 - jnp.arange(_RET_HEADS, dtype=_F32)))
    idx = jnp.arange(_BLOCK, dtype=_F32)
    rel = idx[:, None] - idx[None, :]
    causal = rel >= 0
    decay_in = jnp.where(causal[None],
                         jnp.exp(log_gamma[:, None, None] * jnp.where(causal, rel, 0.0)[None]), 0.0)
    q_dec = jnp.exp(log_gamma[:, None] * (idx + 1.0))
    k_dec = jnp.exp(log_gamma[:, None] * (_BLOCK - 1.0 - idx))
    q_dec_t = jnp.repeat(q_dec.T, _RET_DK, axis=1)
    k_dec_t = jnp.repeat(k_dec.T, _RET_DK, axis=1)
    return cos_t, sin_t, q_dec_t, k_dec_t, decay_in


def _retention(x16, w_r, bsz, seq):
    tokens, d_model = x16.shape
    tr = min(_TR_RET, seq)
    nstep = seq // tr
    cos_t, sin_t, q_dec_t, k_dec_t, decay_in = _retention_tables(seq)
    chunk_dec = tuple(float((1.0 - 2.0 ** (-5.0 - h)) ** _BLOCK) for h in range(_RET_HEADS))
    qk = _RET_HEADS * _RET_DK
    wv = _RET_HEADS * _RET_DV
    row = lambda b, c: b * nstep + c
    return pl.pallas_call(
        functools.partial(_retention_kernel, chunk_dec=chunk_dec),
        out_shape=jax.ShapeDtypeStruct((tokens, wv), _BF16),
        grid=(bsz, nstep),
        in_specs=[
            pl.BlockSpec((tr, d_model), lambda b, c: (row(b, c), 0)),
            pl.BlockSpec(w_r.shape, lambda b, c: (0, 0)),
            pl.BlockSpec((tr, qk), lambda b, c: (c, 0)),
            pl.BlockSpec((tr, qk), lambda b, c: (c, 0)),
            pl.BlockSpec((_BLOCK, qk), lambda b, c: (0, 0)),
            pl.BlockSpec((_BLOCK, qk), lambda b, c: (0, 0)),
            pl.BlockSpec((_RET_HEADS, _BLOCK, _BLOCK), lambda b, c: (0, 0, 0)),
        ],
        out_specs=pl.BlockSpec((tr, wv), lambda b, c: (row(b, c), 0)),
        scratch_shapes=[pltpu.VMEM((tr, wv), _BF16), pltpu.VMEM((tr, wv), _F32),
                        pltpu.VMEM((_RET_HEADS, qk, _RET_DV), _F32)],
        compiler_params=_params(2),
        name="retention",
    )(x16, w_r, cos_t, sin_t, q_dec_t, k_dec_t, decay_in)


_C_PARTS = 3
_TS_PREP = 512


def _fox_selectors():
    half = _LANES // 2
    one_row = _C_PARTS * _FOX_HEADS
    sel = np.zeros((_FOX_HEADS, _LANES, 2 * _LANES), np.float32)
    for h in range(_FOX_HEADS):
        base = half if h % 2 == 0 else 0
        for n in range(_C_PARTS):
            sel[h, n * _FOX_HEADS + h, base + n] = 1.0
            sel[h, one_row, base + _C_PARTS + n] = 1.0
            sel[h, one_row, _LANES + base + n] = 1.0
            sel[h, n * _FOX_HEADS + h, _LANES + base + _C_PARTS + n] = -1.0
    return jnp.asarray(sel, _BF16)


def _fox_prep_kernel(x16_ref, w_ref, b_ref, sel_ref, qa_ref, ka_ref, va_ref, fq_ref, fk_ref, fv_ref,
                     fl_ref, carry_sc, *, nblk):
    @pl.when(pl.program_id(1) == 0)
    def _():
        carry_sc[...] = jnp.zeros_like(carry_sc)

    x16 = x16_ref[...]
    width = _FOX_HEADS * _FOX_DH
    for n, dst in enumerate((fq_ref, fk_ref, fv_ref)):
        dst[...] = _dot(x16, w_ref[:, n * width:(n + 1) * width]).astype(_BF16)
    fl_ref[...] = _dot(x16, w_ref[:, 3 * width:3 * width + _LANES])

    r = lax.broadcasted_iota(jnp.int32, (_BLOCK, _LANES), 0)
    lane = lax.broadcasted_iota(jnp.int32, (_BLOCK, _LANES), 1)
    tri = (r >= lane).astype(_F32)
    half = _LANES // 2
    q_scale = jnp.asarray(_FOX_DH ** -0.5, _BF16)
    carry = carry_sc[...]
    packed = []
    for blk in range(nblk):
        rows = slice(blk * _BLOCK, (blk + 1) * _BLOCK)
        z = fl_ref[rows, :] + b_ref[...]
        log_f = jnp.minimum(z, 0.0) - jnp.log1p(jnp.exp(-jnp.abs(z)))
        cum = jnp.dot(tri, log_f, preferred_element_type=_F32,
                      precision=lax.Precision.HIGHEST) + carry
        carry = cum[_BLOCK - 1:_BLOCK, :]
        row = jnp.where(lane == _C_PARTS * _FOX_HEADS, 1.0, 0.0)
        rest = cum
        for n in range(_C_PARTS):
            piece = rest.astype(_BF16).astype(_F32)
            rest = rest - piece
            shifted = piece if n == 0 else pltpu.roll(piece, n * _FOX_HEADS, 1)
            row = jnp.where((lane >= n * _FOX_HEADS) & (lane < (n + 1) * _FOX_HEADS), shifted, row)
        packed.append(row.astype(_BF16))
    carry_sc[...] = carry
    packed = jnp.concatenate(packed, axis=0)
    lane_all = lax.broadcasted_iota(jnp.int32, (nblk * _BLOCK, _LANES), 1)
    for h in range(_FOX_HEADS):
        pair = slice((h // 2) * _LANES, (h // 2 + 1) * _LANES)
        in_head = (lane_all < half) if h % 2 == 0 else (lane_all >= half)
        extra = _dot(packed, sel_ref[h]).astype(_BF16)
        qa_ref[h] = jnp.where(in_head, fq_ref[:, pair] * q_scale, extra[:, :_LANES])
        ka_ref[h] = jnp.where(in_head, fk_ref[:, pair], extra[:, _LANES:])
        va_ref[h] = jnp.where(in_head, fv_ref[:, pair], jnp.ones((), _BF16))


def _fox_prep(x16, w_f, bias, bsz, seq):
    d_model = x16.shape[1]
    ts = min(_TS_PREP, seq)
    ns = seq // ts
    width = _FOX_HEADS * _FOX_DH
    sel = _fox_selectors()
    aug = jax.ShapeDtypeStruct((bsz, _FOX_HEADS, seq, _LANES), _BF16)
    aug_spec = pl.BlockSpec((None, _FOX_HEADS, ts, _LANES), lambda b, s: (b, 0, s, 0))
    return pl.pallas_call(
        functools.partial(_fox_prep_kernel, nblk=ts // _BLOCK),
        out_shape=(aug, aug, aug),
        grid=(bsz, ns),
        in_specs=[pl.BlockSpec((ts, d_model), lambda b, s: (b * ns + s, 0)),
                  pl.BlockSpec(w_f.shape, lambda b, s: (0, 0)),
                  pl.BlockSpec((1, _LANES), lambda b, s: (0, 0)),
                  pl.BlockSpec(sel.shape, lambda b, s: (0, 0, 0))],
        out_specs=(aug_spec, aug_spec, aug_spec),
        scratch_shapes=[pltpu.VMEM((ts, width), _BF16), pltpu.VMEM((ts, width), _BF16),
                        pltpu.VMEM((ts, width), _BF16), pltpu.VMEM((ts, _LANES), _F32),
                        pltpu.VMEM((1, _LANES), _F32)],
        compiler_params=_params(2),
        name="fox_prep",
    )(x16, w_f, bias, sel)


def _fox_kernel(qa_ref, ka_ref, va_ref, o_ref, m0_sc, m1_sc, acc0_sc, acc1_sc, *, tq):
    i = pl.program_id(2)
    m_scs = (m0_sc, m1_sc)
    acc_scs = (acc0_sc, acc1_sc)
    for s in range(2):
        m_scs[s][...] = jnp.full_like(m_scs[s], -jnp.inf)
        acc_scs[s][...] = jnp.zeros_like(acc_scs[s])

    row_id = lax.broadcasted_iota(jnp.int32, (tq, tq), 0)
    col_id = lax.broadcasted_iota(jnp.int32, (tq, tq), 1)
    keep = row_id >= col_id

    def block(j, masked):
        start = pl.multiple_of(j * tq, tq)
        logits = [_dot_nt(qa_ref[s], ka_ref[s, pl.ds(start, tq), :]) for s in range(2)]
        pexp, scale, m_new = [], [], []
        for s in range(2):
            lg = jnp.where(keep, logits[s], -jnp.inf) if masked else logits[s]
            m_old = m_scs[s][...]
            m_new.append(jnp.maximum(m_old, jnp.max(lg, axis=1, keepdims=True)))
            pexp.append(jnp.exp(lg - jnp.concatenate([m_new[s]] * (tq // _LANES), axis=1)))
            scale.append(jnp.exp(m_old - m_new[s]))
        pv = [_dot(pexp[s].astype(_BF16), va_ref[s, pl.ds(start, tq), :]) for s in range(2)]
        for s in range(2):
            acc_scs[s][...] = scale[s] * acc_scs[s][...] + pv[s]
            m_scs[s][...] = m_new[s]

    def body(j, carry):
        block(j, False)
        return carry

    lax.fori_loop(0, i, body, 0)
    block(i, True)
    half = _LANES // 2
    acc0 = acc0_sc[...]
    acc1 = acc1_sc[...]
    low = lax.broadcasted_iota(jnp.int32, (tq, _LANES), 1) < half
    out = jnp.where(low, acc0 / acc0[:, half:half + 1], acc1 / acc1[:, 0:1])
    o_ref[...] = out.astype(o_ref.dtype)


def _fox(qa, ka, va, bsz, seq):
    tq = min(_TQ, seq)
    nq = seq // tq
    npair = _FOX_HEADS // 2
    return pl.pallas_call(
        functools.partial(_fox_kernel, tq=tq),
        out_shape=jax.ShapeDtypeStruct((bsz * seq, _FOX_HEADS * _FOX_DH), _BF16),
        grid=(bsz, npair, nq),
        in_specs=[
            pl.BlockSpec((None, 2, tq, _LANES), lambda b, p, i: (b, p, i, 0)),
            pl.BlockSpec((None, 2, seq, _LANES), lambda b, p, i: (b, p, 0, 0)),
            pl.BlockSpec((None, 2, seq, _LANES), lambda b, p, i: (b, p, 0, 0)),
        ],
        out_specs=pl.BlockSpec((tq, _LANES), lambda b, p, i: (b * nq + i, p)),
        scratch_shapes=[pltpu.VMEM((tq, _LANES), _F32), pltpu.VMEM((tq, _LANES), _F32),
                        pltpu.VMEM((tq, _LANES), _F32), pltpu.VMEM((tq, _LANES), _F32)],
        compiler_params=_params(3),
        name="fox_attention",
    )(qa, ka, va)


def _gmlp_tile(gu, gv, w_ref, bst_ref, lng_ref, lnb_ref, yc_sc):
    u = _gelu_tanh(gu)
    v = _layer_norm(_gelu_tanh(gv), lng_ref[...], lnb_ref[...], _LN_EPS).astype(_BF16)
    r = lax.broadcasted_iota(jnp.int32, (_BLOCK, _BLOCK), 0)
    c = lax.broadcasted_iota(jnp.int32, (_BLOCK, _BLOCK), 1)
    causal = r >= c
    for g in range(_GM_GROUPS):
        cols = slice(g * _GM_DG, (g + 1) * _GM_DG)
        wg = jnp.where(causal, w_ref[g], 0.0).astype(_BF16)
        bias = bst_ref[:, g:g + 1]
        for ch in range(gu.shape[0] // _BLOCK):
            rows = slice(ch * _BLOCK, (ch + 1) * _BLOCK)
            sp = _dot(wg, v[rows, cols]) + bias
            yc_sc[rows, cols] = (u[rows, cols] * sp).astype(yc_sc.dtype)


def _mixer_out_kernel(x16_ref, ya_ref, yb_ref, wc_ref, gmw_ref, bst_ref, gmg_ref, gmb_ref, gbias_ref,
                      wbr_ref, wout_ref, x_ref, lng_ref, lnb_ref, *rest, alpha, d_model, with_router):
    if with_router:
        wr_ref, xo_ref, xo16_ref, route_ref, yc_sc = rest
    else:
        xo_ref, xo16_ref, yc_sc = rest
    x16 = x16_ref[...]
    gw = _GM_GROUPS * _GM_DG
    gu = _dot(x16, wc_ref[:, 0:gw])
    gv = _dot(x16, wc_ref[:, gw:2 * gw])
    _gmlp_tile(gu, gv, gmw_ref, bst_ref, gmg_ref, gmb_ref, yc_sc)
    merged = None
    for n, y_ref in enumerate((ya_ref, yb_ref, yc_sc)):
        cols = slice(2 * gw + n * d_model, 2 * gw + (n + 1) * d_model)
        z = _dot(x16, wc_ref[:, cols]) + gbias_ref[:, n * d_model:(n + 1) * d_model]
        gate = 1.0 / (1.0 + jnp.exp(-z))
        term = gate * _dot(y_ref[...], wbr_ref[n])
        merged = term if merged is None else merged + term
    h = _dot(merged.astype(_BF16), wout_ref[...])
    out = _layer_norm(alpha * x_ref[...] + h, lng_ref[...], lnb_ref[...], _LN_EPS)
    xo_ref[...] = out
    xo16_ref[...] = out.astype(_BF16)
    if with_router:
        route_ref[...] = _route_top2(out, wr_ref)


def _mixer_out(x16, y_a, y_b, w_c, gm_w_s, gm_b_s, gm_ln_g, gm_ln_b, gate_b, w_branch, w_out, x,
               ln_g, ln_b, w_router, *, alpha, tm):
    tokens, d_model = x.shape
    bw = y_a.shape[1]
    gw = _GM_GROUPS * _GM_DG
    row = lambda i: (i, 0)
    const2 = lambda i: (0, 0)
    const3 = lambda i: (0, 0, 0)
    with_router = w_router is not None
    in_specs = [
        pl.BlockSpec((tm, d_model), row),
        pl.BlockSpec((tm, bw), row), pl.BlockSpec((tm, bw), row),
        pl.BlockSpec(w_c.shape, const2),
        pl.BlockSpec((_GM_GROUPS, _BLOCK, _BLOCK), const3),
        pl.BlockSpec((_BLOCK, _GM_GROUPS), const2),
        pl.BlockSpec((1, gw), const2), pl.BlockSpec((1, gw), const2),
        pl.BlockSpec((1, 3 * d_model), const2),
        pl.BlockSpec((3, bw, d_model), const3),
        pl.BlockSpec((d_model, d_model), const2),
        pl.BlockSpec((tm, d_model), row),
        pl.BlockSpec((1, d_model), const2), pl.BlockSpec((1, d_model), const2),
    ]
    args = [x16, y_a, y_b, w_c, gm_w_s, gm_b_s.T, gm_ln_g.reshape(1, gw), gm_ln_b.reshape(1, gw),
            gate_b.reshape(1, -1), w_branch.astype(_BF16), w_out.astype(_BF16), x,
            ln_g.reshape(1, -1), ln_b.reshape(1, -1)]
    out_shape = [jax.ShapeDtypeStruct((tokens, d_model), _F32),
                 jax.ShapeDtypeStruct((tokens, d_model), _BF16)]
    out_specs = [pl.BlockSpec((tm, d_model), row), pl.BlockSpec((tm, d_model), row)]
    if with_router:
        in_specs.append(pl.BlockSpec((d_model, 2 * _LANES), const2))
        args.append(_router_weights(w_router))
        out_shape.append(jax.ShapeDtypeStruct((tokens, _LANES), _F32))
        out_specs.append(pl.BlockSpec((tm, _LANES), row))
    outs = pl.pallas_call(
        functools.partial(_mixer_out_kernel, alpha=alpha, d_model=d_model, with_router=with_router),
        out_shape=tuple(out_shape),
        grid=(tokens // tm,),
        in_specs=in_specs,
        out_specs=tuple(out_specs),
        scratch_shapes=[pltpu.VMEM((tm, gw), _BF16)],
        compiler_params=_params(1),
        name="mixer_out",
    )(*args)
    return outs if with_router else (*outs, None)


def _weights_changed(te_ref, i):
    return jnp.logical_or(i == 0, te_ref[i] != te_ref[jnp.maximum(i - 1, 0)])


def _up_kernel(te_ref, nused_ref, x_ref, wa_ref, wb_ref, h_ref, wa_sc, wb_sc):
    i = pl.program_id(1)

    @pl.when(_weights_changed(te_ref, i))
    def _():
        wa_sc[...] = wa_ref[...].astype(_BF16)
        wb_sc[...] = wb_ref[...].astype(_BF16)

    @pl.when(i < nused_ref[0])
    def _():
        x = x_ref[...].astype(_BF16)
        a = _dot(x, wa_sc[...])
        b = _dot(x, wb_sc[...])
        h_ref[...] = (_silu(a) * b).astype(h_ref.dtype)

    @pl.when(i >= nused_ref[0])
    def _():
        h_ref[...] = jnp.zeros_like(h_ref)


def _swiglu_up(x16, w_up, te, nused, *, tm, tn):
    rows, k = x16.shape
    d_ff = w_up.shape[2] // 2
    nb = d_ff // tn
    grid_spec = pltpu.PrefetchScalarGridSpec(
        num_scalar_prefetch=2,
        grid=(nb, rows // tm),
        in_specs=[
            pl.BlockSpec((tm, k), lambda j, i, te, nu: (i, 0)),
            pl.BlockSpec((None, k, tn), lambda j, i, te, nu: (te[i], 0, j)),
            pl.BlockSpec((None, k, tn), lambda j, i, te, nu: (te[i], 0, j + nb)),
        ],
        out_specs=pl.BlockSpec((tm, tn), lambda j, i, te, nu: (i, j)),
        scratch_shapes=[pltpu.VMEM((k, tn), _BF16), pltpu.VMEM((k, tn), _BF16)],
    )
    return pl.pallas_call(
        _up_kernel,
        out_shape=jax.ShapeDtypeStruct((rows, d_ff), _BF16),
        grid_spec=grid_spec,
        compiler_params=_params(2),
        name="swiglu_up",
    )(te, nused, x16, w_up, w_up)


def _down_kernel(te_ref, nused_ref, h_ref, w_ref, y_ref, w_sc):
    i = pl.program_id(1)

    @pl.when(_weights_changed(te_ref, i))
    def _():
        w_sc[...] = w_ref[...].astype(_BF16)

    @pl.when(i < nused_ref[0])
    def _():
        y_ref[...] = _dot(h_ref[...], w_sc[...])

    @pl.when(i >= nused_ref[0])
    def _():
        y_ref[...] = jnp.zeros_like(y_ref)


def _expert_down(h16, w_down, te, nused, *, tm, tn):
    rows, k = h16.shape
    n = w_down.shape[2]
    grid_spec = pltpu.PrefetchScalarGridSpec(
        num_scalar_prefetch=2,
        grid=(n // tn, rows // tm),
        in_specs=[
            pl.BlockSpec((tm, k), lambda j, i, te, nu: (i, 0)),
            pl.BlockSpec((None, k, tn), lambda j, i, te, nu: (te[i], 0, j)),
        ],
        out_specs=pl.BlockSpec((tm, tn), lambda j, i, te, nu: (i, j)),
        scratch_shapes=[pltpu.VMEM((k, tn), _BF16)],
    )
    return pl.pallas_call(
        _down_kernel,
        out_shape=jax.ShapeDtypeStruct((rows, n), _F32),
        grid_spec=grid_spec,
        compiler_params=_params(2),
        name="expert_down",
    )(te, nused, h16, w_down)


def _dense_ffn_kernel(x16_ref, wup_ref, wdn_ref, x_ref, lng_ref, lnb_ref, xo_ref, xo16_ref,
                      *, alpha, d_ff, chunks):
    x16 = x16_ref[...]
    f = None
    for c0, c1 in chunks:
        a = _dot(x16, wup_ref[:, c0:c1])
        b = _dot(x16, wup_ref[:, d_ff + c0:d_ff + c1])
        part = _dot((_silu(a) * b).astype(_BF16), wdn_ref[c0:c1, :])
        f = part if f is None else f + part
    out = _layer_norm(alpha * x_ref[...] + f, lng_ref[...], lnb_ref[...], _LN_EPS)
    xo_ref[...] = out
    xo16_ref[...] = out.astype(_BF16)


def _dense_ffn_ln(x, x16, w_up, w_down, ln_g, ln_b, *, alpha, tm):
    tokens, d_model = x.shape
    d_ff = w_down.shape[0]
    split = (d_ff // (2 * _MXU_COLS)) * _MXU_COLS
    chunks = ((0, split), (split, d_ff)) if 0 < split < d_ff else ((0, d_ff),)
    row = lambda i: (i, 0)
    const2 = lambda i: (0, 0)
    return pl.pallas_call(
        functools.partial(_dense_ffn_kernel, alpha=alpha, d_ff=d_ff, chunks=chunks),
        out_shape=(jax.ShapeDtypeStruct((tokens, d_model), _F32),
                   jax.ShapeDtypeStruct((tokens, d_model), _BF16)),
        grid=(tokens // tm,),
        in_specs=[pl.BlockSpec((tm, d_model), row),
                  pl.BlockSpec((d_model, 2 * d_ff), const2),
                  pl.BlockSpec((d_ff, d_model), const2),
                  pl.BlockSpec((tm, d_model), row),
                  pl.BlockSpec((1, d_model), const2), pl.BlockSpec((1, d_model), const2)],
        out_specs=(pl.BlockSpec((tm, d_model), row), pl.BlockSpec((tm, d_model), row)),
        compiler_params=_params(1),
        name="dense_ffn_ln",
    )(x16, w_up.astype(_BF16), w_down.astype(_BF16), x, ln_g.reshape(1, -1), ln_b.reshape(1, -1))


def _route_top2(x, wr_ref):
    x_hi = x.astype(_BF16)
    x_lo = (x - x_hi.astype(_F32)).astype(_BF16)
    both = _dot(x_hi, wr_ref[...])
    logits = both[:, :_LANES] + (both[:, _LANES:] + _dot(x_lo, wr_ref[:, :_LANES]))
    lane = lax.broadcasted_iota(jnp.int32, logits.shape, 1).astype(_F32)
    lg = jnp.where(lane < _N_EXPERTS, logits, -jnp.inf)
    m1 = jnp.max(lg, axis=1, keepdims=True)
    i1 = jnp.min(jnp.where(lg == m1, lane, float(_LANES)), axis=1, keepdims=True)
    lg2 = jnp.where(lane == i1, -jnp.inf, lg)
    m2 = jnp.max(lg2, axis=1, keepdims=True)
    i2 = jnp.min(jnp.where(lg2 == m2, lane, float(_LANES)), axis=1, keepdims=True)
    e2 = jnp.exp(m2 - m1)
    den = 1.0 + e2
    w1 = 1.0 / den
    w2 = e2 / den
    return jnp.where(lane == 0, i1, jnp.where(lane == 1, i2,
                     jnp.where(lane == 2, w1, jnp.where(lane == 3, w2, 0.0))))


def _router_weights(w_router):
    d_model = w_router.shape[0]
    wr = jnp.zeros((d_model, _LANES), _F32).at[:, :_N_EXPERTS].set(w_router)
    wr_hi = wr.astype(_BF16)
    wr_lo = (wr - wr_hi.astype(_F32)).astype(_BF16)
    return jnp.concatenate([wr_hi, wr_lo], axis=1)


_ROW_UNROLL = 8


def _dispatch_kernel(zstart_ref, zvalid_ref, d0_ref, d1_ref, x_ref, xs_hbm, zero_sc, stage_sc, sem, zsem,
                     *, tt, tm, nsteps):
    @pl.when(pl.program_id(0) == 0)
    def _():
        zero_sc[...] = jnp.zeros_like(zero_sc)

        def zero_copy(n):
            start = pl.multiple_of(zstart_ref[n], tm)
            return pltpu.make_async_copy(zero_sc, xs_hbm.at[pl.ds(start, tm), :], zsem)

        for n in range(2 * _N_EXPERTS):
            @pl.when(zvalid_ref[n] > 0)
            def _():
                zero_copy(n).start()
        for n in range(2 * _N_EXPERTS):
            @pl.when(zvalid_ref[n] > 0)
            def _():
                zero_copy(n).wait()

    i = pl.program_id(0)
    slot = i % 2
    stage_sc[slot] = x_ref[...]

    def issue(g, carry):
        for u in range(_ROW_UNROLL):
            r = g * _ROW_UNROLL + u
            for d_ref in (d0_ref, d1_ref):
                pltpu.make_async_copy(stage_sc.at[slot, pl.ds(r, 1), :],
                                      xs_hbm.at[pl.ds(d_ref[0, 0, r], 1), :],
                                      sem.at[slot]).start(priority=u % 2)
        return carry

    def drain(s):
        for _ in range(2):
            pltpu.make_async_copy(stage_sc.at[s], xs_hbm.at[pl.ds(0, tt), :], sem.at[s]).wait()

    lax.fori_loop(0, tt // _ROW_UNROLL, issue, 0)

    @pl.when(i > 0)
    def _():
        drain(1 - slot)

    @pl.when(i == nsteps - 1)
    def _():
        drain(slot)


def _dispatch_rows(x, d0, d1, zstart, zvalid, *, rows, tt, tm):
    tokens, d_model = x.shape
    smem = lambda: pl.BlockSpec((1, 1, tt), lambda i, zs, zv: (i, 0, 0), memory_space=pltpu.SMEM)
    grid_spec = pltpu.PrefetchScalarGridSpec(
        num_scalar_prefetch=2,
        grid=(tokens // tt,),
        in_specs=[smem(), smem(), pl.BlockSpec((tt, d_model), lambda i, zs, zv: (i, 0))],
        out_specs=pl.BlockSpec(memory_space=pl.ANY),
        scratch_shapes=[pltpu.VMEM((tm, d_model), _F32), pltpu.VMEM((2, tt, d_model), _F32),
                        pltpu.SemaphoreType.DMA((2,)), pltpu.SemaphoreType.DMA],
    )
    return pl.pallas_call(
        functools.partial(_dispatch_kernel, tt=tt, tm=tm, nsteps=tokens // tt),
        out_shape=jax.ShapeDtypeStruct((rows, d_model), _F32),
        grid_spec=grid_spec,
        compiler_params=_params(1),
        name="moe_dispatch",
    )(zstart, zvalid, d0, d1, x)


def _combine_kernel(d0_ref, d1_ref, n0_ref, n1_ref, y_hbm, route_ref, x_ref, lng_ref, lnb_ref,
                    xo_ref, xo16_ref, buf, sem, *, alpha, tm, nsteps):
    i = pl.program_id(0)
    slot = i % 2

    def gather(dst_slot, a_ref, b_ref):
        def issue(g, carry):
            for u in range(_ROW_UNROLL):
                r = g * _ROW_UNROLL + u
                for s, d_ref in enumerate((a_ref, b_ref)):
                    pltpu.make_async_copy(y_hbm.at[pl.ds(d_ref[0, 0, r], 1), :],
                                          buf.at[dst_slot, s, pl.ds(r, 1), :],
                                          sem.at[dst_slot]).start(priority=u % 2)
            return carry

        lax.fori_loop(0, tm // _ROW_UNROLL, issue, 0)

    @pl.when(i == 0)
    def _():
        gather(0, d0_ref, d1_ref)

    @pl.when(i + 1 < nsteps)
    def _():
        gather(1 - slot, n0_ref, n1_ref)

    for s in range(2):
        pltpu.make_async_copy(y_hbm.at[pl.ds(0, tm), :], buf.at[slot, s], sem.at[slot]).wait()
    route = route_ref[...]
    f = route[:, 2:3] * buf[slot, 0] + route[:, 3:4] * buf[slot, 1]
    out = _layer_norm(alpha * x_ref[...] + f, lng_ref[...], lnb_ref[...], _LN_EPS)
    xo_ref[...] = out
    xo16_ref[...] = out.astype(_BF16)


def _combine_ln(ys, d0, d1, route, x, ln_g, ln_b, *, alpha, tm):
    tokens, d_model = x.shape
    nsteps = tokens // tm
    row = lambda i: (i, 0)
    const2 = lambda i: (0, 0)
    cur = lambda: pl.BlockSpec((1, 1, tm), lambda i: (i, 0, 0), memory_space=pltpu.SMEM)
    nxt = lambda: pl.BlockSpec((1, 1, tm), lambda i: (jnp.minimum(i + 1, nsteps - 1), 0, 0),
                               memory_space=pltpu.SMEM)
    return pl.pallas_call(
        functools.partial(_combine_kernel, alpha=alpha, tm=tm, nsteps=nsteps),
        out_shape=(jax.ShapeDtypeStruct((tokens, d_model), _F32),
                   jax.ShapeDtypeStruct((tokens, d_model), _BF16)),
        grid=(nsteps,),
        in_specs=[cur(), cur(), nxt(), nxt(), pl.BlockSpec(memory_space=pl.ANY),
                  pl.BlockSpec((tm, _LANES), row), pl.BlockSpec((tm, d_model), row),
                  pl.BlockSpec((1, d_model), const2), pl.BlockSpec((1, d_model), const2)],
        out_specs=(pl.BlockSpec((tm, d_model), row), pl.BlockSpec((tm, d_model), row)),
        scratch_shapes=[pltpu.VMEM((2, 2, tm, d_model), _F32), pltpu.SemaphoreType.DMA((2,))],
        compiler_params=_params(1),
        name="moe_combine_ln",
    )(d0, d1, d0, d1, ys, route, x, ln_g.reshape(1, -1), ln_b.reshape(1, -1))


def _routing_plan(route, *, tm):
    tokens = route.shape[0]
    pairs = 2 * tokens
    ntiles = pairs // tm + _N_EXPERTS
    flat_e = route[:, 0:2].astype(jnp.int32).reshape(pairs)
    onehot = (flat_e[:, None] == jnp.arange(_N_EXPERTS, dtype=jnp.int32)[None, :]).astype(jnp.int32)
    csum = jnp.cumsum(onehot, axis=0)
    counts = csum[-1]
    padded = ((counts + tm - 1) // tm) * tm
    ends = jnp.cumsum(padded)
    starts = ends - padded
    dest = jnp.sum(onehot * (starts[None, :] + csum - 1), axis=1)
    tile_start = jnp.arange(ntiles, dtype=jnp.int32) * tm
    te = jnp.minimum(jnp.sum((ends[None, :] <= tile_start[:, None]).astype(jnp.int32), axis=1),
                     _N_EXPERTS - 1)
    nused = (ends[-1] // tm).reshape(1)
    tail = ends[-1] + jnp.arange(_N_EXPERTS, dtype=jnp.int32) * tm
    zstart = jnp.concatenate([jnp.maximum(ends - tm, 0), jnp.minimum(tail, (ntiles - 1) * tm)])
    zvalid = jnp.concatenate([padded > 0, tail < ntiles * tm]).astype(jnp.int32)
    dest2 = dest.reshape(tokens, 2)
    return ntiles * tm, te, nused, dest2[:, 0], dest2[:, 1], zstart, zvalid


def _moe_ffn_ln(x, route, w_up, w_down, ln_g, ln_b, *, alpha, tm):
    tokens = x.shape[0]
    tc = min(256, tokens)
    rows, te, nused, d0, d1, zstart, zvalid = _routing_plan(route, tm=tm)
    d0 = d0.reshape(tokens // tc, 1, tc)
    d1 = d1.reshape(tokens // tc, 1, tc)
    xs = _dispatch_rows(x, d0, d1, zstart, zvalid, rows=rows, tt=tc, tm=tm)
    h = _swiglu_up(xs, w_up, te, nused, tm=tm, tn=w_up.shape[2] // 4)
    ys = _expert_down(h, w_down, te, nused, tm=tm, tn=w_down.shape[2])
    return _combine_ln(ys, d0, d1, route, x, ln_g, ln_b, alpha=alpha, tm=tc)


def _mixer_ln(x, x16, w_in, fox_b_f, gate_b, gm_w_s, gm_b_s, gm_ln_g, gm_ln_b, w_branch, w_out,
              ln_g, ln_b, w_router, *, bsz, seq, alpha, tm):
    d_model = x.shape[1]
    n_r = 2 * _RET_HEADS * _RET_DK + 2 * _RET_HEADS * _RET_DV
    n_f = 3 * _FOX_HEADS * _FOX_DH
    n_fl = _FOX_HEADS
    w_r = w_in[:, :n_r].astype(_BF16)
    w_f = jnp.concatenate([w_in[:, n_r:n_r + n_f + n_fl],
                           jnp.zeros((d_model, _LANES - n_fl), w_in.dtype)], axis=1).astype(_BF16)
    w_c = w_in[:, n_r + n_f + n_fl:].astype(_BF16)

    y_a = _retention(x16, w_r, bsz, seq)
    bias = jnp.zeros((1, _LANES), _F32).at[0, :n_fl].set(fox_b_f)
    qa, ka, va = _fox_prep(x16, w_f, bias, bsz, seq)
    y_b = _fox(qa, ka, va, bsz, seq)
    return _mixer_out(x16, y_a, y_b, w_c, gm_w_s, gm_b_s, gm_ln_g, gm_ln_b, gate_b, w_branch, w_out,
                      x, ln_g, ln_b, w_router, alpha=alpha, tm=tm)


def kernel(x, w_in, fox_b_f, gate_b, gm_w_s, gm_b_s, gm_ln_g, gm_ln_b, w_branch, w_out, ln_g, ln_b,
           dense_w_up, dense_w_down, moe_router, moe_w_up, moe_w_down):
    bsz, seq, d_model = x.shape
    depth = w_in.shape[0]
    alpha = (2 * depth) ** 0.25
    tokens = bsz * seq
    tm = min(_TM, tokens)
    xf = x.reshape(tokens, d_model)
    x16 = xf.astype(_BF16)
    for l in range(depth):
        dense = l % 2 == 0
        xf, x16, route = _mixer_ln(
            xf, x16, w_in[l], fox_b_f[l], gate_b[l], gm_w_s[l], gm_b_s[l], gm_ln_g[l], gm_ln_b[l],
            w_branch[l], w_out[l], ln_g[l, 0], ln_b[l, 0], None if dense else moe_router[l // 2],
            bsz=bsz, seq=seq, alpha=alpha, tm=tm)
        if dense:
            xf, x16 = _dense_ffn_ln(xf, x16, dense_w_up[l // 2], dense_w_down[l // 2],
                                    ln_g[l, 1], ln_b[l, 1], alpha=alpha, tm=tm)
        else:
            xf, x16 = _moe_ffn_ln(xf, route, moe_w_up[l // 2], moe_w_down[l // 2],
                                  ln_g[l, 1], ln_b[l, 1], alpha=alpha, tm=tm)
    return xf.reshape(bsz, seq, d_model)
```

```python
import functools
import math

import jax
import jax.numpy as jnp
import numpy as np
from jax import lax
from jax.experimental import pallas as pl
from jax.experimental.pallas import tpu as pltpu

_BF16 = jnp.bfloat16
_F32 = jnp.float32

_BLOCK = 128
_RET_HEADS, _RET_DK, _RET_DV = 4, 64, 128
_FOX_HEADS, _FOX_DH = 8, 64
_GM_GROUPS, _GM_DG = 4, 128
_N_EXPERTS = 8
_ROPE_BASE = 10000.0
_LN_EPS = 1e-5
_GN_EPS = 1e-6

_LANES = 128
_MXU_COLS = 256
_VMEM_PHYSICAL_BYTES = 64 * 1024 * 1024
_VMEM_LIMIT_BYTES = (_VMEM_PHYSICAL_BYTES * 7) // 8

_TM = 512
_TQ = 512
_TR_RET = 512


def _params(n_axes):
    return pltpu.CompilerParams(
        dimension_semantics=("arbitrary",) * n_axes,
        vmem_limit_bytes=_VMEM_LIMIT_BYTES,
    )


def _layer_norm(x, g, b, eps):
    mu = jnp.mean(x, axis=-1, keepdims=True)
    xc = x - mu
    var = jnp.mean(xc * xc, axis=-1, keepdims=True)
    return xc * lax.rsqrt(var + eps) * g + b


def _silu(x):
    return x * (1.0 / (1.0 + jnp.exp(-x)))


def _gelu_tanh(x):
    c = math.sqrt(2.0 / math.pi)
    return x * (0.5 * (1.0 + jnp.tanh(c * (x + 0.044715 * (x * x * x)))))


def _dot(a, b):
    return jnp.dot(a, b, preferred_element_type=_F32)


def _dot_nt(a, b):
    return lax.dot_general(a, b, (((1,), (1,)), ((), ())), preferred_element_type=_F32)


def _dot_tn(a, b):
    return lax.dot_general(a, b, (((0,), (0,)), ((), ())), preferred_element_type=_F32)


def _matmul_kernel(x_ref, w_ref, o_ref):
    o_ref[...] = _dot(x_ref[...], w_ref[...]).astype(o_ref.dtype)


def _matmul(x, w, *, tm, tn, out_dtype, name):
    m, k = x.shape
    n = w.shape[1]
    return pl.pallas_call(
        _matmul_kernel,
        out_shape=jax.ShapeDtypeStruct((m, n), out_dtype),
        grid=(n // tn, m // tm),
        in_specs=[pl.BlockSpec((tm, k), lambda j, i: (i, 0)),
                  pl.BlockSpec((k, tn), lambda j, i: (0, j))],
        out_specs=pl.BlockSpec((tm, tn), lambda j, i: (i, j)),
        compiler_params=_params(2),
        name=name,
    )(x, w)


def _retention_kernel(x16_ref, w_ref, cos_ref, sin_ref, qdec_ref, kdec_ref, din_ref,
                      y_ref, v_sc, g_sc, state_sc, *, chunk_dec):
    c = pl.program_id(1)

    @pl.when(c == 0)
    def _():
        state_sc[...] = jnp.zeros_like(state_sc)

    width = _RET_HEADS * _RET_DK
    wv = _RET_HEADS * _RET_DV
    rows_total = x16_ref.shape[0]
    lane_all = lax.broadcasted_iota(jnp.int32, (rows_total, width), 1)
    first_half = (lane_all % _RET_DK) < (_RET_DK // 2)

    def rotary(x):
        partner = jnp.where(first_half,
                            pltpu.roll(x, width - _RET_DK // 2, 1),
                            pltpu.roll(x, _RET_DK // 2, 1))
        return x * cos_ref[...] + partner * sin_ref[...]

    x16 = x16_ref[...]
    q_all = rotary(_dot(x16, w_ref[:, 0:width]))
    k_all = rotary(_dot(x16, w_ref[:, width:2 * width])) * (_RET_DK ** -0.5)
    v_sc[...] = _dot(x16, w_ref[:, 2 * width:2 * width + wv]).astype(_BF16)
    g_sc[...] = _dot(x16, w_ref[:, 2 * width + wv:2 * width + 2 * wv])
    v_ref, g_ref = v_sc, g_sc
    lane = lax.broadcasted_iota(jnp.int32, (_BLOCK, width), 1)
    for cc in range(rows_total // _BLOCK):
        rows = slice(cc * _BLOCK, (cc + 1) * _BLOCK)
        q = q_all[rows]
        k = k_all[rows]
        k_bf = k.astype(_BF16)
        q_cross = q * qdec_ref[...]
        k_state = k * kdec_ref[...]
        for h in range(_RET_HEADS):
            cols = slice(h * _RET_DV, (h + 1) * _RET_DV)
            in_head = (lane // _RET_DK) == h
            vh = v_ref[rows, cols].astype(_BF16)
            qh = jnp.where(in_head, q, 0.0).astype(_BF16)
            scores = _dot_nt(qh, k_bf) * din_ref[h]
            inner = _dot(scores.astype(_BF16), vh)
            state = state_sc[h]
            cross = _dot(jnp.where(in_head, q_cross, 0.0).astype(_BF16), state.astype(_BF16))
            state_sc[h] = state * chunk_dec[h] + _dot_tn(
                jnp.where(in_head, k_state, 0.0).astype(_BF16), vh)
            out = inner + cross
            mu = jnp.mean(out, axis=-1, keepdims=True)
            oc = out - mu
            var = jnp.mean(oc * oc, axis=-1, keepdims=True)
            normed = oc * lax.rsqrt(var + _GN_EPS)
            y_ref[rows, cols] = (_silu(g_ref[rows, cols]) * normed).astype(y_ref.dtype)


def _retention_tables(seq):
    half = _RET_DK // 2
    inv_freq = _ROPE_BASE ** (-jnp.arange(half, dtype=_F32) / half)
    ang = jnp.arange(seq, dtype=jnp.int32).astype(_F32)[:, None] * inv_freq[None, :]
    cos, sin = jnp.cos(ang), jnp.sin(ang)
    cos_t = jnp.tile(jnp.concatenate([cos, cos], axis=1), (1, _RET_HEADS))
    sin_t = jnp.tile(jnp.concatenate([-sin, sin], axis=1), (1, _RET_HEADS))
    log_gamma = jnp.log1p(-jnp.exp2(-5.0 - jnp.arange(_RET_HEADS, dtype=_F32)))
    idx = jnp.arange(_BLOCK, dtype=_F32)
    rel = idx[:, None] - idx[None, :]
    causal = rel >= 0
    decay_in = jnp.where(causal[None],
                         jnp.exp(log_gamma[:, None, None] * jnp.where(causal, rel, 0.0)[None]), 0.0)
    q_dec = jnp.exp(log_gamma[:, None] * (idx + 1.0))
    k_dec = jnp.exp(log_gamma[:, None] * (_BLOCK - 1.0 - idx))
    q_dec_t = jnp.repeat(q_dec.T, _RET_DK, axis=1)
    k_dec_t = jnp.repeat(k_dec.T, _RET_DK, axis=1)
    return cos_t, sin_t, q_dec_t, k_dec_t, decay_in


def _retention(x16, w_r, bsz, seq):
    tokens, d_model = x16.shape
    tr = min(_TR_RET, seq)
    nstep = seq // tr
    cos_t, sin_t, q_dec_t, k_dec_t, decay_in = _retention_tables(seq)
    chunk_dec = tuple(float((1.0 - 2.0 ** (-5.0 - h)) ** _BLOCK) for h in range(_RET_HEADS))
    qk = _RET_HEADS * _RET_DK
    wv = _RET_HEADS * _RET_DV
    row = lambda b, c: b * nstep + c
    return pl.pallas_call(
        functools.partial(_retention_kernel, chunk_dec=chunk_dec),
        out_shape=jax.ShapeDtypeStruct((tokens, wv), _BF16),
        grid=(bsz, nstep),
        in_specs=[
            pl.BlockSpec((tr, d_model), lambda b, c: (row(b, c), 0)),
            pl.BlockSpec(w_r.shape, lambda b, c: (0, 0)),
            pl.BlockSpec((tr, qk), lambda b, c: (c, 0)),
            pl.BlockSpec((tr, qk), lambda b, c: (c, 0)),
            pl.BlockSpec((_BLOCK, qk), lambda b, c: (0, 0)),
            pl.BlockSpec((_BLOCK, qk), lambda b, c: (0, 0)),
            pl.BlockSpec((_RET_HEADS, _BLOCK, _BLOCK), lambda b, c: (0, 0, 0)),
        ],
        out_specs=pl.BlockSpec((tr, wv), lambda b, c: (row(b, c), 0)),
        scratch_shapes=[pltpu.VMEM((tr, wv), _BF16), pltpu.VMEM((tr, wv), _F32),
                        pltpu.VMEM((_RET_HEADS, qk, _RET_DV), _F32)],
        compiler_params=_params(2),
        name="retention",
    )(x16, w_r, cos_t, sin_t, q_dec_t, k_dec_t, decay_in)


_C_PARTS = 3
_TS_PREP = 512


def _fox_selectors():
    half = _LANES // 2
    one_row = _C_PARTS * _FOX_HEADS
    sel = np.zeros((_FOX_HEADS, _LANES, 2 * _LANES), np.float32)
    for h in range(_FOX_HEADS):
        base = half if h % 2 == 0 else 0
        for n in range(_C_PARTS):
            sel[h, n * _FOX_HEADS + h, base + n] = 1.0
            sel[h, one_row, base + _C_PARTS + n] = 1.0
            sel[h, one_row, _LANES + base + n] = 1.0
            sel[h, n * _FOX_HEADS + h, _LANES + base + _C_PARTS + n] = -1.0
    return jnp.asarray(sel, _BF16)


def _fox_prep_kernel(x16_ref, w_ref, b_ref, sel_ref, qa_ref, ka_ref, va_ref, fq_ref, fk_ref, fv_ref,
                     fl_ref, carry_sc, *, nblk):
    @pl.when(pl.program_id(1) == 0)
    def _():
        carry_sc[...] = jnp.zeros_like(carry_sc)

    x16 = x16_ref[...]
    width = _FOX_HEADS * _FOX_DH
    for n, dst in enumerate((fq_ref, fk_ref, fv_ref)):
        dst[...] = _dot(x16, w_ref[:, n * width:(n + 1) * width]).astype(_BF16)
    fl_ref[...] = _dot(x16, w_ref[:, 3 * width:3 * width + _LANES])

    r = lax.broadcasted_iota(jnp.int32, (_BLOCK, _LANES), 0)
    lane = lax.broadcasted_iota(jnp.int32, (_BLOCK, _LANES), 1)
    tri = (r >= lane).astype(_F32)
    half = _LANES // 2
    q_scale = jnp.asarray(_FOX_DH ** -0.5, _BF16)
    carry = carry_sc[...]
    packed = []
    for blk in range(nblk):
        rows = slice(blk * _BLOCK, (blk + 1) * _BLOCK)
        z = fl_ref[rows, :] + b_ref[...]
        log_f = jnp.minimum(z, 0.0) - jnp.log1p(jnp.exp(-jnp.abs(z)))
        cum = jnp.dot(tri, log_f, preferred_element_type=_F32,
                      precision=lax.Precision.HIGHEST) + carry
        carry = cum[_BLOCK - 1:_BLOCK, :]
        row = jnp.where(lane == _C_PARTS * _FOX_HEADS, 1.0, 0.0)
        rest = cum
        for n in range(_C_PARTS):
            piece = rest.astype(_BF16).astype(_F32)
            rest = rest - piece
            shifted = piece if n == 0 else pltpu.roll(piece, n * _FOX_HEADS, 1)
            row = jnp.where((lane >= n * _FOX_HEADS) & (lane < (n + 1) * _FOX_HEADS), shifted, row)
        packed.append(row.astype(_BF16))
    carry_sc[...] = carry
    packed = jnp.concatenate(packed, axis=0)
    lane_all = lax.broadcasted_iota(jnp.int32, (nblk * _BLOCK, _LANES), 1)
    for h in range(_FOX_HEADS):
        pair = slice((h // 2) * _LANES, (h // 2 + 1) * _LANES)
        in_head = (lane_all < half) if h % 2 == 0 else (lane_all >= half)
        extra = _dot(packed, sel_ref[h]).astype(_BF16)
        qa_ref[h] = jnp.where(in_head, fq_ref[:, pair] * q_scale, extra[:, :_LANES])
        ka_ref[h] = jnp.where(in_head, fk_ref[:, pair], extra[:, _LANES:])
        va_ref[h] = jnp.where(in_head, fv_ref[:, pair], jnp.ones((), _BF16))


def _fox_prep(x16, w_f, bias, bsz, seq):
    d_model = x16.shape[1]
    ts = min(_TS_PREP, seq)
    ns = seq // ts
    width = _FOX_HEADS * _FOX_DH
    sel = _fox_selectors()
    aug = jax.ShapeDtypeStruct((bsz, _FOX_HEADS, seq, _LANES), _BF16)
    aug_spec = pl.BlockSpec((None, _FOX_HEADS, ts, _LANES), lambda b, s: (b, 0, s, 0))
    return pl.pallas_call(
        functools.partial(_fox_prep_kernel, nblk=ts // _BLOCK),
        out_shape=(aug, aug, aug),
        grid=(bsz, ns),
        in_specs=[pl.BlockSpec((ts, d_model), lambda b, s: (b * ns + s, 0)),
                  pl.BlockSpec(w_f.shape, lambda b, s: (0, 0)),
                  pl.BlockSpec((1, _LANES), lambda b, s: (0, 0)),
                  pl.BlockSpec(sel.shape, lambda b, s: (0, 0, 0))],
        out_specs=(aug_spec, aug_spec, aug_spec),
        scratch_shapes=[pltpu.VMEM((ts, width), _BF16), pltpu.VMEM((ts, width), _BF16),
                        pltpu.VMEM((ts, width), _BF16), pltpu.VMEM((ts, _LANES), _F32),
                        pltpu.VMEM((1, _LANES), _F32)],
        compiler_params=_params(2),
        name="fox_prep",
    )(x16, w_f, bias, sel)


def _fox_kernel(qa_ref, ka_ref, va_ref, o_ref, m0_sc, m1_sc, acc0_sc, acc1_sc, *, tq):
    i = pl.program_id(2)
    m_scs = (m0_sc, m1_sc)
    acc_scs = (acc0_sc, acc1_sc)
    for s in range(2):
        m_scs[s][...] = jnp.full_like(m_scs[s], -jnp.inf)
        acc_scs[s][...] = jnp.zeros_like(acc_scs[s])

    row_id = lax.broadcasted_iota(jnp.int32, (tq, tq), 0)
    col_id = lax.broadcasted_iota(jnp.int32, (tq, tq), 1)
    keep = row_id >= col_id

    def block(j, masked):
        start = pl.multiple_of(j * tq, tq)
        logits = [_dot_nt(qa_ref[s], ka_ref[s, pl.ds(start, tq), :]) for s in range(2)]
        pexp, scale, m_new = [], [], []
        for s in range(2):
            lg = jnp.where(keep, logits[s], -jnp.inf) if masked else logits[s]
            m_old = m_scs[s][...]
            m_new.append(jnp.maximum(m_old, jnp.max(lg, axis=1, keepdims=True)))
            pexp.append(jnp.exp(lg - jnp.concatenate([m_new[s]] * (tq // _LANES), axis=1)))
            scale.append(jnp.exp(m_old - m_new[s]))
        pv = [_dot(pexp[s].astype(_BF16), va_ref[s, pl.ds(start, tq), :]) for s in range(2)]
        for s in range(2):
            acc_scs[s][...] = scale[s] * acc_scs[s][...] + pv[s]
            m_scs[s][...] = m_new[s]

    def body(j, carry):
        block(j, False)
        return carry

    lax.fori_loop(0, i, body, 0)
    block(i, True)
    half = _LANES // 2
    acc0 = acc0_sc[...]
    acc1 = acc1_sc[...]
    low = lax.broadcasted_iota(jnp.int32, (tq, _LANES), 1) < half
    out = jnp.where(low, acc0 / acc0[:, half:half + 1], acc1 / acc1[:, 0:1])
    o_ref[...] = out.astype(o_ref.dtype)


def _fox(qa, ka, va, bsz, seq):
    tq = min(_TQ, seq)
    nq = seq // tq
    npair = _FOX_HEADS // 2
    return pl.pallas_call(
        functools.partial(_fox_kernel, tq=tq),
        out_shape=jax.ShapeDtypeStruct((bsz * seq, _FOX_HEADS * _FOX_DH), _BF16),
        grid=(bsz, npair, nq),
        in_specs=[
            pl.BlockSpec((None, 2, tq, _LANES), lambda b, p, i: (b, p, i, 0)),
            pl.BlockSpec((None, 2, seq, _LANES), lambda b, p, i: (b, p, 0, 0)),
            pl.BlockSpec((None, 2, seq, _LANES), lambda b, p, i: (b, p, 0, 0)),
        ],
        out_specs=pl.BlockSpec((tq, _LANES), lambda b, p, i: (b * nq + i, p)),
        scratch_shapes=[pltpu.VMEM((tq, _LANES), _F32), pltpu.VMEM((tq, _LANES), _F32),
                        pltpu.VMEM((tq, _LANES), _F32), pltpu.VMEM((tq, _LANES), _F32)],
        compiler_params=_params(3),
        name="fox_attention",
    )(qa, ka, va)


def _gmlp_tile(gu, gv, w_ref, bst_ref, lng_ref, lnb_ref, yc_sc):
    u = _gelu_tanh(gu)
    v = _layer_norm(_gelu_tanh(gv), lng_ref[...], lnb_ref[...], _LN_EPS).astype(_BF16)
    r = lax.broadcasted_iota(jnp.int32, (_BLOCK, _BLOCK), 0)
    c = lax.broadcasted_iota(jnp.int32, (_BLOCK, _BLOCK), 1)
    causal = r >= c
    for g in range(_GM_GROUPS):
        cols = slice(g * _GM_DG, (g + 1) * _GM_DG)
        wg = jnp.where(causal, w_ref[g], 0.0).astype(_BF16)
        bias = bst_ref[:, g:g + 1]
        for ch in range(gu.shape[0] // _BLOCK):
            rows = slice(ch * _BLOCK, (ch + 1) * _BLOCK)
            sp = _dot(wg, v[rows, cols]) + bias
            yc_sc[rows, cols] = (u[rows, cols] * sp).astype(yc_sc.dtype)


def _mixer_out_kernel(x16_ref, ya_ref, yb_ref, wc_ref, gmw_ref, bst_ref, gmg_ref, gmb_ref, gbias_ref,
                      wbr_ref, wout_ref, x_ref, lng_ref, lnb_ref, *rest, alpha, d_model, with_router):
    if with_router:
        wr_ref, xo_ref, xtok_ref, route_ref, yc_sc = rest
    else:
        xo_ref, xo16_ref, yc_sc = rest
    x16 = x16_ref[...]
    gw = _GM_GROUPS * _GM_DG
    gu = _dot(x16, wc_ref[:, 0:gw])
    gv = _dot(x16, wc_ref[:, gw:2 * gw])
    _gmlp_tile(gu, gv, gmw_ref, bst_ref, gmg_ref, gmb_ref, yc_sc)
    merged = None
    for n, y_ref in enumerate((ya_ref, yb_ref, yc_sc)):
        cols = slice(2 * gw + n * d_model, 2 * gw + (n + 1) * d_model)
        z = _dot(x16, wc_ref[:, cols]) + gbias_ref[:, n * d_model:(n + 1) * d_model]
        gate = 1.0 / (1.0 + jnp.exp(-z))
        term = gate * _dot(y_ref[...], wbr_ref[n])
        merged = term if merged is None else merged + term
    h = _dot(merged.astype(_BF16), wout_ref[...])
    out = _layer_norm(alpha * x_ref[...] + h, lng_ref[...], lnb_ref[...], _LN_EPS)
    xo_ref[...] = out
    if with_router:
        _store_token_rows(xtok_ref, out)
        route_ref[...] = _route_top2(out, wr_ref)
    else:
        xo16_ref[...] = out.astype(_BF16)


def _mixer_out(x16, y_a, y_b, w_c, gm_w_s, gm_b_s, gm_ln_g, gm_ln_b, gate_b, w_branch, w_out, x,
               ln_g, ln_b, w_router, *, alpha, tm):
    tokens, d_model = x.shape
    bw = y_a.shape[1]
    gw = _GM_GROUPS * _GM_DG
    row = lambda i: (i, 0)
    const2 = lambda i: (0, 0)
    const3 = lambda i: (0, 0, 0)
    with_router = w_router is not None
    in_specs = [
        pl.BlockSpec((tm, d_model), row),
        pl.BlockSpec((tm, bw), row), pl.BlockSpec((tm, bw), row),
        pl.BlockSpec(w_c.shape, const2),
        pl.BlockSpec((_GM_GROUPS, _BLOCK, _BLOCK), const3),
        pl.BlockSpec((_BLOCK, _GM_GROUPS), const2),
        pl.BlockSpec((1, gw), const2), pl.BlockSpec((1, gw), const2),
        pl.BlockSpec((1, 3 * d_model), const2),
        pl.BlockSpec((3, bw, d_model), const3),
        pl.BlockSpec((d_model, d_model), const2),
        pl.BlockSpec((tm, d_model), row),
        pl.BlockSpec((1, d_model), const2), pl.BlockSpec((1, d_model), const2),
    ]
    args = [x16, y_a, y_b, w_c, gm_w_s, gm_b_s.T, gm_ln_g.reshape(1, gw), gm_ln_b.reshape(1, gw),
            gate_b.reshape(1, -1), w_branch.astype(_BF16), w_out.astype(_BF16), x,
            ln_g.reshape(1, -1), ln_b.reshape(1, -1)]
    out_shape = [jax.ShapeDtypeStruct((tokens, d_model), _F32)]
    out_specs = [pl.BlockSpec((tm, d_model), row)]
    if with_router:
        sub = d_model // _LANES
        in_specs.append(pl.BlockSpec((d_model, 2 * _LANES), const2))
        args.append(_router_weights(w_router))
        out_shape += [jax.ShapeDtypeStruct((tokens * sub, _LANES), _F32),
                      jax.ShapeDtypeStruct((tokens, _LANES), _F32)]
        out_specs += [pl.BlockSpec((tm * sub, _LANES), row), pl.BlockSpec((tm, _LANES), row)]
    else:
        out_shape.append(jax.ShapeDtypeStruct((tokens, d_model), _BF16))
        out_specs.append(pl.BlockSpec((tm, d_model), row))
    outs = pl.pallas_call(
        functools.partial(_mixer_out_kernel, alpha=alpha, d_model=d_model, with_router=with_router),
        out_shape=tuple(out_shape),
        grid=(tokens // tm,),
        in_specs=in_specs,
        out_specs=tuple(out_specs),
        scratch_shapes=[pltpu.VMEM((tm, gw), _BF16)],
        compiler_params=_params(1),
        name="mixer_out",
    )(*args)
    return outs if with_router else (*outs, None)


def _weights_changed(te_ref, i):
    return jnp.logical_or(i == 0, te_ref[i] != te_ref[jnp.maximum(i - 1, 0)])


def _load_token_rows(ref, n):
    s = ref.shape[0] // n
    return jnp.concatenate([ref[pl.ds(c, n, stride=s), :] for c in range(s)], axis=1)


def _store_token_rows(ref, x):
    n = x.shape[0]
    s = x.shape[1] // _LANES
    for c in range(s):
        ref[pl.ds(c, n, stride=s), :] = x[:, c * _LANES:(c + 1) * _LANES]


def _up_kernel(te_ref, nused_ref, x_ref, wa_ref, wb_ref, h_ref, wa_sc, wb_sc):
    i = pl.program_id(1)

    @pl.when(_weights_changed(te_ref, i))
    def _():
        wa_sc[...] = wa_ref[...].astype(_BF16)
        wb_sc[...] = wb_ref[...].astype(_BF16)

    @pl.when(i < nused_ref[0])
    def _():
        x = _load_token_rows(x_ref, h_ref.shape[0]).astype(_BF16)
        a = _dot(x, wa_sc[...])
        b = _dot(x, wb_sc[...])
        h_ref[...] = (_silu(a) * b).astype(h_ref.dtype)

    @pl.when(i >= nused_ref[0])
    def _():
        h_ref[...] = jnp.zeros_like(h_ref)


def _swiglu_up(xs, w_up, te, nused, *, tm, tn):
    k = w_up.shape[1]
    sub = k // _LANES
    rows = xs.shape[0] // sub
    d_ff = w_up.shape[2] // 2
    nb = d_ff // tn
    grid_spec = pltpu.PrefetchScalarGridSpec(
        num_scalar_prefetch=2,
        grid=(nb, rows // tm),
        in_specs=[
            pl.BlockSpec((tm * sub, _LANES), lambda j, i, te, nu: (i, 0)),
            pl.BlockSpec((None, k, tn), lambda j, i, te, nu: (te[i], 0, j)),
            pl.BlockSpec((None, k, tn), lambda j, i, te, nu: (te[i], 0, j + nb)),
        ],
        out_specs=pl.BlockSpec((tm, tn), lambda j, i, te, nu: (i, j)),
        scratch_shapes=[pltpu.VMEM((k, tn), _BF16), pltpu.VMEM((k, tn), _BF16)],
    )
    return pl.pallas_call(
        _up_kernel,
        out_shape=jax.ShapeDtypeStruct((rows, d_ff), _BF16),
        grid_spec=grid_spec,
        compiler_params=_params(2),
        name="swiglu_up",
    )(te, nused, xs, w_up, w_up)


def _down_kernel(te_ref, nused_ref, h_ref, w_ref, y_ref, w_sc):
    i = pl.program_id(0)

    @pl.when(_weights_changed(te_ref, i))
    def _():
        w_sc[...] = w_ref[...].astype(_BF16)

    @pl.when(i < nused_ref[0])
    def _():
        _store_token_rows(y_ref, _dot(h_ref[...], w_sc[...]))

    @pl.when(i >= nused_ref[0])
    def _():
        y_ref[...] = jnp.zeros_like(y_ref)


def _expert_down(h16, w_down, te, nused, *, tm):
    rows, k = h16.shape
    n = w_down.shape[2]
    sub = n // _LANES
    grid_spec = pltpu.PrefetchScalarGridSpec(
        num_scalar_prefetch=2,
        grid=(rows // tm,),
        in_specs=[
            pl.BlockSpec((tm, k), lambda i, te, nu: (i, 0)),
            pl.BlockSpec((None, k, n), lambda i, te, nu: (te[i], 0, 0)),
        ],
        out_specs=pl.BlockSpec((tm * sub, _LANES), lambda i, te, nu: (i, 0)),
        scratch_shapes=[pltpu.VMEM((k, n), _BF16)],
    )
    return pl.pallas_call(
        _down_kernel,
        out_shape=jax.ShapeDtypeStruct((rows * sub, _LANES), _F32),
        grid_spec=grid_spec,
        compiler_params=_params(1),
        name="expert_down",
    )(te, nused, h16, w_down)


def _dense_ffn_kernel(x16_ref, wup_ref, wdn_ref, x_ref, lng_ref, lnb_ref, xo_ref, xo16_ref,
                      *, alpha, d_ff, chunks):
    x16 = x16_ref[...]
    f = None
    for c0, c1 in chunks:
        a = _dot(x16, wup_ref[:, c0:c1])
        b = _dot(x16, wup_ref[:, d_ff + c0:d_ff + c1])
        part = _dot((_silu(a) * b).astype(_BF16), wdn_ref[c0:c1, :])
        f = part if f is None else f + part
    out = _layer_norm(alpha * x_ref[...] + f, lng_ref[...], lnb_ref[...], _LN_EPS)
    xo_ref[...] = out
    xo16_ref[...] = out.astype(_BF16)


def _dense_ffn_ln(x, x16, w_up, w_down, ln_g, ln_b, *, alpha, tm):
    tokens, d_model = x.shape
    d_ff = w_down.shape[0]
    split = (d_ff // (2 * _MXU_COLS)) * _MXU_COLS
    chunks = ((0, split), (split, d_ff)) if 0 < split < d_ff else ((0, d_ff),)
    row = lambda i: (i, 0)
    const2 = lambda i: (0, 0)
    return pl.pallas_call(
        functools.partial(_dense_ffn_kernel, alpha=alpha, d_ff=d_ff, chunks=chunks),
        out_shape=(jax.ShapeDtypeStruct((tokens, d_model), _F32),
                   jax.ShapeDtypeStruct((tokens, d_model), _BF16)),
        grid=(tokens // tm,),
        in_specs=[pl.BlockSpec((tm, d_model), row),
                  pl.BlockSpec((d_model, 2 * d_ff), const2),
                  pl.BlockSpec((d_ff, d_model), const2),
                  pl.BlockSpec((tm, d_model), row),
                  pl.BlockSpec((1, d_model), const2), pl.BlockSpec((1, d_model), const2)],
        out_specs=(pl.BlockSpec((tm, d_model), row), pl.BlockSpec((tm, d_model), row)),
        compiler_params=_params(1),
        name="dense_ffn_ln",
    )(x16, w_up.astype(_BF16), w_down.astype(_BF16), x, ln_g.reshape(1, -1), ln_b.reshape(1, -1))


def _route_top2(x, wr_ref):
    x_hi = x.astype(_BF16)
    x_lo = (x - x_hi.astype(_F32)).astype(_BF16)
    both = _dot(x_hi, wr_ref[...])
    logits = both[:, :_LANES] + (both[:, _LANES:] + _dot(x_lo, wr_ref[:, :_LANES]))
    lane = lax.broadcasted_iota(jnp.int32, logits.shape, 1).astype(_F32)
    lg = jnp.where(lane < _N_EXPERTS, logits, -jnp.inf)
    m1 = jnp.max(lg, axis=1, keepdims=True)
    i1 = jnp.min(jnp.where(lg == m1, lane, float(_LANES)), axis=1, keepdims=True)
    lg2 = jnp.where(lane == i1, -jnp.inf, lg)
    m2 = jnp.max(lg2, axis=1, keepdims=True)
    i2 = jnp.min(jnp.where(lg2 == m2, lane, float(_LANES)), axis=1, keepdims=True)
    e2 = jnp.exp(m2 - m1)
    den = 1.0 + e2
    w1 = 1.0 / den
    w2 = e2 / den
    return jnp.where(lane == 0, i1, jnp.where(lane == 1, i2,
                     jnp.where(lane == 2, w1, jnp.where(lane == 3, w2, 0.0))))


def _router_weights(w_router):
    d_model = w_router.shape[0]
    wr = jnp.zeros((d_model, _LANES), _F32).at[:, :_N_EXPERTS].set(w_router)
    wr_hi = wr.astype(_BF16)
    wr_lo = (wr - wr_hi.astype(_F32)).astype(_BF16)
    return jnp.concatenate([wr_hi, wr_lo], axis=1)


_ROW_UNROLL = 8


def _dispatch_kernel(zstart_ref, zvalid_ref, d0_ref, d1_ref, x_ref, xs_hbm, zero_sc, stage_sc, sem, zsem,
                     *, tt, tm, sub, nsteps):
    @pl.when(pl.program_id(0) == 0)
    def _():
        zero_sc[...] = jnp.zeros_like(zero_sc)

        def zero_copy(n):
            start = pl.multiple_of(zstart_ref[n] * sub, tm * sub)
            return pltpu.make_async_copy(zero_sc, xs_hbm.at[pl.ds(start, tm * sub), :], zsem)

        for n in range(2 * _N_EXPERTS):
            @pl.when(zvalid_ref[n] > 0)
            def _():
                zero_copy(n).start()
        for n in range(2 * _N_EXPERTS):
            @pl.when(zvalid_ref[n] > 0)
            def _():
                zero_copy(n).wait()

    i = pl.program_id(0)
    slot = i % 2
    stage_sc[slot] = x_ref[...]

    def issue(g, carry):
        for u in range(_ROW_UNROLL):
            r = g * _ROW_UNROLL + u
            src = stage_sc.at[slot, pl.ds(pl.multiple_of(r * sub, sub), sub), :]
            for d_ref in (d0_ref, d1_ref):
                dst = xs_hbm.at[pl.ds(pl.multiple_of(d_ref[0, 0, r] * sub, sub), sub), :]
                pltpu.make_async_copy(src, dst, sem.at[slot]).start(priority=u % 2)
        return carry

    def drain(s):
        for _ in range(2):
            pltpu.make_async_copy(stage_sc.at[s], xs_hbm.at[pl.ds(0, tt * sub), :], sem.at[s]).wait()

    lax.fori_loop(0, tt // _ROW_UNROLL, issue, 0)

    @pl.when(i > 0)
    def _():
        drain(1 - slot)

    @pl.when(i == nsteps - 1)
    def _():
        drain(slot)


def _dispatch_rows(x_tok, d0, d1, zstart, zvalid, *, tokens, rows, tt, tm):
    sub = x_tok.shape[0] // tokens
    smem = lambda: pl.BlockSpec((1, 1, tt), lambda i, zs, zv: (i, 0, 0), memory_space=pltpu.SMEM)
    grid_spec = pltpu.PrefetchScalarGridSpec(
        num_scalar_prefetch=2,
        grid=(tokens // tt,),
        in_specs=[smem(), smem(), pl.BlockSpec((tt * sub, _LANES), lambda i, zs, zv: (i, 0))],
        out_specs=pl.BlockSpec(memory_space=pl.ANY),
        scratch_shapes=[pltpu.VMEM((tm * sub, _LANES), _F32), pltpu.VMEM((2, tt * sub, _LANES), _F32),
                        pltpu.SemaphoreType.DMA((2,)), pltpu.SemaphoreType.DMA],
    )
    return pl.pallas_call(
        functools.partial(_dispatch_kernel, tt=tt, tm=tm, sub=sub, nsteps=tokens // tt),
        out_shape=jax.ShapeDtypeStruct((rows * sub, _LANES), _F32),
        grid_spec=grid_spec,
        compiler_params=_params(1),
        name="moe_dispatch",
    )(zstart, zvalid, d0, d1, x_tok)


def _combine_kernel(d0_ref, d1_ref, n0_ref, n1_ref, y_hbm, route_ref, x_ref, lng_ref, lnb_ref,
                    xo_ref, xo16_ref, buf, sem, *, alpha, tm, sub, nsteps):
    i = pl.program_id(0)
    slot = i % 2

    def gather(dst_slot, a_ref, b_ref):
        def issue(g, carry):
            for u in range(_ROW_UNROLL):
                r = g * _ROW_UNROLL + u
                for s, d_ref in enumerate((a_ref, b_ref)):
                    src = y_hbm.at[pl.ds(pl.multiple_of(d_ref[0, 0, r] * sub, sub), sub), :]
                    dst = buf.at[dst_slot, s, pl.ds(pl.multiple_of(r * sub, sub), sub), :]
                    pltpu.make_async_copy(src, dst, sem.at[dst_slot]).start(priority=u % 2)
            return carry

        lax.fori_loop(0, tm // _ROW_UNROLL, issue, 0)

    @pl.when(i == 0)
    def _():
        gather(0, d0_ref, d1_ref)

    @pl.when(i + 1 < nsteps)
    def _():
        gather(1 - slot, n0_ref, n1_ref)

    for s in range(2):
        pltpu.make_async_copy(y_hbm.at[pl.ds(0, tm * sub), :], buf.at[slot, s], sem.at[slot]).wait()
    route = route_ref[...]
    f = (route[:, 2:3] * _load_token_rows(buf.at[slot, 0], tm)
         + route[:, 3:4] * _load_token_rows(buf.at[slot, 1], tm))
    out = _layer_norm(alpha * x_ref[...] + f, lng_ref[...], lnb_ref[...], _LN_EPS)
    xo_ref[...] = out
    xo16_ref[...] = out.astype(_BF16)


def _combine_ln(ys, d0, d1, route, x, ln_g, ln_b, *, alpha, tm):
    tokens, d_model = x.shape
    sub = d_model // _LANES
    nsteps = tokens // tm
    row = lambda i: (i, 0)
    const2 = lambda i: (0, 0)
    cur = lambda: pl.BlockSpec((1, 1, tm), lambda i: (i, 0, 0), memory_space=pltpu.SMEM)
    nxt = lambda: pl.BlockSpec((1, 1, tm), lambda i: (jnp.minimum(i + 1, nsteps - 1), 0, 0),
                               memory_space=pltpu.SMEM)
    return pl.pallas_call(
        functools.partial(_combine_kernel, alpha=alpha, tm=tm, sub=sub, nsteps=nsteps),
        out_shape=(jax.ShapeDtypeStruct((tokens, d_model), _F32),
                   jax.ShapeDtypeStruct((tokens, d_model), _BF16)),
        grid=(nsteps,),
        in_specs=[cur(), cur(), nxt(), nxt(), pl.BlockSpec(memory_space=pl.ANY),
                  pl.BlockSpec((tm, _LANES), row), pl.BlockSpec((tm, d_model), row),
                  pl.BlockSpec((1, d_model), const2), pl.BlockSpec((1, d_model), const2)],
        out_specs=(pl.BlockSpec((tm, d_model), row), pl.BlockSpec((tm, d_model), row)),
        scratch_shapes=[pltpu.VMEM((2, 2, tm * sub, _LANES), _F32), pltpu.SemaphoreType.DMA((2,))],
        compiler_params=_params(1),
        name="moe_combine_ln",
    )(d0, d1, d0, d1, ys, route, x, ln_g.reshape(1, -1), ln_b.reshape(1, -1))


def _routing_plan(route, *, tm):
    tokens = route.shape[0]
    pairs = 2 * tokens
    ntiles = pairs // tm + _N_EXPERTS
    flat_e = route[:, 0:2].astype(jnp.int32).reshape(pairs)
    onehot = (flat_e[:, None] == jnp.arange(_N_EXPERTS, dtype=jnp.int32)[None, :]).astype(jnp.int32)
    csum = jnp.cumsum(onehot, axis=0)
    counts = csum[-1]
    padded = ((counts + tm - 1) // tm) * tm
    ends = jnp.cumsum(padded)
    starts = ends - padded
    dest = jnp.sum(onehot * (starts[None, :] + csum - 1), axis=1)
    tile_start = jnp.arange(ntiles, dtype=jnp.int32) * tm
    te = jnp.minimum(jnp.sum((ends[None, :] <= tile_start[:, None]).astype(jnp.int32), axis=1),
                     _N_EXPERTS - 1)
    nused = (ends[-1] // tm).reshape(1)
    tail = ends[-1] + jnp.arange(_N_EXPERTS, dtype=jnp.int32) * tm
    zstart = jnp.concatenate([jnp.maximum(ends - tm, 0), jnp.minimum(tail, (ntiles - 1) * tm)])
    zvalid = jnp.concatenate([padded > 0, tail < ntiles * tm]).astype(jnp.int32)
    dest2 = dest.reshape(tokens, 2)
    return ntiles * tm, te, nused, dest2[:, 0], dest2[:, 1], zstart, zvalid


def _moe_ffn_ln(x, x_tok, route, w_up, w_down, ln_g, ln_b, *, alpha, tm):
    tokens = x.shape[0]
    tc = min(256, tokens)
    rows, te, nused, d0, d1, zstart, zvalid = _routing_plan(route, tm=tm)
    d0 = d0.reshape(tokens // tc, 1, tc)
    d1 = d1.reshape(tokens // tc, 1, tc)
    xs = _dispatch_rows(x_tok, d0, d1, zstart, zvalid, tokens=tokens, rows=rows, tt=tc, tm=tm)
    h = _swiglu_up(xs, w_up, te, nused, tm=tm, tn=w_up.shape[2] // 4)
    ys = _expert_down(h, w_down, te, nused, tm=tm)
    return _combine_ln(ys, d0, d1, route, x, ln_g, ln_b, alpha=alpha, tm=tc)


def _mixer_ln(x, x16, w_in, fox_b_f, gate_b, gm_w_s, gm_b_s, gm_ln_g, gm_ln_b, w_branch, w_out,
              ln_g, ln_b, w_router, *, bsz, seq, alpha, tm):
    d_model = x.shape[1]
    n_r = 2 * _RET_HEADS * _RET_DK + 2 * _RET_HEADS * _RET_DV
    n_f = 3 * _FOX_HEADS * _FOX_DH
    n_fl = _FOX_HEADS
    w_r = w_in[:, :n_r].astype(_BF16)
    w_f = jnp.concatenate([w_in[:, n_r:n_r + n_f + n_fl],
                           jnp.zeros((d_model, _LANES - n_fl), w_in.dtype)], axis=1).astype(_BF16)
    w_c = w_in[:, n_r + n_f + n_fl:].astype(_BF16)

    y_a = _retention(x16, w_r, bsz, seq)
    bias = jnp.zeros((1, _LANES), _F32).at[0, :n_fl].set(fox_b_f)
    qa, ka, va = _fox_prep(x16, w_f, bias, bsz, seq)
    y_b = _fox(qa, ka, va, bsz, seq)
    return _mixer_out(x16, y_a, y_b, w_c, gm_w_s, gm_b_s, gm_ln_g, gm_ln_b, gate_b, w_branch, w_out,
                      x, ln_g, ln_b, w_router, alpha=alpha, tm=tm)


def kernel(x, w_in, fox_b_f, gate_b, gm_w_s, gm_b_s, gm_ln_g, gm_ln_b, w_branch, w_out, ln_g, ln_b,
           dense_w_up, dense_w_down, moe_router, moe_w_up, moe_w_down):
    bsz, seq, d_model = x.shape
    depth = w_in.shape[0]
    alpha = (2 * depth) ** 0.25
    tokens = bsz * seq
    tm = min(_TM, tokens)
    xf = x.reshape(tokens, d_model)
    x16 = xf.astype(_BF16)
    for l in range(depth):
        dense = l % 2 == 0
        xf, x16, route = _mixer_ln(
            xf, x16, w_in[l], fox_b_f[l], gate_b[l], gm_w_s[l], gm_b_s[l], gm_ln_g[l], gm_ln_b[l],
            w_branch[l], w_out[l], ln_g[l, 0], ln_b[l, 0], None if dense else moe_router[l // 2],
            bsz=bsz, seq=seq, alpha=alpha, tm=tm)
        if dense:
            xf, x16 = _dense_ffn_ln(xf, x16, dense_w_up[l // 2], dense_w_down[l // 2],
                                    ln_g[l, 1], ln_b[l, 1], alpha=alpha, tm=tm)
        else:
            xf, x16 = _moe_ffn_ln(xf, x16, route, moe_w_up[l // 2], moe_w_down[l // 2],
                                  ln_g[l, 1], ln_b[l, 1], alpha=alpha, tm=tm)
    return xf.reshape(bsz, seq, d_model)
```

```python
import functools
import math

import jax
import jax.numpy as jnp
import numpy as np
from jax import lax
from jax.experimental import pallas as pl
from jax.experimental.pallas import tpu as pltpu

_BF16 = jnp.bfloat16
_F32 = jnp.float32

_BLOCK = 128
_RET_HEADS, _RET_DK, _RET_DV = 4, 64, 128
_FOX_HEADS, _FOX_DH = 8, 64
_GM_GROUPS, _GM_DG = 4, 128
_N_EXPERTS = 8
_ROPE_BASE = 10000.0
_LN_EPS = 1e-5
_GN_EPS = 1e-6

_LANES = 128
_MXU_COLS = 256
_VMEM_PHYSICAL_BYTES = 64 * 1024 * 1024
_VMEM_LIMIT_BYTES = (_VMEM_PHYSICAL_BYTES * 7) // 8

_TM = 512
_TQ = 512
_TR_RET = 512


def _params(n_axes):
    return pltpu.CompilerParams(
        dimension_semantics=("arbitrary",) * n_axes,
        vmem_limit_bytes=_VMEM_LIMIT_BYTES,
    )


def _layer_norm(x, g, b, eps):
    mu = jnp.mean(x, axis=-1, keepdims=True)
    xc = x - mu
    var = jnp.mean(xc * xc, axis=-1, keepdims=True)
    return xc * lax.rsqrt(var + eps) * g + b


def _silu(x):
    return x * (1.0 / (1.0 + jnp.exp(-x)))


def _gelu_tanh(x):
    c = math.sqrt(2.0 / math.pi)
    return x * (0.5 * (1.0 + jnp.tanh(c * (x + 0.044715 * (x * x * x)))))


def _dot(a, b):
    return jnp.dot(a, b, preferred_element_type=_F32)


def _dot_nt(a, b):
    return lax.dot_general(a, b, (((1,), (1,)), ((), ())), preferred_element_type=_F32)


def _dot_tn(a, b):
    return lax.dot_general(a, b, (((0,), (0,)), ((), ())), preferred_element_type=_F32)


def _matmul_kernel(x_ref, w_ref, o_ref):
    o_ref[...] = _dot(x_ref[...], w_ref[...]).astype(o_ref.dtype)


def _matmul(x, w, *, tm, tn, out_dtype, name):
    m, k = x.shape
    n = w.shape[1]
    return pl.pallas_call(
        _matmul_kernel,
        out_shape=jax.ShapeDtypeStruct((m, n), out_dtype),
        grid=(n // tn, m // tm),
        in_specs=[pl.BlockSpec((tm, k), lambda j, i: (i, 0)),
                  pl.BlockSpec((k, tn), lambda j, i: (0, j))],
        out_specs=pl.BlockSpec((tm, tn), lambda j, i: (i, j)),
        compiler_params=_params(2),
        name=name,
    )(x, w)


def _retention_kernel(x16_ref, w_ref, cos_ref, sin_ref, qdec_ref, kdec_ref, din_ref,
                      y_ref, v_sc, g_sc, state_sc, *, chunk_dec):
    c = pl.program_id(1)

    @pl.when(c == 0)
    def _():
        state_sc[...] = jnp.zeros_like(state_sc)

    width = _RET_HEADS * _RET_DK
    wv = _RET_HEADS * _RET_DV
    rows_total = x16_ref.shape[0]
    lane_all = lax.broadcasted_iota(jnp.int32, (rows_total, width), 1)
    first_half = (lane_all % _RET_DK) < (_RET_DK // 2)

    def rotary(x):
        partner = jnp.where(first_half,
                            pltpu.roll(x, width - _RET_DK // 2, 1),
                            pltpu.roll(x, _RET_DK // 2, 1))
        return x * cos_ref[...] + partner * sin_ref[...]

    x16 = x16_ref[...].astype(_BF16)
    q_all = rotary(_dot(x16, w_ref[:, 0:width]))
    k_all = rotary(_dot(x16, w_ref[:, width:2 * width])) * (_RET_DK ** -0.5)
    v_sc[...] = _dot(x16, w_ref[:, 2 * width:2 * width + wv]).astype(_BF16)
    g_sc[...] = _dot(x16, w_ref[:, 2 * width + wv:2 * width + 2 * wv])
    lane = lax.broadcasted_iota(jnp.int32, (_BLOCK, width), 1)
    srow = lax.broadcasted_iota(jnp.int32, (width, wv), 0) // _RET_DK
    scol = lax.broadcasted_iota(jnp.int32, (width, wv), 1) // _RET_DV
    on_diag = srow == scol
    col_head = lax.broadcasted_iota(jnp.int32, (1, wv), 1) // _RET_DV
    decay_cols = jnp.zeros((1, wv), _F32)
    for h in range(_RET_HEADS):
        decay_cols = jnp.where(col_head == h, chunk_dec[h], decay_cols)
    for cc in range(rows_total // _BLOCK):
        rows = slice(cc * _BLOCK, (cc + 1) * _BLOCK)
        q = q_all[rows]
        k = k_all[rows]
        v = v_sc[rows, :]
        q_heads = jnp.concatenate(
            [jnp.where((lane // _RET_DK) == h, q, 0.0) for h in range(_RET_HEADS)], axis=0)
        scores = _dot_nt(q_heads.astype(_BF16), k.astype(_BF16))
        state = state_sc[...]
        cross = _dot((q * qdec_ref[...]).astype(_BF16), state.astype(_BF16))
        kv = _dot_tn((k * kdec_ref[...]).astype(_BF16), v)
        state_sc[...] = state * decay_cols + jnp.where(on_diag, kv, 0.0)
        for h in range(_RET_HEADS):
            cols = slice(h * _RET_DV, (h + 1) * _RET_DV)
            sh = scores[h * _BLOCK:(h + 1) * _BLOCK] * din_ref[h]
            out = _dot(sh.astype(_BF16), v[:, cols]) + cross[:, cols]
            mu = jnp.mean(out, axis=-1, keepdims=True)
            oc = out - mu
            var = jnp.mean(oc * oc, axis=-1, keepdims=True)
            normed = oc * lax.rsqrt(var + _GN_EPS)
            y_ref[rows, cols] = (_silu(g_sc[rows, cols]) * normed).astype(y_ref.dtype)


def _retention_tables(seq):
    half = _RET_DK // 2
    inv_freq = _ROPE_BASE ** (-jnp.arange(half, dtype=_F32) / half)
    ang = jnp.arange(seq, dtype=jnp.int32).astype(_F32)[:, None] * inv_freq[None, :]
    cos, sin = jnp.cos(ang), jnp.sin(ang)
    cos_t = jnp.tile(jnp.concatenate([cos, cos], axis=1), (1, _RET_HEADS))
    sin_t = jnp.tile(jnp.concatenate([-sin, sin], axis=1), (1, _RET_HEADS))
    log_gamma = jnp.log1p(-jnp.exp2(-5.0 - jnp.arange(_RET_HEADS, dtype=_F32)))
    idx = jnp.arange(_BLOCK, dtype=_F32)
    rel = idx[:, None] - idx[None, :]
    causal = rel >= 0
    decay_in = jnp.where(causal[None],
                         jnp.exp(log_gamma[:, None, None] * jnp.where(causal, rel, 0.0)[None]), 0.0)
    q_dec = jnp.exp(log_gamma[:, None] * (idx + 1.0))
    k_dec = jnp.exp(log_gamma[:, None] * (_BLOCK - 1.0 - idx))
    q_dec_t = jnp.repeat(q_dec.T, _RET_DK, axis=1)
    k_dec_t = jnp.repeat(k_dec.T, _RET_DK, axis=1)
    return cos_t, sin_t, q_dec_t, k_dec_t, decay_in


def _retention(x16, w_r, bsz, seq):
    tokens, d_model = x16.shape
    tr = min(_TR_RET, seq)
    nstep = seq // tr
    cos_t, sin_t, q_dec_t, k_dec_t, decay_in = _retention_tables(seq)
    chunk_dec = tuple(float((1.0 - 2.0 ** (-5.0 - h)) ** _BLOCK) for h in range(_RET_HEADS))
    qk = _RET_HEADS * _RET_DK
    wv = _RET_HEADS * _RET_DV
    row = lambda b, c: b * nstep + c
    return pl.pallas_call(
        functools.partial(_retention_kernel, chunk_dec=chunk_dec),
        out_shape=jax.ShapeDtypeStruct((tokens, wv), _BF16),
        grid=(bsz, nstep),
        in_specs=[
            pl.BlockSpec((tr, d_model), lambda b, c: (row(b, c), 0)),
            pl.BlockSpec(w_r.shape, lambda b, c: (0, 0)),
            pl.BlockSpec((tr, qk), lambda b, c: (c, 0)),
            pl.BlockSpec((tr, qk), lambda b, c: (c, 0)),
            pl.BlockSpec((_BLOCK, qk), lambda b, c: (0, 0)),
            pl.BlockSpec((_BLOCK, qk), lambda b, c: (0, 0)),
            pl.BlockSpec((_RET_HEADS, _BLOCK, _BLOCK), lambda b, c: (0, 0, 0)),
        ],
        out_specs=pl.BlockSpec((tr, wv), lambda b, c: (row(b, c), 0)),
        scratch_shapes=[pltpu.VMEM((tr, wv), _BF16), pltpu.VMEM((tr, wv), _F32),
                        pltpu.VMEM((qk, wv), _F32)],
        compiler_params=_params(2),
        name="retention",
    )(x16, w_r, cos_t, sin_t, q_dec_t, k_dec_t, decay_in)


_C_PARTS = 3
_TS_PREP = 512


def _fox_selectors():
    half = _LANES // 2
    one_row = _C_PARTS * _FOX_HEADS
    sel = np.zeros((_FOX_HEADS, _LANES, 2 * _LANES), np.float32)
    for h in range(_FOX_HEADS):
        base = half if h % 2 == 0 else 0
        for n in range(_C_PARTS):
            sel[h, n * _FOX_HEADS + h, base + n] = 1.0
            sel[h, one_row, base + _C_PARTS + n] = 1.0
            sel[h, one_row, _LANES + base + n] = 1.0
            sel[h, n * _FOX_HEADS + h, _LANES + base + _C_PARTS + n] = -1.0
    return jnp.asarray(sel, _BF16)


def _fox_prep_kernel(x16_ref, w_ref, b_ref, sel_ref, qa_ref, ka_ref, va_ref, fq_ref, fk_ref, fv_ref,
                     fl_ref, carry_sc, *, nblk):
    @pl.when(pl.program_id(1) == 0)
    def _():
        carry_sc[...] = jnp.zeros_like(carry_sc)

    x16 = x16_ref[...].astype(_BF16)
    width = _FOX_HEADS * _FOX_DH
    for n, dst in enumerate((fq_ref, fk_ref, fv_ref)):
        dst[...] = _dot(x16, w_ref[:, n * width:(n + 1) * width]).astype(_BF16)
    fl_ref[...] = _dot(x16, w_ref[:, 3 * width:3 * width + _LANES])

    r = lax.broadcasted_iota(jnp.int32, (_BLOCK, _LANES), 0)
    lane = lax.broadcasted_iota(jnp.int32, (_BLOCK, _LANES), 1)
    tri = (r >= lane).astype(_F32)
    half = _LANES // 2
    q_scale = jnp.asarray(_FOX_DH ** -0.5, _BF16)
    carry = carry_sc[...]
    packed = []
    for blk in range(nblk):
        rows = slice(blk * _BLOCK, (blk + 1) * _BLOCK)
        z = fl_ref[rows, :] + b_ref[...]
        log_f = jnp.minimum(z, 0.0) - jnp.log1p(jnp.exp(-jnp.abs(z)))
        cum = jnp.dot(tri, log_f, preferred_element_type=_F32,
                      precision=lax.Precision.HIGHEST) + carry
        carry = cum[_BLOCK - 1:_BLOCK, :]
        row = jnp.where(lane == _C_PARTS * _FOX_HEADS, 1.0, 0.0)
        rest = cum
        for n in range(_C_PARTS):
            piece = rest.astype(_BF16).astype(_F32)
            rest = rest - piece
            shifted = piece if n == 0 else pltpu.roll(piece, n * _FOX_HEADS, 1)
            row = jnp.where((lane >= n * _FOX_HEADS) & (lane < (n + 1) * _FOX_HEADS), shifted, row)
        packed.append(row.astype(_BF16))
    carry_sc[...] = carry
    packed = jnp.concatenate(packed, axis=0)
    lane_all = lax.broadcasted_iota(jnp.int32, (nblk * _BLOCK, _LANES), 1)
    for h in range(_FOX_HEADS):
        pair = slice((h // 2) * _LANES, (h // 2 + 1) * _LANES)
        in_head = (lane_all < half) if h % 2 == 0 else (lane_all >= half)
        extra = _dot(packed, sel_ref[h]).astype(_BF16)
        qa_ref[h] = jnp.where(in_head, fq_ref[:, pair] * q_scale, extra[:, :_LANES])
        ka_ref[h] = jnp.where(in_head, fk_ref[:, pair], extra[:, _LANES:])
        va_ref[h] = jnp.where(in_head, fv_ref[:, pair], jnp.ones((), _BF16))


def _fox_prep(x16, w_f, bias, bsz, seq):
    d_model = x16.shape[1]
    ts = min(_TS_PREP, seq)
    ns = seq // ts
    width = _FOX_HEADS * _FOX_DH
    sel = _fox_selectors()
    aug = jax.ShapeDtypeStruct((bsz, _FOX_HEADS, seq, _LANES), _BF16)
    aug_spec = pl.BlockSpec((None, _FOX_HEADS, ts, _LANES), lambda b, s: (b, 0, s, 0))
    return pl.pallas_call(
        functools.partial(_fox_prep_kernel, nblk=ts // _BLOCK),
        out_shape=(aug, aug, aug),
        grid=(bsz, ns),
        in_specs=[pl.BlockSpec((ts, d_model), lambda b, s: (b * ns + s, 0)),
                  pl.BlockSpec(w_f.shape, lambda b, s: (0, 0)),
                  pl.BlockSpec((1, _LANES), lambda b, s: (0, 0)),
                  pl.BlockSpec(sel.shape, lambda b, s: (0, 0, 0))],
        out_specs=(aug_spec, aug_spec, aug_spec),
        scratch_shapes=[pltpu.VMEM((ts, width), _BF16), pltpu.VMEM((ts, width), _BF16),
                        pltpu.VMEM((ts, width), _BF16), pltpu.VMEM((ts, _LANES), _F32),
                        pltpu.VMEM((1, _LANES), _F32)],
        compiler_params=_params(2),
        name="fox_prep",
    )(x16, w_f, bias, sel)


def _fox_kernel(qa_ref, ka_ref, va_ref, o_ref, m0_sc, m1_sc, acc0_sc, acc1_sc, *, tq):
    i = pl.program_id(2)
    m_scs = (m0_sc, m1_sc)
    acc_scs = (acc0_sc, acc1_sc)
    for s in range(2):
        m_scs[s][...] = jnp.full_like(m_scs[s], -jnp.inf)
        acc_scs[s][...] = jnp.zeros_like(acc_scs[s])

    def block(start, nkeys, row0, nrows, masked):
        rows = slice(row0, row0 + nrows)
        logits = [_dot_nt(qa_ref[s, rows, :], ka_ref[s, pl.ds(start, nkeys), :]) for s in range(2)]
        if masked:
            keep = (lax.broadcasted_iota(jnp.int32, (nrows, nkeys), 0)
                    >= lax.broadcasted_iota(jnp.int32, (nrows, nkeys), 1))
        pexp, scale, m_new = [], [], []
        for s in range(2):
            lg = jnp.where(keep, logits[s], -jnp.inf) if masked else logits[s]
            m_old = m_scs[s][rows, :]
            m_new.append(jnp.maximum(m_old, jnp.max(lg, axis=1, keepdims=True)))
            pexp.append(jnp.exp(lg - jnp.concatenate([m_new[s]] * (nkeys // _LANES), axis=1)))
            scale.append(jnp.exp(m_old - m_new[s]))
        pv = [_dot(pexp[s].astype(_BF16), va_ref[s, pl.ds(start, nkeys), :]) for s in range(2)]
        for s in range(2):
            acc_scs[s][rows, :] = scale[s] * acc_scs[s][rows, :] + pv[s]
            m_scs[s][rows, :] = m_new[s]

    def body(jj, carry):
        block(pl.multiple_of(jj * 2 * tq, 2 * tq), 2 * tq, 0, tq, False)
        return carry

    lax.fori_loop(0, i // 2, body, 0)

    @pl.when(i % 2 == 1)
    def _():
        block(pl.multiple_of((i - 1) * tq, tq), tq, 0, tq, False)

    block(pl.multiple_of(i * tq, tq), tq, 0, tq, True)
    half = _LANES // 2
    acc0 = acc0_sc[...]
    acc1 = acc1_sc[...]
    low = lax.broadcasted_iota(jnp.int32, (tq, _LANES), 1) < half
    out = jnp.where(low, acc0 / acc0[:, half:half + 1], acc1 / acc1[:, 0:1])
    o_ref[...] = out.astype(o_ref.dtype)


def _fox(qa, ka, va, bsz, seq):
    tq = min(_TQ, seq)
    nq = seq // tq
    npair = _FOX_HEADS // 2
    return pl.pallas_call(
        functools.partial(_fox_kernel, tq=tq),
        out_shape=jax.ShapeDtypeStruct((bsz * seq, _FOX_HEADS * _FOX_DH), _BF16),
        grid=(bsz, npair, nq),
        in_specs=[
            pl.BlockSpec((None, 2, tq, _LANES), lambda b, p, i: (b, p, i, 0)),
            pl.BlockSpec((None, 2, seq, _LANES), lambda b, p, i: (b, p, 0, 0)),
            pl.BlockSpec((None, 2, seq, _LANES), lambda b, p, i: (b, p, 0, 0)),
        ],
        out_specs=pl.BlockSpec((tq, _LANES), lambda b, p, i: (b * nq + i, p)),
        scratch_shapes=[pltpu.VMEM((tq, _LANES), _F32), pltpu.VMEM((tq, _LANES), _F32),
                        pltpu.VMEM((tq, _LANES), _F32), pltpu.VMEM((tq, _LANES), _F32)],
        compiler_params=_params(3),
        name="fox_attention",
    )(qa, ka, va)


def _gmlp_tile(gu, gv, w_ref, bst_ref, lng_ref, lnb_ref, yc_sc):
    u = _gelu_tanh(gu)
    v = _layer_norm(_gelu_tanh(gv), lng_ref[...], lnb_ref[...], _LN_EPS).astype(_BF16)
    r = lax.broadcasted_iota(jnp.int32, (_BLOCK, _BLOCK), 0)
    c = lax.broadcasted_iota(jnp.int32, (_BLOCK, _BLOCK), 1)
    causal = r >= c
    nchunk = gu.shape[0] // _BLOCK
    for g in range(_GM_GROUPS):
        cols = slice(g * _GM_DG, (g + 1) * _GM_DG)
        wg = jnp.where(causal, w_ref[g], 0.0).astype(_BF16)
        bias = bst_ref[:, g:g + 1]
        v_chunks = jnp.concatenate(
            [v[ch * _BLOCK:(ch + 1) * _BLOCK, cols] for ch in range(nchunk)], axis=1)
        sp = _dot(wg, v_chunks) + bias
        for ch in range(nchunk):
            rows = slice(ch * _BLOCK, (ch + 1) * _BLOCK)
            yc_sc[rows, cols] = (u[rows, cols] * sp[:, ch * _GM_DG:(ch + 1) * _GM_DG]).astype(yc_sc.dtype)


def _mixer_out_kernel(x16_ref, ya_ref, yb_ref, wc_ref, gmw_ref, bst_ref, gmg_ref, gmb_ref, gbias_ref,
                      wbr_ref, wout_ref, x_ref, lng_ref, lnb_ref, *rest, alpha, d_model, with_router):
    if with_router:
        wr_ref, xo_ref, xtok_ref, route_ref, yc_sc = rest
    else:
        xo_ref, xo16_ref, yc_sc = rest
    x16 = x16_ref[...].astype(_BF16)
    gw = _GM_GROUPS * _GM_DG
    gu = _dot(x16, wc_ref[:, 0:gw])
    gv = _dot(x16, wc_ref[:, gw:2 * gw])
    _gmlp_tile(gu, gv, gmw_ref, bst_ref, gmg_ref, gmb_ref, yc_sc)
    merged = None
    for n, y_ref in enumerate((ya_ref, yb_ref, yc_sc)):
        cols = slice(2 * gw + n * d_model, 2 * gw + (n + 1) * d_model)
        z = _dot(x16, wc_ref[:, cols]) + gbias_ref[:, n * d_model:(n + 1) * d_model]
        gate = 1.0 / (1.0 + jnp.exp(-z))
        term = gate * _dot(y_ref[...], wbr_ref[n])
        merged = term if merged is None else merged + term
    h = _dot(merged.astype(_BF16), wout_ref[...])
    out = _layer_norm(alpha * x_ref[...] + h, lng_ref[...], lnb_ref[...], _LN_EPS)
    xo_ref[...] = out
    if with_router:
        _store_token_rows(xtok_ref, out)
        route_ref[...] = _route_top2(out, wr_ref)
    else:
        xo16_ref[...] = out.astype(_BF16)


def _mixer_out(x16, y_a, y_b, w_c, gm_w_s, gm_b_s, gm_ln_g, gm_ln_b, gate_b, w_branch, w_out, x,
               ln_g, ln_b, w_router, *, alpha, tm):
    tokens, d_model = x.shape
    bw = y_a.shape[1]
    gw = _GM_GROUPS * _GM_DG
    row = lambda i: (i, 0)
    const2 = lambda i: (0, 0)
    const3 = lambda i: (0, 0, 0)
    with_router = w_router is not None
    in_specs = [
        pl.BlockSpec((tm, d_model), row),
        pl.BlockSpec((tm, bw), row), pl.BlockSpec((tm, bw), row),
        pl.BlockSpec(w_c.shape, const2),
        pl.BlockSpec((_GM_GROUPS, _BLOCK, _BLOCK), const3),
        pl.BlockSpec((_BLOCK, _GM_GROUPS), const2),
        pl.BlockSpec((1, gw), const2), pl.BlockSpec((1, gw), const2),
        pl.BlockSpec((1, 3 * d_model), const2),
        pl.BlockSpec((3, bw, d_model), const3),
        pl.BlockSpec((d_model, d_model), const2),
        pl.BlockSpec((tm, d_model), row),
        pl.BlockSpec((1, d_model), const2), pl.BlockSpec((1, d_model), const2),
    ]
    args = [x16, y_a, y_b, w_c, gm_w_s, gm_b_s.T, gm_ln_g.reshape(1, gw), gm_ln_b.reshape(1, gw),
            gate_b.reshape(1, -1), w_branch.astype(_BF16), w_out.astype(_BF16), x,
            ln_g.reshape(1, -1), ln_b.reshape(1, -1)]
    out_shape = [jax.ShapeDtypeStruct((tokens, d_model), _F32)]
    out_specs = [pl.BlockSpec((tm, d_model), row)]
    if with_router:
        sub = d_model // _LANES
        in_specs.append(pl.BlockSpec((d_model, 2 * _LANES), const2))
        args.append(_router_weights(w_router))
        out_shape += [jax.ShapeDtypeStruct((tokens * sub, _LANES), _F32),
                      jax.ShapeDtypeStruct((tokens, _LANES), _F32)]
        out_specs += [pl.BlockSpec((tm * sub, _LANES), row), pl.BlockSpec((tm, _LANES), row)]
    else:
        out_shape.append(jax.ShapeDtypeStruct((tokens, d_model), _BF16))
        out_specs.append(pl.BlockSpec((tm, d_model), row))
    outs = pl.pallas_call(
        functools.partial(_mixer_out_kernel, alpha=alpha, d_model=d_model, with_router=with_router),
        out_shape=tuple(out_shape),
        grid=(tokens // tm,),
        in_specs=in_specs,
        out_specs=tuple(out_specs),
        scratch_shapes=[pltpu.VMEM((tm, gw), _BF16)],
        compiler_params=_params(1),
        name="mixer_out",
    )(*args)
    return outs if with_router else (*outs, None)


def _weights_changed(te_ref, i):
    return jnp.logical_or(i == 0, te_ref[i] != te_ref[jnp.maximum(i - 1, 0)])


def _load_token_rows(ref, n):
    s = ref.shape[0] // n
    return jnp.concatenate([ref[pl.ds(c, n, stride=s), :] for c in range(s)], axis=1)


def _store_token_rows(ref, x):
    n = x.shape[0]
    s = x.shape[1] // _LANES
    for c in range(s):
        ref[pl.ds(c, n, stride=s), :] = x[:, c * _LANES:(c + 1) * _LANES]


def _up_kernel(te_ref, nused_ref, x_ref, wa_ref, wb_ref, h_ref, wa_sc, wb_sc):
    i = pl.program_id(1)

    @pl.when(_weights_changed(te_ref, i))
    def _():
        wa_sc[...] = wa_ref[...].astype(_BF16)
        wb_sc[...] = wb_ref[...].astype(_BF16)

    @pl.when(i < nused_ref[0])
    def _():
        x = _load_token_rows(x_ref, h_ref.shape[0]).astype(_BF16)
        a = _dot(x, wa_sc[...])
        b = _dot(x, wb_sc[...])
        h_ref[...] = (_silu(a) * b).astype(h_ref.dtype)

    @pl.when(i >= nused_ref[0])
    def _():
        h_ref[...] = jnp.zeros_like(h_ref)


def _swiglu_up(xs, w_up, te, nused, *, tm, tn):
    k = w_up.shape[1]
    sub = k // _LANES
    rows = xs.shape[0] // sub
    d_ff = w_up.shape[2] // 2
    nb = d_ff // tn
    grid_spec = pltpu.PrefetchScalarGridSpec(
        num_scalar_prefetch=2,
        grid=(nb, rows // tm),
        in_specs=[
            pl.BlockSpec((tm * sub, _LANES), lambda j, i, te, nu: (i, 0)),
            pl.BlockSpec((None, k, tn), lambda j, i, te, nu: (te[i], 0, j)),
            pl.BlockSpec((None, k, tn), lambda j, i, te, nu: (te[i], 0, j + nb)),
        ],
        out_specs=pl.BlockSpec((tm, tn), lambda j, i, te, nu: (i, j)),
        scratch_shapes=[pltpu.VMEM((k, tn), _BF16), pltpu.VMEM((k, tn), _BF16)],
    )
    return pl.pallas_call(
        _up_kernel,
        out_shape=jax.ShapeDtypeStruct((rows, d_ff), _BF16),
        grid_spec=grid_spec,
        compiler_params=_params(2),
        name="swiglu_up",
    )(te, nused, xs, w_up, w_up)


def _down_kernel(te_ref, nused_ref, h_ref, w_ref, y_ref, w_sc):
    i = pl.program_id(0)

    @pl.when(_weights_changed(te_ref, i))
    def _():
        w_sc[...] = w_ref[...].astype(_BF16)

    @pl.when(i < nused_ref[0])
    def _():
        _store_token_rows(y_ref, _dot(h_ref[...], w_sc[...]))

    @pl.when(i >= nused_ref[0])
    def _():
        y_ref[...] = jnp.zeros_like(y_ref)


def _expert_down(h16, w_down, te, nused, *, tm):
    rows, k = h16.shape
    n = w_down.shape[2]
    sub = n // _LANES
    grid_spec = pltpu.PrefetchScalarGridSpec(
        num_scalar_prefetch=2,
        grid=(rows // tm,),
        in_specs=[
            pl.BlockSpec((tm, k), lambda i, te, nu: (i, 0)),
            pl.BlockSpec((None, k, n), lambda i, te, nu: (te[i], 0, 0)),
        ],
        out_specs=pl.BlockSpec((tm * sub, _LANES), lambda i, te, nu: (i, 0)),
        scratch_shapes=[pltpu.VMEM((k, n), _BF16)],
    )
    return pl.pallas_call(
        _down_kernel,
        out_shape=jax.ShapeDtypeStruct((rows * sub, _LANES), _F32),
        grid_spec=grid_spec,
        compiler_params=_params(1),
        name="expert_down",
    )(te, nused, h16, w_down)


def _dense_ffn_kernel(x16_ref, wup_ref, wdn_ref, x_ref, lng_ref, lnb_ref, xo_ref, xo16_ref,
                      *, alpha, d_ff, chunks):
    x16 = x16_ref[...]
    f = None
    for c0, c1 in chunks:
        a = _dot(x16, wup_ref[:, c0:c1])
        b = _dot(x16, wup_ref[:, d_ff + c0:d_ff + c1])
        part = _dot((_silu(a) * b).astype(_BF16), wdn_ref[c0:c1, :])
        f = part if f is None else f + part
    out = _layer_norm(alpha * x_ref[...] + f, lng_ref[...], lnb_ref[...], _LN_EPS)
    xo_ref[...] = out
    xo16_ref[...] = out.astype(_BF16)


def _dense_ffn_ln(x, x16, w_up, w_down, ln_g, ln_b, *, alpha, tm):
    tokens, d_model = x.shape
    d_ff = w_down.shape[0]
    split = (d_ff // (2 * _MXU_COLS)) * _MXU_COLS
    chunks = ((0, split), (split, d_ff)) if 0 < split < d_ff else ((0, d_ff),)
    row = lambda i: (i, 0)
    const2 = lambda i: (0, 0)
    return pl.pallas_call(
        functools.partial(_dense_ffn_kernel, alpha=alpha, d_ff=d_ff, chunks=chunks),
        out_shape=(jax.ShapeDtypeStruct((tokens, d_model), _F32),
                   jax.ShapeDtypeStruct((tokens, d_model), _BF16)),
        grid=(tokens // tm,),
        in_specs=[pl.BlockSpec((tm, d_model), row),
                  pl.BlockSpec((d_model, 2 * d_ff), const2),
                  pl.BlockSpec((d_ff, d_model), const2),
                  pl.BlockSpec((tm, d_model), row),
                  pl.BlockSpec((1, d_model), const2), pl.BlockSpec((1, d_model), const2)],
        out_specs=(pl.BlockSpec((tm, d_model), row), pl.BlockSpec((tm, d_model), row)),
        compiler_params=_params(1),
        name="dense_ffn_ln",
    )(x16, w_up.astype(_BF16), w_down.astype(_BF16), x, ln_g.reshape(1, -1), ln_b.reshape(1, -1))


def _route_top2(x, wr_ref):
    x_hi = x.astype(_BF16)
    x_lo = (x - x_hi.astype(_F32)).astype(_BF16)
    both = _dot(x_hi, wr_ref[...])
    logits = both[:, :_LANES] + (both[:, _LANES:] + _dot(x_lo, wr_ref[:, :_LANES]))
    lane = lax.broadcasted_iota(jnp.int32, logits.shape, 1).astype(_F32)
    lg = jnp.where(lane < _N_EXPERTS, logits, -jnp.inf)
    m1 = jnp.max(lg, axis=1, keepdims=True)
    i1 = jnp.min(jnp.where(lg == m1, lane, float(_LANES)), axis=1, keepdims=True)
    lg2 = jnp.where(lane == i1, -jnp.inf, lg)
    m2 = jnp.max(lg2, axis=1, keepdims=True)
    i2 = jnp.min(jnp.where(lg2 == m2, lane, float(_LANES)), axis=1, keepdims=True)
    e2 = jnp.exp(m2 - m1)
    den = 1.0 + e2
    w1 = 1.0 / den
    w2 = e2 / den
    return jnp.where(lane == 0, i1, jnp.where(lane == 1, i2,
                     jnp.where(lane == 2, w1, jnp.where(lane == 3, w2, 0.0))))


def _router_weights(w_router):
    d_model = w_router.shape[0]
    wr = jnp.zeros((d_model, _LANES), _F32).at[:, :_N_EXPERTS].set(w_router)
    wr_hi = wr.astype(_BF16)
    wr_lo = (wr - wr_hi.astype(_F32)).astype(_BF16)
    return jnp.concatenate([wr_hi, wr_lo], axis=1)


_ROW_UNROLL = 8


def _dispatch_kernel(zstart_ref, zvalid_ref, d0_ref, d1_ref, x_ref, xs_hbm, zero_sc, stage_sc, sem, zsem,
                     *, tt, tm, sub, nsteps):
    @pl.when(pl.program_id(0) == 0)
    def _():
        zero_sc[...] = jnp.zeros_like(zero_sc)

        def zero_copy(n):
            start = pl.multiple_of(zstart_ref[n] * sub, tm * sub)
            return pltpu.make_async_copy(zero_sc, xs_hbm.at[pl.ds(start, tm * sub), :], zsem)

        for n in range(2 * _N_EXPERTS):
            @pl.when(zvalid_ref[n] > 0)
            def _():
                zero_copy(n).start()
        for n in range(2 * _N_EXPERTS):
            @pl.when(zvalid_ref[n] > 0)
            def _():
                zero_copy(n).wait()

    i = pl.program_id(0)
    slot = i % 2
    stage_sc[slot] = x_ref[...]

    def issue(g, carry):
        for u in range(_ROW_UNROLL):
            r = g * _ROW_UNROLL + u
            src = stage_sc.at[slot, pl.ds(pl.multiple_of(r * sub, sub), sub), :]
            for d_ref in (d0_ref, d1_ref):
                dst = xs_hbm.at[pl.ds(pl.multiple_of(d_ref[0, 0, r] * sub, sub), sub), :]
                pltpu.make_async_copy(src, dst, sem.at[slot]).start(priority=u % 2)
        return carry

    def drain(s):
        for _ in range(2):
            pltpu.make_async_copy(stage_sc.at[s], xs_hbm.at[pl.ds(0, tt * sub), :], sem.at[s]).wait()

    lax.fori_loop(0, tt // _ROW_UNROLL, issue, 0)

    @pl.when(i > 0)
    def _():
        drain(1 - slot)

    @pl.when(i == nsteps - 1)
    def _():
        drain(slot)


def _dispatch_rows(x_tok, d0, d1, zstart, zvalid, *, tokens, rows, tt, tm):
    sub = x_tok.shape[0] // tokens
    smem = lambda: pl.BlockSpec((1, 1, tt), lambda i, zs, zv: (i, 0, 0), memory_space=pltpu.SMEM)
    grid_spec = pltpu.PrefetchScalarGridSpec(
        num_scalar_prefetch=2,
        grid=(tokens // tt,),
        in_specs=[smem(), smem(), pl.BlockSpec((tt * sub, _LANES), lambda i, zs, zv: (i, 0))],
        out_specs=pl.BlockSpec(memory_space=pl.ANY),
        scratch_shapes=[pltpu.VMEM((tm * sub, _LANES), _F32), pltpu.VMEM((2, tt * sub, _LANES), _F32),
                        pltpu.SemaphoreType.DMA((2,)), pltpu.SemaphoreType.DMA],
    )
    return pl.pallas_call(
        functools.partial(_dispatch_kernel, tt=tt, tm=tm, sub=sub, nsteps=tokens // tt),
        out_shape=jax.ShapeDtypeStruct((rows * sub, _LANES), _F32),
        grid_spec=grid_spec,
        compiler_params=_params(1),
        name="moe_dispatch",
    )(zstart, zvalid, d0, d1, x_tok)


def _combine_kernel(d0_ref, d1_ref, n0_ref, n1_ref, y_hbm, route_ref, x_ref, lng_ref, lnb_ref,
                    xo_ref, xo16_ref, buf, sem, *, alpha, tm, sub, nsteps):
    i = pl.program_id(0)
    slot = i % 2

    def gather(dst_slot, a_ref, b_ref):
        def issue(g, carry):
            for u in range(_ROW_UNROLL):
                r = g * _ROW_UNROLL + u
                for s, d_ref in enumerate((a_ref, b_ref)):
                    src = y_hbm.at[pl.ds(pl.multiple_of(d_ref[0, 0, r] * sub, sub), sub), :]
                    dst = buf.at[dst_slot, s, pl.ds(pl.multiple_of(r * sub, sub), sub), :]
                    pltpu.make_async_copy(src, dst, sem.at[dst_slot]).start(priority=u % 2)
            return carry

        lax.fori_loop(0, tm // _ROW_UNROLL, issue, 0)

    @pl.when(i == 0)
    def _():
        gather(0, d0_ref, d1_ref)

    @pl.when(i + 1 < nsteps)
    def _():
        gather(1 - slot, n0_ref, n1_ref)

    for s in range(2):
        pltpu.make_async_copy(y_hbm.at[pl.ds(0, tm * sub), :], buf.at[slot, s], sem.at[slot]).wait()
    route = route_ref[...]
    f = (route[:, 2:3] * _load_token_rows(buf.at[slot, 0], tm)
         + route[:, 3:4] * _load_token_rows(buf.at[slot, 1], tm))
    out = _layer_norm(alpha * x_ref[...] + f, lng_ref[...], lnb_ref[...], _LN_EPS)
    xo_ref[...] = out
    xo16_ref[...] = out.astype(_BF16)


def _combine_ln(ys, d0, d1, route, x, ln_g, ln_b, *, alpha, tm):
    tokens, d_model = x.shape
    sub = d_model // _LANES
    nsteps = tokens // tm
    row = lambda i: (i, 0)
    const2 = lambda i: (0, 0)
    cur = lambda: pl.BlockSpec((1, 1, tm), lambda i: (i, 0, 0), memory_space=pltpu.SMEM)
    nxt = lambda: pl.BlockSpec((1, 1, tm), lambda i: (jnp.minimum(i + 1, nsteps - 1), 0, 0),
                               memory_space=pltpu.SMEM)
    return pl.pallas_call(
        functools.partial(_combine_kernel, alpha=alpha, tm=tm, sub=sub, nsteps=nsteps),
        out_shape=(jax.ShapeDtypeStruct((tokens, d_model), _F32),
                   jax.ShapeDtypeStruct((tokens, d_model), _BF16)),
        grid=(nsteps,),
        in_specs=[cur(), cur(), nxt(), nxt(), pl.BlockSpec(memory_space=pl.ANY),
                  pl.BlockSpec((tm, _LANES), row), pl.BlockSpec((tm, d_model), row),
                  pl.BlockSpec((1, d_model), const2), pl.BlockSpec((1, d_model), const2)],
        out_specs=(pl.BlockSpec((tm, d_model), row), pl.BlockSpec((tm, d_model), row)),
        scratch_shapes=[pltpu.VMEM((2, 2, tm * sub, _LANES), _F32), pltpu.SemaphoreType.DMA((2,))],
        compiler_params=_params(1),
        name="moe_combine_ln",
    )(d0, d1, d0, d1, ys, route, x, ln_g.reshape(1, -1), ln_b.reshape(1, -1))


def _routing_plan(route, *, tm):
    tokens = route.shape[0]
    pairs = 2 * tokens
    ntiles = pairs // tm + _N_EXPERTS
    flat_e = route[:, 0:2].astype(jnp.int32).reshape(pairs)
    onehot = (flat_e[:, None] == jnp.arange(_N_EXPERTS, dtype=jnp.int32)[None, :]).astype(jnp.int32)
    csum = jnp.cumsum(onehot, axis=0)
    counts = csum[-1]
    padded = ((counts + tm - 1) // tm) * tm
    ends = jnp.cumsum(padded)
    starts = ends - padded
    dest = jnp.sum(onehot * (starts[None, :] + csum - 1), axis=1)
    tile_start = jnp.arange(ntiles, dtype=jnp.int32) * tm
    te = jnp.minimum(jnp.sum((ends[None, :] <= tile_start[:, None]).astype(jnp.int32), axis=1),
                     _N_EXPERTS - 1)
    nused = (ends[-1] // tm).reshape(1)
    tail = ends[-1] + jnp.arange(_N_EXPERTS, dtype=jnp.int32) * tm
    zstart = jnp.concatenate([jnp.maximum(ends - tm, 0), jnp.minimum(tail, (ntiles - 1) * tm)])
    zvalid = jnp.concatenate([padded > 0, tail < ntiles * tm]).astype(jnp.int32)
    dest2 = dest.reshape(tokens, 2)
    return ntiles * tm, te, nused, dest2[:, 0], dest2[:, 1], zstart, zvalid


def _moe_ffn_ln(x, x_tok, route, w_up, w_down, ln_g, ln_b, *, alpha, tm):
    tokens = x.shape[0]
    tc = min(256, tokens)
    rows, te, nused, d0, d1, zstart, zvalid = _routing_plan(route, tm=tm)
    d0 = d0.reshape(tokens // tc, 1, tc)
    d1 = d1.reshape(tokens // tc, 1, tc)
    xs = _dispatch_rows(x_tok, d0, d1, zstart, zvalid, tokens=tokens, rows=rows, tt=tc, tm=tm)
    h = _swiglu_up(xs, w_up, te, nused, tm=tm, tn=w_up.shape[2] // 4)
    ys = _expert_down(h, w_down, te, nused, tm=tm)
    return _combine_ln(ys, d0, d1, route, x, ln_g, ln_b, alpha=alpha, tm=tc)


def _mixer_ln(x, x16, w_in, fox_b_f, gate_b, gm_w_s, gm_b_s, gm_ln_g, gm_ln_b, w_branch, w_out,
              ln_g, ln_b, w_router, *, bsz, seq, alpha, tm):
    d_model = x.shape[1]
    n_r = 2 * _RET_HEADS * _RET_DK + 2 * _RET_HEADS * _RET_DV
    n_f = 3 * _FOX_HEADS * _FOX_DH
    n_fl = _FOX_HEADS
    w_r = w_in[:, :n_r].astype(_BF16)
    w_f = jnp.concatenate([w_in[:, n_r:n_r + n_f + n_fl],
                           jnp.zeros((d_model, _LANES - n_fl), w_in.dtype)], axis=1).astype(_BF16)
    w_c = w_in[:, n_r + n_f + n_fl:].astype(_BF16)

    y_a = _retention(x16, w_r, bsz, seq)
    bias = jnp.zeros((1, _LANES), _F32).at[0, :n_fl].set(fox_b_f)
    qa, ka, va = _fox_prep(x16, w_f, bias, bsz, seq)
    y_b = _fox(qa, ka, va, bsz, seq)
    return _mixer_out(x16, y_a, y_b, w_c, gm_w_s, gm_b_s, gm_ln_g, gm_ln_b, gate_b, w_branch, w_out,
                      x, ln_g, ln_b, w_router, alpha=alpha, tm=tm)


def kernel(x, w_in, fox_b_f, gate_b, gm_w_s, gm_b_s, gm_ln_g, gm_ln_b, w_branch, w_out, ln_g, ln_b,
           dense_w_up, dense_w_down, moe_router, moe_w_up, moe_w_down):
    bsz, seq, d_model = x.shape
    depth = w_in.shape[0]
    alpha = (2 * depth) ** 0.25
    tokens = bsz * seq
    tm = min(_TM, tokens)
    xf = x.reshape(tokens, d_model)
    x16 = xf
    for l in range(depth):
        dense = l % 2 == 0
        xf, x16, route = _mixer_ln(
            xf, x16, w_in[l], fox_b_f[l], gate_b[l], gm_w_s[l], gm_b_s[l], gm_ln_g[l], gm_ln_b[l],
            w_branch[l], w_out[l], ln_g[l, 0], ln_b[l, 0], None if dense else moe_router[l // 2],
            bsz=bsz, seq=seq, alpha=alpha, tm=tm)
        if dense:
            xf, x16 = _dense_ffn_ln(xf, x16, dense_w_up[l // 2], dense_w_down[l // 2],
                                    ln_g[l, 1], ln_b[l, 1], alpha=alpha, tm=tm)
        else:
            xf, x16 = _moe_ffn_ln(xf, x16, route, moe_w_up[l // 2], moe_w_down[l // 2],
                                  ln_g[l, 1], ln_b[l, 1], alpha=alpha, tm=tm)
    return xf.reshape(bsz, seq, d_model)
```

```python
import functools
import math

import jax
import jax.numpy as jnp
import numpy as np
from jax import lax
from jax.experimental import pallas as pl
from jax.experimental.pallas import tpu as pltpu

_BF16 = jnp.bfloat16
_F32 = jnp.float32

_BLOCK = 128
_RET_HEADS, _RET_DK, _RET_DV = 4, 64, 128
_FOX_HEADS, _FOX_DH = 8, 64
_GM_GROUPS, _GM_DG = 4, 128
_N_EXPERTS = 8
_ROPE_BASE = 10000.0
_LN_EPS = 1e-5
_GN_EPS = 1e-6

_LANES = 128
_MXU_COLS = 256
_VMEM_PHYSICAL_BYTES = 64 * 1024 * 1024
_VMEM_LIMIT_BYTES = (_VMEM_PHYSICAL_BYTES * 7) // 8

_TM = 512
_TQ = 512
_TR_RET = 512


def _params(n_axes):
    return pltpu.CompilerParams(
        dimension_semantics=("arbitrary",) * n_axes,
        vmem_limit_bytes=_VMEM_LIMIT_BYTES,
    )


def _layer_norm(x, g, b, eps):
    mu = jnp.mean(x, axis=-1, keepdims=True)
    xc = x - mu
    var = jnp.mean(xc * xc, axis=-1, keepdims=True)
    return xc * lax.rsqrt(var + eps) * g + b


def _silu(x):
    return x * (1.0 / (1.0 + jnp.exp(-x)))


def _gelu_tanh(x):
    c = math.sqrt(2.0 / math.pi)
    return x * (0.5 * (1.0 + jnp.tanh(c * (x + 0.044715 * (x * x * x)))))


def _dot(a, b):
    return jnp.dot(a, b, preferred_element_type=_F32)


def _dot_nt(a, b):
    return lax.dot_general(a, b, (((1,), (1,)), ((), ())), preferred_element_type=_F32)


def _dot_tn(a, b):
    return lax.dot_general(a, b, (((0,), (0,)), ((), ())), preferred_element_type=_F32)


def _matmul_kernel(x_ref, w_ref, o_ref):
    o_ref[...] = _dot(x_ref[...], w_ref[...]).astype(o_ref.dtype)


def _matmul(x, w, *, tm, tn, out_dtype, name):
    m, k = x.shape
    n = w.shape[1]
    return pl.pallas_call(
        _matmul_kernel,
        out_shape=jax.ShapeDtypeStruct((m, n), out_dtype),
        grid=(n // tn, m // tm),
        in_specs=[pl.BlockSpec((tm, k), lambda j, i: (i, 0)),
                  pl.BlockSpec((k, tn), lambda j, i: (0, j))],
        out_specs=pl.BlockSpec((tm, tn), lambda j, i: (i, j)),
        compiler_params=_params(2),
        name=name,
    )(x, w)


def _retention_body(x16_ref, w_ref, cos_ref, sin_ref, qdec_ref, kdec_ref, din_ref,
                    y_ref, v_sc, g_sc, state_sc, *, chunk_dec):
    width = _RET_HEADS * _RET_DK
    wv = _RET_HEADS * _RET_DV
    rows_total = x16_ref.shape[0]
    lane_all = lax.broadcasted_iota(jnp.int32, (rows_total, width), 1)
    first_half = (lane_all % _RET_DK) < (_RET_DK // 2)

    def rotary(x):
        partner = jnp.where(first_half,
                            pltpu.roll(x, width - _RET_DK // 2, 1),
                            pltpu.roll(x, _RET_DK // 2, 1))
        return x * cos_ref[...] + partner * sin_ref[...]

    x16 = x16_ref[...].astype(_BF16)
    q_all = rotary(_dot(x16, w_ref[:, 0:width]))
    k_all = rotary(_dot(x16, w_ref[:, width:2 * width])) * (_RET_DK ** -0.5)
    v_sc[...] = _dot(x16, w_ref[:, 2 * width:2 * width + wv]).astype(_BF16)
    g_sc[...] = _dot(x16, w_ref[:, 2 * width + wv:2 * width + 2 * wv])
    lane = lax.broadcasted_iota(jnp.int32, (_BLOCK, width), 1)
    srow = lax.broadcasted_iota(jnp.int32, (width, wv), 0) // _RET_DK
    scol = lax.broadcasted_iota(jnp.int32, (width, wv), 1) // _RET_DV
    on_diag = srow == scol
    col_head = lax.broadcasted_iota(jnp.int32, (1, wv), 1) // _RET_DV
    decay_cols = jnp.zeros((1, wv), _F32)
    for h in range(_RET_HEADS):
        decay_cols = jnp.where(col_head == h, chunk_dec[h], decay_cols)
    for cc in range(rows_total // _BLOCK):
        rows = slice(cc * _BLOCK, (cc + 1) * _BLOCK)
        q = q_all[rows]
        k = k_all[rows]
        v = v_sc[rows, :]
        q_heads = jnp.concatenate(
            [jnp.where((lane // _RET_DK) == h, q, 0.0) for h in range(_RET_HEADS)], axis=0)
        scores = _dot_nt(q_heads.astype(_BF16), k.astype(_BF16))
        state = state_sc[...]
        cross = _dot((q * qdec_ref[...]).astype(_BF16), state.astype(_BF16))
        kv = _dot_tn((k * kdec_ref[...]).astype(_BF16), v)
        state_sc[...] = state * decay_cols + jnp.where(on_diag, kv, 0.0)
        for h in range(_RET_HEADS):
            cols = slice(h * _RET_DV, (h + 1) * _RET_DV)
            sh = scores[h * _BLOCK:(h + 1) * _BLOCK] * din_ref[h]
            out = _dot(sh.astype(_BF16), v[:, cols]) + cross[:, cols]
            mu = jnp.mean(out, axis=-1, keepdims=True)
            oc = out - mu
            var = jnp.mean(oc * oc, axis=-1, keepdims=True)
            normed = oc * lax.rsqrt(var + _GN_EPS)
            y_ref[rows, cols] = (_silu(g_sc[rows, cols]) * normed).astype(y_ref.dtype)


def _retention_tables(seq):
    half = _RET_DK // 2
    inv_freq = _ROPE_BASE ** (-jnp.arange(half, dtype=_F32) / half)
    ang = jnp.arange(seq, dtype=jnp.int32).astype(_F32)[:, None] * inv_freq[None, :]
    cos, sin = jnp.cos(ang), jnp.sin(ang)
    cos_t = jnp.tile(jnp.concatenate([cos, cos], axis=1), (1, _RET_HEADS))
    sin_t = jnp.tile(jnp.concatenate([-sin, sin], axis=1), (1, _RET_HEADS))
    log_gamma = jnp.log1p(-jnp.exp2(-5.0 - jnp.arange(_RET_HEADS, dtype=_F32)))
    idx = jnp.arange(_BLOCK, dtype=_F32)
    rel = idx[:, None] - idx[None, :]
    causal = rel >= 0
    decay_in = jnp.where(causal[None],
                         jnp.exp(log_gamma[:, None, None] * jnp.where(causal, rel, 0.0)[None]), 0.0)
    q_dec = jnp.exp(log_gamma[:, None] * (idx + 1.0))
    k_dec = jnp.exp(log_gamma[:, None] * (_BLOCK - 1.0 - idx))
    q_dec_t = jnp.repeat(q_dec.T, _RET_DK, axis=1)
    k_dec_t = jnp.repeat(k_dec.T, _RET_DK, axis=1)
    return cos_t, sin_t, q_dec_t, k_dec_t, decay_in


def _mixer_ab_kernel(x_ref, wr_ref, cos_ref, sin_ref, qdec_ref, kdec_ref, din_ref, wf_ref, b_ref, sel_ref,
                     y_ref, qa_ref, ka_ref, va_ref,
                     v_sc, g_sc, state_sc, fq_sc, fk_sc, fv_sc, fl_sc, carry_sc, *, chunk_dec, nblk):
    @pl.when(pl.program_id(1) == 0)
    def _():
        state_sc[...] = jnp.zeros_like(state_sc)
        carry_sc[...] = jnp.zeros_like(carry_sc)

    _fox_prep_body(x_ref, wf_ref, b_ref, sel_ref, qa_ref, ka_ref, va_ref, fq_sc, fk_sc, fv_sc, fl_sc,
                   carry_sc, nblk=nblk)
    _retention_body(x_ref, wr_ref, cos_ref, sin_ref, qdec_ref, kdec_ref, din_ref, y_ref, v_sc, g_sc,
                    state_sc, chunk_dec=chunk_dec)


def _mixer_ab(x16, w_r, w_f, bias, bsz, seq):
    tokens, d_model = x16.shape
    tr = min(_TR_RET, seq)
    nstep = seq // tr
    cos_t, sin_t, q_dec_t, k_dec_t, decay_in = _retention_tables(seq)
    chunk_dec = tuple(float((1.0 - 2.0 ** (-5.0 - h)) ** _BLOCK) for h in range(_RET_HEADS))
    qk = _RET_HEADS * _RET_DK
    wv = _RET_HEADS * _RET_DV
    fw = _FOX_HEADS * _FOX_DH
    sel = _fox_selectors()
    row = lambda b, c: (b * nstep + c, 0)
    const2 = lambda b, c: (0, 0)
    const3 = lambda b, c: (0, 0, 0)
    aug = jax.ShapeDtypeStruct((bsz, _FOX_HEADS, seq, _LANES), _BF16)
    aug_spec = pl.BlockSpec((None, _FOX_HEADS, tr, _LANES), lambda b, c: (b, 0, c, 0))
    return pl.pallas_call(
        functools.partial(_mixer_ab_kernel, chunk_dec=chunk_dec, nblk=tr // _BLOCK),
        out_shape=(jax.ShapeDtypeStruct((tokens, wv), _BF16), aug, aug, aug),
        grid=(bsz, nstep),
        in_specs=[
            pl.BlockSpec((tr, d_model), row),
            pl.BlockSpec(w_r.shape, const2),
            pl.BlockSpec((tr, qk), lambda b, c: (c, 0)),
            pl.BlockSpec((tr, qk), lambda b, c: (c, 0)),
            pl.BlockSpec((_BLOCK, qk), const2),
            pl.BlockSpec((_BLOCK, qk), const2),
            pl.BlockSpec((_RET_HEADS, _BLOCK, _BLOCK), const3),
            pl.BlockSpec(w_f.shape, const2),
            pl.BlockSpec((1, _LANES), const2),
            pl.BlockSpec(sel.shape, const3),
        ],
        out_specs=(pl.BlockSpec((tr, wv), row), aug_spec, aug_spec, aug_spec),
        scratch_shapes=[pltpu.VMEM((tr, wv), _BF16), pltpu.VMEM((tr, wv), _F32),
                        pltpu.VMEM((qk, wv), _F32),
                        pltpu.VMEM((tr, fw), _BF16), pltpu.VMEM((tr, fw), _BF16),
                        pltpu.VMEM((tr, fw), _BF16), pltpu.VMEM((tr, _LANES), _F32),
                        pltpu.VMEM((1, _LANES), _F32)],
        compiler_params=_params(2),
        name="mixer_ab",
    )(x16, w_r, cos_t, sin_t, q_dec_t, k_dec_t, decay_in, w_f, bias, sel)


_C_PARTS = 3
_TS_PREP = 512


def _fox_selectors():
    half = _LANES // 2
    one_row = _C_PARTS * _FOX_HEADS
    sel = np.zeros((_FOX_HEADS, _LANES, 2 * _LANES), np.float32)
    for h in range(_FOX_HEADS):
        base = half if h % 2 == 0 else 0
        for n in range(_C_PARTS):
            sel[h, n * _FOX_HEADS + h, base + n] = 1.0
            sel[h, one_row, base + _C_PARTS + n] = 1.0
            sel[h, one_row, _LANES + base + n] = 1.0
            sel[h, n * _FOX_HEADS + h, _LANES + base + _C_PARTS + n] = -1.0
    return jnp.asarray(sel, _BF16)


def _fox_prep_body(x16_ref, w_ref, b_ref, sel_ref, qa_ref, ka_ref, va_ref, fq_ref, fk_ref, fv_ref,
                   fl_ref, carry_sc, *, nblk):
    x16 = x16_ref[...].astype(_BF16)
    width = _FOX_HEADS * _FOX_DH
    for n, dst in enumerate((fq_ref, fk_ref, fv_ref)):
        dst[...] = _dot(x16, w_ref[:, n * width:(n + 1) * width]).astype(_BF16)
    fl_ref[...] = _dot(x16, w_ref[:, 3 * width:3 * width + _LANES])

    r = lax.broadcasted_iota(jnp.int32, (_BLOCK, _LANES), 0)
    lane = lax.broadcasted_iota(jnp.int32, (_BLOCK, _LANES), 1)
    tri = (r >= lane).astype(_F32)
    half = _LANES // 2
    q_scale = jnp.asarray(_FOX_DH ** -0.5, _BF16)
    carry = carry_sc[...]
    packed = []
    for blk in range(nblk):
        rows = slice(blk * _BLOCK, (blk + 1) * _BLOCK)
        z = fl_ref[rows, :] + b_ref[...]
        log_f = jnp.minimum(z, 0.0) - jnp.log1p(jnp.exp(-jnp.abs(z)))
        cum = jnp.dot(tri, log_f, preferred_element_type=_F32,
                      precision=lax.Precision.HIGHEST) + carry
        carry = cum[_BLOCK - 1:_BLOCK, :]
        row = jnp.where(lane == _C_PARTS * _FOX_HEADS, 1.0, 0.0)
        rest = cum
        for n in range(_C_PARTS):
            piece = rest.astype(_BF16).astype(_F32)
            rest = rest - piece
            shifted = piece if n == 0 else pltpu.roll(piece, n * _FOX_HEADS, 1)
            row = jnp.where((lane >= n * _FOX_HEADS) & (lane < (n + 1) * _FOX_HEADS), shifted, row)
        packed.append(row.astype(_BF16))
    carry_sc[...] = carry
    packed = jnp.concatenate(packed, axis=0)
    lane_all = lax.broadcasted_iota(jnp.int32, (nblk * _BLOCK, _LANES), 1)
    for h in range(_FOX_HEADS):
        pair = slice((h // 2) * _LANES, (h // 2 + 1) * _LANES)
        in_head = (lane_all < half) if h % 2 == 0 else (lane_all >= half)
        extra = _dot(packed, sel_ref[h]).astype(_BF16)
        qa_ref[h] = jnp.where(in_head, fq_ref[:, pair] * q_scale, extra[:, :_LANES])
        ka_ref[h] = jnp.where(in_head, fk_ref[:, pair], extra[:, _LANES:])
        va_ref[h] = jnp.where(in_head, fv_ref[:, pair], jnp.ones((), _BF16))


def _fox_kernel(qa_ref, ka_ref, va_ref, o_ref, m0_sc, m1_sc, acc0_sc, acc1_sc, *, tq):
    i = pl.program_id(2)
    m_scs = (m0_sc, m1_sc)
    acc_scs = (acc0_sc, acc1_sc)
    for s in range(2):
        m_scs[s][...] = jnp.full_like(m_scs[s], -jnp.inf)
        acc_scs[s][...] = jnp.zeros_like(acc_scs[s])

    def block(start, nkeys, row0, nrows, masked):
        rows = slice(row0, row0 + nrows)
        logits = [_dot_nt(qa_ref[s, rows, :], ka_ref[s, pl.ds(start, nkeys), :]) for s in range(2)]
        if masked:
            keep = (lax.broadcasted_iota(jnp.int32, (nrows, nkeys), 0)
                    >= lax.broadcasted_iota(jnp.int32, (nrows, nkeys), 1))
        pexp, scale, m_new = [], [], []
        for s in range(2):
            lg = jnp.where(keep, logits[s], -jnp.inf) if masked else logits[s]
            m_old = m_scs[s][rows, :]
            m_new.append(jnp.maximum(m_old, jnp.max(lg, axis=1, keepdims=True)))
            pexp.append(jnp.exp(lg - jnp.concatenate([m_new[s]] * (nkeys // _LANES), axis=1)))
            scale.append(jnp.exp(m_old - m_new[s]))
        pv = [_dot(pexp[s].astype(_BF16), va_ref[s, pl.ds(start, nkeys), :]) for s in range(2)]
        for s in range(2):
            acc_scs[s][rows, :] = scale[s] * acc_scs[s][rows, :] + pv[s]
            m_scs[s][rows, :] = m_new[s]

    def body(jj, carry):
        block(pl.multiple_of(jj * 2 * tq, 2 * tq), 2 * tq, 0, tq, False)
        return carry

    lax.fori_loop(0, i // 2, body, 0)

    @pl.when(i % 2 == 1)
    def _():
        block(pl.multiple_of((i - 1) * tq, tq), tq, 0, tq, False)

    block(pl.multiple_of(i * tq, tq), tq, 0, tq, True)
    half = _LANES // 2
    acc0 = acc0_sc[...]
    acc1 = acc1_sc[...]
    low = lax.broadcasted_iota(jnp.int32, (tq, _LANES), 1) < half
    out = jnp.where(low, acc0 / acc0[:, half:half + 1], acc1 / acc1[:, 0:1])
    o_ref[...] = out.astype(o_ref.dtype)


def _fox(qa, ka, va, bsz, seq):
    tq = min(_TQ, seq)
    nq = seq // tq
    npair = _FOX_HEADS // 2
    return pl.pallas_call(
        functools.partial(_fox_kernel, tq=tq),
        out_shape=jax.ShapeDtypeStruct((bsz * seq, _FOX_HEADS * _FOX_DH), _BF16),
        grid=(bsz, npair, nq),
        in_specs=[
            pl.BlockSpec((None, 2, tq, _LANES), lambda b, p, i: (b, p, i, 0)),
            pl.BlockSpec((None, 2, seq, _LANES), lambda b, p, i: (b, p, 0, 0)),
            pl.BlockSpec((None, 2, seq, _LANES), lambda b, p, i: (b, p, 0, 0)),
        ],
        out_specs=pl.BlockSpec((tq, _LANES), lambda b, p, i: (b * nq + i, p)),
        scratch_shapes=[pltpu.VMEM((tq, _LANES), _F32), pltpu.VMEM((tq, _LANES), _F32),
                        pltpu.VMEM((tq, _LANES), _F32), pltpu.VMEM((tq, _LANES), _F32)],
        compiler_params=_params(3),
        name="fox_attention",
    )(qa, ka, va)


def _gmlp_tile(gu, gv, w_ref, bst_ref, lng_ref, lnb_ref, yc_sc):
    u = _gelu_tanh(gu)
    v = _layer_norm(_gelu_tanh(gv), lng_ref[...], lnb_ref[...], _LN_EPS).astype(_BF16)
    r = lax.broadcasted_iota(jnp.int32, (_BLOCK, _BLOCK), 0)
    c = lax.broadcasted_iota(jnp.int32, (_BLOCK, _BLOCK), 1)
    causal = r >= c
    nchunk = gu.shape[0] // _BLOCK
    for g in range(_GM_GROUPS):
        cols = slice(g * _GM_DG, (g + 1) * _GM_DG)
        wg = jnp.where(causal, w_ref[g], 0.0).astype(_BF16)
        bias = bst_ref[:, g:g + 1]
        v_chunks = jnp.concatenate(
            [v[ch * _BLOCK:(ch + 1) * _BLOCK, cols] for ch in range(nchunk)], axis=1)
        sp = _dot(wg, v_chunks) + bias
        for ch in range(nchunk):
            rows = slice(ch * _BLOCK, (ch + 1) * _BLOCK)
            yc_sc[rows, cols] = (u[rows, cols] * sp[:, ch * _GM_DG:(ch + 1) * _GM_DG]).astype(yc_sc.dtype)


def _mixer_out_kernel(x16_ref, ya_ref, yb_ref, wc_ref, gmw_ref, bst_ref, gmg_ref, gmb_ref, gbias_ref,
                      wbr_ref, wout_ref, x_ref, lng_ref, lnb_ref, *rest, alpha, d_model, with_router):
    if with_router:
        wr_ref, xo_ref, xtok_ref, route_ref, yc_sc = rest
    else:
        xo_ref, xo16_ref, yc_sc = rest
    x16 = x16_ref[...].astype(_BF16)
    gw = _GM_GROUPS * _GM_DG
    gu = _dot(x16, wc_ref[:, 0:gw])
    gv = _dot(x16, wc_ref[:, gw:2 * gw])

    def gate(n):
        cols = slice(2 * gw + n * d_model, 2 * gw + (n + 1) * d_model)
        z = _dot(x16, wc_ref[:, cols]) + gbias_ref[:, n * d_model:(n + 1) * d_model]
        return 1.0 / (1.0 + jnp.exp(-z))

    merged = gate(0) * _dot(ya_ref[...], wbr_ref[0]) + gate(1) * _dot(yb_ref[...], wbr_ref[1])
    gate_c = gate(2)
    _gmlp_tile(gu, gv, gmw_ref, bst_ref, gmg_ref, gmb_ref, yc_sc)
    merged = merged + gate_c * _dot(yc_sc[...], wbr_ref[2])
    h = _dot(merged.astype(_BF16), wout_ref[...])
    out = _layer_norm(alpha * x_ref[...] + h, lng_ref[...], lnb_ref[...], _LN_EPS)
    xo_ref[...] = out
    if with_router:
        _store_token_rows(xtok_ref, out)
        route_ref[...] = _route_top2(out, wr_ref)
    else:
        xo16_ref[...] = out.astype(_BF16)


def _mixer_out(x16, y_a, y_b, w_c, gm_w_s, gm_b_s, gm_ln_g, gm_ln_b, gate_b, w_branch, w_out, x,
               ln_g, ln_b, w_router, *, alpha, tm):
    tokens, d_model = x.shape
    bw = y_a.shape[1]
    gw = _GM_GROUPS * _GM_DG
    row = lambda i: (i, 0)
    const2 = lambda i: (0, 0)
    const3 = lambda i: (0, 0, 0)
    with_router = w_router is not None
    in_specs = [
        pl.BlockSpec((tm, d_model), row),
        pl.BlockSpec((tm, bw), row), pl.BlockSpec((tm, bw), row),
        pl.BlockSpec(w_c.shape, const2),
        pl.BlockSpec((_GM_GROUPS, _BLOCK, _BLOCK), const3),
        pl.BlockSpec((_BLOCK, _GM_GROUPS), const2),
        pl.BlockSpec((1, gw), const2), pl.BlockSpec((1, gw), const2),
        pl.BlockSpec((1, 3 * d_model), const2),
        pl.BlockSpec((3, bw, d_model), const3),
        pl.BlockSpec((d_model, d_model), const2),
        pl.BlockSpec((tm, d_model), row),
        pl.BlockSpec((1, d_model), const2), pl.BlockSpec((1, d_model), const2),
    ]
    args = [x16, y_a, y_b, w_c, gm_w_s, gm_b_s.T, gm_ln_g.reshape(1, gw), gm_ln_b.reshape(1, gw),
            gate_b.reshape(1, -1), w_branch.astype(_BF16), w_out.astype(_BF16), x,
            ln_g.reshape(1, -1), ln_b.reshape(1, -1)]
    out_shape = [jax.ShapeDtypeStruct((tokens, d_model), _F32)]
    out_specs = [pl.BlockSpec((tm, d_model), row)]
    if with_router:
        sub = d_model // _LANES
        in_specs.append(pl.BlockSpec((d_model, 2 * _LANES), const2))
        args.append(_router_weights(w_router))
        out_shape += [jax.ShapeDtypeStruct((tokens * sub, _LANES), _F32),
                      jax.ShapeDtypeStruct((tokens, _LANES), _F32)]
        out_specs += [pl.BlockSpec((tm * sub, _LANES), row), pl.BlockSpec((tm, _LANES), row)]
    else:
        out_shape.append(jax.ShapeDtypeStruct((tokens, d_model), _BF16))
        out_specs.append(pl.BlockSpec((tm, d_model), row))
    outs = pl.pallas_call(
        functools.partial(_mixer_out_kernel, alpha=alpha, d_model=d_model, with_router=with_router),
        out_shape=tuple(out_shape),
        grid=(tokens // tm,),
        in_specs=in_specs,
        out_specs=tuple(out_specs),
        scratch_shapes=[pltpu.VMEM((tm, gw), _BF16)],
        compiler_params=_params(1),
        name="mixer_out",
    )(*args)
    return outs if with_router else (*outs, None)


def _weights_changed(te_ref, i):
    return jnp.logical_or(i == 0, te_ref[i] != te_ref[jnp.maximum(i - 1, 0)])


def _load_token_rows(ref, n):
    s = ref.shape[0] // n
    return jnp.concatenate([ref[pl.ds(c, n, stride=s), :] for c in range(s)], axis=1)


def _store_token_rows(ref, x):
    n = x.shape[0]
    s = x.shape[1] // _LANES
    for c in range(s):
        ref[pl.ds(c, n, stride=s), :] = x[:, c * _LANES:(c + 1) * _LANES]


def _up_kernel(te_ref, nused_ref, x_ref, wa_ref, wb_ref, h_ref, wa_sc, wb_sc):
    i = pl.program_id(1)

    @pl.when(_weights_changed(te_ref, i))
    def _():
        wa_sc[...] = wa_ref[...].astype(_BF16)
        wb_sc[...] = wb_ref[...].astype(_BF16)

    @pl.when(i < nused_ref[0])
    def _():
        x = _load_token_rows(x_ref, h_ref.shape[0]).astype(_BF16)
        a = _dot(x, wa_sc[...])
        b = _dot(x, wb_sc[...])
        h_ref[...] = (_silu(a) * b).astype(h_ref.dtype)

    @pl.when(i >= nused_ref[0])
    def _():
        h_ref[...] = jnp.zeros_like(h_ref)


def _swiglu_up(xs, w_up, te, nused, *, tm, tn):
    k = w_up.shape[1]
    sub = k // _LANES
    rows = xs.shape[0] // sub
    d_ff = w_up.shape[2] // 2
    nb = d_ff // tn
    grid_spec = pltpu.PrefetchScalarGridSpec(
        num_scalar_prefetch=2,
        grid=(nb, rows // tm),
        in_specs=[
            pl.BlockSpec((tm * sub, _LANES), lambda j, i, te, nu: (i, 0)),
            pl.BlockSpec((None, k, tn), lambda j, i, te, nu: (te[i], 0, j)),
            pl.BlockSpec((None, k, tn), lambda j, i, te, nu: (te[i], 0, j + nb)),
        ],
        out_specs=pl.BlockSpec((tm, tn), lambda j, i, te, nu: (i, j)),
        scratch_shapes=[pltpu.VMEM((k, tn), _BF16), pltpu.VMEM((k, tn), _BF16)],
    )
    return pl.pallas_call(
        _up_kernel,
        out_shape=jax.ShapeDtypeStruct((rows, d_ff), _BF16),
        grid_spec=grid_spec,
        compiler_params=_params(2),
        name="swiglu_up",
    )(te, nused, xs, w_up, w_up)


def _down_kernel(te_ref, nused_ref, h_ref, w_ref, y_ref, w_sc):
    i = pl.program_id(0)

    @pl.when(_weights_changed(te_ref, i))
    def _():
        w_sc[...] = w_ref[...].astype(_BF16)

    @pl.when(i < nused_ref[0])
    def _():
        _store_token_rows(y_ref, _dot(h_ref[...], w_sc[...]))

    @pl.when(i >= nused_ref[0])
    def _():
        y_ref[...] = jnp.zeros_like(y_ref)


def _expert_down(h16, w_down, te, nused, *, tm):
    rows, k = h16.shape
    n = w_down.shape[2]
    sub = n // _LANES
    grid_spec = pltpu.PrefetchScalarGridSpec(
        num_scalar_prefetch=2,
        grid=(rows // tm,),
        in_specs=[
            pl.BlockSpec((tm, k), lambda i, te, nu: (i, 0)),
            pl.BlockSpec((None, k, n), lambda i, te, nu: (te[i], 0, 0)),
        ],
        out_specs=pl.BlockSpec((tm * sub, _LANES), lambda i, te, nu: (i, 0)),
        scratch_shapes=[pltpu.VMEM((k, n), _BF16)],
    )
    return pl.pallas_call(
        _down_kernel,
        out_shape=jax.ShapeDtypeStruct((rows * sub, _LANES), _F32),
        grid_spec=grid_spec,
        compiler_params=_params(1),
        name="expert_down",
    )(te, nused, h16, w_down)


def _dense_ffn_kernel(x16_ref, wup_ref, wdn_ref, x_ref, lng_ref, lnb_ref, xo_ref, xo16_ref,
                      *, alpha, d_ff, chunks):
    x16 = x16_ref[...]
    f = None
    for c0, c1 in chunks:
        a = _dot(x16, wup_ref[:, c0:c1])
        b = _dot(x16, wup_ref[:, d_ff + c0:d_ff + c1])
        part = _dot((_silu(a) * b).astype(_BF16), wdn_ref[c0:c1, :])
        f = part if f is None else f + part
    out = _layer_norm(alpha * x_ref[...] + f, lng_ref[...], lnb_ref[...], _LN_EPS)
    xo_ref[...] = out
    xo16_ref[...] = out.astype(_BF16)


def _dense_ffn_ln(x, x16, w_up, w_down, ln_g, ln_b, *, alpha, tm):
    tokens, d_model = x.shape
    d_ff = w_down.shape[0]
    split = (d_ff // (2 * _MXU_COLS)) * _MXU_COLS
    chunks = ((0, split), (split, d_ff)) if 0 < split < d_ff else ((0, d_ff),)
    row = lambda i: (i, 0)
    const2 = lambda i: (0, 0)
    return pl.pallas_call(
        functools.partial(_dense_ffn_kernel, alpha=alpha, d_ff=d_ff, chunks=chunks),
        out_shape=(jax.ShapeDtypeStruct((tokens, d_model), _F32),
                   jax.ShapeDtypeStruct((tokens, d_model), _BF16)),
        grid=(tokens // tm,),
        in_specs=[pl.BlockSpec((tm, d_model), row),
                  pl.BlockSpec((d_model, 2 * d_ff), const2),
                  pl.BlockSpec((d_ff, d_model), const2),
                  pl.BlockSpec((tm, d_model), row),
                  pl.BlockSpec((1, d_model), const2), pl.BlockSpec((1, d_model), const2)],
        out_specs=(pl.BlockSpec((tm, d_model), row), pl.BlockSpec((tm, d_model), row)),
        compiler_params=_params(1),
        name="dense_ffn_ln",
    )(x16, w_up.astype(_BF16), w_down.astype(_BF16), x, ln_g.reshape(1, -1), ln_b.reshape(1, -1))


def _route_top2(x, wr_ref):
    x_hi = x.astype(_BF16)
    x_lo = (x - x_hi.astype(_F32)).astype(_BF16)
    both = _dot(x_hi, wr_ref[...])
    logits = both[:, :_LANES] + (both[:, _LANES:] + _dot(x_lo, wr_ref[:, :_LANES]))
    lane = lax.broadcasted_iota(jnp.int32, logits.shape, 1).astype(_F32)
    lg = jnp.where(lane < _N_EXPERTS, logits, -jnp.inf)
    m1 = jnp.max(lg, axis=1, keepdims=True)
    i1 = jnp.min(jnp.where(lg == m1, lane, float(_LANES)), axis=1, keepdims=True)
    lg2 = jnp.where(lane == i1, -jnp.inf, lg)
    m2 = jnp.max(lg2, axis=1, keepdims=True)
    i2 = jnp.min(jnp.where(lg2 == m2, lane, float(_LANES)), axis=1, keepdims=True)
    e2 = jnp.exp(m2 - m1)
    den = 1.0 + e2
    w1 = 1.0 / den
    w2 = e2 / den
    return jnp.where(lane == 0, i1, jnp.where(lane == 1, i2,
                     jnp.where(lane == 2, w1, jnp.where(lane == 3, w2, 0.0))))


def _router_weights(w_router):
    d_model = w_router.shape[0]
    wr = jnp.zeros((d_model, _LANES), _F32).at[:, :_N_EXPERTS].set(w_router)
    wr_hi = wr.astype(_BF16)
    wr_lo = (wr - wr_hi.astype(_F32)).astype(_BF16)
    return jnp.concatenate([wr_hi, wr_lo], axis=1)


_ROW_UNROLL = 8


def _dispatch_kernel(zstart_ref, zvalid_ref, d0_ref, d1_ref, x_ref, xs_hbm, zero_sc, stage_sc, sem, zsem,
                     *, tt, tm, sub, nsteps):
    @pl.when(pl.program_id(0) == 0)
    def _():
        zero_sc[...] = jnp.zeros_like(zero_sc)

        def zero_copy(n):
            start = pl.multiple_of(zstart_ref[n] * sub, tm * sub)
            return pltpu.make_async_copy(zero_sc, xs_hbm.at[pl.ds(start, tm * sub), :], zsem)

        for n in range(2 * _N_EXPERTS):
            @pl.when(zvalid_ref[n] > 0)
            def _():
                zero_copy(n).start()
        for n in range(2 * _N_EXPERTS):
            @pl.when(zvalid_ref[n] > 0)
            def _():
                zero_copy(n).wait()

    i = pl.program_id(0)
    slot = i % 2
    stage_sc[slot] = x_ref[...]

    def issue(g, carry):
        for u in range(_ROW_UNROLL):
            r = g * _ROW_UNROLL + u
            src = stage_sc.at[slot, pl.ds(pl.multiple_of(r * sub, sub), sub), :]
            for d_ref in (d0_ref, d1_ref):
                dst = xs_hbm.at[pl.ds(pl.multiple_of(d_ref[0, 0, r] * sub, sub), sub), :]
                pltpu.make_async_copy(src, dst, sem.at[slot]).start(priority=u % 2)
        return carry

    def drain(s):
        for _ in range(2):
            pltpu.make_async_copy(stage_sc.at[s], xs_hbm.at[pl.ds(0, tt * sub), :], sem.at[s]).wait()

    lax.fori_loop(0, tt // _ROW_UNROLL, issue, 0)

    @pl.when(i > 0)
    def _():
        drain(1 - slot)

    @pl.when(i == nsteps - 1)
    def _():
        drain(slot)


def _dispatch_rows(x_tok, d0, d1, zstart, zvalid, *, tokens, rows, tt, tm):
    sub = x_tok.shape[0] // tokens
    smem = lambda: pl.BlockSpec((1, 1, tt), lambda i, zs, zv: (i, 0, 0), memory_space=pltpu.SMEM)
    grid_spec = pltpu.PrefetchScalarGridSpec(
        num_scalar_prefetch=2,
        grid=(tokens // tt,),
        in_specs=[smem(), smem(), pl.BlockSpec((tt * sub, _LANES), lambda i, zs, zv: (i, 0))],
        out_specs=pl.BlockSpec(memory_space=pl.ANY),
        scratch_shapes=[pltpu.VMEM((tm * sub, _LANES), _F32), pltpu.VMEM((2, tt * sub, _LANES), _F32),
                        pltpu.SemaphoreType.DMA((2,)), pltpu.SemaphoreType.DMA],
    )
    return pl.pallas_call(
        functools.partial(_dispatch_kernel, tt=tt, tm=tm, sub=sub, nsteps=tokens // tt),
        out_shape=jax.ShapeDtypeStruct((rows * sub, _LANES), _F32),
        grid_spec=grid_spec,
        compiler_params=_params(1),
        name="moe_dispatch",
    )(zstart, zvalid, d0, d1, x_tok)


def _combine_kernel(d0_ref, d1_ref, n0_ref, n1_ref, y_hbm, route_ref, x_ref, lng_ref, lnb_ref,
                    xo_ref, xo16_ref, buf, sem, *, alpha, tm, sub, nsteps):
    i = pl.program_id(0)
    slot = i % 2

    def gather(dst_slot, a_ref, b_ref):
        def issue(g, carry):
            for u in range(_ROW_UNROLL):
                r = g * _ROW_UNROLL + u
                for s, d_ref in enumerate((a_ref, b_ref)):
                    src = y_hbm.at[pl.ds(pl.multiple_of(d_ref[0, 0, r] * sub, sub), sub), :]
                    dst = buf.at[dst_slot, s, pl.ds(pl.multiple_of(r * sub, sub), sub), :]
                    pltpu.make_async_copy(src, dst, sem.at[dst_slot]).start(priority=u % 2)
            return carry

        lax.fori_loop(0, tm // _ROW_UNROLL, issue, 0)

    @pl.when(i == 0)
    def _():
        gather(0, d0_ref, d1_ref)

    @pl.when(i + 1 < nsteps)
    def _():
        gather(1 - slot, n0_ref, n1_ref)

    for s in range(2):
        pltpu.make_async_copy(y_hbm.at[pl.ds(0, tm * sub), :], buf.at[slot, s], sem.at[slot]).wait()
    route = route_ref[...]
    f = (route[:, 2:3] * _load_token_rows(buf.at[slot, 0], tm)
         + route[:, 3:4] * _load_token_rows(buf.at[slot, 1], tm))
    out = _layer_norm(alpha * x_ref[...] + f, lng_ref[...], lnb_ref[...], _LN_EPS)
    xo_ref[...] = out
    xo16_ref[...] = out.astype(_BF16)


def _combine_ln(ys, d0, d1, route, x, ln_g, ln_b, *, alpha, tm):
    tokens, d_model = x.shape
    sub = d_model // _LANES
    nsteps = tokens // tm
    row = lambda i: (i, 0)
    const2 = lambda i: (0, 0)
    cur = lambda: pl.BlockSpec((1, 1, tm), lambda i: (i, 0, 0), memory_space=pltpu.SMEM)
    nxt = lambda: pl.BlockSpec((1, 1, tm), lambda i: (jnp.minimum(i + 1, nsteps - 1), 0, 0),
                               memory_space=pltpu.SMEM)
    return pl.pallas_call(
        functools.partial(_combine_kernel, alpha=alpha, tm=tm, sub=sub, nsteps=nsteps),
        out_shape=(jax.ShapeDtypeStruct((tokens, d_model), _F32),
                   jax.ShapeDtypeStruct((tokens, d_model), _BF16)),
        grid=(nsteps,),
        in_specs=[cur(), cur(), nxt(), nxt(), pl.BlockSpec(memory_space=pl.ANY),
                  pl.BlockSpec((tm, _LANES), row), pl.BlockSpec((tm, d_model), row),
                  pl.BlockSpec((1, d_model), const2), pl.BlockSpec((1, d_model), const2)],
        out_specs=(pl.BlockSpec((tm, d_model), row), pl.BlockSpec((tm, d_model), row)),
        scratch_shapes=[pltpu.VMEM((2, 2, tm * sub, _LANES), _F32), pltpu.SemaphoreType.DMA((2,))],
        compiler_params=_params(1),
        name="moe_combine_ln",
    )(d0, d1, d0, d1, ys, route, x, ln_g.reshape(1, -1), ln_b.reshape(1, -1))


def _routing_plan(route, *, tm):
    tokens = route.shape[0]
    ntiles = (2 * tokens) // tm + _N_EXPERTS
    choice = route[:, 0:2].T.astype(jnp.int32)
    experts = jnp.arange(_N_EXPERTS, dtype=jnp.int32)[:, None]
    onehot = [(choice[k][None, :] == experts).astype(jnp.int32) for k in range(2)]
    csum = [jnp.cumsum(oh, axis=1) for oh in onehot]
    first_counts = csum[0][:, -1]
    counts = first_counts + csum[1][:, -1]
    padded = ((counts + tm - 1) // tm) * tm
    ends = jnp.cumsum(padded)
    starts = ends - padded
    base = [starts, starts + first_counts]
    dest = [jnp.sum(onehot[k] * (base[k][:, None] + csum[k] - 1), axis=0) for k in range(2)]
    tile_start = jnp.arange(ntiles, dtype=jnp.int32) * tm
    te = jnp.minimum(jnp.sum((ends[None, :] <= tile_start[:, None]).astype(jnp.int32), axis=1),
                     _N_EXPERTS - 1)
    nused = (ends[-1] // tm).reshape(1)
    tail = ends[-1] + jnp.arange(_N_EXPERTS, dtype=jnp.int32) * tm
    zstart = jnp.concatenate([jnp.maximum(ends - tm, 0), jnp.minimum(tail, (ntiles - 1) * tm)])
    zvalid = jnp.concatenate([padded > 0, tail < ntiles * tm]).astype(jnp.int32)
    return ntiles * tm, te, nused, dest[0], dest[1], zstart, zvalid


def _moe_ffn_ln(x, x_tok, route, w_up, w_down, ln_g, ln_b, *, alpha, tm):
    tokens = x.shape[0]
    tc = min(256, tokens)
    rows, te, nused, d0, d1, zstart, zvalid = _routing_plan(route, tm=tm)
    d0 = d0.reshape(tokens // tc, 1, tc)
    d1 = d1.reshape(tokens // tc, 1, tc)
    xs = _dispatch_rows(x_tok, d0, d1, zstart, zvalid, tokens=tokens, rows=rows, tt=tc, tm=tm)
    h = _swiglu_up(xs, w_up, te, nused, tm=tm, tn=w_up.shape[2] // 4)
    ys = _expert_down(h, w_down, te, nused, tm=tm)
    return _combine_ln(ys, d0, d1, route, x, ln_g, ln_b, alpha=alpha, tm=tc)


def _mixer_ln(x, x16, w_in, fox_b_f, gate_b, gm_w_s, gm_b_s, gm_ln_g, gm_ln_b, w_branch, w_out,
              ln_g, ln_b, w_router, *, bsz, seq, alpha, tm):
    d_model = x.shape[1]
    n_r = 2 * _RET_HEADS * _RET_DK + 2 * _RET_HEADS * _RET_DV
    n_f = 3 * _FOX_HEADS * _FOX_DH
    n_fl = _FOX_HEADS
    w_r = w_in[:, :n_r].astype(_BF16)
    w_f = jnp.concatenate([w_in[:, n_r:n_r + n_f + n_fl],
                           jnp.zeros((d_model, _LANES - n_fl), w_in.dtype)], axis=1).astype(_BF16)
    w_c = w_in[:, n_r + n_f + n_fl:].astype(_BF16)

    bias = jnp.zeros((1, _LANES), _F32).at[0, :n_fl].set(fox_b_f)
    y_a, qa, ka, va = _mixer_ab(x16, w_r, w_f, bias, bsz, seq)
    y_b = _fox(qa, ka, va, bsz, seq)
    return _mixer_out(x16, y_a, y_b, w_c, gm_w_s, gm_b_s, gm_ln_g, gm_ln_b, gate_b, w_branch, w_out,
                      x, ln_g, ln_b, w_router, alpha=alpha, tm=tm)


def kernel(x, w_in, fox_b_f, gate_b, gm_w_s, gm_b_s, gm_ln_g, gm_ln_b, w_branch, w_out, ln_g, ln_b,
           dense_w_up, dense_w_down, moe_router, moe_w_up, moe_w_down):
    bsz, seq, d_model = x.shape
    depth = w_in.shape[0]
    alpha = (2 * depth) ** 0.25
    tokens = bsz * seq
    tm = min(_TM, tokens)
    xf = x.reshape(tokens, d_model)
    x16 = xf
    for l in range(depth):
        dense = l % 2 == 0
        xf, x16, route = _mixer_ln(
            xf, x16, w_in[l], fox_b_f[l], gate_b[l], gm_w_s[l], gm_b_s[l], gm_ln_g[l], gm_ln_b[l],
            w_branch[l], w_out[l], ln_g[l, 0], ln_b[l, 0], None if dense else moe_router[l // 2],
            bsz=bsz, seq=seq, alpha=alpha, tm=tm)
        if dense:
            xf, x16 = _dense_ffn_ln(xf, x16, dense_w_up[l // 2], dense_w_down[l // 2],
                                    ln_g[l, 1], ln_b[l, 1], alpha=alpha, tm=tm)
        else:
            xf, x16 = _moe_ffn_ln(xf, x16, route, moe_w_up[l // 2], moe_w_down[l // 2],
                                  ln_g[l, 1], ln_b[l, 1], alpha=alpha, tm=tm)
    return xf.reshape(bsz, seq, d_model)
```

```python
import functools
import math

import jax
import jax.numpy as jnp
import numpy as np
from jax import lax
from jax.experimental import pallas as pl
from jax.experimental.pallas import tpu as pltpu

_BF16 = jnp.bfloat16
_F32 = jnp.float32

_BLOCK = 128
_RET_HEADS, _RET_DK, _RET_DV = 4, 64, 128
_FOX_HEADS, _FOX_DH = 8, 64
_GM_GROUPS, _GM_DG = 4, 128
_N_EXPERTS = 8
_ROPE_BASE = 10000.0
_LN_EPS = 1e-5
_GN_EPS = 1e-6

_LANES = 128
_MXU_COLS = 256
_VMEM_PHYSICAL_BYTES = 64 * 1024 * 1024
_VMEM_LIMIT_BYTES = (_VMEM_PHYSICAL_BYTES * 7) // 8

_TM = 512
_TQ = 512
_TR_RET = 512


def _params(n_axes):
    return pltpu.CompilerParams(
        dimension_semantics=("arbitrary",) * n_axes,
        vmem_limit_bytes=_VMEM_LIMIT_BYTES,
    )


def _layer_norm(x, g, b, eps):
    mu = jnp.mean(x, axis=-1, keepdims=True)
    xc = x - mu
    var = jnp.mean(xc * xc, axis=-1, keepdims=True)
    return xc * lax.rsqrt(var + eps) * g + b


def _silu(x):
    return x * (1.0 / (1.0 + jnp.exp(-x)))


def _gelu_tanh(x):
    c = math.sqrt(2.0 / math.pi)
    return x * (0.5 * (1.0 + jnp.tanh(c * (x + 0.044715 * (x * x * x)))))


def _dot(a, b):
    return jnp.dot(a, b, preferred_element_type=_F32)


def _dot_nt(a, b):
    return lax.dot_general(a, b, (((1,), (1,)), ((), ())), preferred_element_type=_F32)


def _dot_tn(a, b):
    return lax.dot_general(a, b, (((0,), (0,)), ((), ())), preferred_element_type=_F32)


def _matmul_kernel(x_ref, w_ref, o_ref):
    o_ref[...] = _dot(x_ref[...], w_ref[...]).astype(o_ref.dtype)


def _matmul(x, w, *, tm, tn, out_dtype, name):
    m, k = x.shape
    n = w.shape[1]
    return pl.pallas_call(
        _matmul_kernel,
        out_shape=jax.ShapeDtypeStruct((m, n), out_dtype),
        grid=(n // tn, m // tm),
        in_specs=[pl.BlockSpec((tm, k), lambda j, i: (i, 0)),
                  pl.BlockSpec((k, tn), lambda j, i: (0, j))],
        out_specs=pl.BlockSpec((tm, tn), lambda j, i: (i, j)),
        compiler_params=_params(2),
        name=name,
    )(x, w)


def _retention_body(x16_ref, w_ref, cos_ref, sin_ref, qdec_ref, kdec_ref, din_ref,
                    y_ref, v_sc, g_sc, state_sc, *, chunk_dec):
    width = _RET_HEADS * _RET_DK
    wv = _RET_HEADS * _RET_DV
    rows_total = x16_ref.shape[0]
    lane_all = lax.broadcasted_iota(jnp.int32, (rows_total, width), 1)
    first_half = (lane_all % _RET_DK) < (_RET_DK // 2)

    def rotary(x):
        partner = jnp.where(first_half,
                            pltpu.roll(x, width - _RET_DK // 2, 1),
                            pltpu.roll(x, _RET_DK // 2, 1))
        return x * cos_ref[...] + partner * sin_ref[...]

    x16 = x16_ref[...].astype(_BF16)
    q_all = rotary(_dot(x16, w_ref[:, 0:width]))
    k_all = rotary(_dot(x16, w_ref[:, width:2 * width])) * (_RET_DK ** -0.5)
    v_sc[...] = _dot(x16, w_ref[:, 2 * width:2 * width + wv]).astype(_BF16)
    g_sc[...] = _dot(x16, w_ref[:, 2 * width + wv:2 * width + 2 * wv])
    lane = lax.broadcasted_iota(jnp.int32, (_BLOCK, width), 1)
    srow = lax.broadcasted_iota(jnp.int32, (width, wv), 0) // _RET_DK
    scol = lax.broadcasted_iota(jnp.int32, (width, wv), 1) // _RET_DV
    on_diag = srow == scol
    col_head = lax.broadcasted_iota(jnp.int32, (1, wv), 1) // _RET_DV
    decay_cols = jnp.zeros((1, wv), _F32)
    for h in range(_RET_HEADS):
        decay_cols = jnp.where(col_head == h, chunk_dec[h], decay_cols)
    for cc in range(rows_total // _BLOCK):
        rows = slice(cc * _BLOCK, (cc + 1) * _BLOCK)
        q = q_all[rows]
        k = k_all[rows]
        v = v_sc[rows, :]
        q_heads = jnp.concatenate(
            [jnp.where((lane // _RET_DK) == h, q, 0.0) for h in range(_RET_HEADS)], axis=0)
        scores = _dot_nt(q_heads.astype(_BF16), k.astype(_BF16))
        state = state_sc[...]
        cross = _dot((q * qdec_ref[...]).astype(_BF16), state.astype(_BF16))
        kv = _dot_tn((k * kdec_ref[...]).astype(_BF16), v)
        state_sc[...] = state * decay_cols + jnp.where(on_diag, kv, 0.0)
        for h in range(_RET_HEADS):
            cols = slice(h * _RET_DV, (h + 1) * _RET_DV)
            sh = scores[h * _BLOCK:(h + 1) * _BLOCK] * din_ref[h]
            out = _dot(sh.astype(_BF16), v[:, cols]) + cross[:, cols]
            mu = jnp.mean(out, axis=-1, keepdims=True)
            oc = out - mu
            var = jnp.mean(oc * oc, axis=-1, keepdims=True)
            normed = oc * lax.rsqrt(var + _GN_EPS)
            y_ref[rows, cols] = (_silu(g_sc[rows, cols]) * normed).astype(y_ref.dtype)


def _retention_tables(seq):
    half = _RET_DK // 2
    inv_freq = _ROPE_BASE ** (-jnp.arange(half, dtype=_F32) / half)
    ang = jnp.arange(seq, dtype=jnp.int32).astype(_F32)[:, None] * inv_freq[None, :]
    cos, sin = jnp.cos(ang), jnp.sin(ang)
    cos_t = jnp.tile(jnp.concatenate([cos, cos], axis=1), (1, _RET_HEADS))
    sin_t = jnp.tile(jnp.concatenate([-sin, sin], axis=1), (1, _RET_HEADS))
    log_gamma = jnp.log1p(-jnp.exp2(-5.0 - jnp.arange(_RET_HEADS, dtype=_F32)))
    idx = jnp.arange(_BLOCK, dtype=_F32)
    rel = idx[:, None] - idx[None, :]
    causal = rel >= 0
    decay_in = jnp.where(causal[None],
                         jnp.exp(log_gamma[:, None, None] * jnp.where(causal, rel, 0.0)[None]), 0.0)
    q_dec = jnp.exp(log_gamma[:, None] * (idx + 1.0))
    k_dec = jnp.exp(log_gamma[:, None] * (_BLOCK - 1.0 - idx))
    q_dec_t = jnp.repeat(q_dec.T, _RET_DK, axis=1)
    k_dec_t = jnp.repeat(k_dec.T, _RET_DK, axis=1)
    return cos_t, sin_t, q_dec_t, k_dec_t, decay_in


def _mixer_ab_kernel(x_ref, wr_ref, cos_ref, sin_ref, qdec_ref, kdec_ref, din_ref, wf_ref, b_ref, sel_ref,
                     y_ref, qa_ref, ka_ref, va_ref,
                     v_sc, g_sc, state_sc, fq_sc, fk_sc, fv_sc, fl_sc, carry_sc, *, chunk_dec, nblk):
    @pl.when(pl.program_id(1) == 0)
    def _():
        state_sc[...] = jnp.zeros_like(state_sc)
        carry_sc[...] = jnp.zeros_like(carry_sc)

    _fox_prep_body(x_ref, wf_ref, b_ref, sel_ref, qa_ref, ka_ref, va_ref, fq_sc, fk_sc, fv_sc, fl_sc,
                   carry_sc, nblk=nblk)
    _retention_body(x_ref, wr_ref, cos_ref, sin_ref, qdec_ref, kdec_ref, din_ref, y_ref, v_sc, g_sc,
                    state_sc, chunk_dec=chunk_dec)


def _mixer_ab(x16, w_r, w_f, bias, bsz, seq):
    tokens, d_model = x16.shape
    tr = min(_TR_RET, seq)
    nstep = seq // tr
    cos_t, sin_t, q_dec_t, k_dec_t, decay_in = _retention_tables(seq)
    chunk_dec = tuple(float((1.0 - 2.0 ** (-5.0 - h)) ** _BLOCK) for h in range(_RET_HEADS))
    qk = _RET_HEADS * _RET_DK
    wv = _RET_HEADS * _RET_DV
    fw = _FOX_HEADS * _FOX_DH
    sel = _fox_selectors()
    row = lambda b, c: (b * nstep + c, 0)
    const2 = lambda b, c: (0, 0)
    const3 = lambda b, c: (0, 0, 0)
    aug = jax.ShapeDtypeStruct((bsz, _FOX_HEADS, seq, _LANES), _BF16)
    aug_spec = pl.BlockSpec((None, _FOX_HEADS, tr, _LANES), lambda b, c: (b, 0, c, 0))
    return pl.pallas_call(
        functools.partial(_mixer_ab_kernel, chunk_dec=chunk_dec, nblk=tr // _BLOCK),
        out_shape=(jax.ShapeDtypeStruct((tokens, wv), _BF16), aug, aug, aug),
        grid=(bsz, nstep),
        in_specs=[
            pl.BlockSpec((tr, d_model), row),
            pl.BlockSpec(w_r.shape, const2),
            pl.BlockSpec((tr, qk), lambda b, c: (c, 0)),
            pl.BlockSpec((tr, qk), lambda b, c: (c, 0)),
            pl.BlockSpec((_BLOCK, qk), const2),
            pl.BlockSpec((_BLOCK, qk), const2),
            pl.BlockSpec((_RET_HEADS, _BLOCK, _BLOCK), const3),
            pl.BlockSpec(w_f.shape, const2),
            pl.BlockSpec((1, _LANES), const2),
            pl.BlockSpec(sel.shape, const3),
        ],
        out_specs=(pl.BlockSpec((tr, wv), row), aug_spec, aug_spec, aug_spec),
        scratch_shapes=[pltpu.VMEM((tr, wv), _BF16), pltpu.VMEM((tr, wv), _F32),
                        pltpu.VMEM((qk, wv), _F32),
                        pltpu.VMEM((tr, fw), _BF16), pltpu.VMEM((tr, fw), _BF16),
                        pltpu.VMEM((tr, fw), _BF16), pltpu.VMEM((tr, _LANES), _F32),
                        pltpu.VMEM((1, _LANES), _F32)],
        compiler_params=_params(2),
        name="mixer_ab",
    )(x16, w_r, cos_t, sin_t, q_dec_t, k_dec_t, decay_in, w_f, bias, sel)


_C_PARTS = 3
_TS_PREP = 512


def _fox_selectors():
    half = _LANES // 2
    one_row = _C_PARTS * _FOX_HEADS
    sel = np.zeros((_FOX_HEADS, _LANES, 2 * _LANES), np.float32)
    for h in range(_FOX_HEADS):
        base = half if h % 2 == 0 else 0
        for n in range(_C_PARTS):
            sel[h, n * _FOX_HEADS + h, base + n] = 1.0
            sel[h, one_row, base + _C_PARTS + n] = 1.0
            sel[h, one_row, _LANES + base + n] = 1.0
            sel[h, n * _FOX_HEADS + h, _LANES + base + _C_PARTS + n] = -1.0
    return jnp.asarray(sel, _BF16)


def _fox_prep_body(x16_ref, w_ref, b_ref, sel_ref, qa_ref, ka_ref, va_ref, fq_ref, fk_ref, fv_ref,
                   fl_ref, carry_sc, *, nblk):
    x16 = x16_ref[...].astype(_BF16)
    width = _FOX_HEADS * _FOX_DH
    for n, dst in enumerate((fq_ref, fk_ref, fv_ref)):
        dst[...] = _dot(x16, w_ref[:, n * width:(n + 1) * width]).astype(_BF16)
    fl_ref[...] = _dot(x16, w_ref[:, 3 * width:3 * width + _LANES])

    r = lax.broadcasted_iota(jnp.int32, (_BLOCK, _LANES), 0)
    lane = lax.broadcasted_iota(jnp.int32, (_BLOCK, _LANES), 1)
    tri = (r >= lane).astype(_F32)
    half = _LANES // 2
    q_scale = jnp.asarray(_FOX_DH ** -0.5, _BF16)
    carry = carry_sc[...]
    packed = []
    for blk in range(nblk):
        rows = slice(blk * _BLOCK, (blk + 1) * _BLOCK)
        z = fl_ref[rows, :] + b_ref[...]
        log_f = jnp.minimum(z, 0.0) - jnp.log1p(jnp.exp(-jnp.abs(z)))
        cum = jnp.dot(tri, log_f, preferred_element_type=_F32,
                      precision=lax.Precision.HIGHEST) + carry
        carry = cum[_BLOCK - 1:_BLOCK, :]
        row = jnp.where(lane == _C_PARTS * _FOX_HEADS, 1.0, 0.0)
        rest = cum
        for n in range(_C_PARTS):
            piece = rest.astype(_BF16).astype(_F32)
            rest = rest - piece
            shifted = piece if n == 0 else pltpu.roll(piece, n * _FOX_HEADS, 1)
            row = jnp.where((lane >= n * _FOX_HEADS) & (lane < (n + 1) * _FOX_HEADS), shifted, row)
        packed.append(row.astype(_BF16))
    carry_sc[...] = carry
    packed = jnp.concatenate(packed, axis=0)
    lane_all = lax.broadcasted_iota(jnp.int32, (nblk * _BLOCK, _LANES), 1)
    for h in range(_FOX_HEADS):
        pair = slice((h // 2) * _LANES, (h // 2 + 1) * _LANES)
        in_head = (lane_all < half) if h % 2 == 0 else (lane_all >= half)
        extra = _dot(packed, sel_ref[h]).astype(_BF16)
        qa_ref[h] = jnp.where(in_head, fq_ref[:, pair] * q_scale, extra[:, :_LANES])
        ka_ref[h] = jnp.where(in_head, fk_ref[:, pair], extra[:, _LANES:])
        va_ref[h] = jnp.where(in_head, fv_ref[:, pair], jnp.ones((), _BF16))


def _fox_kernel(qa_ref, ka_ref, va_ref, o_ref, *, tq, nq):
    half = _LANES // 2
    low = lax.broadcasted_iota(jnp.int32, (tq, _LANES), 1) < half
    keep = (lax.broadcasted_iota(jnp.int32, (tq, tq), 0)
            >= lax.broadcasted_iota(jnp.int32, (tq, tq), 1))
    order = list(reversed(range(nq)))
    logits = {n: [_dot_nt(qa_ref[s, n * tq:(n + 1) * tq, :], ka_ref[s, 0:(n + 1) * tq, :])
                  for s in range(2)] for n in order}
    for n in order:
        rows = slice(n * tq, (n + 1) * tq)
        width = (n + 1) * tq
        pv = []
        for s in range(2):
            diag = jnp.where(keep, logits[n][s][:, width - tq:], -jnp.inf)
            lg = diag if n == 0 else jnp.concatenate([logits[n][s][:, :width - tq], diag], axis=1)
            pexp = jnp.exp(lg - jnp.max(lg, axis=1, keepdims=True))
            pv.append(_dot(pexp.astype(_BF16), va_ref[s, 0:width, :]))
        out = jnp.where(low, pv[0] / pv[0][:, half:half + 1], pv[1] / pv[1][:, 0:1])
        o_ref[rows, :] = out.astype(o_ref.dtype)


def _fox(qa, ka, va, bsz, seq):
    tq = min(_TQ, seq)
    nq = seq // tq
    npair = _FOX_HEADS // 2
    logit_bytes = 2 * tq * tq * (nq * (nq + 1) // 2) * 4
    assert logit_bytes <= _VMEM_LIMIT_BYTES // 2, "sequence too long for the single-pass attention step"
    return pl.pallas_call(
        functools.partial(_fox_kernel, tq=tq, nq=nq),
        out_shape=jax.ShapeDtypeStruct((bsz * seq, _FOX_HEADS * _FOX_DH), _BF16),
        grid=(bsz, npair),
        in_specs=[
            pl.BlockSpec((None, 2, seq, _LANES), lambda b, p: (b, p, 0, 0)),
            pl.BlockSpec((None, 2, seq, _LANES), lambda b, p: (b, p, 0, 0)),
            pl.BlockSpec((None, 2, seq, _LANES), lambda b, p: (b, p, 0, 0)),
        ],
        out_specs=pl.BlockSpec((seq, _LANES), lambda b, p: (b, p)),
        compiler_params=_params(2),
        name="fox_attention",
    )(qa, ka, va)


def _gmlp_tile(gu, gv, w_ref, bst_ref, lng_ref, lnb_ref, yc_sc):
    u = _gelu_tanh(gu)
    v = _layer_norm(_gelu_tanh(gv), lng_ref[...], lnb_ref[...], _LN_EPS).astype(_BF16)
    r = lax.broadcasted_iota(jnp.int32, (_BLOCK, _BLOCK), 0)
    c = lax.broadcasted_iota(jnp.int32, (_BLOCK, _BLOCK), 1)
    causal = r >= c
    nchunk = gu.shape[0] // _BLOCK
    for g in range(_GM_GROUPS):
        cols = slice(g * _GM_DG, (g + 1) * _GM_DG)
        wg = jnp.where(causal, w_ref[g], 0.0).astype(_BF16)
        bias = bst_ref[:, g:g + 1]
        v_chunks = jnp.concatenate(
            [v[ch * _BLOCK:(ch + 1) * _BLOCK, cols] for ch in range(nchunk)], axis=1)
        sp = _dot(wg, v_chunks) + bias
        for ch in range(nchunk):
            rows = slice(ch * _BLOCK, (ch + 1) * _BLOCK)
            yc_sc[rows, cols] = (u[rows, cols] * sp[:, ch * _GM_DG:(ch + 1) * _GM_DG]).astype(yc_sc.dtype)


def _mixer_out_kernel(x16_ref, ya_ref, yb_ref, wc_ref, gmw_ref, bst_ref, gmg_ref, gmb_ref, gbias_ref,
                      wbr_ref, wout_ref, x_ref, lng_ref, lnb_ref, *rest, alpha, d_model, with_router):
    if with_router:
        wr_ref, xo_ref, xtok_ref, route_ref, yc_sc = rest
    else:
        xo_ref, xo16_ref, yc_sc = rest
    x16 = x16_ref[...].astype(_BF16)
    gw = _GM_GROUPS * _GM_DG
    gu = _dot(x16, wc_ref[:, 0:gw])
    gv = _dot(x16, wc_ref[:, gw:2 * gw])

    def gate(n):
        cols = slice(2 * gw + n * d_model, 2 * gw + (n + 1) * d_model)
        z = _dot(x16, wc_ref[:, cols]) + gbias_ref[:, n * d_model:(n + 1) * d_model]
        return 1.0 / (1.0 + jnp.exp(-z))

    merged = gate(0) * _dot(ya_ref[...], wbr_ref[0]) + gate(1) * _dot(yb_ref[...], wbr_ref[1])
    gate_c = gate(2)
    _gmlp_tile(gu, gv, gmw_ref, bst_ref, gmg_ref, gmb_ref, yc_sc)
    merged = merged + gate_c * _dot(yc_sc[...], wbr_ref[2])
    h = _dot(merged.astype(_BF16), wout_ref[...])
    out = _layer_norm(alpha * x_ref[...] + h, lng_ref[...], lnb_ref[...], _LN_EPS)
    xo_ref[...] = out
    if with_router:
        _store_token_rows(xtok_ref, out)
        route_ref[...] = _route_top2(out, wr_ref)
    else:
        xo16_ref[...] = out.astype(_BF16)


def _mixer_out(x16, y_a, y_b, w_c, gm_w_s, gm_b_s, gm_ln_g, gm_ln_b, gate_b, w_branch, w_out, x,
               ln_g, ln_b, w_router, *, alpha, tm):
    tokens, d_model = x.shape
    bw = y_a.shape[1]
    gw = _GM_GROUPS * _GM_DG
    row = lambda i: (i, 0)
    const2 = lambda i: (0, 0)
    const3 = lambda i: (0, 0, 0)
    with_router = w_router is not None
    in_specs = [
        pl.BlockSpec((tm, d_model), row),
        pl.BlockSpec((tm, bw), row), pl.BlockSpec((tm, bw), row),
        pl.BlockSpec(w_c.shape, const2),
        pl.BlockSpec((_GM_GROUPS, _BLOCK, _BLOCK), const3),
        pl.BlockSpec((_BLOCK, _GM_GROUPS), const2),
        pl.BlockSpec((1, gw), const2), pl.BlockSpec((1, gw), const2),
        pl.BlockSpec((1, 3 * d_model), const2),
        pl.BlockSpec((3, bw, d_model), const3),
        pl.BlockSpec((d_model, d_model), const2),
        pl.BlockSpec((tm, d_model), row),
        pl.BlockSpec((1, d_model), const2), pl.BlockSpec((1, d_model), const2),
    ]
    args = [x16, y_a, y_b, w_c, gm_w_s, gm_b_s.T, gm_ln_g.reshape(1, gw), gm_ln_b.reshape(1, gw),
            gate_b.reshape(1, -1), w_branch.astype(_BF16), w_out.astype(_BF16), x,
            ln_g.reshape(1, -1), ln_b.reshape(1, -1)]
    out_shape = [jax.ShapeDtypeStruct((tokens, d_model), _F32)]
    out_specs = [pl.BlockSpec((tm, d_model), row)]
    if with_router:
        sub = d_model // _LANES
        in_specs.append(pl.BlockSpec((d_model, 2 * _LANES), const2))
        args.append(_router_weights(w_router))
        out_shape += [jax.ShapeDtypeStruct((tokens * sub, _LANES), _F32),
                      jax.ShapeDtypeStruct((tokens, _LANES), _F32)]
        out_specs += [pl.BlockSpec((tm * sub, _LANES), row), pl.BlockSpec((tm, _LANES), row)]
    else:
        out_shape.append(jax.ShapeDtypeStruct((tokens, d_model), _BF16))
        out_specs.append(pl.BlockSpec((tm, d_model), row))
    outs = pl.pallas_call(
        functools.partial(_mixer_out_kernel, alpha=alpha, d_model=d_model, with_router=with_router),
        out_shape=tuple(out_shape),
        grid=(tokens // tm,),
        in_specs=in_specs,
        out_specs=tuple(out_specs),
        scratch_shapes=[pltpu.VMEM((tm, gw), _BF16)],
        compiler_params=_params(1),
        name="mixer_out",
    )(*args)
    return outs if with_router else (*outs, None)


def _weights_changed(te_ref, i):
    return jnp.logical_or(i == 0, te_ref[i] != te_ref[jnp.maximum(i - 1, 0)])


def _load_token_rows(ref, n):
    s = ref.shape[0] // n
    return jnp.concatenate([ref[pl.ds(c, n, stride=s), :] for c in range(s)], axis=1)


def _store_token_rows(ref, x):
    n = x.shape[0]
    s = x.shape[1] // _LANES
    for c in range(s):
        ref[pl.ds(c, n, stride=s), :] = x[:, c * _LANES:(c + 1) * _LANES]


def _up_kernel(te_ref, nused_ref, x_ref, wa_ref, wb_ref, h_ref, wa_sc, wb_sc):
    i = pl.program_id(1)

    @pl.when(_weights_changed(te_ref, i))
    def _():
        wa_sc[...] = wa_ref[...].astype(_BF16)
        wb_sc[...] = wb_ref[...].astype(_BF16)

    @pl.when(i < nused_ref[0])
    def _():
        x = _load_token_rows(x_ref, h_ref.shape[0]).astype(_BF16)
        a = _dot(x, wa_sc[...])
        b = _dot(x, wb_sc[...])
        h_ref[...] = (_silu(a) * b).astype(h_ref.dtype)

    @pl.when(i >= nused_ref[0])
    def _():
        h_ref[...] = jnp.zeros_like(h_ref)


def _swiglu_up(xs, w_up, te, nused, *, tm, tn):
    k = w_up.shape[1]
    sub = k // _LANES
    rows = xs.shape[0] // sub
    d_ff = w_up.shape[2] // 2
    nb = d_ff // tn
    grid_spec = pltpu.PrefetchScalarGridSpec(
        num_scalar_prefetch=2,
        grid=(nb, rows // tm),
        in_specs=[
            pl.BlockSpec((tm * sub, _LANES), lambda j, i, te, nu: (i, 0)),
            pl.BlockSpec((None, k, tn), lambda j, i, te, nu: (te[i], 0, j)),
            pl.BlockSpec((None, k, tn), lambda j, i, te, nu: (te[i], 0, j + nb)),
        ],
        out_specs=pl.BlockSpec((tm, tn), lambda j, i, te, nu: (i, j)),
        scratch_shapes=[pltpu.VMEM((k, tn), _BF16), pltpu.VMEM((k, tn), _BF16)],
    )
    return pl.pallas_call(
        _up_kernel,
        out_shape=jax.ShapeDtypeStruct((rows, d_ff), _BF16),
        grid_spec=grid_spec,
        compiler_params=_params(2),
        name="swiglu_up",
    )(te, nused, xs, w_up, w_up)


def _down_kernel(te_ref, nused_ref, h_ref, w_ref, y_ref, w_sc):
    i = pl.program_id(0)

    @pl.when(_weights_changed(te_ref, i))
    def _():
        w_sc[...] = w_ref[...].astype(_BF16)

    @pl.when(i < nused_ref[0])
    def _():
        _store_token_rows(y_ref, _dot(h_ref[...], w_sc[...]))

    @pl.when(i >= nused_ref[0])
    def _():
        y_ref[...] = jnp.zeros_like(y_ref)


def _expert_down(h16, w_down, te, nused, *, tm):
    rows, k = h16.shape
    n = w_down.shape[2]
    sub = n // _LANES
    grid_spec = pltpu.PrefetchScalarGridSpec(
        num_scalar_prefetch=2,
        grid=(rows // tm,),
        in_specs=[
            pl.BlockSpec((tm, k), lambda i, te, nu: (i, 0)),
            pl.BlockSpec((None, k, n), lambda i, te, nu: (te[i], 0, 0)),
        ],
        out_specs=pl.BlockSpec((tm * sub, _LANES), lambda i, te, nu: (i, 0)),
        scratch_shapes=[pltpu.VMEM((k, n), _BF16)],
    )
    return pl.pallas_call(
        _down_kernel,
        out_shape=jax.ShapeDtypeStruct((rows * sub, _LANES), _F32),
        grid_spec=grid_spec,
        compiler_params=_params(1),
        name="expert_down",
    )(te, nused, h16, w_down)


def _dense_ffn_kernel(x16_ref, wup_ref, wdn_ref, x_ref, lng_ref, lnb_ref, xo_ref, xo16_ref,
                      *, alpha, d_ff, chunks):
    x16 = x16_ref[...]
    f = None
    for c0, c1 in chunks:
        a = _dot(x16, wup_ref[:, c0:c1])
        b = _dot(x16, wup_ref[:, d_ff + c0:d_ff + c1])
        part = _dot((_silu(a) * b).astype(_BF16), wdn_ref[c0:c1, :])
        f = part if f is None else f + part
    out = _layer_norm(alpha * x_ref[...] + f, lng_ref[...], lnb_ref[...], _LN_EPS)
    xo_ref[...] = out
    xo16_ref[...] = out.astype(_BF16)


def _dense_ffn_ln(x, x16, w_up, w_down, ln_g, ln_b, *, alpha, tm):
    tokens, d_model = x.shape
    d_ff = w_down.shape[0]
    split = (d_ff // (2 * _MXU_COLS)) * _MXU_COLS
    chunks = ((0, split), (split, d_ff)) if 0 < split < d_ff else ((0, d_ff),)
    row = lambda i: (i, 0)
    const2 = lambda i: (0, 0)
    return pl.pallas_call(
        functools.partial(_dense_ffn_kernel, alpha=alpha, d_ff=d_ff, chunks=chunks),
        out_shape=(jax.ShapeDtypeStruct((tokens, d_model), _F32),
                   jax.ShapeDtypeStruct((tokens, d_model), _BF16)),
        grid=(tokens // tm,),
        in_specs=[pl.BlockSpec((tm, d_model), row),
                  pl.BlockSpec((d_model, 2 * d_ff), const2),
                  pl.BlockSpec((d_ff, d_model), const2),
                  pl.BlockSpec((tm, d_model), row),
                  pl.BlockSpec((1, d_model), const2), pl.BlockSpec((1, d_model), const2)],
        out_specs=(pl.BlockSpec((tm, d_model), row), pl.BlockSpec((tm, d_model), row)),
        compiler_params=_params(1),
        name="dense_ffn_ln",
    )(x16, w_up.astype(_BF16), w_down.astype(_BF16), x, ln_g.reshape(1, -1), ln_b.reshape(1, -1))


def _route_top2(x, wr_ref):
    x_hi = x.astype(_BF16)
    x_lo = (x - x_hi.astype(_F32)).astype(_BF16)
    both = _dot(x_hi, wr_ref[...])
    logits = both[:, :_LANES] + (both[:, _LANES:] + _dot(x_lo, wr_ref[:, :_LANES]))
    lane = lax.broadcasted_iota(jnp.int32, logits.shape, 1).astype(_F32)
    lg = jnp.where(lane < _N_EXPERTS, logits, -jnp.inf)
    m1 = jnp.max(lg, axis=1, keepdims=True)
    i1 = jnp.min(jnp.where(lg == m1, lane, float(_LANES)), axis=1, keepdims=True)
    lg2 = jnp.where(lane == i1, -jnp.inf, lg)
    m2 = jnp.max(lg2, axis=1, keepdims=True)
    i2 = jnp.min(jnp.where(lg2 == m2, lane, float(_LANES)), axis=1, keepdims=True)
    e2 = jnp.exp(m2 - m1)
    den = 1.0 + e2
    w1 = 1.0 / den
    w2 = e2 / den
    return jnp.where(lane == 0, i1, jnp.where(lane == 1, i2,
                     jnp.where(lane == 2, w1, jnp.where(lane == 3, w2, 0.0))))


def _router_weights(w_router):
    d_model = w_router.shape[0]
    wr = jnp.zeros((d_model, _LANES), _F32).at[:, :_N_EXPERTS].set(w_router)
    wr_hi = wr.astype(_BF16)
    wr_lo = (wr - wr_hi.astype(_F32)).astype(_BF16)
    return jnp.concatenate([wr_hi, wr_lo], axis=1)


_ROW_UNROLL = 8


def _dispatch_kernel(zstart_ref, zvalid_ref, d0_ref, d1_ref, x_ref, xs_hbm, zero_sc, stage_sc, sem, zsem,
                     *, tt, tm, sub, nsteps):
    @pl.when(pl.program_id(0) == 0)
    def _():
        zero_sc[...] = jnp.zeros_like(zero_sc)

        def zero_copy(n):
            start = pl.multiple_of(zstart_ref[n] * sub, tm * sub)
            return pltpu.make_async_copy(zero_sc, xs_hbm.at[pl.ds(start, tm * sub), :], zsem)

        for n in range(2 * _N_EXPERTS):
            @pl.when(zvalid_ref[n] > 0)
            def _():
                zero_copy(n).start()
        for n in range(2 * _N_EXPERTS):
            @pl.when(zvalid_ref[n] > 0)
            def _():
                zero_copy(n).wait()

    i = pl.program_id(0)
    slot = i % 2
    stage_sc[slot] = x_ref[...]

    def issue(g, carry):
        for u in range(_ROW_UNROLL):
            r = g * _ROW_UNROLL + u
            src = stage_sc.at[slot, pl.ds(pl.multiple_of(r * sub, sub), sub), :]
            for d_ref in (d0_ref, d1_ref):
                dst = xs_hbm.at[pl.ds(pl.multiple_of(d_ref[0, 0, r] * sub, sub), sub), :]
                pltpu.make_async_copy(src, dst, sem.at[slot]).start(priority=u % 2)
        return carry

    def drain(s):
        for _ in range(2):
            pltpu.make_async_copy(stage_sc.at[s], xs_hbm.at[pl.ds(0, tt * sub), :], sem.at[s]).wait()

    lax.fori_loop(0, tt // _ROW_UNROLL, issue, 0)

    @pl.when(i > 0)
    def _():
        drain(1 - slot)

    @pl.when(i == nsteps - 1)
    def _():
        drain(slot)


def _dispatch_rows(x_tok, d0, d1, zstart, zvalid, *, tokens, rows, tt, tm):
    sub = x_tok.shape[0] // tokens
    smem = lambda: pl.BlockSpec((1, 1, tt), lambda i, zs, zv: (i, 0, 0), memory_space=pltpu.SMEM)
    grid_spec = pltpu.PrefetchScalarGridSpec(
        num_scalar_prefetch=2,
        grid=(tokens // tt,),
        in_specs=[smem(), smem(), pl.BlockSpec((tt * sub, _LANES), lambda i, zs, zv: (i, 0))],
        out_specs=pl.BlockSpec(memory_space=pl.ANY),
        scratch_shapes=[pltpu.VMEM((tm * sub, _LANES), _F32), pltpu.VMEM((2, tt * sub, _LANES), _F32),
                        pltpu.SemaphoreType.DMA((2,)), pltpu.SemaphoreType.DMA],
    )
    return pl.pallas_call(
        functools.partial(_dispatch_kernel, tt=tt, tm=tm, sub=sub, nsteps=tokens // tt),
        out_shape=jax.ShapeDtypeStruct((rows * sub, _LANES), _F32),
        grid_spec=grid_spec,
        compiler_params=_params(1),
        name="moe_dispatch",
    )(zstart, zvalid, d0, d1, x_tok)


def _combine_kernel(d0_ref, d1_ref, n0_ref, n1_ref, y_hbm, route_ref, x_ref, lng_ref, lnb_ref,
                    xo_ref, xo16_ref, buf, sem, *, alpha, tm, sub, nsteps):
    i = pl.program_id(0)
    slot = i % 2

    def gather(dst_slot, a_ref, b_ref):
        def issue(g, carry):
            for u in range(_ROW_UNROLL):
                r = g * _ROW_UNROLL + u
                for s, d_ref in enumerate((a_ref, b_ref)):
                    src = y_hbm.at[pl.ds(pl.multiple_of(d_ref[0, 0, r] * sub, sub), sub), :]
                    dst = buf.at[dst_slot, s, pl.ds(pl.multiple_of(r * sub, sub), sub), :]
                    pltpu.make_async_copy(src, dst, sem.at[dst_slot]).start(priority=u % 2)
            return carry

        lax.fori_loop(0, tm // _ROW_UNROLL, issue, 0)

    @pl.when(i == 0)
    def _():
        gather(0, d0_ref, d1_ref)

    @pl.when(i + 1 < nsteps)
    def _():
        gather(1 - slot, n0_ref, n1_ref)

    for s in range(2):
        pltpu.make_async_copy(y_hbm.at[pl.ds(0, tm * sub), :], buf.at[slot, s], sem.at[slot]).wait()
    route = route_ref[...]
    f = (route[:, 2:3] * _load_token_rows(buf.at[slot, 0], tm)
         + route[:, 3:4] * _load_token_rows(buf.at[slot, 1], tm))
    out = _layer_norm(alpha * x_ref[...] + f, lng_ref[...], lnb_ref[...], _LN_EPS)
    xo_ref[...] = out
    xo16_ref[...] = out.astype(_BF16)


def _combine_ln(ys, d0, d1, route, x, ln_g, ln_b, *, alpha, tm):
    tokens, d_model = x.shape
    sub = d_model // _LANES
    nsteps = tokens // tm
    row = lambda i: (i, 0)
    const2 = lambda i: (0, 0)
    cur = lambda: pl.BlockSpec((1, 1, tm), lambda i: (i, 0, 0), memory_space=pltpu.SMEM)
    nxt = lambda: pl.BlockSpec((1, 1, tm), lambda i: (jnp.minimum(i + 1, nsteps - 1), 0, 0),
                               memory_space=pltpu.SMEM)
    return pl.pallas_call(
        functools.partial(_combine_kernel, alpha=alpha, tm=tm, sub=sub, nsteps=nsteps),
        out_shape=(jax.ShapeDtypeStruct((tokens, d_model), _F32),
                   jax.ShapeDtypeStruct((tokens, d_model), _BF16)),
        grid=(nsteps,),
        in_specs=[cur(), cur(), nxt(), nxt(), pl.BlockSpec(memory_space=pl.ANY),
                  pl.BlockSpec((tm, _LANES), row), pl.BlockSpec((tm, d_model), row),
                  pl.BlockSpec((1, d_model), const2), pl.BlockSpec((1, d_model), const2)],
        out_specs=(pl.BlockSpec((tm, d_model), row), pl.BlockSpec((tm, d_model), row)),
        scratch_shapes=[pltpu.VMEM((2, 2, tm * sub, _LANES), _F32), pltpu.SemaphoreType.DMA((2,))],
        compiler_params=_params(1),
        name="moe_combine_ln",
    )(d0, d1, d0, d1, ys, route, x, ln_g.reshape(1, -1), ln_b.reshape(1, -1))


def _routing_plan(route, *, tm):
    tokens = route.shape[0]
    ntiles = (2 * tokens) // tm + _N_EXPERTS
    choice = route[:, 0:2].T.astype(jnp.int32)
    experts = jnp.arange(_N_EXPERTS, dtype=jnp.int32)[:, None]
    onehot = [(choice[k][None, :] == experts).astype(jnp.int32) for k in range(2)]
    csum = [jnp.cumsum(oh, axis=1) for oh in onehot]
    first_counts = csum[0][:, -1]
    counts = first_counts + csum[1][:, -1]
    padded = ((counts + tm - 1) // tm) * tm
    ends = jnp.cumsum(padded)
    starts = ends - padded
    base = [starts, starts + first_counts]
    dest = [jnp.sum(onehot[k] * (base[k][:, None] + csum[k] - 1), axis=0) for k in range(2)]
    tile_start = jnp.arange(ntiles, dtype=jnp.int32) * tm
    te = jnp.minimum(jnp.sum((ends[None, :] <= tile_start[:, None]).astype(jnp.int32), axis=1),
                     _N_EXPERTS - 1)
    nused = (ends[-1] // tm).reshape(1)
    tail = ends[-1] + jnp.arange(_N_EXPERTS, dtype=jnp.int32) * tm
    zstart = jnp.concatenate([jnp.maximum(ends - tm, 0), jnp.minimum(tail, (ntiles - 1) * tm)])
    zvalid = jnp.concatenate([padded > 0, tail < ntiles * tm]).astype(jnp.int32)
    return ntiles * tm, te, nused, dest[0], dest[1], zstart, zvalid


def _moe_ffn_ln(x, x_tok, route, w_up, w_down, ln_g, ln_b, *, alpha, tm):
    tokens = x.shape[0]
    tc = min(256, tokens)
    rows, te, nused, d0, d1, zstart, zvalid = _routing_plan(route, tm=tm)
    d0 = d0.reshape(tokens // tc, 1, tc)
    d1 = d1.reshape(tokens // tc, 1, tc)
    xs = _dispatch_rows(x_tok, d0, d1, zstart, zvalid, tokens=tokens, rows=rows, tt=tc, tm=tm)
    h = _swiglu_up(xs, w_up, te, nused, tm=tm, tn=w_up.shape[2] // 4)
    ys = _expert_down(h, w_down, te, nused, tm=tm)
    return _combine_ln(ys, d0, d1, route, x, ln_g, ln_b, alpha=alpha, tm=tc)


def _mixer_ln(x, x16, w_in, fox_b_f, gate_b, gm_w_s, gm_b_s, gm_ln_g, gm_ln_b, w_branch, w_out,
              ln_g, ln_b, w_router, *, bsz, seq, alpha, tm):
    d_model = x.shape[1]
    n_r = 2 * _RET_HEADS * _RET_DK + 2 * _RET_HEADS * _RET_DV
    n_f = 3 * _FOX_HEADS * _FOX_DH
    n_fl = _FOX_HEADS
    w_r = w_in[:, :n_r].astype(_BF16)
    w_f = jnp.concatenate([w_in[:, n_r:n_r + n_f + n_fl],
                           jnp.zeros((d_model, _LANES - n_fl), w_in.dtype)], axis=1).astype(_BF16)
    w_c = w_in[:, n_r + n_f + n_fl:].astype(_BF16)

    bias = jnp.zeros((1, _LANES), _F32).at[0, :n_fl].set(fox_b_f)
    y_a, qa, ka, va = _mixer_ab(x16, w_r, w_f, bias, bsz, seq)
    y_b = _fox(qa, ka, va, bsz, seq)
    return _mixer_out(x16, y_a, y_b, w_c, gm_w_s, gm_b_s, gm_ln_g, gm_ln_b, gate_b, w_branch, w_out,
                      x, ln_g, ln_b, w_router, alpha=alpha, tm=tm)


def kernel(x, w_in, fox_b_f, gate_b, gm_w_s, gm_b_s, gm_ln_g, gm_ln_b, w_branch, w_out, ln_g, ln_b,
           dense_w_up, dense_w_down, moe_router, moe_w_up, moe_w_down):
    bsz, seq, d_model = x.shape
    depth = w_in.shape[0]
    alpha = (2 * depth) ** 0.25
    tokens = bsz * seq
    tm = min(_TM, tokens)
    xf = x.reshape(tokens, d_model)
    x16 = xf
    for l in range(depth):
        dense = l % 2 == 0
        xf, x16, route = _mixer_ln(
            xf, x16, w_in[l], fox_b_f[l], gate_b[l], gm_w_s[l], gm_b_s[l], gm_ln_g[l], gm_ln_b[l],
            w_branch[l], w_out[l], ln_g[l, 0], ln_b[l, 0], None if dense else moe_router[l // 2],
            bsz=bsz, seq=seq, alpha=alpha, tm=tm)
        if dense:
            xf, x16 = _dense_ffn_ln(xf, x16, dense_w_up[l // 2], dense_w_down[l // 2],
                                    ln_g[l, 1], ln_b[l, 1], alpha=alpha, tm=tm)
        else:
            xf, x16 = _moe_ffn_ln(xf, x16, route, moe_w_up[l // 2], moe_w_down[l // 2],
                                  ln_g[l, 1], ln_b[l, 1], alpha=alpha, tm=tm)
    return xf.reshape(bsz, seq, d_model)
```

```python
import functools
import math

import jax
import jax.numpy as jnp
import numpy as np
from jax import lax
from jax.experimental import pallas as pl
from jax.experimental.pallas import tpu as pltpu

_BF16 = jnp.bfloat16
_F32 = jnp.float32

_BLOCK = 128
_RET_HEADS, _RET_DK, _RET_DV = 4, 64, 128
_FOX_HEADS, _FOX_DH = 8, 64
_GM_GROUPS, _GM_DG = 4, 128
_N_EXPERTS = 8
_ROPE_BASE = 10000.0
_LN_EPS = 1e-5
_GN_EPS = 1e-6

_LANES = 128
_MXU_COLS = 256
_VMEM_PHYSICAL_BYTES = 64 * 1024 * 1024
_VMEM_LIMIT_BYTES = (_VMEM_PHYSICAL_BYTES * 7) // 8

_TM = 512
_TQ = 256
_TR_RET = 512

def _params(n_axes):
    return pltpu.CompilerParams(
        dimension_semantics=("arbitrary",) * n_axes,
        vmem_limit_bytes=_VMEM_LIMIT_BYTES,
    )


def _layer_norm(x, g, b, eps):
    mu = jnp.mean(x, axis=-1, keepdims=True)
    xc = x - mu
    var = jnp.mean(xc * xc, axis=-1, keepdims=True)
    return xc * lax.rsqrt(var + eps) * g + b


def _silu(x):
    return x * (1.0 / (1.0 + jnp.exp(-x)))


def _gelu_tanh(x):
    c = math.sqrt(2.0 / math.pi)
    return x * (0.5 * (1.0 + jnp.tanh(c * (x + 0.044715 * (x * x * x)))))


def _dot(a, b):
    return jnp.dot(a, b, preferred_element_type=_F32)


def _dot_nt(a, b):
    return lax.dot_general(a, b, (((1,), (1,)), ((), ())), preferred_element_type=_F32)


def _dot_tn(a, b):
    return lax.dot_general(a, b, (((0,), (0,)), ((), ())), preferred_element_type=_F32)


def _matmul_kernel(x_ref, w_ref, o_ref):
    o_ref[...] = _dot(x_ref[...], w_ref[...]).astype(o_ref.dtype)


def _matmul(x, w, *, tm, tn, out_dtype, name):
    m, k = x.shape
    n = w.shape[1]
    return pl.pallas_call(
        _matmul_kernel,
        out_shape=jax.ShapeDtypeStruct((m, n), out_dtype),
        grid=(n // tn, m // tm),
        in_specs=[pl.BlockSpec((tm, k), lambda j, i: (i, 0)),
                  pl.BlockSpec((k, tn), lambda j, i: (0, j))],
        out_specs=pl.BlockSpec((tm, tn), lambda j, i: (i, j)),
        compiler_params=_params(2),
        name=name,
    )(x, w)


def _retention_project(x16_ref, w_ref, v_sc, g_sc):
    width = _RET_HEADS * _RET_DK
    wv = _RET_HEADS * _RET_DV
    x16 = x16_ref[...].astype(_BF16)
    q = _dot(x16, w_ref[:, 0:width])
    k = _dot(x16, w_ref[:, width:2 * width])
    v_sc[...] = _dot(x16, w_ref[:, 2 * width:2 * width + wv]).astype(_BF16)
    g_sc[...] = _dot(x16, w_ref[:, 2 * width + wv:2 * width + 2 * wv])
    return q, k


def _retention_core(q_raw, k_raw, cos_ref, sin_ref, qdec_ref, kdec_ref, din_ref,
                    y_ref, v_sc, g_sc, state_sc, *, chunk_dec):
    width = _RET_HEADS * _RET_DK
    wv = _RET_HEADS * _RET_DV
    rows_total = q_raw.shape[0]
    lane_all = lax.broadcasted_iota(jnp.int32, (rows_total, width), 1)
    first_half = (lane_all % _RET_DK) < (_RET_DK // 2)

    def rotary(x):
        partner = jnp.where(first_half,
                            pltpu.roll(x, width - _RET_DK // 2, 1),
                            pltpu.roll(x, _RET_DK // 2, 1))
        return x * cos_ref[...] + partner * sin_ref[...]

    q_all = rotary(q_raw)
    k_all = rotary(k_raw) * (_RET_DK ** -0.5)
    lane = lax.broadcasted_iota(jnp.int32, (_BLOCK, width), 1)
    srow = lax.broadcasted_iota(jnp.int32, (width, wv), 0) // _RET_DK
    scol = lax.broadcasted_iota(jnp.int32, (width, wv), 1) // _RET_DV
    on_diag = srow == scol
    col_head = lax.broadcasted_iota(jnp.int32, (1, wv), 1) // _RET_DV
    decay_cols = jnp.zeros((1, wv), _F32)
    for h in range(_RET_HEADS):
        decay_cols = jnp.where(col_head == h, chunk_dec[h], decay_cols)
    for cc in range(rows_total // _BLOCK):
        rows = slice(cc * _BLOCK, (cc + 1) * _BLOCK)
        q = q_all[rows]
        k = k_all[rows]
        v = v_sc[rows, :]
        q_heads = jnp.concatenate(
            [jnp.where((lane // _RET_DK) == h, q, 0.0) for h in range(_RET_HEADS)], axis=0)
        scores = _dot_nt(q_heads.astype(_BF16), k.astype(_BF16))
        state = state_sc[...]
        cross = _dot((q * qdec_ref[...]).astype(_BF16), state.astype(_BF16))
        kv = _dot_tn((k * kdec_ref[...]).astype(_BF16), v)
        state_sc[...] = state * decay_cols + jnp.where(on_diag, kv, 0.0)
        for h in range(_RET_HEADS):
            cols = slice(h * _RET_DV, (h + 1) * _RET_DV)
            sh = scores[h * _BLOCK:(h + 1) * _BLOCK] * din_ref[h]
            out = _dot(sh.astype(_BF16), v[:, cols]) + cross[:, cols]
            mu = jnp.mean(out, axis=-1, keepdims=True)
            oc = out - mu
            var = jnp.mean(oc * oc, axis=-1, keepdims=True)
            normed = oc * lax.rsqrt(var + _GN_EPS)
            y_ref[rows, cols] = (_silu(g_sc[rows, cols]) * normed).astype(y_ref.dtype)


def _retention_tables(seq):
    half = _RET_DK // 2
    inv_freq = _ROPE_BASE ** (-jnp.arange(half, dtype=_F32) / half)
    ang = jnp.arange(seq, dtype=jnp.int32).astype(_F32)[:, None] * inv_freq[None, :]
    cos, sin = jnp.cos(ang), jnp.sin(ang)
    cos_t = jnp.tile(jnp.concatenate([cos, cos], axis=1), (1, _RET_HEADS))
    sin_t = jnp.tile(jnp.concatenate([-sin, sin], axis=1), (1, _RET_HEADS))
    log_gamma = jnp.log1p(-jnp.exp2(-5.0 - jnp.arange(_RET_HEADS, dtype=_F32)))
    idx = jnp.arange(_BLOCK, dtype=_F32)
    rel = idx[:, None] - idx[None, :]
    causal = rel >= 0
    decay_in = jnp.where(causal[None],
                         jnp.exp(log_gamma[:, None, None] * jnp.where(causal, rel, 0.0)[None]), 0.0)
    q_dec = jnp.exp(log_gamma[:, None] * (idx + 1.0))
    k_dec = jnp.exp(log_gamma[:, None] * (_BLOCK - 1.0 - idx))
    q_dec_t = jnp.repeat(q_dec.T, _RET_DK, axis=1)
    k_dec_t = jnp.repeat(k_dec.T, _RET_DK, axis=1)
    return cos_t, sin_t, q_dec_t, k_dec_t, decay_in


def _mixer_ab_kernel(x_ref, wr_ref, cos_ref, sin_ref, qdec_ref, kdec_ref, din_ref, wf_ref, b_ref, sel_ref,
                     y_ref, qa_ref, ka_ref, va_ref,
                     v_sc, g_sc, state_sc, fq_sc, fk_sc, fv_sc, fl_sc, carry_sc, *, chunk_dec, nblk):
    @pl.when(pl.program_id(1) == 0)
    def _():
        state_sc[...] = jnp.zeros_like(state_sc)
        carry_sc[...] = jnp.zeros_like(carry_sc)

    _fox_prep_body(x_ref, wf_ref, b_ref, sel_ref, qa_ref, ka_ref, va_ref, fq_sc, fk_sc, fv_sc, fl_sc,
                   carry_sc, nblk=nblk)
    q_raw, k_raw = _retention_project(x_ref, wr_ref, v_sc, g_sc)
    _retention_core(q_raw, k_raw, cos_ref, sin_ref, qdec_ref, kdec_ref, din_ref, y_ref, v_sc, g_sc,
                    state_sc, chunk_dec=chunk_dec)


def _mixer_ab(x16, w_r, w_f, bias, bsz, seq):
    tokens, d_model = x16.shape
    tr = min(_TR_RET, seq)
    nstep = seq // tr
    cos_t, sin_t, q_dec_t, k_dec_t, decay_in = _retention_tables(seq)
    chunk_dec = tuple(float((1.0 - 2.0 ** (-5.0 - h)) ** _BLOCK) for h in range(_RET_HEADS))
    qk = _RET_HEADS * _RET_DK
    wv = _RET_HEADS * _RET_DV
    fw = _FOX_HEADS * _FOX_DH
    sel = _fox_selectors()
    row = lambda b, c: (b * nstep + c, 0)
    const2 = lambda b, c: (0, 0)
    const3 = lambda b, c: (0, 0, 0)
    aug = jax.ShapeDtypeStruct((bsz, _FOX_HEADS, seq, _LANES), _BF16)
    aug_spec = pl.BlockSpec((None, _FOX_HEADS, tr, _LANES), lambda b, c: (b, 0, c, 0))
    return pl.pallas_call(
        functools.partial(_mixer_ab_kernel, chunk_dec=chunk_dec, nblk=tr // _BLOCK),
        out_shape=(jax.ShapeDtypeStruct((tokens, wv), _BF16), aug, aug, aug),
        grid=(bsz, nstep),
        in_specs=[
            pl.BlockSpec((tr, d_model), row),
            pl.BlockSpec(w_r.shape, const2),
            pl.BlockSpec((tr, qk), lambda b, c: (c, 0)),
            pl.BlockSpec((tr, qk), lambda b, c: (c, 0)),
            pl.BlockSpec((_BLOCK, qk), const2),
            pl.BlockSpec((_BLOCK, qk), const2),
            pl.BlockSpec((_RET_HEADS, _BLOCK, _BLOCK), const3),
            pl.BlockSpec(w_f.shape, const2),
            pl.BlockSpec((1, _LANES), const2),
            pl.BlockSpec(sel.shape, const3),
        ],
        out_specs=(pl.BlockSpec((tr, wv), row), aug_spec, aug_spec, aug_spec),
        scratch_shapes=[pltpu.VMEM((tr, wv), _BF16), pltpu.VMEM((tr, wv), _F32),
                        pltpu.VMEM((qk, wv), _F32),
                        pltpu.VMEM((tr, fw), _BF16), pltpu.VMEM((tr, fw), _BF16),
                        pltpu.VMEM((tr, fw), _BF16), pltpu.VMEM((tr, _LANES), _F32),
                        pltpu.VMEM((1, _LANES), _F32)],
        compiler_params=_params(2),
        name="mixer_ab",
    )(x16, w_r, cos_t, sin_t, q_dec_t, k_dec_t, decay_in, w_f, bias, sel)


_C_PARTS = 3
_TS_PREP = 512


def _fox_selectors():
    half = _LANES // 2
    one_row = _C_PARTS * _FOX_HEADS
    sel = np.zeros((_FOX_HEADS, _LANES, 2 * _LANES), np.float32)
    for h in range(_FOX_HEADS):
        base = half if h % 2 == 0 else 0
        for n in range(_C_PARTS):
            sel[h, n * _FOX_HEADS + h, base + n] = 1.0
            sel[h, one_row, base + _C_PARTS + n] = 1.0
            sel[h, one_row, _LANES + base + n] = 1.0
            sel[h, n * _FOX_HEADS + h, _LANES + base + _C_PARTS + n] = -1.0
    return jnp.asarray(sel, _BF16)


def _fox_prep_body(x16_ref, w_ref, b_ref, sel_ref, qa_ref, ka_ref, va_ref, fq_ref, fk_ref, fv_ref,
                   fl_ref, carry_sc, *, nblk):
    x16 = x16_ref[...].astype(_BF16)
    width = _FOX_HEADS * _FOX_DH
    for n, dst in enumerate((fq_ref, fk_ref, fv_ref)):
        dst[...] = _dot(x16, w_ref[:, n * width:(n + 1) * width]).astype(_BF16)
    fl_ref[...] = _dot(x16, w_ref[:, 3 * width:3 * width + _LANES])

    r = lax.broadcasted_iota(jnp.int32, (_BLOCK, _LANES), 0)
    lane = lax.broadcasted_iota(jnp.int32, (_BLOCK, _LANES), 1)
    tri = (r >= lane).astype(_BF16)
    half = _LANES // 2
    q_scale = jnp.asarray(_FOX_DH ** -0.5, _BF16)
    carry = carry_sc[...]
    packed = []
    for blk in range(nblk):
        rows = slice(blk * _BLOCK, (blk + 1) * _BLOCK)
        z = fl_ref[rows, :] + b_ref[...]
        log_f = jnp.minimum(z, 0.0) - jnp.log1p(jnp.exp(-jnp.abs(z)))
        pieces = []
        rest = log_f
        for _ in range(_C_PARTS):
            piece = rest.astype(_BF16)
            pieces.append(piece)
            rest = rest - piece.astype(_F32)
        sums = _dot(tri, jnp.concatenate(pieces, axis=1))
        cum = sums[:, (_C_PARTS - 1) * _LANES:]
        for n in reversed(range(_C_PARTS - 1)):
            cum = cum + sums[:, n * _LANES:(n + 1) * _LANES]
        cum = cum + carry
        carry = cum[_BLOCK - 1:_BLOCK, :]
        row = jnp.where(lane == _C_PARTS * _FOX_HEADS, 1.0, 0.0)
        rest = cum
        for n in range(_C_PARTS):
            piece = rest.astype(_BF16).astype(_F32)
            rest = rest - piece
            shifted = piece if n == 0 else pltpu.roll(piece, n * _FOX_HEADS, 1)
            row = jnp.where((lane >= n * _FOX_HEADS) & (lane < (n + 1) * _FOX_HEADS), shifted, row)
        packed.append(row.astype(_BF16))
    carry_sc[...] = carry
    packed = jnp.concatenate(packed, axis=0)
    lane_all = lax.broadcasted_iota(jnp.int32, (nblk * _BLOCK, _LANES), 1)
    for h in range(_FOX_HEADS):
        pair = slice((h // 2) * _LANES, (h // 2 + 1) * _LANES)
        in_head = (lane_all < half) if h % 2 == 0 else (lane_all >= half)
        extra = _dot(packed, sel_ref[h]).astype(_BF16)
        qa_ref[h] = jnp.where(in_head, fq_ref[:, pair] * q_scale, extra[:, :_LANES])
        ka_ref[h] = jnp.where(in_head, fk_ref[:, pair], extra[:, _LANES:])
        va_ref[h] = jnp.where(in_head, fv_ref[:, pair], jnp.ones((), _BF16))


def _fox_kernel(qa_ref, ka_ref, va_ref, o_ref, *, tq, nq):
    half = _LANES // 2
    low = lax.broadcasted_iota(jnp.int32, (tq, _LANES), 1) < half
    keep = (lax.broadcasted_iota(jnp.int32, (tq, tq), 0)
            >= lax.broadcasted_iota(jnp.int32, (tq, tq), 1))
    order = list(reversed(range(nq)))
    logits = {n: [_dot_nt(qa_ref[s, n * tq:(n + 1) * tq, :], ka_ref[s, 0:(n + 1) * tq, :])
                  for s in range(2)] for n in order}
    for n in order:
        rows = slice(n * tq, (n + 1) * tq)
        width = (n + 1) * tq
        pv = []
        for s in range(2):
            diag = jnp.where(keep, logits[n][s][:, width - tq:], -jnp.inf)
            lg = diag if n == 0 else jnp.concatenate([logits[n][s][:, :width - tq], diag], axis=1)
            pexp = jnp.exp(lg - jnp.max(lg, axis=1, keepdims=True))
            pv.append(_dot(pexp.astype(_BF16), va_ref[s, 0:width, :]))
        out = jnp.where(low, pv[0] / pv[0][:, half:half + 1], pv[1] / pv[1][:, 0:1])
        o_ref[rows, :] = out.astype(o_ref.dtype)


def _fox(qa, ka, va, bsz, seq):
    tq = min(_TQ, seq)
    nq = seq // tq
    npair = _FOX_HEADS // 2
    logit_bytes = 2 * tq * tq * (nq * (nq + 1) // 2) * 4
    assert logit_bytes <= _VMEM_LIMIT_BYTES // 2, "sequence too long for the single-pass attention step"
    return pl.pallas_call(
        functools.partial(_fox_kernel, tq=tq, nq=nq),
        out_shape=jax.ShapeDtypeStruct((bsz * seq, _FOX_HEADS * _FOX_DH), _BF16),
        grid=(bsz, npair),
        in_specs=[
            pl.BlockSpec((None, 2, seq, _LANES), lambda b, p: (b, p, 0, 0)),
            pl.BlockSpec((None, 2, seq, _LANES), lambda b, p: (b, p, 0, 0)),
            pl.BlockSpec((None, 2, seq, _LANES), lambda b, p: (b, p, 0, 0)),
        ],
        out_specs=pl.BlockSpec((seq, _LANES), lambda b, p: (b, p)),
        compiler_params=_params(2),
        name="fox_attention",
    )(qa, ka, va)


def _gmlp_tile(gu, gv, w_ref, bst_ref, lng_ref, lnb_ref, yc_sc):
    u = _gelu_tanh(gu)
    v = _layer_norm(_gelu_tanh(gv), lng_ref[...], lnb_ref[...], _LN_EPS).astype(_BF16)
    r = lax.broadcasted_iota(jnp.int32, (_BLOCK, _BLOCK), 0)
    c = lax.broadcasted_iota(jnp.int32, (_BLOCK, _BLOCK), 1)
    causal = r >= c
    nchunk = gu.shape[0] // _BLOCK
    for g in range(_GM_GROUPS):
        cols = slice(g * _GM_DG, (g + 1) * _GM_DG)
        wg = jnp.where(causal, w_ref[g], 0.0).astype(_BF16)
        bias = bst_ref[:, g:g + 1]
        v_chunks = jnp.concatenate(
            [v[ch * _BLOCK:(ch + 1) * _BLOCK, cols] for ch in range(nchunk)], axis=1)
        sp = _dot(wg, v_chunks) + bias
        for ch in range(nchunk):
            rows = slice(ch * _BLOCK, (ch + 1) * _BLOCK)
            yc_sc[rows, cols] = (u[rows, cols] * sp[:, ch * _GM_DG:(ch + 1) * _GM_DG]).astype(yc_sc.dtype)


def _mixer_out_kernel(x16_ref, ya_ref, yb_ref, wc_ref, gmw_ref, bst_ref, gmg_ref, gmb_ref, gbias_ref,
                      wbr_ref, wout_ref, x_ref, lng_ref, lnb_ref, *rest, alpha, d_model, with_router):
    if with_router:
        wr_ref, xo_ref, xtok_ref, route_ref, yc_sc = rest
    else:
        xo_ref, xo16_ref, yc_sc = rest
    gw = _GM_GROUPS * _GM_DG
    sub = d_model // _LANES

    def front(rows):
        x16 = x16_ref[rows, :].astype(_BF16)
        gu = _dot(x16, wc_ref[:, 0:gw])
        gv = _dot(x16, wc_ref[:, gw:2 * gw])

        def gate(n):
            cols = slice(2 * gw + n * d_model, 2 * gw + (n + 1) * d_model)
            z = _dot(x16, wc_ref[:, cols]) + gbias_ref[:, n * d_model:(n + 1) * d_model]
            return 1.0 / (1.0 + jnp.exp(-z))

        merged = (gate(0) * _dot(ya_ref[rows, :], wbr_ref[0])
                  + gate(1) * _dot(yb_ref[rows, :], wbr_ref[1]))
        return gu, gv, gate(2), merged

    def back(rows, gu, gv, gate_c, merged):
        yc = yc_sc.at[rows, :]
        _gmlp_tile(gu, gv, gmw_ref, bst_ref, gmg_ref, gmb_ref, yc)
        merged = merged + gate_c * _dot(yc[...], wbr_ref[2])
        h = _dot(merged.astype(_BF16), wout_ref[...])
        out = _layer_norm(alpha * x_ref[rows, :] + h, lng_ref[...], lnb_ref[...], _LN_EPS)
        xo_ref[rows, :] = out
        if with_router:
            _store_token_rows(xtok_ref.at[pl.ds(rows.start * sub, (rows.stop - rows.start) * sub), :], out)
            route_ref[rows, :] = _route_top2(out, wr_ref)
        else:
            xo16_ref[rows, :] = out.astype(_BF16)

    rows = slice(0, x16_ref.shape[0])
    back(rows, *front(rows))


def _mixer_out(x16, y_a, y_b, w_c, gm_w_s, gm_b_s, gm_ln_g, gm_ln_b, gate_b, w_branch, w_out, x,
               ln_g, ln_b, w_router, *, alpha, tm):
    tokens, d_model = x.shape
    bw = y_a.shape[1]
    gw = _GM_GROUPS * _GM_DG
    row = lambda i: (i, 0)
    const2 = lambda i: (0, 0)
    const3 = lambda i: (0, 0, 0)
    with_router = w_router is not None
    in_specs = [
        pl.BlockSpec((tm, d_model), row),
        pl.BlockSpec((tm, bw), row), pl.BlockSpec((tm, bw), row),
        pl.BlockSpec(w_c.shape, const2),
        pl.BlockSpec((_GM_GROUPS, _BLOCK, _BLOCK), const3),
        pl.BlockSpec((_BLOCK, _GM_GROUPS), const2),
        pl.BlockSpec((1, gw), const2), pl.BlockSpec((1, gw), const2),
        pl.BlockSpec((1, 3 * d_model), const2),
        pl.BlockSpec((3, bw, d_model), const3),
        pl.BlockSpec((d_model, d_model), const2),
        pl.BlockSpec((tm, d_model), row),
        pl.BlockSpec((1, d_model), const2), pl.BlockSpec((1, d_model), const2),
    ]
    args = [x16, y_a, y_b, w_c, gm_w_s, gm_b_s.T, gm_ln_g.reshape(1, gw), gm_ln_b.reshape(1, gw),
            gate_b.reshape(1, -1), w_branch.astype(_BF16), w_out.astype(_BF16), x,
            ln_g.reshape(1, -1), ln_b.reshape(1, -1)]
    out_shape = [jax.ShapeDtypeStruct((tokens, d_model), _F32)]
    out_specs = [pl.BlockSpec((tm, d_model), row)]
    if with_router:
        sub = d_model // _LANES
        in_specs.append(pl.BlockSpec((d_model, 2 * _LANES), const2))
        args.append(_router_weights(w_router))
        out_shape += [jax.ShapeDtypeStruct((tokens * sub, _LANES), _F32),
                      jax.ShapeDtypeStruct((tokens, _LANES), _F32)]
        out_specs += [pl.BlockSpec((tm * sub, _LANES), row), pl.BlockSpec((tm, _LANES), row)]
    else:
        out_shape.append(jax.ShapeDtypeStruct((tokens, d_model), _BF16))
        out_specs.append(pl.BlockSpec((tm, d_model), row))
    outs = pl.pallas_call(
        functools.partial(_mixer_out_kernel, alpha=alpha, d_model=d_model, with_router=with_router),
        out_shape=tuple(out_shape),
        grid=(tokens // tm,),
        in_specs=in_specs,
        out_specs=tuple(out_specs),
        scratch_shapes=[pltpu.VMEM((tm, gw), _BF16)],
        compiler_params=_params(1),
        name="mixer_out",
    )(*args)
    return outs if with_router else (*outs, None)


def _weights_changed(te_ref, i):
    return jnp.logical_or(i == 0, te_ref[i] != te_ref[jnp.maximum(i - 1, 0)])


def _load_token_rows(ref, n):
    s = ref.shape[0] // n
    return jnp.concatenate([ref[pl.ds(c, n, stride=s), :] for c in range(s)], axis=1)


def _store_token_rows(ref, x):
    n = x.shape[0]
    s = x.shape[1] // _LANES
    for c in range(s):
        ref[pl.ds(c, n, stride=s), :] = x[:, c * _LANES:(c + 1) * _LANES]


def _up_kernel(te_ref, nused_ref, x_ref, wa_ref, wb_ref, h_ref, wa_sc, wb_sc):
    i = pl.program_id(1)

    @pl.when(_weights_changed(te_ref, i))
    def _():
        wa_sc[...] = wa_ref[...].astype(_BF16)
        wb_sc[...] = wb_ref[...].astype(_BF16)

    @pl.when(i < nused_ref[0])
    def _():
        x = _load_token_rows(x_ref, h_ref.shape[0]).astype(_BF16)
        a = _dot(x, wa_sc[...])
        b = _dot(x, wb_sc[...])
        h_ref[...] = (_silu(a) * b).astype(h_ref.dtype)

    @pl.when(i >= nused_ref[0])
    def _():
        h_ref[...] = jnp.zeros_like(h_ref)


def _swiglu_up(xs, w_up, te, nused, *, tm, tn):
    k = w_up.shape[1]
    sub = k // _LANES
    rows = xs.shape[0] // sub
    d_ff = w_up.shape[2] // 2
    nb = d_ff // tn
    grid_spec = pltpu.PrefetchScalarGridSpec(
        num_scalar_prefetch=2,
        grid=(nb, rows // tm),
        in_specs=[
            pl.BlockSpec((tm * sub, _LANES), lambda j, i, te, nu: (i, 0)),
            pl.BlockSpec((None, k, tn), lambda j, i, te, nu: (te[i], 0, j)),
            pl.BlockSpec((None, k, tn), lambda j, i, te, nu: (te[i], 0, j + nb)),
        ],
        out_specs=pl.BlockSpec((tm, tn), lambda j, i, te, nu: (i, j)),
        scratch_shapes=[pltpu.VMEM((k, tn), _BF16), pltpu.VMEM((k, tn), _BF16)],
    )
    return pl.pallas_call(
        _up_kernel,
        out_shape=jax.ShapeDtypeStruct((rows, d_ff), _BF16),
        grid_spec=grid_spec,
        compiler_params=_params(2),
        name="swiglu_up",
    )(te, nused, xs, w_up, w_up)


def _down_kernel(te_ref, nused_ref, h_ref, w_ref, y_ref, w_sc):
    i = pl.program_id(0)

    @pl.when(_weights_changed(te_ref, i))
    def _():
        w_sc[...] = w_ref[...].astype(_BF16)

    @pl.when(i < nused_ref[0])
    def _():
        _store_token_rows(y_ref, _dot(h_ref[...], w_sc[...]))

    @pl.when(i >= nused_ref[0])
    def _():
        y_ref[...] = jnp.zeros_like(y_ref)


def _expert_down(h16, w_down, te, nused, *, tm):
    rows, k = h16.shape
    n = w_down.shape[2]
    sub = n // _LANES
    grid_spec = pltpu.PrefetchScalarGridSpec(
        num_scalar_prefetch=2,
        grid=(rows // tm,),
        in_specs=[
            pl.BlockSpec((tm, k), lambda i, te, nu: (i, 0)),
            pl.BlockSpec((None, k, n), lambda i, te, nu: (te[i], 0, 0)),
        ],
        out_specs=pl.BlockSpec((tm * sub, _LANES), lambda i, te, nu: (i, 0)),
        scratch_shapes=[pltpu.VMEM((k, n), _BF16)],
    )
    return pl.pallas_call(
        _down_kernel,
        out_shape=jax.ShapeDtypeStruct((rows * sub, _LANES), _F32),
        grid_spec=grid_spec,
        compiler_params=_params(1),
        name="expert_down",
    )(te, nused, h16, w_down)


def _dense_ffn_kernel(x16_ref, wup_ref, wdn_ref, x_ref, lng_ref, lnb_ref, xo_ref, xo16_ref,
                      *, alpha, d_ff, chunks):
    x16 = x16_ref[...]
    f = None
    for c0, c1 in chunks:
        a = _dot(x16, wup_ref[:, c0:c1])
        b = _dot(x16, wup_ref[:, d_ff + c0:d_ff + c1])
        part = _dot((_silu(a) * b).astype(_BF16), wdn_ref[c0:c1, :])
        f = part if f is None else f + part
    out = _layer_norm(alpha * x_ref[...] + f, lng_ref[...], lnb_ref[...], _LN_EPS)
    xo_ref[...] = out
    xo16_ref[...] = out.astype(_BF16)


def _dense_ffn_ln(x, x16, w_up, w_down, ln_g, ln_b, *, alpha, tm):
    tokens, d_model = x.shape
    d_ff = w_down.shape[0]
    split = (d_ff // (2 * _MXU_COLS)) * _MXU_COLS
    chunks = ((0, split), (split, d_ff)) if 0 < split < d_ff else ((0, d_ff),)
    row = lambda i: (i, 0)
    const2 = lambda i: (0, 0)
    return pl.pallas_call(
        functools.partial(_dense_ffn_kernel, alpha=alpha, d_ff=d_ff, chunks=chunks),
        out_shape=(jax.ShapeDtypeStruct((tokens, d_model), _F32),
                   jax.ShapeDtypeStruct((tokens, d_model), _BF16)),
        grid=(tokens // tm,),
        in_specs=[pl.BlockSpec((tm, d_model), row),
                  pl.BlockSpec((d_model, 2 * d_ff), const2),
                  pl.BlockSpec((d_ff, d_model), const2),
                  pl.BlockSpec((tm, d_model), row),
                  pl.BlockSpec((1, d_model), const2), pl.BlockSpec((1, d_model), const2)],
        out_specs=(pl.BlockSpec((tm, d_model), row), pl.BlockSpec((tm, d_model), row)),
        compiler_params=_params(1),
        name="dense_ffn_ln",
    )(x16, w_up.astype(_BF16), w_down.astype(_BF16), x, ln_g.reshape(1, -1), ln_b.reshape(1, -1))


def _route_top2(x, wr_ref):
    x_hi = x.astype(_BF16)
    x_lo = (x - x_hi.astype(_F32)).astype(_BF16)
    both = _dot(x_hi, wr_ref[...])
    logits = both[:, :_LANES] + (both[:, _LANES:] + _dot(x_lo, wr_ref[:, :_LANES]))
    lane = lax.broadcasted_iota(jnp.int32, logits.shape, 1).astype(_F32)
    lg = jnp.where(lane < _N_EXPERTS, logits, -jnp.inf)
    m1 = jnp.max(lg, axis=1, keepdims=True)
    i1 = jnp.min(jnp.where(lg == m1, lane, float(_LANES)), axis=1, keepdims=True)
    lg2 = jnp.where(lane == i1, -jnp.inf, lg)
    m2 = jnp.max(lg2, axis=1, keepdims=True)
    i2 = jnp.min(jnp.where(lg2 == m2, lane, float(_LANES)), axis=1, keepdims=True)
    e2 = jnp.exp(m2 - m1)
    den = 1.0 + e2
    w1 = 1.0 / den
    w2 = e2 / den
    return jnp.where(lane == 0, i1, jnp.where(lane == 1, i2,
                     jnp.where(lane == 2, w1, jnp.where(lane == 3, w2, 0.0))))


def _router_weights(w_router):
    d_model = w_router.shape[0]
    wr = jnp.zeros((d_model, _LANES), _F32).at[:, :_N_EXPERTS].set(w_router)
    wr_hi = wr.astype(_BF16)
    wr_lo = (wr - wr_hi.astype(_F32)).astype(_BF16)
    return jnp.concatenate([wr_hi, wr_lo], axis=1)


_ROW_UNROLL = 8


def _dispatch_kernel(zstart_ref, zvalid_ref, d0_ref, d1_ref, x_ref, xs_hbm, zero_sc, stage_sc, sem, zsem,
                     *, tt, tm, sub, nsteps):
    @pl.when(pl.program_id(0) == 0)
    def _():
        zero_sc[...] = jnp.zeros_like(zero_sc)

        def zero_copy(n):
            start = pl.multiple_of(zstart_ref[n] * sub, tm * sub)
            return pltpu.make_async_copy(zero_sc, xs_hbm.at[pl.ds(start, tm * sub), :], zsem)

        for n in range(2 * _N_EXPERTS):
            @pl.when(zvalid_ref[n] > 0)
            def _():
                zero_copy(n).start()
        for n in range(2 * _N_EXPERTS):
            @pl.when(zvalid_ref[n] > 0)
            def _():
                zero_copy(n).wait()

    i = pl.program_id(0)
    slot = i % 2
    stage_sc[slot] = x_ref[...]

    def issue(g, carry):
        for u in range(_ROW_UNROLL):
            r = g * _ROW_UNROLL + u
            src = stage_sc.at[slot, pl.ds(pl.multiple_of(r * sub, sub), sub), :]
            for d_ref in (d0_ref, d1_ref):
                dst = xs_hbm.at[pl.ds(pl.multiple_of(d_ref[0, 0, r] * sub, sub), sub), :]
                pltpu.make_async_copy(src, dst, sem.at[slot]).start(priority=u % 2)
        return carry

    def drain(s):
        for _ in range(2):
            pltpu.make_async_copy(stage_sc.at[s], xs_hbm.at[pl.ds(0, tt * sub), :], sem.at[s]).wait()

    lax.fori_loop(0, tt // _ROW_UNROLL, issue, 0)

    @pl.when(i > 0)
    def _():
        drain(1 - slot)

    @pl.when(i == nsteps - 1)
    def _():
        drain(slot)


def _dispatch_rows(x_tok, d0, d1, zstart, zvalid, *, tokens, rows, tt, tm):
    sub = x_tok.shape[0] // tokens
    smem = lambda: pl.BlockSpec((1, 1, tt), lambda i, zs, zv: (i, 0, 0), memory_space=pltpu.SMEM)
    grid_spec = pltpu.PrefetchScalarGridSpec(
        num_scalar_prefetch=2,
        grid=(tokens // tt,),
        in_specs=[smem(), smem(), pl.BlockSpec((tt * sub, _LANES), lambda i, zs, zv: (i, 0))],
        out_specs=pl.BlockSpec(memory_space=pl.ANY),
        scratch_shapes=[pltpu.VMEM((tm * sub, _LANES), _F32), pltpu.VMEM((2, tt * sub, _LANES), _F32),
                        pltpu.SemaphoreType.DMA((2,)), pltpu.SemaphoreType.DMA],
    )
    return pl.pallas_call(
        functools.partial(_dispatch_kernel, tt=tt, tm=tm, sub=sub, nsteps=tokens // tt),
        out_shape=jax.ShapeDtypeStruct((rows * sub, _LANES), _F32),
        grid_spec=grid_spec,
        compiler_params=_params(1),
        name="moe_dispatch",
    )(zstart, zvalid, d0, d1, x_tok)


def _combine_kernel(d0_ref, d1_ref, n0_ref, n1_ref, y_hbm, route_ref, x_ref, lng_ref, lnb_ref,
                    xo_ref, xo16_ref, buf, sem, *, alpha, tm, sub, nsteps):
    i = pl.program_id(0)
    slot = i % 2

    def gather(dst_slot, a_ref, b_ref):
        def issue(g, carry):
            for u in range(_ROW_UNROLL):
                r = g * _ROW_UNROLL + u
                for s, d_ref in enumerate((a_ref, b_ref)):
                    src = y_hbm.at[pl.ds(pl.multiple_of(d_ref[0, 0, r] * sub, sub), sub), :]
                    dst = buf.at[dst_slot, s, pl.ds(pl.multiple_of(r * sub, sub), sub), :]
                    pltpu.make_async_copy(src, dst, sem.at[dst_slot]).start(priority=u % 2)
            return carry

        lax.fori_loop(0, tm // _ROW_UNROLL, issue, 0)

    @pl.when(i == 0)
    def _():
        gather(0, d0_ref, d1_ref)

    @pl.when(i + 1 < nsteps)
    def _():
        gather(1 - slot, n0_ref, n1_ref)

    for s in range(2):
        pltpu.make_async_copy(y_hbm.at[pl.ds(0, tm * sub), :], buf.at[slot, s], sem.at[slot]).wait()
    route = route_ref[...]
    f = (route[:, 2:3] * _load_token_rows(buf.at[slot, 0], tm)
         + route[:, 3:4] * _load_token_rows(buf.at[slot, 1], tm))
    out = _layer_norm(alpha * x_ref[...] + f, lng_ref[...], lnb_ref[...], _LN_EPS)
    xo_ref[...] = out
    xo16_ref[...] = out.astype(_BF16)


def _combine_ln(ys, d0, d1, route, x, ln_g, ln_b, *, alpha, tm):
    tokens, d_model = x.shape
    sub = d_model // _LANES
    nsteps = tokens // tm
    row = lambda i: (i, 0)
    const2 = lambda i: (0, 0)
    cur = lambda: pl.BlockSpec((1, 1, tm), lambda i: (i, 0, 0), memory_space=pltpu.SMEM)
    nxt = lambda: pl.BlockSpec((1, 1, tm), lambda i: (jnp.minimum(i + 1, nsteps - 1), 0, 0),
                               memory_space=pltpu.SMEM)
    return pl.pallas_call(
        functools.partial(_combine_kernel, alpha=alpha, tm=tm, sub=sub, nsteps=nsteps),
        out_shape=(jax.ShapeDtypeStruct((tokens, d_model), _F32),
                   jax.ShapeDtypeStruct((tokens, d_model), _BF16)),
        grid=(nsteps,),
        in_specs=[cur(), cur(), nxt(), nxt(), pl.BlockSpec(memory_space=pl.ANY),
                  pl.BlockSpec((tm, _LANES), row), pl.BlockSpec((tm, d_model), row),
                  pl.BlockSpec((1, d_model), const2), pl.BlockSpec((1, d_model), const2)],
        out_specs=(pl.BlockSpec((tm, d_model), row), pl.BlockSpec((tm, d_model), row)),
        scratch_shapes=[pltpu.VMEM((2, 2, tm * sub, _LANES), _F32), pltpu.SemaphoreType.DMA((2,))],
        compiler_params=_params(1),
        name="moe_combine_ln",
    )(d0, d1, d0, d1, ys, route, x, ln_g.reshape(1, -1), ln_b.reshape(1, -1))


def _routing_plan(route, *, tm):
    tokens = route.shape[0]
    ntiles = (2 * tokens) // tm + _N_EXPERTS
    choice = route[:, 0:2].T.astype(jnp.int32)
    experts = jnp.arange(_N_EXPERTS, dtype=jnp.int32)[:, None]
    onehot = [(choice[k][None, :] == experts).astype(jnp.int32) for k in range(2)]
    csum = [jnp.cumsum(oh, axis=1) for oh in onehot]
    first_counts = csum[0][:, -1]
    counts = first_counts + csum[1][:, -1]
    padded = ((counts + tm - 1) // tm) * tm
    ends = jnp.cumsum(padded)
    starts = ends - padded
    base = [starts, starts + first_counts]
    dest = [jnp.sum(onehot[k] * (base[k][:, None] + csum[k] - 1), axis=0) for k in range(2)]
    tile_start = jnp.arange(ntiles, dtype=jnp.int32) * tm
    te = jnp.minimum(jnp.sum((ends[None, :] <= tile_start[:, None]).astype(jnp.int32), axis=1),
                     _N_EXPERTS - 1)
    nused = (ends[-1] // tm).reshape(1)
    tail = ends[-1] + jnp.arange(_N_EXPERTS, dtype=jnp.int32) * tm
    zstart = jnp.concatenate([jnp.maximum(ends - tm, 0), jnp.minimum(tail, (ntiles - 1) * tm)])
    zvalid = jnp.concatenate([padded > 0, tail < ntiles * tm]).astype(jnp.int32)
    return ntiles * tm, te, nused, dest[0], dest[1], zstart, zvalid


def _moe_ffn_ln(x, x_tok, route, w_up, w_down, ln_g, ln_b, *, alpha, tm):
    tokens = x.shape[0]
    tc = min(_TM, tokens)
    rows, te, nused, d0, d1, zstart, zvalid = _routing_plan(route, tm=tm)
    d0 = d0.reshape(tokens // tc, 1, tc)
    d1 = d1.reshape(tokens // tc, 1, tc)
    xs = _dispatch_rows(x_tok, d0, d1, zstart, zvalid, tokens=tokens, rows=rows, tt=tc, tm=tm)
    h = _swiglu_up(xs, w_up, te, nused, tm=tm, tn=w_up.shape[2] // 4)
    ys = _expert_down(h, w_down, te, nused, tm=tm)
    return _combine_ln(ys, d0, d1, route, x, ln_g, ln_b, alpha=alpha, tm=tc)


def _mixer_ln(x, x16, w_in, fox_b_f, gate_b, gm_w_s, gm_b_s, gm_ln_g, gm_ln_b, w_branch, w_out,
              ln_g, ln_b, w_router, *, bsz, seq, alpha, tm):
    d_model = x.shape[1]
    n_r = 2 * _RET_HEADS * _RET_DK + 2 * _RET_HEADS * _RET_DV
    n_f = 3 * _FOX_HEADS * _FOX_DH
    n_fl = _FOX_HEADS
    w_r = w_in[:, :n_r].astype(_BF16)
    w_f = jnp.concatenate([w_in[:, n_r:n_r + n_f + n_fl],
                           jnp.zeros((d_model, _LANES - n_fl), w_in.dtype)], axis=1).astype(_BF16)
    w_c = w_in[:, n_r + n_f + n_fl:].astype(_BF16)

    bias = jnp.zeros((1, _LANES), _F32).at[0, :n_fl].set(fox_b_f)
    y_a, qa, ka, va = _mixer_ab(x16, w_r, w_f, bias, bsz, seq)
    y_b = _fox(qa, ka, va, bsz, seq)
    return _mixer_out(x16, y_a, y_b, w_c, gm_w_s, gm_b_s, gm_ln_g, gm_ln_b, gate_b, w_branch, w_out,
                      x, ln_g, ln_b, w_router, alpha=alpha, tm=tm)


def kernel(x, w_in, fox_b_f, gate_b, gm_w_s, gm_b_s, gm_ln_g, gm_ln_b, w_branch, w_out, ln_g, ln_b,
           dense_w_up, dense_w_down, moe_router, moe_w_up, moe_w_down):
    bsz, seq, d_model = x.shape
    depth = w_in.shape[0]
    alpha = (2 * depth) ** 0.25
    tokens = bsz * seq
    tm = min(_TM, tokens)
    xf = x.reshape(tokens, d_model)
    x16 = xf
    for l in range(depth):
        dense = l % 2 == 0
        xf, x16, route = _mixer_ln(
            xf, x16, w_in[l], fox_b_f[l], gate_b[l], gm_w_s[l], gm_b_s[l], gm_ln_g[l], gm_ln_b[l],
            w_branch[l], w_out[l], ln_g[l, 0], ln_b[l, 0], None if dense else moe_router[l // 2],
            bsz=bsz, seq=seq, alpha=alpha, tm=tm)
        if dense:
            xf, x16 = _dense_ffn_ln(xf, x16, dense_w_up[l // 2], dense_w_down[l // 2],
                                    ln_g[l, 1], ln_b[l, 1], alpha=alpha, tm=tm)
        else:
            xf, x16 = _moe_ffn_ln(xf, x16, route, moe_w_up[l // 2], moe_w_down[l // 2],
                                  ln_g[l, 1], ln_b[l, 1], alpha=alpha, tm=tm)
    return xf.reshape(bsz, seq, d_model)
```

```python
import functools
import math

import jax
import jax.numpy as jnp
import numpy as np
from jax import lax
from jax.experimental import pallas as pl
from jax.experimental.pallas import tpu as pltpu

_BF16 = jnp.bfloat16
_F32 = jnp.float32

_BLOCK = 128
_RET_HEADS, _RET_DK, _RET_DV = 4, 64, 128
_FOX_HEADS, _FOX_DH = 8, 64
_GM_GROUPS, _GM_DG = 4, 128
_N_EXPERTS = 8
_ROPE_BASE = 10000.0
_LN_EPS = 1e-5
_GN_EPS = 1e-6

_LANES = 128
_MXU_COLS = 256
_VMEM_PHYSICAL_BYTES = 64 * 1024 * 1024
_VMEM_LIMIT_BYTES = (_VMEM_PHYSICAL_BYTES * 7) // 8

_TM = 512
_TQ = 256
_TR_RET = 512

def _params(n_axes):
    return pltpu.CompilerParams(
        dimension_semantics=("arbitrary",) * n_axes,
        vmem_limit_bytes=_VMEM_LIMIT_BYTES,
    )


def _layer_norm(x, g, b, eps):
    mu = jnp.mean(x, axis=-1, keepdims=True)
    xc = x - mu
    var = jnp.mean(xc * xc, axis=-1, keepdims=True)
    return xc * lax.rsqrt(var + eps) * g + b


def _silu(x):
    return x * (1.0 / (1.0 + jnp.exp(-x)))


def _gelu_tanh(x):
    c = math.sqrt(2.0 / math.pi)
    return x * (0.5 * (1.0 + jnp.tanh(c * (x + 0.044715 * (x * x * x)))))


def _dot(a, b):
    return jnp.dot(a, b, preferred_element_type=_F32)


def _dot_nt(a, b):
    return lax.dot_general(a, b, (((1,), (1,)), ((), ())), preferred_element_type=_F32)


def _dot_tn(a, b):
    return lax.dot_general(a, b, (((0,), (0,)), ((), ())), preferred_element_type=_F32)


def _matmul_kernel(x_ref, w_ref, o_ref):
    o_ref[...] = _dot(x_ref[...], w_ref[...]).astype(o_ref.dtype)


def _matmul(x, w, *, tm, tn, out_dtype, name):
    m, k = x.shape
    n = w.shape[1]
    return pl.pallas_call(
        _matmul_kernel,
        out_shape=jax.ShapeDtypeStruct((m, n), out_dtype),
        grid=(n // tn, m // tm),
        in_specs=[pl.BlockSpec((tm, k), lambda j, i: (i, 0)),
                  pl.BlockSpec((k, tn), lambda j, i: (0, j))],
        out_specs=pl.BlockSpec((tm, tn), lambda j, i: (i, j)),
        compiler_params=_params(2),
        name=name,
    )(x, w)


def _retention_project(x16_ref, w_ref, v_sc, g_sc):
    width = _RET_HEADS * _RET_DK
    wv = _RET_HEADS * _RET_DV
    x16 = x16_ref[...].astype(_BF16)
    q = _dot(x16, w_ref[:, 0:width])
    k = _dot(x16, w_ref[:, width:2 * width])
    v_sc[...] = _dot(x16, w_ref[:, 2 * width:2 * width + wv]).astype(_BF16)
    g_sc[...] = _dot(x16, w_ref[:, 2 * width + wv:2 * width + 2 * wv])
    return q, k


def _retention_core(q_raw, k_raw, cos_ref, sin_ref, qdec_ref, kdec_ref, din_ref,
                    y_ref, v_sc, g_sc, state_sc, *, chunk_dec):
    width = _RET_HEADS * _RET_DK
    wv = _RET_HEADS * _RET_DV
    rows_total = q_raw.shape[0]
    lane_all = lax.broadcasted_iota(jnp.int32, (rows_total, width), 1)
    first_half = (lane_all % _RET_DK) < (_RET_DK // 2)

    def rotary(x):
        partner = jnp.where(first_half,
                            pltpu.roll(x, width - _RET_DK // 2, 1),
                            pltpu.roll(x, _RET_DK // 2, 1))
        return x * cos_ref[...] + partner * sin_ref[...]

    q_all = rotary(q_raw)
    k_all = rotary(k_raw) * (_RET_DK ** -0.5)
    lane = lax.broadcasted_iota(jnp.int32, (_BLOCK, width), 1)
    srow = lax.broadcasted_iota(jnp.int32, (width, wv), 0) // _RET_DK
    scol = lax.broadcasted_iota(jnp.int32, (width, wv), 1) // _RET_DV
    on_diag = srow == scol
    col_head = lax.broadcasted_iota(jnp.int32, (1, wv), 1) // _RET_DV
    decay_cols = jnp.zeros((1, wv), _F32)
    for h in range(_RET_HEADS):
        decay_cols = jnp.where(col_head == h, chunk_dec[h], decay_cols)
    for cc in range(rows_total // _BLOCK):
        rows = slice(cc * _BLOCK, (cc + 1) * _BLOCK)
        q = q_all[rows]
        k = k_all[rows]
        v = v_sc[rows, :]
        q_heads = jnp.concatenate(
            [jnp.where((lane // _RET_DK) == h, q, 0.0) for h in range(_RET_HEADS)], axis=0)
        scores = _dot_nt(q_heads.astype(_BF16), k.astype(_BF16))
        state = state_sc[...]
        cross = _dot((q * qdec_ref[...]).astype(_BF16), state.astype(_BF16))
        kv = _dot_tn((k * kdec_ref[...]).astype(_BF16), v)
        state_sc[...] = state * decay_cols + jnp.where(on_diag, kv, 0.0)
        for h in range(_RET_HEADS):
            cols = slice(h * _RET_DV, (h + 1) * _RET_DV)
            sh = scores[h * _BLOCK:(h + 1) * _BLOCK] * din_ref[h]
            out = _dot(sh.astype(_BF16), v[:, cols]) + cross[:, cols]
            mu = jnp.mean(out, axis=-1, keepdims=True)
            oc = out - mu
            var = jnp.mean(oc * oc, axis=-1, keepdims=True)
            normed = oc * lax.rsqrt(var + _GN_EPS)
            y_ref[rows, cols] = (_silu(g_sc[rows, cols]) * normed).astype(y_ref.dtype)


def _retention_tables(seq):
    half = _RET_DK // 2
    inv_freq = _ROPE_BASE ** (-jnp.arange(half, dtype=_F32) / half)
    ang = jnp.arange(seq, dtype=jnp.int32).astype(_F32)[:, None] * inv_freq[None, :]
    cos, sin = jnp.cos(ang), jnp.sin(ang)
    cos_t = jnp.tile(jnp.concatenate([cos, cos], axis=1), (1, _RET_HEADS))
    sin_t = jnp.tile(jnp.concatenate([-sin, sin], axis=1), (1, _RET_HEADS))
    log_gamma = jnp.log1p(-jnp.exp2(-5.0 - jnp.arange(_RET_HEADS, dtype=_F32)))
    idx = jnp.arange(_BLOCK, dtype=_F32)
    rel = idx[:, None] - idx[None, :]
    causal = rel >= 0
    decay_in = jnp.where(causal[None],
                         jnp.exp(log_gamma[:, None, None] * jnp.where(causal, rel, 0.0)[None]), 0.0)
    q_dec = jnp.exp(log_gamma[:, None] * (idx + 1.0))
    k_dec = jnp.exp(log_gamma[:, None] * (_BLOCK - 1.0 - idx))
    q_dec_t = jnp.repeat(q_dec.T, _RET_DK, axis=1)
    k_dec_t = jnp.repeat(k_dec.T, _RET_DK, axis=1)
    return cos_t, sin_t, q_dec_t, k_dec_t, decay_in


def _mixer_ab_kernel(x_ref, wr_ref, cos_ref, sin_ref, qdec_ref, kdec_ref, din_ref, wf_ref, b_ref, sel_ref,
                     y_ref, qa_ref, ka_ref, va_ref,
                     v_sc, g_sc, state_sc, fq_sc, fk_sc, fv_sc, fl_sc, carry_sc, *, chunk_dec, nblk):
    @pl.when(pl.program_id(1) == 0)
    def _():
        state_sc[...] = jnp.zeros_like(state_sc)
        carry_sc[...] = jnp.zeros_like(carry_sc)

    _fox_prep_body(x_ref, wf_ref, b_ref, sel_ref, qa_ref, ka_ref, va_ref, fq_sc, fk_sc, fv_sc, fl_sc,
                   carry_sc, nblk=nblk)
    q_raw, k_raw = _retention_project(x_ref, wr_ref, v_sc, g_sc)
    _retention_core(q_raw, k_raw, cos_ref, sin_ref, qdec_ref, kdec_ref, din_ref, y_ref, v_sc, g_sc,
                    state_sc, chunk_dec=chunk_dec)


def _mixer_ab(x16, w_r, w_f, bias, bsz, seq):
    tokens, d_model = x16.shape
    tr = min(_TR_RET, seq)
    nstep = seq // tr
    cos_t, sin_t, q_dec_t, k_dec_t, decay_in = _retention_tables(seq)
    chunk_dec = tuple(float((1.0 - 2.0 ** (-5.0 - h)) ** _BLOCK) for h in range(_RET_HEADS))
    qk = _RET_HEADS * _RET_DK
    wv = _RET_HEADS * _RET_DV
    fw = _FOX_HEADS * _FOX_DH
    sel = _fox_selectors()
    row = lambda b, c: (b * nstep + c, 0)
    const2 = lambda b, c: (0, 0)
    const3 = lambda b, c: (0, 0, 0)
    aug = jax.ShapeDtypeStruct((bsz, _FOX_HEADS, seq, _LANES), _BF16)
    aug_spec = pl.BlockSpec((None, _FOX_HEADS, tr, _LANES), lambda b, c: (b, 0, c, 0))
    return pl.pallas_call(
        functools.partial(_mixer_ab_kernel, chunk_dec=chunk_dec, nblk=tr // _BLOCK),
        out_shape=(jax.ShapeDtypeStruct((tokens, wv), _BF16), aug, aug, aug),
        grid=(bsz, nstep),
        in_specs=[
            pl.BlockSpec((tr, d_model), row),
            pl.BlockSpec(w_r.shape, const2),
            pl.BlockSpec((tr, qk), lambda b, c: (c, 0)),
            pl.BlockSpec((tr, qk), lambda b, c: (c, 0)),
            pl.BlockSpec((_BLOCK, qk), const2),
            pl.BlockSpec((_BLOCK, qk), const2),
            pl.BlockSpec((_RET_HEADS, _BLOCK, _BLOCK), const3),
            pl.BlockSpec(w_f.shape, const2),
            pl.BlockSpec((1, _LANES), const2),
            pl.BlockSpec(sel.shape, const3),
        ],
        out_specs=(pl.BlockSpec((tr, wv), row), aug_spec, aug_spec, aug_spec),
        scratch_shapes=[pltpu.VMEM((tr, wv), _BF16), pltpu.VMEM((tr, wv), _F32),
                        pltpu.VMEM((qk, wv), _F32),
                        pltpu.VMEM((tr, fw), _BF16), pltpu.VMEM((tr, fw), _BF16),
                        pltpu.VMEM((tr, fw), _BF16), pltpu.VMEM((tr, _LANES), _F32),
                        pltpu.VMEM((1, _LANES), _F32)],
        compiler_params=_params(2),
        name="mixer_ab",
    )(x16, w_r, cos_t, sin_t, q_dec_t, k_dec_t, decay_in, w_f, bias, sel)


_C_PARTS = 3
_TS_PREP = 512


def _fox_selectors():
    half = _LANES // 2
    one_row = _C_PARTS * _FOX_HEADS
    sel = np.zeros((_FOX_HEADS, _LANES, 2 * _LANES), np.float32)
    for h in range(_FOX_HEADS):
        base = half if h % 2 == 0 else 0
        for n in range(_C_PARTS):
            sel[h, n * _FOX_HEADS + h, base + n] = 1.0
            sel[h, one_row, base + _C_PARTS + n] = 1.0
            sel[h, one_row, _LANES + base + n] = 1.0
            sel[h, n * _FOX_HEADS + h, _LANES + base + _C_PARTS + n] = -1.0
    return jnp.asarray(sel, _BF16)


def _fox_prep_body(x16_ref, w_ref, b_ref, sel_ref, qa_ref, ka_ref, va_ref, fq_ref, fk_ref, fv_ref,
                   fl_ref, carry_sc, *, nblk):
    x16 = x16_ref[...].astype(_BF16)
    width = _FOX_HEADS * _FOX_DH
    for n, dst in enumerate((fq_ref, fk_ref, fv_ref)):
        dst[...] = _dot(x16, w_ref[:, n * width:(n + 1) * width]).astype(_BF16)
    fl_ref[...] = _dot(x16, w_ref[:, 3 * width:3 * width + _LANES])

    r = lax.broadcasted_iota(jnp.int32, (_BLOCK, _LANES), 0)
    lane = lax.broadcasted_iota(jnp.int32, (_BLOCK, _LANES), 1)
    tri = (r >= lane).astype(_BF16)
    half = _LANES // 2
    q_scale = jnp.asarray(_FOX_DH ** -0.5, _BF16)
    carry = carry_sc[...]
    packed = []
    for blk in range(nblk):
        rows = slice(blk * _BLOCK, (blk + 1) * _BLOCK)
        z = fl_ref[rows, :] + b_ref[...]
        log_f = jnp.minimum(z, 0.0) - jnp.log1p(jnp.exp(-jnp.abs(z)))
        pieces = []
        rest = log_f
        for _ in range(_C_PARTS):
            piece = rest.astype(_BF16)
            pieces.append(piece)
            rest = rest - piece.astype(_F32)
        sums = _dot(tri, jnp.concatenate(pieces, axis=1))
        cum = sums[:, (_C_PARTS - 1) * _LANES:]
        for n in reversed(range(_C_PARTS - 1)):
            cum = cum + sums[:, n * _LANES:(n + 1) * _LANES]
        cum = cum + carry
        carry = cum[_BLOCK - 1:_BLOCK, :]
        row = jnp.where(lane == _C_PARTS * _FOX_HEADS, 1.0, 0.0)
        rest = cum
        for n in range(_C_PARTS):
            piece = rest.astype(_BF16).astype(_F32)
            rest = rest - piece
            shifted = piece if n == 0 else pltpu.roll(piece, n * _FOX_HEADS, 1)
            row = jnp.where((lane >= n * _FOX_HEADS) & (lane < (n + 1) * _FOX_HEADS), shifted, row)
        packed.append(row.astype(_BF16))
    carry_sc[...] = carry
    packed = jnp.concatenate(packed, axis=0)
    lane_all = lax.broadcasted_iota(jnp.int32, (nblk * _BLOCK, _LANES), 1)
    for h in range(_FOX_HEADS):
        pair = slice((h // 2) * _LANES, (h // 2 + 1) * _LANES)
        in_head = (lane_all < half) if h % 2 == 0 else (lane_all >= half)
        extra = _dot(packed, sel_ref[h]).astype(_BF16)
        qa_ref[h] = jnp.where(in_head, fq_ref[:, pair] * q_scale, extra[:, :_LANES])
        ka_ref[h] = jnp.where(in_head, fk_ref[:, pair], extra[:, _LANES:])
        va_ref[h] = jnp.where(in_head, fv_ref[:, pair], jnp.ones((), _BF16))


def _fox_kernel(qa_ref, ka_ref, va_ref, o_ref, *, tq, nq):
    half = _LANES // 2
    low = lax.broadcasted_iota(jnp.int32, (tq, _LANES), 1) < half
    keep = (lax.broadcasted_iota(jnp.int32, (tq, tq), 0)
            >= lax.broadcasted_iota(jnp.int32, (tq, tq), 1))
    order = list(reversed(range(nq)))
    logits = {n: [_dot_nt(qa_ref[s, n * tq:(n + 1) * tq, :], ka_ref[s, 0:(n + 1) * tq, :])
                  for s in range(2)] for n in order}
    for n in order:
        rows = slice(n * tq, (n + 1) * tq)
        width = (n + 1) * tq
        pv = []
        for s in range(2):
            diag = jnp.where(keep, logits[n][s][:, width - tq:], -jnp.inf)
            lg = diag if n == 0 else jnp.concatenate([logits[n][s][:, :width - tq], diag], axis=1)
            pexp = jnp.exp(lg - jnp.max(lg, axis=1, keepdims=True))
            pv.append(_dot(pexp.astype(_BF16), va_ref[s, 0:width, :]))
        out = jnp.where(low, pv[0] / pv[0][:, half:half + 1], pv[1] / pv[1][:, 0:1])
        o_ref[rows, :] = out.astype(o_ref.dtype)


def _fox(qa, ka, va, bsz, seq):
    tq = min(_TQ, seq)
    nq = seq // tq
    npair = _FOX_HEADS // 2
    logit_bytes = 2 * tq * tq * (nq * (nq + 1) // 2) * 4
    assert logit_bytes <= _VMEM_LIMIT_BYTES // 2, "sequence too long for the single-pass attention step"
    return pl.pallas_call(
        functools.partial(_fox_kernel, tq=tq, nq=nq),
        out_shape=jax.ShapeDtypeStruct((bsz * seq, _FOX_HEADS * _FOX_DH), _BF16),
        grid=(bsz, npair),
        in_specs=[
            pl.BlockSpec((None, 2, seq, _LANES), lambda b, p: (b, p, 0, 0)),
            pl.BlockSpec((None, 2, seq, _LANES), lambda b, p: (b, p, 0, 0)),
            pl.BlockSpec((None, 2, seq, _LANES), lambda b, p: (b, p, 0, 0)),
        ],
        out_specs=pl.BlockSpec((seq, _LANES), lambda b, p: (b, p)),
        compiler_params=_params(2),
        name="fox_attention",
    )(qa, ka, va)


def _gmlp_tile(gu, gv, w_ref, bst_ref, lng_ref, lnb_ref, yc_sc):
    u = _gelu_tanh(gu)
    v = _layer_norm(_gelu_tanh(gv), lng_ref[...], lnb_ref[...], _LN_EPS).astype(_BF16)
    r = lax.broadcasted_iota(jnp.int32, (_BLOCK, _BLOCK), 0)
    c = lax.broadcasted_iota(jnp.int32, (_BLOCK, _BLOCK), 1)
    causal = r >= c
    nchunk = gu.shape[0] // _BLOCK
    for g in range(_GM_GROUPS):
        cols = slice(g * _GM_DG, (g + 1) * _GM_DG)
        wg = jnp.where(causal, w_ref[g], 0.0).astype(_BF16)
        bias = bst_ref[:, g:g + 1]
        v_chunks = jnp.concatenate(
            [v[ch * _BLOCK:(ch + 1) * _BLOCK, cols] for ch in range(nchunk)], axis=1)
        sp = _dot(wg, v_chunks) + bias
        for ch in range(nchunk):
            rows = slice(ch * _BLOCK, (ch + 1) * _BLOCK)
            yc_sc[rows, cols] = (u[rows, cols] * sp[:, ch * _GM_DG:(ch + 1) * _GM_DG]).astype(yc_sc.dtype)


def _mixer_out_kernel(x16_ref, ya_ref, yb_ref, wc_ref, gmw_ref, bst_ref, gmg_ref, gmb_ref, gbias_ref,
                      wbr_ref, wout_ref, x_ref, lng_ref, lnb_ref, *rest, alpha, d_model, with_router):
    if with_router:
        wr_ref, xo_ref, xtok_ref, route_ref, yc_sc = rest
    else:
        xo_ref, xo16_ref, yc_sc = rest
    gw = _GM_GROUPS * _GM_DG
    sub = d_model // _LANES

    def front(rows):
        x16 = x16_ref[rows, :].astype(_BF16)
        gu = _dot(x16, wc_ref[:, 0:gw])
        gv = _dot(x16, wc_ref[:, gw:2 * gw])

        def gate(n):
            cols = slice(2 * gw + n * d_model, 2 * gw + (n + 1) * d_model)
            z = _dot(x16, wc_ref[:, cols]) + gbias_ref[:, n * d_model:(n + 1) * d_model]
            return 1.0 / (1.0 + jnp.exp(-z))

        merged = (gate(0) * _dot(ya_ref[rows, :], wbr_ref[0])
                  + gate(1) * _dot(yb_ref[rows, :], wbr_ref[1]))
        return gu, gv, gate(2), merged

    def back(rows, gu, gv, gate_c, merged):
        yc = yc_sc.at[rows, :]
        _gmlp_tile(gu, gv, gmw_ref, bst_ref, gmg_ref, gmb_ref, yc)
        merged = merged + gate_c * _dot(yc[...], wbr_ref[2])
        h = _dot(merged.astype(_BF16), wout_ref[...])
        out = _layer_norm(alpha * x_ref[rows, :] + h, lng_ref[...], lnb_ref[...], _LN_EPS)
        xo_ref[rows, :] = out
        if with_router:
            _store_token_rows(xtok_ref.at[pl.ds(rows.start * sub, (rows.stop - rows.start) * sub), :], out)
            route_ref[rows, :] = _route_top2(out, wr_ref)
        else:
            xo16_ref[rows, :] = out.astype(_BF16)

    rows = slice(0, x16_ref.shape[0])
    back(rows, *front(rows))


def _mixer_out(x16, y_a, y_b, w_c, gm_w_s, gm_b_s, gm_ln_g, gm_ln_b, gate_b, w_branch, w_out, x,
               ln_g, ln_b, w_router, *, alpha, tm):
    tokens, d_model = x.shape
    bw = y_a.shape[1]
    gw = _GM_GROUPS * _GM_DG
    row = lambda i: (i, 0)
    const2 = lambda i: (0, 0)
    const3 = lambda i: (0, 0, 0)
    with_router = w_router is not None
    in_specs = [
        pl.BlockSpec((tm, d_model), row),
        pl.BlockSpec((tm, bw), row), pl.BlockSpec((tm, bw), row),
        pl.BlockSpec(w_c.shape, const2),
        pl.BlockSpec((_GM_GROUPS, _BLOCK, _BLOCK), const3),
        pl.BlockSpec((_BLOCK, _GM_GROUPS), const2),
        pl.BlockSpec((1, gw), const2), pl.BlockSpec((1, gw), const2),
        pl.BlockSpec((1, 3 * d_model), const2),
        pl.BlockSpec((3, bw, d_model), const3),
        pl.BlockSpec((d_model, d_model), const2),
        pl.BlockSpec((tm, d_model), row),
        pl.BlockSpec((1, d_model), const2), pl.BlockSpec((1, d_model), const2),
    ]
    args = [x16, y_a, y_b, w_c, gm_w_s, gm_b_s.T, gm_ln_g.reshape(1, gw), gm_ln_b.reshape(1, gw),
            gate_b.reshape(1, -1), w_branch.astype(_BF16), w_out.astype(_BF16), x,
            ln_g.reshape(1, -1), ln_b.reshape(1, -1)]
    out_shape = [jax.ShapeDtypeStruct((tokens, d_model), _F32)]
    out_specs = [pl.BlockSpec((tm, d_model), row)]
    if with_router:
        sub = d_model // _LANES
        in_specs.append(pl.BlockSpec((d_model, 2 * _LANES), const2))
        args.append(_router_weights(w_router))
        out_shape += [jax.ShapeDtypeStruct((tokens * sub, _LANES), _F32),
                      jax.ShapeDtypeStruct((tokens, _LANES), _F32)]
        out_specs += [pl.BlockSpec((tm * sub, _LANES), row), pl.BlockSpec((tm, _LANES), row)]
    else:
        out_shape.append(jax.ShapeDtypeStruct((tokens, d_model), _BF16))
        out_specs.append(pl.BlockSpec((tm, d_model), row))
    outs = pl.pallas_call(
        functools.partial(_mixer_out_kernel, alpha=alpha, d_model=d_model, with_router=with_router),
        out_shape=tuple(out_shape),
        grid=(tokens // tm,),
        in_specs=in_specs,
        out_specs=tuple(out_specs),
        scratch_shapes=[pltpu.VMEM((tm, gw), _BF16)],
        compiler_params=_params(1),
        name="mixer_out",
    )(*args)
    return outs if with_router else (*outs, None)


def _weights_changed(te_ref, i):
    return jnp.logical_or(i == 0, te_ref[i] != te_ref[jnp.maximum(i - 1, 0)])


def _load_token_rows(ref, n):
    s = ref.shape[0] // n
    return jnp.concatenate([ref[pl.ds(c, n, stride=s), :] for c in range(s)], axis=1)


def _store_token_rows(ref, x):
    n = x.shape[0]
    s = x.shape[1] // _LANES
    for c in range(s):
        ref[pl.ds(c, n, stride=s), :] = x[:, c * _LANES:(c + 1) * _LANES]


def _up_kernel(te_ref, nused_ref, first_ref, last_ref, nexte_ref, wrap_ref, x_ref, w_hbm, h_ref,
               stage_a, stage_b, wa_sc, wb_sc, sem, slot_sc, *, tn, nb, d_ff):
    j = pl.program_id(0)
    i = pl.program_id(1)

    def weight_copies(e, jj):
        col = pl.multiple_of(jj * tn, tn)
        return (pltpu.make_async_copy(w_hbm.at[e, :, pl.ds(col, tn)], stage_a, sem.at[0]),
                pltpu.make_async_copy(w_hbm.at[e, :, pl.ds(d_ff + col, tn)], stage_b, sem.at[1]))

    def round_into(slot):
        wa_sc[slot] = stage_a[...].astype(_BF16)
        wb_sc[slot] = stage_b[...].astype(_BF16)

    @pl.when((j == 0) & (i == 0))
    def _():
        for c in weight_copies(te_ref[0], 0):
            c.start()
        for c in weight_copies(te_ref[0], 0):
            c.wait()
        round_into(0)
        slot_sc[0] = 0

    has_next = (wrap_ref[i] == 0) | (j + 1 < nb)
    next_j = jnp.where(wrap_ref[i] == 1, j + 1, j)

    @pl.when((first_ref[i] == 1) & has_next)
    def _():
        for c in weight_copies(nexte_ref[i], next_j):
            c.start()

    slot = slot_sc[0]

    @pl.when(i < nused_ref[0])
    def _():
        x = _load_token_rows(x_ref, h_ref.shape[0]).astype(_BF16)
        a = _dot(x, wa_sc[slot])
        b = _dot(x, wb_sc[slot])
        h_ref[...] = (_silu(a) * b).astype(h_ref.dtype)

    @pl.when(i >= nused_ref[0])
    def _():
        h_ref[...] = jnp.zeros_like(h_ref)

    @pl.when((last_ref[i] == 1) & has_next)
    def _():
        for c in weight_copies(nexte_ref[i], next_j):
            c.wait()
        round_into(1 - slot)
        slot_sc[0] = 1 - slot


def _swiglu_up(xs, w_up, te, nused, *, tm, tn):
    k = w_up.shape[1]
    sub = k // _LANES
    rows = xs.shape[0] // sub
    d_ff = w_up.shape[2] // 2
    nb = d_ff // tn
    change = te[1:] != te[:-1]
    one = jnp.ones((1,), jnp.bool_)
    first = jnp.concatenate([one, change]).astype(jnp.int32)
    last = jnp.concatenate([change, one]).astype(jnp.int32)
    later = jnp.where(te[None, :] > te[:, None], te[None, :], _N_EXPERTS)
    nexte = jnp.min(later, axis=1)
    wrap = (nexte == _N_EXPERTS).astype(jnp.int32)
    nexte = jnp.where(wrap == 1, te[0], nexte)
    idx = lambda j, i, *prefetch: (i, 0)
    grid_spec = pltpu.PrefetchScalarGridSpec(
        num_scalar_prefetch=6,
        grid=(nb, rows // tm),
        in_specs=[pl.BlockSpec((tm * sub, _LANES), idx), pl.BlockSpec(memory_space=pl.ANY)],
        out_specs=pl.BlockSpec((tm, tn), lambda j, i, *prefetch: (i, j)),
        scratch_shapes=[pltpu.VMEM((k, tn), _F32), pltpu.VMEM((k, tn), _F32),
                        pltpu.VMEM((2, k, tn), _BF16), pltpu.VMEM((2, k, tn), _BF16),
                        pltpu.SemaphoreType.DMA((2,)), pltpu.SMEM((1,), jnp.int32)],
    )
    return pl.pallas_call(
        functools.partial(_up_kernel, tn=tn, nb=nb, d_ff=d_ff),
        out_shape=jax.ShapeDtypeStruct((rows, d_ff), _BF16),
        grid_spec=grid_spec,
        compiler_params=_params(2),
        name="swiglu_up",
    )(te, nused, first, last, nexte, wrap, xs, w_up)


def _down_kernel(te_ref, nused_ref, h_ref, w_ref, y_ref, w_sc):
    i = pl.program_id(0)

    @pl.when(_weights_changed(te_ref, i))
    def _():
        w_sc[...] = w_ref[...].astype(_BF16)

    @pl.when(i < nused_ref[0])
    def _():
        _store_token_rows(y_ref, _dot(h_ref[...], w_sc[...]))

    @pl.when(i >= nused_ref[0])
    def _():
        y_ref[...] = jnp.zeros_like(y_ref)


def _expert_down(h16, w_down, te, nused, *, tm):
    rows, k = h16.shape
    n = w_down.shape[2]
    sub = n // _LANES
    grid_spec = pltpu.PrefetchScalarGridSpec(
        num_scalar_prefetch=2,
        grid=(rows // tm,),
        in_specs=[
            pl.BlockSpec((tm, k), lambda i, te, nu: (i, 0)),
            pl.BlockSpec((None, k, n), lambda i, te, nu: (te[i], 0, 0)),
        ],
        out_specs=pl.BlockSpec((tm * sub, _LANES), lambda i, te, nu: (i, 0)),
        scratch_shapes=[pltpu.VMEM((k, n), _BF16)],
    )
    return pl.pallas_call(
        _down_kernel,
        out_shape=jax.ShapeDtypeStruct((rows * sub, _LANES), _F32),
        grid_spec=grid_spec,
        compiler_params=_params(1),
        name="expert_down",
    )(te, nused, h16, w_down)


def _dense_ffn_kernel(x16_ref, wup_ref, wdn_ref, x_ref, lng_ref, lnb_ref, xo_ref, xo16_ref,
                      *, alpha, d_ff, chunks):
    x16 = x16_ref[...]
    f = None
    for c0, c1 in chunks:
        a = _dot(x16, wup_ref[:, c0:c1])
        b = _dot(x16, wup_ref[:, d_ff + c0:d_ff + c1])
        part = _dot((_silu(a) * b).astype(_BF16), wdn_ref[c0:c1, :])
        f = part if f is None else f + part
    out = _layer_norm(alpha * x_ref[...] + f, lng_ref[...], lnb_ref[...], _LN_EPS)
    xo_ref[...] = out
    xo16_ref[...] = out.astype(_BF16)


def _dense_ffn_ln(x, x16, w_up, w_down, ln_g, ln_b, *, alpha, tm):
    tokens, d_model = x.shape
    d_ff = w_down.shape[0]
    split = (d_ff // (2 * _MXU_COLS)) * _MXU_COLS
    chunks = ((0, split), (split, d_ff)) if 0 < split < d_ff else ((0, d_ff),)
    row = lambda i: (i, 0)
    const2 = lambda i: (0, 0)
    return pl.pallas_call(
        functools.partial(_dense_ffn_kernel, alpha=alpha, d_ff=d_ff, chunks=chunks),
        out_shape=(jax.ShapeDtypeStruct((tokens, d_model), _F32),
                   jax.ShapeDtypeStruct((tokens, d_model), _BF16)),
        grid=(tokens // tm,),
        in_specs=[pl.BlockSpec((tm, d_model), row),
                  pl.BlockSpec((d_model, 2 * d_ff), const2),
                  pl.BlockSpec((d_ff, d_model), const2),
                  pl.BlockSpec((tm, d_model), row),
                  pl.BlockSpec((1, d_model), const2), pl.BlockSpec((1, d_model), const2)],
        out_specs=(pl.BlockSpec((tm, d_model), row), pl.BlockSpec((tm, d_model), row)),
        compiler_params=_params(1),
        name="dense_ffn_ln",
    )(x16, w_up.astype(_BF16), w_down.astype(_BF16), x, ln_g.reshape(1, -1), ln_b.reshape(1, -1))


def _route_top2(x, wr_ref):
    x_hi = x.astype(_BF16)
    x_lo = (x - x_hi.astype(_F32)).astype(_BF16)
    both = _dot(x_hi, wr_ref[...])
    logits = both[:, :_LANES] + (both[:, _LANES:] + _dot(x_lo, wr_ref[:, :_LANES]))
    lane = lax.broadcasted_iota(jnp.int32, logits.shape, 1).astype(_F32)
    lg = jnp.where(lane < _N_EXPERTS, logits, -jnp.inf)
    m1 = jnp.max(lg, axis=1, keepdims=True)
    i1 = jnp.min(jnp.where(lg == m1, lane, float(_LANES)), axis=1, keepdims=True)
    lg2 = jnp.where(lane == i1, -jnp.inf, lg)
    m2 = jnp.max(lg2, axis=1, keepdims=True)
    i2 = jnp.min(jnp.where(lg2 == m2, lane, float(_LANES)), axis=1, keepdims=True)
    e2 = jnp.exp(m2 - m1)
    den = 1.0 + e2
    w1 = 1.0 / den
    w2 = e2 / den
    return jnp.where(lane == 0, i1, jnp.where(lane == 1, i2,
                     jnp.where(lane == 2, w1, jnp.where(lane == 3, w2, 0.0))))


def _router_weights(w_router):
    d_model = w_router.shape[0]
    wr = jnp.zeros((d_model, _LANES), _F32).at[:, :_N_EXPERTS].set(w_router)
    wr_hi = wr.astype(_BF16)
    wr_lo = (wr - wr_hi.astype(_F32)).astype(_BF16)
    return jnp.concatenate([wr_hi, wr_lo], axis=1)


_ROW_UNROLL = 8


def _dispatch_kernel(zstart_ref, zvalid_ref, d0_ref, d1_ref, x_ref, xs_hbm, zero_sc, stage_sc, sem, zsem,
                     *, tt, tm, sub, nsteps):
    @pl.when(pl.program_id(0) == 0)
    def _():
        zero_sc[...] = jnp.zeros_like(zero_sc)

        def zero_copy(n):
            start = pl.multiple_of(zstart_ref[n] * sub, tm * sub)
            return pltpu.make_async_copy(zero_sc, xs_hbm.at[pl.ds(start, tm * sub), :], zsem)

        for n in range(2 * _N_EXPERTS):
            @pl.when(zvalid_ref[n] > 0)
            def _():
                zero_copy(n).start()
        for n in range(2 * _N_EXPERTS):
            @pl.when(zvalid_ref[n] > 0)
            def _():
                zero_copy(n).wait()

    i = pl.program_id(0)
    slot = i % 2
    stage_sc[slot] = x_ref[...]

    def issue(g, carry):
        for u in range(_ROW_UNROLL):
            r = g * _ROW_UNROLL + u
            src = stage_sc.at[slot, pl.ds(pl.multiple_of(r * sub, sub), sub), :]
            for d_ref in (d0_ref, d1_ref):
                dst = xs_hbm.at[pl.ds(pl.multiple_of(d_ref[0, 0, r] * sub, sub), sub), :]
                pltpu.make_async_copy(src, dst, sem.at[slot]).start(priority=u % 2)
        return carry

    def drain(s):
        for _ in range(2):
            pltpu.make_async_copy(stage_sc.at[s], xs_hbm.at[pl.ds(0, tt * sub), :], sem.at[s]).wait()

    lax.fori_loop(0, tt // _ROW_UNROLL, issue, 0)

    @pl.when(i > 0)
    def _():
        drain(1 - slot)

    @pl.when(i == nsteps - 1)
    def _():
        drain(slot)


def _dispatch_rows(x_tok, d0, d1, zstart, zvalid, *, tokens, rows, tt, tm):
    sub = x_tok.shape[0] // tokens
    smem = lambda: pl.BlockSpec((1, 1, tt), lambda i, zs, zv: (i, 0, 0), memory_space=pltpu.SMEM)
    grid_spec = pltpu.PrefetchScalarGridSpec(
        num_scalar_prefetch=2,
        grid=(tokens // tt,),
        in_specs=[smem(), smem(), pl.BlockSpec((tt * sub, _LANES), lambda i, zs, zv: (i, 0))],
        out_specs=pl.BlockSpec(memory_space=pl.ANY),
        scratch_shapes=[pltpu.VMEM((tm * sub, _LANES), _F32), pltpu.VMEM((2, tt * sub, _LANES), _F32),
                        pltpu.SemaphoreType.DMA((2,)), pltpu.SemaphoreType.DMA],
    )
    return pl.pallas_call(
        functools.partial(_dispatch_kernel, tt=tt, tm=tm, sub=sub, nsteps=tokens // tt),
        out_shape=jax.ShapeDtypeStruct((rows * sub, _LANES), _F32),
        grid_spec=grid_spec,
        compiler_params=_params(1),
        name="moe_dispatch",
    )(zstart, zvalid, d0, d1, x_tok)


def _combine_kernel(d0_ref, d1_ref, n0_ref, n1_ref, y_hbm, route_ref, x_ref, lng_ref, lnb_ref,
                    xo_ref, xo16_ref, buf, sem, *, alpha, tm, sub, nsteps):
    i = pl.program_id(0)
    slot = i % 2

    def gather(dst_slot, a_ref, b_ref):
        def issue(g, carry):
            for u in range(_ROW_UNROLL):
                r = g * _ROW_UNROLL + u
                for s, d_ref in enumerate((a_ref, b_ref)):
                    src = y_hbm.at[pl.ds(pl.multiple_of(d_ref[0, 0, r] * sub, sub), sub), :]
                    dst = buf.at[dst_slot, s, pl.ds(pl.multiple_of(r * sub, sub), sub), :]
                    pltpu.make_async_copy(src, dst, sem.at[dst_slot]).start(priority=u % 2)
            return carry

        lax.fori_loop(0, tm // _ROW_UNROLL, issue, 0)

    @pl.when(i == 0)
    def _():
        gather(0, d0_ref, d1_ref)

    @pl.when(i + 1 < nsteps)
    def _():
        gather(1 - slot, n0_ref, n1_ref)

    for s in range(2):
        pltpu.make_async_copy(y_hbm.at[pl.ds(0, tm * sub), :], buf.at[slot, s], sem.at[slot]).wait()
    route = route_ref[...]
    f = (route[:, 2:3] * _load_token_rows(buf.at[slot, 0], tm)
         + route[:, 3:4] * _load_token_rows(buf.at[slot, 1], tm))
    out = _layer_norm(alpha * x_ref[...] + f, lng_ref[...], lnb_ref[...], _LN_EPS)
    xo_ref[...] = out
    xo16_ref[...] = out.astype(_BF16)


def _combine_ln(ys, d0, d1, route, x, ln_g, ln_b, *, alpha, tm):
    tokens, d_model = x.shape
    sub = d_model // _LANES
    nsteps = tokens // tm
    row = lambda i: (i, 0)
    const2 = lambda i: (0, 0)
    cur = lambda: pl.BlockSpec((1, 1, tm), lambda i: (i, 0, 0), memory_space=pltpu.SMEM)
    nxt = lambda: pl.BlockSpec((1, 1, tm), lambda i: (jnp.minimum(i + 1, nsteps - 1), 0, 0),
                               memory_space=pltpu.SMEM)
    return pl.pallas_call(
        functools.partial(_combine_kernel, alpha=alpha, tm=tm, sub=sub, nsteps=nsteps),
        out_shape=(jax.ShapeDtypeStruct((tokens, d_model), _F32),
                   jax.ShapeDtypeStruct((tokens, d_model), _BF16)),
        grid=(nsteps,),
        in_specs=[cur(), cur(), nxt(), nxt(), pl.BlockSpec(memory_space=pl.ANY),
                  pl.BlockSpec((tm, _LANES), row), pl.BlockSpec((tm, d_model), row),
                  pl.BlockSpec((1, d_model), const2), pl.BlockSpec((1, d_model), const2)],
        out_specs=(pl.BlockSpec((tm, d_model), row), pl.BlockSpec((tm, d_model), row)),
        scratch_shapes=[pltpu.VMEM((2, 2, tm * sub, _LANES), _F32), pltpu.SemaphoreType.DMA((2,))],
        compiler_params=_params(1),
        name="moe_combine_ln",
    )(d0, d1, d0, d1, ys, route, x, ln_g.reshape(1, -1), ln_b.reshape(1, -1))


def _routing_plan(route, *, tm):
    tokens = route.shape[0]
    ntiles = (2 * tokens) // tm + _N_EXPERTS
    choice = route[:, 0:2].T.astype(jnp.int32)
    experts = jnp.arange(_N_EXPERTS, dtype=jnp.int32)[:, None]
    onehot = [(choice[k][None, :] == experts).astype(jnp.int32) for k in range(2)]
    csum = [jnp.cumsum(oh, axis=1) for oh in onehot]
    first_counts = csum[0][:, -1]
    counts = first_counts + csum[1][:, -1]
    padded = ((counts + tm - 1) // tm) * tm
    ends = jnp.cumsum(padded)
    starts = ends - padded
    base = [starts, starts + first_counts]
    dest = [jnp.sum(onehot[k] * (base[k][:, None] + csum[k] - 1), axis=0) for k in range(2)]
    tile_start = jnp.arange(ntiles, dtype=jnp.int32) * tm
    te = jnp.minimum(jnp.sum((ends[None, :] <= tile_start[:, None]).astype(jnp.int32), axis=1),
                     _N_EXPERTS - 1)
    nused = (ends[-1] // tm).reshape(1)
    tail = ends[-1] + jnp.arange(_N_EXPERTS, dtype=jnp.int32) * tm
    zstart = jnp.concatenate([jnp.maximum(ends - tm, 0), jnp.minimum(tail, (ntiles - 1) * tm)])
    zvalid = jnp.concatenate([padded > 0, tail < ntiles * tm]).astype(jnp.int32)
    return ntiles * tm, te, nused, dest[0], dest[1], zstart, zvalid


def _moe_ffn_ln(x, x_tok, route, w_up, w_down, ln_g, ln_b, *, alpha, tm):
    tokens = x.shape[0]
    tc = min(_TM, tokens)
    rows, te, nused, d0, d1, zstart, zvalid = _routing_plan(route, tm=tm)
    d0 = d0.reshape(tokens // tc, 1, tc)
    d1 = d1.reshape(tokens // tc, 1, tc)
    xs = _dispatch_rows(x_tok, d0, d1, zstart, zvalid, tokens=tokens, rows=rows, tt=tc, tm=tm)
    h = _swiglu_up(xs, w_up, te, nused, tm=tm, tn=w_up.shape[2] // 4)
    ys = _expert_down(h, w_down, te, nused, tm=tm)
    return _combine_ln(ys, d0, d1, route, x, ln_g, ln_b, alpha=alpha, tm=tc)


def _mixer_ln(x, x16, w_in, fox_b_f, gate_b, gm_w_s, gm_b_s, gm_ln_g, gm_ln_b, w_branch, w_out,
              ln_g, ln_b, w_router, *, bsz, seq, alpha, tm):
    d_model = x.shape[1]
    n_r = 2 * _RET_HEADS * _RET_DK + 2 * _RET_HEADS * _RET_DV
    n_f = 3 * _FOX_HEADS * _FOX_DH
    n_fl = _FOX_HEADS
    w_r = w_in[:, :n_r].astype(_BF16)
    w_f = jnp.concatenate([w_in[:, n_r:n_r + n_f + n_fl],
                           jnp.zeros((d_model, _LANES - n_fl), w_in.dtype)], axis=1).astype(_BF16)
    w_c = w_in[:, n_r + n_f + n_fl:].astype(_BF16)

    bias = jnp.zeros((1, _LANES), _F32).at[0, :n_fl].set(fox_b_f)
    y_a, qa, ka, va = _mixer_ab(x16, w_r, w_f, bias, bsz, seq)
    y_b = _fox(qa, ka, va, bsz, seq)
    return _mixer_out(x16, y_a, y_b, w_c, gm_w_s, gm_b_s, gm_ln_g, gm_ln_b, gate_b, w_branch, w_out,
                      x, ln_g, ln_b, w_router, alpha=alpha, tm=tm)


def kernel(x, w_in, fox_b_f, gate_b, gm_w_s, gm_b_s, gm_ln_g, gm_ln_b, w_branch, w_out, ln_g, ln_b,
           dense_w_up, dense_w_down, moe_router, moe_w_up, moe_w_down):
    bsz, seq, d_model = x.shape
    depth = w_in.shape[0]
    alpha = (2 * depth) ** 0.25
    tokens = bsz * seq
    tm = min(_TM, tokens)
    xf = x.reshape(tokens, d_model)
    x16 = xf
    for l in range(depth):
        dense = l % 2 == 0
        xf, x16, route = _mixer_ln(
            xf, x16, w_in[l], fox_b_f[l], gate_b[l], gm_w_s[l], gm_b_s[l], gm_ln_g[l], gm_ln_b[l],
            w_branch[l], w_out[l], ln_g[l, 0], ln_b[l, 0], None if dense else moe_router[l // 2],
            bsz=bsz, seq=seq, alpha=alpha, tm=tm)
        if dense:
            xf, x16 = _dense_ffn_ln(xf, x16, dense_w_up[l // 2], dense_w_down[l // 2],
                                    ln_g[l, 1], ln_b[l, 1], alpha=alpha, tm=tm)
        else:
            xf, x16 = _moe_ffn_ln(xf, x16, route, moe_w_up[l // 2], moe_w_down[l // 2],
                                  ln_g[l, 1], ln_b[l, 1], alpha=alpha, tm=tm)
    return xf.reshape(bsz, seq, d_model)
```

```python
import functools
import math

import jax
import jax.numpy as jnp
import numpy as np
from jax import lax
from jax.experimental import pallas as pl
from jax.experimental.pallas import tpu as pltpu

_BF16 = jnp.bfloat16
_F32 = jnp.float32

_BLOCK = 128
_RET_HEADS, _RET_DK, _RET_DV = 4, 64, 128
_FOX_HEADS, _FOX_DH = 8, 64
_GM_GROUPS, _GM_DG = 4, 128
_N_EXPERTS = 8
_ROPE_BASE = 10000.0
_LN_EPS = 1e-5
_GN_EPS = 1e-6

_LANES = 128
_MXU_COLS = 256
_VMEM_PHYSICAL_BYTES = 64 * 1024 * 1024
_VMEM_LIMIT_BYTES = (_VMEM_PHYSICAL_BYTES * 7) // 8

_TM = 512
_TQ = 256
_TR_RET = 512

def _params(n_axes):
    return pltpu.CompilerParams(
        dimension_semantics=("arbitrary",) * n_axes,
        vmem_limit_bytes=_VMEM_LIMIT_BYTES,
    )


def _layer_norm(x, g, b, eps):
    mu = jnp.mean(x, axis=-1, keepdims=True)
    xc = x - mu
    var = jnp.mean(xc * xc, axis=-1, keepdims=True)
    return xc * lax.rsqrt(var + eps) * g + b


def _silu(x):
    return x * (1.0 / (1.0 + jnp.exp(-x)))


def _gelu_tanh(x):
    c = math.sqrt(2.0 / math.pi)
    return x * (0.5 * (1.0 + jnp.tanh(c * (x + 0.044715 * (x * x * x)))))


def _dot(a, b):
    return jnp.dot(a, b, preferred_element_type=_F32)


def _dot_nt(a, b):
    return lax.dot_general(a, b, (((1,), (1,)), ((), ())), preferred_element_type=_F32)


def _dot_tn(a, b):
    return lax.dot_general(a, b, (((0,), (0,)), ((), ())), preferred_element_type=_F32)


def _matmul_kernel(x_ref, w_ref, o_ref):
    o_ref[...] = _dot(x_ref[...], w_ref[...]).astype(o_ref.dtype)


def _matmul(x, w, *, tm, tn, out_dtype, name):
    m, k = x.shape
    n = w.shape[1]
    return pl.pallas_call(
        _matmul_kernel,
        out_shape=jax.ShapeDtypeStruct((m, n), out_dtype),
        grid=(n // tn, m // tm),
        in_specs=[pl.BlockSpec((tm, k), lambda j, i: (i, 0)),
                  pl.BlockSpec((k, tn), lambda j, i: (0, j))],
        out_specs=pl.BlockSpec((tm, tn), lambda j, i: (i, j)),
        compiler_params=_params(2),
        name=name,
    )(x, w)


def _retention_project(x16_ref, w_ref, v_sc, g_sc):
    width = _RET_HEADS * _RET_DK
    wv = _RET_HEADS * _RET_DV
    x16 = x16_ref[...].astype(_BF16)
    q = _dot(x16, w_ref[:, 0:width])
    k = _dot(x16, w_ref[:, width:2 * width])
    v_sc[...] = _dot(x16, w_ref[:, 2 * width:2 * width + wv]).astype(_BF16)
    g_sc[...] = _dot(x16, w_ref[:, 2 * width + wv:2 * width + 2 * wv])
    return q, k


def _retention_core(q_raw, k_raw, cos_ref, sin_ref, qdec_ref, kdec_ref, din_ref,
                    y_ref, v_sc, g_sc, state_sc, *, chunk_dec):
    width = _RET_HEADS * _RET_DK
    wv = _RET_HEADS * _RET_DV
    rows_total = q_raw.shape[0]
    lane_all = lax.broadcasted_iota(jnp.int32, (rows_total, width), 1)
    first_half = (lane_all % _RET_DK) < (_RET_DK // 2)

    def rotary(x):
        partner = jnp.where(first_half,
                            pltpu.roll(x, width - _RET_DK // 2, 1),
                            pltpu.roll(x, _RET_DK // 2, 1))
        return x * cos_ref[...] + partner * sin_ref[...]

    q_all = rotary(q_raw)
    k_all = rotary(k_raw) * (_RET_DK ** -0.5)
    lane = lax.broadcasted_iota(jnp.int32, (_BLOCK, width), 1)
    srow = lax.broadcasted_iota(jnp.int32, (width, wv), 0) // _RET_DK
    scol = lax.broadcasted_iota(jnp.int32, (width, wv), 1) // _RET_DV
    on_diag = srow == scol
    col_head = lax.broadcasted_iota(jnp.int32, (1, wv), 1) // _RET_DV
    decay_cols = jnp.zeros((1, wv), _F32)
    for h in range(_RET_HEADS):
        decay_cols = jnp.where(col_head == h, chunk_dec[h], decay_cols)
    for cc in range(rows_total // _BLOCK):
        rows = slice(cc * _BLOCK, (cc + 1) * _BLOCK)
        q = q_all[rows]
        k = k_all[rows]
        v = v_sc[rows, :]
        q_heads = jnp.concatenate(
            [jnp.where((lane // _RET_DK) == h, q, 0.0) for h in range(_RET_HEADS)], axis=0)
        scores = _dot_nt(q_heads.astype(_BF16), k.astype(_BF16))
        state = state_sc[...]
        cross = _dot((q * qdec_ref[...]).astype(_BF16), state.astype(_BF16))
        kv = _dot_tn((k * kdec_ref[...]).astype(_BF16), v)
        state_sc[...] = state * decay_cols + jnp.where(on_diag, kv, 0.0)
        for h in range(_RET_HEADS):
            cols = slice(h * _RET_DV, (h + 1) * _RET_DV)
            sh = scores[h * _BLOCK:(h + 1) * _BLOCK] * din_ref[h]
            out = _dot(sh.astype(_BF16), v[:, cols]) + cross[:, cols]
            mu = jnp.mean(out, axis=-1, keepdims=True)
            oc = out - mu
            var = jnp.mean(oc * oc, axis=-1, keepdims=True)
            normed = oc * lax.rsqrt(var + _GN_EPS)
            y_ref[rows, cols] = (_silu(g_sc[rows, cols]) * normed).astype(y_ref.dtype)


def _retention_tables(seq):
    half = _RET_DK // 2
    inv_freq = _ROPE_BASE ** (-jnp.arange(half, dtype=_F32) / half)
    ang = jnp.arange(seq, dtype=jnp.int32).astype(_F32)[:, None] * inv_freq[None, :]
    cos, sin = jnp.cos(ang), jnp.sin(ang)
    cos_t = jnp.tile(jnp.concatenate([cos, cos], axis=1), (1, _RET_HEADS))
    sin_t = jnp.tile(jnp.concatenate([-sin, sin], axis=1), (1, _RET_HEADS))
    log_gamma = jnp.log1p(-jnp.exp2(-5.0 - jnp.arange(_RET_HEADS, dtype=_F32)))
    idx = jnp.arange(_BLOCK, dtype=_F32)
    rel = idx[:, None] - idx[None, :]
    causal = rel >= 0
    decay_in = jnp.where(causal[None],
                         jnp.exp(log_gamma[:, None, None] * jnp.where(causal, rel, 0.0)[None]), 0.0)
    q_dec = jnp.exp(log_gamma[:, None] * (idx + 1.0))
    k_dec = jnp.exp(log_gamma[:, None] * (_BLOCK - 1.0 - idx))
    q_dec_t = jnp.repeat(q_dec.T, _RET_DK, axis=1)
    k_dec_t = jnp.repeat(k_dec.T, _RET_DK, axis=1)
    return cos_t, sin_t, q_dec_t, k_dec_t, decay_in


def _mixer_ab_kernel(x_ref, wr_ref, cos_ref, sin_ref, qdec_ref, kdec_ref, din_ref, wf_ref, b_ref, sel_ref,
                     y_ref, qa_ref, ka_ref, va_ref,
                     v_sc, g_sc, state_sc, fq_sc, fk_sc, fv_sc, fl_sc, carry_sc, *, chunk_dec, nblk):
    @pl.when(pl.program_id(1) == 0)
    def _():
        state_sc[...] = jnp.zeros_like(state_sc)
        carry_sc[...] = jnp.zeros_like(carry_sc)

    _fox_prep_body(x_ref, wf_ref, b_ref, sel_ref, qa_ref, ka_ref, va_ref, fq_sc, fk_sc, fv_sc, fl_sc,
                   carry_sc, nblk=nblk)
    q_raw, k_raw = _retention_project(x_ref, wr_ref, v_sc, g_sc)
    _retention_core(q_raw, k_raw, cos_ref, sin_ref, qdec_ref, kdec_ref, din_ref, y_ref, v_sc, g_sc,
                    state_sc, chunk_dec=chunk_dec)


def _mixer_ab(x16, w_r, w_f, bias, bsz, seq):
    tokens, d_model = x16.shape
    tr = min(_TR_RET, seq)
    nstep = seq // tr
    cos_t, sin_t, q_dec_t, k_dec_t, decay_in = _retention_tables(seq)
    chunk_dec = tuple(float((1.0 - 2.0 ** (-5.0 - h)) ** _BLOCK) for h in range(_RET_HEADS))
    qk = _RET_HEADS * _RET_DK
    wv = _RET_HEADS * _RET_DV
    fw = _FOX_HEADS * _FOX_DH
    sel = _fox_selectors()
    row = lambda b, c: (b * nstep + c, 0)
    const2 = lambda b, c: (0, 0)
    const3 = lambda b, c: (0, 0, 0)
    aug = jax.ShapeDtypeStruct((bsz, _FOX_HEADS, seq, _LANES), _BF16)
    aug_spec = pl.BlockSpec((None, _FOX_HEADS, tr, _LANES), lambda b, c: (b, 0, c, 0))
    return pl.pallas_call(
        functools.partial(_mixer_ab_kernel, chunk_dec=chunk_dec, nblk=tr // _BLOCK),
        out_shape=(jax.ShapeDtypeStruct((tokens, wv), _BF16), aug, aug, aug),
        grid=(bsz, nstep),
        in_specs=[
            pl.BlockSpec((tr, d_model), row),
            pl.BlockSpec(w_r.shape, const2),
            pl.BlockSpec((tr, qk), lambda b, c: (c, 0)),
            pl.BlockSpec((tr, qk), lambda b, c: (c, 0)),
            pl.BlockSpec((_BLOCK, qk), const2),
            pl.BlockSpec((_BLOCK, qk), const2),
            pl.BlockSpec((_RET_HEADS, _BLOCK, _BLOCK), const3),
            pl.BlockSpec(w_f.shape, const2),
            pl.BlockSpec((1, _LANES), const2),
            pl.BlockSpec(sel.shape, const3),
        ],
        out_specs=(pl.BlockSpec((tr, wv), row), aug_spec, aug_spec, aug_spec),
        scratch_shapes=[pltpu.VMEM((tr, wv), _BF16), pltpu.VMEM((tr, wv), _F32),
                        pltpu.VMEM((qk, wv), _F32),
                        pltpu.VMEM((tr, fw), _BF16), pltpu.VMEM((tr, fw), _BF16),
                        pltpu.VMEM((tr, fw), _BF16), pltpu.VMEM((tr, _LANES), _F32),
                        pltpu.VMEM((1, _LANES), _F32)],
        compiler_params=_params(2),
        name="mixer_ab",
    )(x16, w_r, cos_t, sin_t, q_dec_t, k_dec_t, decay_in, w_f, bias, sel)


_C_PARTS = 3
_TS_PREP = 512


def _fox_selectors():
    half = _LANES // 2
    one_row = _C_PARTS * _FOX_HEADS
    sel = np.zeros((_FOX_HEADS, _LANES, 2 * _LANES), np.float32)
    for h in range(_FOX_HEADS):
        base = half if h % 2 == 0 else 0
        for n in range(_C_PARTS):
            sel[h, n * _FOX_HEADS + h, base + n] = 1.0
            sel[h, one_row, base + _C_PARTS + n] = 1.0
            sel[h, one_row, _LANES + base + n] = 1.0
            sel[h, n * _FOX_HEADS + h, _LANES + base + _C_PARTS + n] = -1.0
    return jnp.asarray(sel, _BF16)


def _fox_prep_body(x16_ref, w_ref, b_ref, sel_ref, qa_ref, ka_ref, va_ref, fq_ref, fk_ref, fv_ref,
                   fl_ref, carry_sc, *, nblk):
    x16 = x16_ref[...].astype(_BF16)
    width = _FOX_HEADS * _FOX_DH
    for n, dst in enumerate((fq_ref, fk_ref, fv_ref)):
        dst[...] = _dot(x16, w_ref[:, n * width:(n + 1) * width]).astype(_BF16)
    fl_ref[...] = _dot(x16, w_ref[:, 3 * width:3 * width + _LANES])

    r = lax.broadcasted_iota(jnp.int32, (_BLOCK, _LANES), 0)
    lane = lax.broadcasted_iota(jnp.int32, (_BLOCK, _LANES), 1)
    tri = (r >= lane).astype(_BF16)
    half = _LANES // 2
    q_scale = jnp.asarray(_FOX_DH ** -0.5, _BF16)
    carry = carry_sc[...]
    packed = []
    for blk in range(nblk):
        rows = slice(blk * _BLOCK, (blk + 1) * _BLOCK)
        z = fl_ref[rows, :] + b_ref[...]
        log_f = jnp.minimum(z, 0.0) - jnp.log1p(jnp.exp(-jnp.abs(z)))
        pieces = []
        rest = log_f
        for _ in range(_C_PARTS):
            piece = rest.astype(_BF16)
            pieces.append(piece)
            rest = rest - piece.astype(_F32)
        sums = _dot(tri, jnp.concatenate(pieces, axis=1))
        cum = sums[:, (_C_PARTS - 1) * _LANES:]
        for n in reversed(range(_C_PARTS - 1)):
            cum = cum + sums[:, n * _LANES:(n + 1) * _LANES]
        cum = cum + carry
        carry = cum[_BLOCK - 1:_BLOCK, :]
        row = jnp.where(lane == _C_PARTS * _FOX_HEADS, 1.0, 0.0)
        rest = cum
        for n in range(_C_PARTS):
            piece = rest.astype(_BF16).astype(_F32)
            rest = rest - piece
            shifted = piece if n == 0 else pltpu.roll(piece, n * _FOX_HEADS, 1)
            row = jnp.where((lane >= n * _FOX_HEADS) & (lane < (n + 1) * _FOX_HEADS), shifted, row)
        packed.append(row.astype(_BF16))
    carry_sc[...] = carry
    packed = jnp.concatenate(packed, axis=0)
    lane_all = lax.broadcasted_iota(jnp.int32, (nblk * _BLOCK, _LANES), 1)
    for h in range(_FOX_HEADS):
        pair = slice((h // 2) * _LANES, (h // 2 + 1) * _LANES)
        in_head = (lane_all < half) if h % 2 == 0 else (lane_all >= half)
        extra = _dot(packed, sel_ref[h]).astype(_BF16)
        qa_ref[h] = jnp.where(in_head, fq_ref[:, pair] * q_scale, extra[:, :_LANES])
        ka_ref[h] = jnp.where(in_head, fk_ref[:, pair], extra[:, _LANES:])
        va_ref[h] = jnp.where(in_head, fv_ref[:, pair], jnp.ones((), _BF16))


def _fox_kernel(qa_ref, ka_ref, va_ref, o_ref, *, tq, nq):
    half = _LANES // 2
    low = lax.broadcasted_iota(jnp.int32, (tq, _LANES), 1) < half
    keep = (lax.broadcasted_iota(jnp.int32, (tq, tq), 0)
            >= lax.broadcasted_iota(jnp.int32, (tq, tq), 1))
    order = list(reversed(range(nq)))
    logits = {n: [_dot_nt(qa_ref[s, n * tq:(n + 1) * tq, :], ka_ref[s, 0:(n + 1) * tq, :])
                  for s in range(2)] for n in order}
    for n in order:
        rows = slice(n * tq, (n + 1) * tq)
        width = (n + 1) * tq
        pv = []
        for s in range(2):
            diag = jnp.where(keep, logits[n][s][:, width - tq:], -jnp.inf)
            lg = diag if n == 0 else jnp.concatenate([logits[n][s][:, :width - tq], diag], axis=1)
            pexp = jnp.exp(lg - jnp.max(lg, axis=1, keepdims=True))
            pv.append(_dot(pexp.astype(_BF16), va_ref[s, 0:width, :]))
        out = jnp.where(low, pv[0] / pv[0][:, half:half + 1], pv[1] / pv[1][:, 0:1])
        o_ref[rows, :] = out.astype(o_ref.dtype)


def _fox(qa, ka, va, bsz, seq):
    tq = min(_TQ, seq)
    nq = seq // tq
    npair = _FOX_HEADS // 2
    logit_bytes = 2 * tq * tq * (nq * (nq + 1) // 2) * 4
    assert logit_bytes <= _VMEM_LIMIT_BYTES // 2, "sequence too long for the single-pass attention step"
    return pl.pallas_call(
        functools.partial(_fox_kernel, tq=tq, nq=nq),
        out_shape=jax.ShapeDtypeStruct((bsz * seq, _FOX_HEADS * _FOX_DH), _BF16),
        grid=(bsz, npair),
        in_specs=[
            pl.BlockSpec((None, 2, seq, _LANES), lambda b, p: (b, p, 0, 0)),
            pl.BlockSpec((None, 2, seq, _LANES), lambda b, p: (b, p, 0, 0)),
            pl.BlockSpec((None, 2, seq, _LANES), lambda b, p: (b, p, 0, 0)),
        ],
        out_specs=pl.BlockSpec((seq, _LANES), lambda b, p: (b, p)),
        compiler_params=_params(2),
        name="fox_attention",
    )(qa, ka, va)


def _gmlp_tile(gu, gv, w_ref, bst_ref, lng_ref, lnb_ref, yc_sc):
    u = _gelu_tanh(gu)
    v = _layer_norm(_gelu_tanh(gv), lng_ref[...], lnb_ref[...], _LN_EPS).astype(_BF16)
    r = lax.broadcasted_iota(jnp.int32, (_BLOCK, _BLOCK), 0)
    c = lax.broadcasted_iota(jnp.int32, (_BLOCK, _BLOCK), 1)
    causal = r >= c
    nchunk = gu.shape[0] // _BLOCK
    for g in range(_GM_GROUPS):
        cols = slice(g * _GM_DG, (g + 1) * _GM_DG)
        wg = jnp.where(causal, w_ref[g], 0.0).astype(_BF16)
        bias = bst_ref[:, g:g + 1]
        v_chunks = jnp.concatenate(
            [v[ch * _BLOCK:(ch + 1) * _BLOCK, cols] for ch in range(nchunk)], axis=1)
        sp = _dot(wg, v_chunks) + bias
        for ch in range(nchunk):
            rows = slice(ch * _BLOCK, (ch + 1) * _BLOCK)
            yc_sc[rows, cols] = (u[rows, cols] * sp[:, ch * _GM_DG:(ch + 1) * _GM_DG]).astype(yc_sc.dtype)


def _mixer_out_kernel(x16_ref, ya_ref, yb_ref, wc_ref, gmw_ref, bst_ref, gmg_ref, gmb_ref, gbias_ref,
                      wbr_ref, wout_ref, x_ref, lng_ref, lnb_ref, *rest, alpha, d_model, with_router):
    if with_router:
        wr_ref, xo_ref, xtok_ref, route_ref, yc_sc = rest
    else:
        xo_ref, xo16_ref, yc_sc = rest
    gw = _GM_GROUPS * _GM_DG
    sub = d_model // _LANES

    def front(rows):
        x16 = x16_ref[rows, :].astype(_BF16)
        gu = _dot(x16, wc_ref[:, 0:gw])
        gv = _dot(x16, wc_ref[:, gw:2 * gw])

        def gate(n):
            cols = slice(2 * gw + n * d_model, 2 * gw + (n + 1) * d_model)
            z = _dot(x16, wc_ref[:, cols]) + gbias_ref[:, n * d_model:(n + 1) * d_model]
            return 1.0 / (1.0 + jnp.exp(-z))

        merged = (gate(0) * _dot(ya_ref[rows, :], wbr_ref[0])
                  + gate(1) * _dot(yb_ref[rows, :], wbr_ref[1]))
        return gu, gv, gate(2), merged

    def back(rows, gu, gv, gate_c, merged):
        yc = yc_sc.at[rows, :]
        _gmlp_tile(gu, gv, gmw_ref, bst_ref, gmg_ref, gmb_ref, yc)
        merged = merged + gate_c * _dot(yc[...], wbr_ref[2])
        h = _dot(merged.astype(_BF16), wout_ref[...])
        out = _layer_norm(alpha * x_ref[rows, :] + h, lng_ref[...], lnb_ref[...], _LN_EPS)
        xo_ref[rows, :] = out
        if with_router:
            _store_token_rows(xtok_ref.at[pl.ds(rows.start * sub, (rows.stop - rows.start) * sub), :], out)
            route_ref[rows, :] = _route_top2(out, wr_ref)
        else:
            xo16_ref[rows, :] = out.astype(_BF16)

    rows = slice(0, x16_ref.shape[0])
    back(rows, *front(rows))


def _mixer_out(x16, y_a, y_b, w_c, gm_w_s, gm_b_s, gm_ln_g, gm_ln_b, gate_b, w_branch, w_out, x,
               ln_g, ln_b, w_router, *, alpha, tm):
    tokens, d_model = x.shape
    bw = y_a.shape[1]
    gw = _GM_GROUPS * _GM_DG
    row = lambda i: (i, 0)
    const2 = lambda i: (0, 0)
    const3 = lambda i: (0, 0, 0)
    with_router = w_router is not None
    in_specs = [
        pl.BlockSpec((tm, d_model), row),
        pl.BlockSpec((tm, bw), row), pl.BlockSpec((tm, bw), row),
        pl.BlockSpec(w_c.shape, const2),
        pl.BlockSpec((_GM_GROUPS, _BLOCK, _BLOCK), const3),
        pl.BlockSpec((_BLOCK, _GM_GROUPS), const2),
        pl.BlockSpec((1, gw), const2), pl.BlockSpec((1, gw), const2),
        pl.BlockSpec((1, 3 * d_model), const2),
        pl.BlockSpec((3, bw, d_model), const3),
        pl.BlockSpec((d_model, d_model), const2),
        pl.BlockSpec((tm, d_model), row),
        pl.BlockSpec((1, d_model), const2), pl.BlockSpec((1, d_model), const2),
    ]
    args = [x16, y_a, y_b, w_c, gm_w_s, gm_b_s.T, gm_ln_g.reshape(1, gw), gm_ln_b.reshape(1, gw),
            gate_b.reshape(1, -1), w_branch.astype(_BF16), w_out.astype(_BF16), x,
            ln_g.reshape(1, -1), ln_b.reshape(1, -1)]
    out_shape = [jax.ShapeDtypeStruct((tokens, d_model), _F32)]
    out_specs = [pl.BlockSpec((tm, d_model), row)]
    if with_router:
        sub = d_model // _LANES
        in_specs.append(pl.BlockSpec((d_model, 2 * _LANES), const2))
        args.append(_router_weights(w_router))
        out_shape += [jax.ShapeDtypeStruct((tokens * sub, _LANES), _F32),
                      jax.ShapeDtypeStruct((tokens, _LANES), _F32)]
        out_specs += [pl.BlockSpec((tm * sub, _LANES), row), pl.BlockSpec((tm, _LANES), row)]
    else:
        out_shape.append(jax.ShapeDtypeStruct((tokens, d_model), _BF16))
        out_specs.append(pl.BlockSpec((tm, d_model), row))
    outs = pl.pallas_call(
        functools.partial(_mixer_out_kernel, alpha=alpha, d_model=d_model, with_router=with_router),
        out_shape=tuple(out_shape),
        grid=(tokens // tm,),
        in_specs=in_specs,
        out_specs=tuple(out_specs),
        scratch_shapes=[pltpu.VMEM((tm, gw), _BF16)],
        compiler_params=_params(1),
        name="mixer_out",
    )(*args)
    return outs if with_router else (*outs, None)


def _tile_groups(te):
    change = te[1:] != te[:-1]
    one = jnp.ones((1,), jnp.bool_)
    first = jnp.concatenate([one, change]).astype(jnp.int32)
    last = jnp.concatenate([change, one]).astype(jnp.int32)
    later = jnp.where(te[None, :] > te[:, None], te[None, :], _N_EXPERTS)
    nexte = jnp.min(later, axis=1)
    wrap = (nexte == _N_EXPERTS).astype(jnp.int32)
    return first, last, jnp.where(wrap == 1, te[0], nexte), wrap


def _load_token_rows(ref, n):
    s = ref.shape[0] // n
    return jnp.concatenate([ref[pl.ds(c, n, stride=s), :] for c in range(s)], axis=1)


def _store_token_rows(ref, x):
    n = x.shape[0]
    s = x.shape[1] // _LANES
    for c in range(s):
        ref[pl.ds(c, n, stride=s), :] = x[:, c * _LANES:(c + 1) * _LANES]


def _up_kernel(te_ref, nused_ref, first_ref, last_ref, nexte_ref, wrap_ref, x_ref, w_hbm, h_ref,
               stage_a, stage_b, wa_sc, wb_sc, sem, slot_sc, *, tn, nb, d_ff):
    j = pl.program_id(0)
    i = pl.program_id(1)

    def weight_copies(e, jj):
        col = pl.multiple_of(jj * tn, tn)
        return (pltpu.make_async_copy(w_hbm.at[e, :, pl.ds(col, tn)], stage_a, sem.at[0]),
                pltpu.make_async_copy(w_hbm.at[e, :, pl.ds(d_ff + col, tn)], stage_b, sem.at[1]))

    def round_into(slot):
        wa_sc[slot] = stage_a[...].astype(_BF16)
        wb_sc[slot] = stage_b[...].astype(_BF16)

    @pl.when((j == 0) & (i == 0))
    def _():
        for c in weight_copies(te_ref[0], 0):
            c.start()
        for c in weight_copies(te_ref[0], 0):
            c.wait()
        round_into(0)
        slot_sc[0] = 0

    has_next = (wrap_ref[i] == 0) | (j + 1 < nb)
    next_j = jnp.where(wrap_ref[i] == 1, j + 1, j)

    @pl.when((first_ref[i] == 1) & has_next)
    def _():
        for c in weight_copies(nexte_ref[i], next_j):
            c.start()

    slot = slot_sc[0]

    @pl.when(i < nused_ref[0])
    def _():
        x = _load_token_rows(x_ref, h_ref.shape[0]).astype(_BF16)
        a = _dot(x, wa_sc[slot])
        b = _dot(x, wb_sc[slot])
        h_ref[...] = (_silu(a) * b).astype(h_ref.dtype)

    @pl.when(i >= nused_ref[0])
    def _():
        h_ref[...] = jnp.zeros_like(h_ref)

    @pl.when((last_ref[i] == 1) & has_next)
    def _():
        for c in weight_copies(nexte_ref[i], next_j):
            c.wait()
        round_into(1 - slot)
        slot_sc[0] = 1 - slot


def _swiglu_up(xs, w_up, te, nused, *, tm, tn):
    k = w_up.shape[1]
    sub = k // _LANES
    rows = xs.shape[0] // sub
    d_ff = w_up.shape[2] // 2
    nb = d_ff // tn
    idx = lambda j, i, *prefetch: (i, 0)
    grid_spec = pltpu.PrefetchScalarGridSpec(
        num_scalar_prefetch=6,
        grid=(nb, rows // tm),
        in_specs=[pl.BlockSpec((tm * sub, _LANES), idx), pl.BlockSpec(memory_space=pl.ANY)],
        out_specs=pl.BlockSpec((tm, tn), lambda j, i, *prefetch: (i, j)),
        scratch_shapes=[pltpu.VMEM((k, tn), _F32), pltpu.VMEM((k, tn), _F32),
                        pltpu.VMEM((2, k, tn), _BF16), pltpu.VMEM((2, k, tn), _BF16),
                        pltpu.SemaphoreType.DMA((2,)), pltpu.SMEM((1,), jnp.int32)],
    )
    return pl.pallas_call(
        functools.partial(_up_kernel, tn=tn, nb=nb, d_ff=d_ff),
        out_shape=jax.ShapeDtypeStruct((rows, d_ff), _BF16),
        grid_spec=grid_spec,
        compiler_params=_params(2),
        name="swiglu_up",
    )(te, nused, *_tile_groups(te), xs, w_up)


def _down_kernel(te_ref, nused_ref, first_ref, last_ref, nexte_ref, wrap_ref, h_ref, w_hbm, y_ref,
                 stage, w_sc, sem, slot_sc):
    i = pl.program_id(0)

    def weight_copy(e):
        return pltpu.make_async_copy(w_hbm.at[e], stage, sem)

    @pl.when(i == 0)
    def _():
        weight_copy(te_ref[0]).start()
        weight_copy(te_ref[0]).wait()
        w_sc[0] = stage[...].astype(_BF16)
        slot_sc[0] = 0

    has_next = wrap_ref[i] == 0

    @pl.when((first_ref[i] == 1) & has_next)
    def _():
        weight_copy(nexte_ref[i]).start()

    slot = slot_sc[0]

    @pl.when(i < nused_ref[0])
    def _():
        _store_token_rows(y_ref, _dot(h_ref[...], w_sc[slot]))

    @pl.when(i >= nused_ref[0])
    def _():
        y_ref[...] = jnp.zeros_like(y_ref)

    @pl.when((last_ref[i] == 1) & has_next)
    def _():
        weight_copy(nexte_ref[i]).wait()
        w_sc[1 - slot] = stage[...].astype(_BF16)
        slot_sc[0] = 1 - slot


def _expert_down(h16, w_down, te, nused, *, tm):
    rows, k = h16.shape
    n = w_down.shape[2]
    sub = n // _LANES
    grid_spec = pltpu.PrefetchScalarGridSpec(
        num_scalar_prefetch=6,
        grid=(rows // tm,),
        in_specs=[pl.BlockSpec((tm, k), lambda i, *prefetch: (i, 0)),
                  pl.BlockSpec(memory_space=pl.ANY)],
        out_specs=pl.BlockSpec((tm * sub, _LANES), lambda i, *prefetch: (i, 0)),
        scratch_shapes=[pltpu.VMEM((k, n), _F32), pltpu.VMEM((2, k, n), _BF16),
                        pltpu.SemaphoreType.DMA, pltpu.SMEM((1,), jnp.int32)],
    )
    return pl.pallas_call(
        _down_kernel,
        out_shape=jax.ShapeDtypeStruct((rows * sub, _LANES), _F32),
        grid_spec=grid_spec,
        compiler_params=_params(1),
        name="expert_down",
    )(te, nused, *_tile_groups(te), h16, w_down)


def _dense_ffn_kernel(x16_ref, wup_ref, wdn_ref, x_ref, lng_ref, lnb_ref, xo_ref, xo16_ref,
                      *, alpha, d_ff, chunks):
    x16 = x16_ref[...]
    f = None
    for c0, c1 in chunks:
        a = _dot(x16, wup_ref[:, c0:c1])
        b = _dot(x16, wup_ref[:, d_ff + c0:d_ff + c1])
        part = _dot((_silu(a) * b).astype(_BF16), wdn_ref[c0:c1, :])
        f = part if f is None else f + part
    out = _layer_norm(alpha * x_ref[...] + f, lng_ref[...], lnb_ref[...], _LN_EPS)
    xo_ref[...] = out
    xo16_ref[...] = out.astype(_BF16)


def _dense_ffn_ln(x, x16, w_up, w_down, ln_g, ln_b, *, alpha, tm):
    tokens, d_model = x.shape
    d_ff = w_down.shape[0]
    split = (d_ff // (2 * _MXU_COLS)) * _MXU_COLS
    chunks = ((0, split), (split, d_ff)) if 0 < split < d_ff else ((0, d_ff),)
    row = lambda i: (i, 0)
    const2 = lambda i: (0, 0)
    return pl.pallas_call(
        functools.partial(_dense_ffn_kernel, alpha=alpha, d_ff=d_ff, chunks=chunks),
        out_shape=(jax.ShapeDtypeStruct((tokens, d_model), _F32),
                   jax.ShapeDtypeStruct((tokens, d_model), _BF16)),
        grid=(tokens // tm,),
        in_specs=[pl.BlockSpec((tm, d_model), row),
                  pl.BlockSpec((d_model, 2 * d_ff), const2),
                  pl.BlockSpec((d_ff, d_model), const2),
                  pl.BlockSpec((tm, d_model), row),
                  pl.BlockSpec((1, d_model), const2), pl.BlockSpec((1, d_model), const2)],
        out_specs=(pl.BlockSpec((tm, d_model), row), pl.BlockSpec((tm, d_model), row)),
        compiler_params=_params(1),
        name="dense_ffn_ln",
    )(x16, w_up.astype(_BF16), w_down.astype(_BF16), x, ln_g.reshape(1, -1), ln_b.reshape(1, -1))


def _route_top2(x, wr_ref):
    x_hi = x.astype(_BF16)
    x_lo = (x - x_hi.astype(_F32)).astype(_BF16)
    both = _dot(x_hi, wr_ref[...])
    logits = both[:, :_LANES] + (both[:, _LANES:] + _dot(x_lo, wr_ref[:, :_LANES]))
    lane = lax.broadcasted_iota(jnp.int32, logits.shape, 1).astype(_F32)
    lg = jnp.where(lane < _N_EXPERTS, logits, -jnp.inf)
    m1 = jnp.max(lg, axis=1, keepdims=True)
    i1 = jnp.min(jnp.where(lg == m1, lane, float(_LANES)), axis=1, keepdims=True)
    lg2 = jnp.where(lane == i1, -jnp.inf, lg)
    m2 = jnp.max(lg2, axis=1, keepdims=True)
    i2 = jnp.min(jnp.where(lg2 == m2, lane, float(_LANES)), axis=1, keepdims=True)
    e2 = jnp.exp(m2 - m1)
    den = 1.0 + e2
    w1 = 1.0 / den
    w2 = e2 / den
    return jnp.where(lane == 0, i1, jnp.where(lane == 1, i2,
                     jnp.where(lane == 2, w1, jnp.where(lane == 3, w2, 0.0))))


def _router_weights(w_router):
    d_model = w_router.shape[0]
    wr = jnp.zeros((d_model, _LANES), _F32).at[:, :_N_EXPERTS].set(w_router)
    wr_hi = wr.astype(_BF16)
    wr_lo = (wr - wr_hi.astype(_F32)).astype(_BF16)
    return jnp.concatenate([wr_hi, wr_lo], axis=1)


_ROW_UNROLL = 8


def _dispatch_kernel(zstart_ref, zvalid_ref, d0_ref, d1_ref, x_ref, xs_hbm, zero_sc, stage_sc, sem, zsem,
                     *, tt, tm, sub, nsteps):
    @pl.when(pl.program_id(0) == 0)
    def _():
        zero_sc[...] = jnp.zeros_like(zero_sc)

        def zero_copy(n):
            start = pl.multiple_of(zstart_ref[n] * sub, tm * sub)
            return pltpu.make_async_copy(zero_sc, xs_hbm.at[pl.ds(start, tm * sub), :], zsem)

        for n in range(2 * _N_EXPERTS):
            @pl.when(zvalid_ref[n] > 0)
            def _():
                zero_copy(n).start()
        for n in range(2 * _N_EXPERTS):
            @pl.when(zvalid_ref[n] > 0)
            def _():
                zero_copy(n).wait()

    i = pl.program_id(0)
    slot = i % 2
    stage_sc[slot] = x_ref[...]

    def issue(g, carry):
        for u in range(_ROW_UNROLL):
            r = g * _ROW_UNROLL + u
            src = stage_sc.at[slot, pl.ds(pl.multiple_of(r * sub, sub), sub), :]
            for d_ref in (d0_ref, d1_ref):
                dst = xs_hbm.at[pl.ds(pl.multiple_of(d_ref[0, 0, r] * sub, sub), sub), :]
                pltpu.make_async_copy(src, dst, sem.at[slot]).start(priority=u % 2)
        return carry

    def drain(s):
        for _ in range(2):
            pltpu.make_async_copy(stage_sc.at[s], xs_hbm.at[pl.ds(0, tt * sub), :], sem.at[s]).wait()

    lax.fori_loop(0, tt // _ROW_UNROLL, issue, 0)

    @pl.when(i > 0)
    def _():
        drain(1 - slot)

    @pl.when(i == nsteps - 1)
    def _():
        drain(slot)


def _dispatch_rows(x_tok, d0, d1, zstart, zvalid, *, tokens, rows, tt, tm):
    sub = x_tok.shape[0] // tokens
    smem = lambda: pl.BlockSpec((1, 1, tt), lambda i, zs, zv: (i, 0, 0), memory_space=pltpu.SMEM)
    grid_spec = pltpu.PrefetchScalarGridSpec(
        num_scalar_prefetch=2,
        grid=(tokens // tt,),
        in_specs=[smem(), smem(), pl.BlockSpec((tt * sub, _LANES), lambda i, zs, zv: (i, 0))],
        out_specs=pl.BlockSpec(memory_space=pl.ANY),
        scratch_shapes=[pltpu.VMEM((tm * sub, _LANES), _F32), pltpu.VMEM((2, tt * sub, _LANES), _F32),
                        pltpu.SemaphoreType.DMA((2,)), pltpu.SemaphoreType.DMA],
    )
    return pl.pallas_call(
        functools.partial(_dispatch_kernel, tt=tt, tm=tm, sub=sub, nsteps=tokens // tt),
        out_shape=jax.ShapeDtypeStruct((rows * sub, _LANES), _F32),
        grid_spec=grid_spec,
        compiler_params=_params(1),
        name="moe_dispatch",
    )(zstart, zvalid, d0, d1, x_tok)


def _combine_kernel(d0_ref, d1_ref, n0_ref, n1_ref, y_hbm, route_ref, x_ref, lng_ref, lnb_ref,
                    xo_ref, xo16_ref, buf, sem, *, alpha, tm, sub, nsteps):
    i = pl.program_id(0)
    slot = i % 2

    def gather(dst_slot, a_ref, b_ref):
        def issue(g, carry):
            for u in range(_ROW_UNROLL):
                r = g * _ROW_UNROLL + u
                for s, d_ref in enumerate((a_ref, b_ref)):
                    src = y_hbm.at[pl.ds(pl.multiple_of(d_ref[0, 0, r] * sub, sub), sub), :]
                    dst = buf.at[dst_slot, s, pl.ds(pl.multiple_of(r * sub, sub), sub), :]
                    pltpu.make_async_copy(src, dst, sem.at[dst_slot]).start(priority=u % 2)
            return carry

        lax.fori_loop(0, tm // _ROW_UNROLL, issue, 0)

    @pl.when(i == 0)
    def _():
        gather(0, d0_ref, d1_ref)

    @pl.when(i + 1 < nsteps)
    def _():
        gather(1 - slot, n0_ref, n1_ref)

    for s in range(2):
        pltpu.make_async_copy(y_hbm.at[pl.ds(0, tm * sub), :], buf.at[slot, s], sem.at[slot]).wait()
    route = route_ref[...]
    f = (route[:, 2:3] * _load_token_rows(buf.at[slot, 0], tm)
         + route[:, 3:4] * _load_token_rows(buf.at[slot, 1], tm))
    out = _layer_norm(alpha * x_ref[...] + f, lng_ref[...], lnb_ref[...], _LN_EPS)
    xo_ref[...] = out
    xo16_ref[...] = out.astype(_BF16)


def _combine_ln(ys, d0, d1, route, x, ln_g, ln_b, *, alpha, tm):
    tokens, d_model = x.shape
    sub = d_model // _LANES
    nsteps = tokens // tm
    row = lambda i: (i, 0)
    const2 = lambda i: (0, 0)
    cur = lambda: pl.BlockSpec((1, 1, tm), lambda i: (i, 0, 0), memory_space=pltpu.SMEM)
    nxt = lambda: pl.BlockSpec((1, 1, tm), lambda i: (jnp.minimum(i + 1, nsteps - 1), 0, 0),
                               memory_space=pltpu.SMEM)
    return pl.pallas_call(
        functools.partial(_combine_kernel, alpha=alpha, tm=tm, sub=sub, nsteps=nsteps),
        out_shape=(jax.ShapeDtypeStruct((tokens, d_model), _F32),
                   jax.ShapeDtypeStruct((tokens, d_model), _BF16)),
        grid=(nsteps,),
        in_specs=[cur(), cur(), nxt(), nxt(), pl.BlockSpec(memory_space=pl.ANY),
                  pl.BlockSpec((tm, _LANES), row), pl.BlockSpec((tm, d_model), row),
                  pl.BlockSpec((1, d_model), const2), pl.BlockSpec((1, d_model), const2)],
        out_specs=(pl.BlockSpec((tm, d_model), row), pl.BlockSpec((tm, d_model), row)),
        scratch_shapes=[pltpu.VMEM((2, 2, tm * sub, _LANES), _F32), pltpu.SemaphoreType.DMA((2,))],
        compiler_params=_params(1),
        name="moe_combine_ln",
    )(d0, d1, d0, d1, ys, route, x, ln_g.reshape(1, -1), ln_b.reshape(1, -1))


def _routing_plan(route, *, tm):
    tokens = route.shape[0]
    ntiles = (2 * tokens) // tm + _N_EXPERTS
    choice = route[:, 0:2].T.astype(jnp.int32)
    experts = jnp.arange(_N_EXPERTS, dtype=jnp.int32)[:, None]
    onehot = [(choice[k][None, :] == experts).astype(jnp.int32) for k in range(2)]
    csum = [jnp.cumsum(oh, axis=1) for oh in onehot]
    first_counts = csum[0][:, -1]
    counts = first_counts + csum[1][:, -1]
    padded = ((counts + tm - 1) // tm) * tm
    ends = jnp.cumsum(padded)
    starts = ends - padded
    base = [starts, starts + first_counts]
    dest = [jnp.sum(onehot[k] * (base[k][:, None] + csum[k] - 1), axis=0) for k in range(2)]
    tile_start = jnp.arange(ntiles, dtype=jnp.int32) * tm
    te = jnp.minimum(jnp.sum((ends[None, :] <= tile_start[:, None]).astype(jnp.int32), axis=1),
                     _N_EXPERTS - 1)
    nused = (ends[-1] // tm).reshape(1)
    tail = ends[-1] + jnp.arange(_N_EXPERTS, dtype=jnp.int32) * tm
    zstart = jnp.concatenate([jnp.maximum(ends - tm, 0), jnp.minimum(tail, (ntiles - 1) * tm)])
    zvalid = jnp.concatenate([padded > 0, tail < ntiles * tm]).astype(jnp.int32)
    return ntiles * tm, te, nused, dest[0], dest[1], zstart, zvalid


def _moe_ffn_ln(x, x_tok, route, w_up, w_down, ln_g, ln_b, *, alpha, tm):
    tokens = x.shape[0]
    tt = min(_TM, tokens)
    tc = min(_TM // 2, tokens)
    rows, te, nused, d0, d1, zstart, zvalid = _routing_plan(route, tm=tm)
    xs = _dispatch_rows(x_tok, d0.reshape(tokens // tt, 1, tt), d1.reshape(tokens // tt, 1, tt),
                        zstart, zvalid, tokens=tokens, rows=rows, tt=tt, tm=tm)
    h = _swiglu_up(xs, w_up, te, nused, tm=tm, tn=w_up.shape[2] // 4)
    ys = _expert_down(h, w_down, te, nused, tm=tm)
    return _combine_ln(ys, d0.reshape(tokens // tc, 1, tc), d1.reshape(tokens // tc, 1, tc),
                       route, x, ln_g, ln_b, alpha=alpha, tm=tc)


def _mixer_ln(x, x16, w_in, fox_b_f, gate_b, gm_w_s, gm_b_s, gm_ln_g, gm_ln_b, w_branch, w_out,
              ln_g, ln_b, w_router, *, bsz, seq, alpha, tm):
    d_model = x.shape[1]
    n_r = 2 * _RET_HEADS * _RET_DK + 2 * _RET_HEADS * _RET_DV
    n_f = 3 * _FOX_HEADS * _FOX_DH
    n_fl = _FOX_HEADS
    w_r = w_in[:, :n_r].astype(_BF16)
    w_f = jnp.concatenate([w_in[:, n_r:n_r + n_f + n_fl],
                           jnp.zeros((d_model, _LANES - n_fl), w_in.dtype)], axis=1).astype(_BF16)
    w_c = w_in[:, n_r + n_f + n_fl:].astype(_BF16)

    bias = jnp.zeros((1, _LANES), _F32).at[0, :n_fl].set(fox_b_f)
    y_a, qa, ka, va = _mixer_ab(x16, w_r, w_f, bias, bsz, seq)
    y_b = _fox(qa, ka, va, bsz, seq)
    return _mixer_out(x16, y_a, y_b, w_c, gm_w_s, gm_b_s, gm_ln_g, gm_ln_b, gate_b, w_branch, w_out,
                      x, ln_g, ln_b, w_router, alpha=alpha, tm=tm)


def kernel(x, w_in, fox_b_f, gate_b, gm_w_s, gm_b_s, gm_ln_g, gm_ln_b, w_branch, w_out, ln_g, ln_b,
           dense_w_up, dense_w_down, moe_router, moe_w_up, moe_w_down):
    bsz, seq, d_model = x.shape
    depth = w_in.shape[0]
    alpha = (2 * depth) ** 0.25
    tokens = bsz * seq
    tm = min(_TM, tokens)
    xf = x.reshape(tokens, d_model)
    x16 = xf
    for l in range(depth):
        dense = l % 2 == 0
        xf, x16, route = _mixer_ln(
            xf, x16, w_in[l], fox_b_f[l], gate_b[l], gm_w_s[l], gm_b_s[l], gm_ln_g[l], gm_ln_b[l],
            w_branch[l], w_out[l], ln_g[l, 0], ln_b[l, 0], None if dense else moe_router[l // 2],
            bsz=bsz, seq=seq, alpha=alpha, tm=tm)
        if dense:
            xf, x16 = _dense_ffn_ln(xf, x16, dense_w_up[l // 2], dense_w_down[l // 2],
                                    ln_g[l, 1], ln_b[l, 1], alpha=alpha, tm=tm)
        else:
            xf, x16 = _moe_ffn_ln(xf, x16, route, moe_w_up[l // 2], moe_w_down[l // 2],
                                  ln_g[l, 1], ln_b[l, 1], alpha=alpha, tm=tm)
    return xf.reshape(bsz, seq, d_model)
```

```python
import functools
import math

import jax
import jax.numpy as jnp
import numpy as np
from jax import lax
from jax.experimental import pallas as pl
from jax.experimental.pallas import tpu as pltpu

_BF16 = jnp.bfloat16
_F32 = jnp.float32

_BLOCK = 128
_RET_HEADS, _RET_DK, _RET_DV = 4, 64, 128
_FOX_HEADS, _FOX_DH = 8, 64
_GM_GROUPS, _GM_DG = 4, 128
_N_EXPERTS = 8
_ROPE_BASE = 10000.0
_LN_EPS = 1e-5
_GN_EPS = 1e-6

_LANES = 128
_MXU_COLS = 256
_VMEM_PHYSICAL_BYTES = 64 * 1024 * 1024
_VMEM_LIMIT_BYTES = (_VMEM_PHYSICAL_BYTES * 7) // 8

_TM = 512
_TQ = 128
_TR_RET = 512

def _params(n_axes):
    return pltpu.CompilerParams(
        dimension_semantics=("arbitrary",) * n_axes,
        vmem_limit_bytes=_VMEM_LIMIT_BYTES,
    )


def _layer_norm(x, g, b, eps):
    mu = jnp.mean(x, axis=-1, keepdims=True)
    xc = x - mu
    var = jnp.mean(xc * xc, axis=-1, keepdims=True)
    return xc * lax.rsqrt(var + eps) * g + b


def _silu(x):
    return x * (1.0 / (1.0 + jnp.exp(-x)))


def _gelu_tanh(x):
    c = math.sqrt(2.0 / math.pi)
    return x * (0.5 * (1.0 + jnp.tanh(c * (x + 0.044715 * (x * x * x)))))


def _dot(a, b):
    return jnp.dot(a, b, preferred_element_type=_F32)


def _dot_nt(a, b):
    return lax.dot_general(a, b, (((1,), (1,)), ((), ())), preferred_element_type=_F32)


def _dot_tn(a, b):
    return lax.dot_general(a, b, (((0,), (0,)), ((), ())), preferred_element_type=_F32)


def _matmul_kernel(x_ref, w_ref, o_ref):
    o_ref[...] = _dot(x_ref[...], w_ref[...]).astype(o_ref.dtype)


def _matmul(x, w, *, tm, tn, out_dtype, name):
    m, k = x.shape
    n = w.shape[1]
    return pl.pallas_call(
        _matmul_kernel,
        out_shape=jax.ShapeDtypeStruct((m, n), out_dtype),
        grid=(n // tn, m // tm),
        in_specs=[pl.BlockSpec((tm, k), lambda j, i: (i, 0)),
                  pl.BlockSpec((k, tn), lambda j, i: (0, j))],
        out_specs=pl.BlockSpec((tm, tn), lambda j, i: (i, j)),
        compiler_params=_params(2),
        name=name,
    )(x, w)


def _retention_project(x16_ref, w_ref, v_sc, g_sc):
    width = _RET_HEADS * _RET_DK
    wv = _RET_HEADS * _RET_DV
    x16 = x16_ref[...].astype(_BF16)
    q = _dot(x16, w_ref[:, 0:width])
    k = _dot(x16, w_ref[:, width:2 * width])
    v_sc[...] = _dot(x16, w_ref[:, 2 * width:2 * width + wv]).astype(_BF16)
    g_sc[...] = _dot(x16, w_ref[:, 2 * width + wv:2 * width + 2 * wv])
    return q, k


def _retention_core(q_raw, k_raw, cos_ref, sin_ref, qdec_ref, kdec_ref, din_ref,
                    y_ref, v_sc, g_sc, state_sc, *, chunk_dec):
    width = _RET_HEADS * _RET_DK
    wv = _RET_HEADS * _RET_DV
    rows_total = q_raw.shape[0]
    lane_all = lax.broadcasted_iota(jnp.int32, (rows_total, width), 1)
    first_half = (lane_all % _RET_DK) < (_RET_DK // 2)

    def rotary(x):
        partner = jnp.where(first_half,
                            pltpu.roll(x, width - _RET_DK // 2, 1),
                            pltpu.roll(x, _RET_DK // 2, 1))
        return x * cos_ref[...] + partner * sin_ref[...]

    q_all = rotary(q_raw)
    k_all = rotary(k_raw) * (_RET_DK ** -0.5)
    lane = lax.broadcasted_iota(jnp.int32, (_BLOCK, width), 1)
    srow = lax.broadcasted_iota(jnp.int32, (width, wv), 0) // _RET_DK
    scol = lax.broadcasted_iota(jnp.int32, (width, wv), 1) // _RET_DV
    on_diag = srow == scol
    col_head = lax.broadcasted_iota(jnp.int32, (1, wv), 1) // _RET_DV
    decay_cols = jnp.zeros((1, wv), _F32)
    for h in range(_RET_HEADS):
        decay_cols = jnp.where(col_head == h, chunk_dec[h], decay_cols)
    for cc in range(rows_total // _BLOCK):
        rows = slice(cc * _BLOCK, (cc + 1) * _BLOCK)
        q = q_all[rows]
        k = k_all[rows]
        v = v_sc[rows, :]
        q_heads = jnp.concatenate(
            [jnp.where((lane // _RET_DK) == h, q, 0.0) for h in range(_RET_HEADS)], axis=0)
        scores = _dot_nt(q_heads.astype(_BF16), k.astype(_BF16))
        state = state_sc[...]
        cross = _dot((q * qdec_ref[...]).astype(_BF16), state.astype(_BF16))
        kv = _dot_tn((k * kdec_ref[...]).astype(_BF16), v)
        state_sc[...] = state * decay_cols + jnp.where(on_diag, kv, 0.0)
        for h in range(_RET_HEADS):
            cols = slice(h * _RET_DV, (h + 1) * _RET_DV)
            sh = scores[h * _BLOCK:(h + 1) * _BLOCK] * din_ref[h]
            out = _dot(sh.astype(_BF16), v[:, cols]) + cross[:, cols]
            mu = jnp.mean(out, axis=-1, keepdims=True)
            oc = out - mu
            var = jnp.mean(oc * oc, axis=-1, keepdims=True)
            normed = oc * lax.rsqrt(var + _GN_EPS)
            y_ref[rows, cols] = (_silu(g_sc[rows, cols]) * normed).astype(y_ref.dtype)


def _retention_tables(seq):
    half = _RET_DK // 2
    inv_freq = _ROPE_BASE ** (-jnp.arange(half, dtype=_F32) / half)
    ang = jnp.arange(seq, dtype=jnp.int32).astype(_F32)[:, None] * inv_freq[None, :]
    cos, sin = jnp.cos(ang), jnp.sin(ang)
    cos_t = jnp.tile(jnp.concatenate([cos, cos], axis=1), (1, _RET_HEADS))
    sin_t = jnp.tile(jnp.concatenate([-sin, sin], axis=1), (1, _RET_HEADS))
    log_gamma = jnp.log1p(-jnp.exp2(-5.0 - jnp.arange(_RET_HEADS, dtype=_F32)))
    idx = jnp.arange(_BLOCK, dtype=_F32)
    rel = idx[:, None] - idx[None, :]
    causal = rel >= 0
    decay_in = jnp.where(causal[None],
                         jnp.exp(log_gamma[:, None, None] * jnp.where(causal, rel, 0.0)[None]), 0.0)
    q_dec = jnp.exp(log_gamma[:, None] * (idx + 1.0))
    k_dec = jnp.exp(log_gamma[:, None] * (_BLOCK - 1.0 - idx))
    q_dec_t = jnp.repeat(q_dec.T, _RET_DK, axis=1)
    k_dec_t = jnp.repeat(k_dec.T, _RET_DK, axis=1)
    return cos_t, sin_t, q_dec_t, k_dec_t, decay_in


def _mixer_ab_kernel(x_ref, wr_ref, cos_ref, sin_ref, qdec_ref, kdec_ref, din_ref, wf_ref, b_ref, sel_ref,
                     y_ref, qa_ref, ka_ref, va_ref,
                     v_sc, g_sc, state_sc, fq_sc, fk_sc, fv_sc, fl_sc, carry_sc, *, chunk_dec, nblk):
    @pl.when(pl.program_id(1) == 0)
    def _():
        state_sc[...] = jnp.zeros_like(state_sc)
        carry_sc[...] = jnp.zeros_like(carry_sc)

    _fox_prep_body(x_ref, wf_ref, b_ref, sel_ref, qa_ref, ka_ref, va_ref, fq_sc, fk_sc, fv_sc, fl_sc,
                   carry_sc, nblk=nblk)
    q_raw, k_raw = _retention_project(x_ref, wr_ref, v_sc, g_sc)
    _retention_core(q_raw, k_raw, cos_ref, sin_ref, qdec_ref, kdec_ref, din_ref, y_ref, v_sc, g_sc,
                    state_sc, chunk_dec=chunk_dec)


def _mixer_ab(x16, w_r, w_f, bias, bsz, seq):
    tokens, d_model = x16.shape
    tr = min(_TR_RET, seq)
    nstep = seq // tr
    cos_t, sin_t, q_dec_t, k_dec_t, decay_in = _retention_tables(seq)
    chunk_dec = tuple(float((1.0 - 2.0 ** (-5.0 - h)) ** _BLOCK) for h in range(_RET_HEADS))
    qk = _RET_HEADS * _RET_DK
    wv = _RET_HEADS * _RET_DV
    fw = _FOX_HEADS * _FOX_DH
    sel = _fox_selectors()
    row = lambda b, c: (b * nstep + c, 0)
    const2 = lambda b, c: (0, 0)
    const3 = lambda b, c: (0, 0, 0)
    aug = jax.ShapeDtypeStruct((bsz, _FOX_HEADS, seq, _LANES), _BF16)
    aug_spec = pl.BlockSpec((None, _FOX_HEADS, tr, _LANES), lambda b, c: (b, 0, c, 0))
    return pl.pallas_call(
        functools.partial(_mixer_ab_kernel, chunk_dec=chunk_dec, nblk=tr // _BLOCK),
        out_shape=(jax.ShapeDtypeStruct((tokens, wv), _BF16), aug, aug, aug),
        grid=(bsz, nstep),
        in_specs=[
            pl.BlockSpec((tr, d_model), row),
            pl.BlockSpec(w_r.shape, const2),
            pl.BlockSpec((tr, qk), lambda b, c: (c, 0)),
            pl.BlockSpec((tr, qk), lambda b, c: (c, 0)),
            pl.BlockSpec((_BLOCK, qk), const2),
            pl.BlockSpec((_BLOCK, qk), const2),
            pl.BlockSpec((_RET_HEADS, _BLOCK, _BLOCK), const3),
            pl.BlockSpec(w_f.shape, const2),
            pl.BlockSpec((1, _LANES), const2),
            pl.BlockSpec(sel.shape, const3),
        ],
        out_specs=(pl.BlockSpec((tr, wv), row), aug_spec, aug_spec, aug_spec),
        scratch_shapes=[pltpu.VMEM((tr, wv), _BF16), pltpu.VMEM((tr, wv), _F32),
                        pltpu.VMEM((qk, wv), _F32),
                        pltpu.VMEM((tr, fw), _BF16), pltpu.VMEM((tr, fw), _BF16),
                        pltpu.VMEM((tr, fw), _BF16), pltpu.VMEM((tr, _LANES), _F32),
                        pltpu.VMEM((1, _LANES), _F32)],
        compiler_params=_params(2),
        name="mixer_ab",
    )(x16, w_r, cos_t, sin_t, q_dec_t, k_dec_t, decay_in, w_f, bias, sel)


_C_PARTS = 3
_TS_PREP = 512


def _fox_selectors():
    half = _LANES // 2
    one_row = _C_PARTS * _FOX_HEADS
    sel = np.zeros((_FOX_HEADS, _LANES, 2 * _LANES), np.float32)
    for h in range(_FOX_HEADS):
        base = half if h % 2 == 0 else 0
        for n in range(_C_PARTS):
            sel[h, n * _FOX_HEADS + h, base + n] = 1.0
            sel[h, one_row, base + _C_PARTS + n] = 1.0
            sel[h, one_row, _LANES + base + n] = 1.0
            sel[h, n * _FOX_HEADS + h, _LANES + base + _C_PARTS + n] = -1.0
    return jnp.asarray(sel, _BF16)


def _fox_prep_body(x16_ref, w_ref, b_ref, sel_ref, qa_ref, ka_ref, va_ref, fq_ref, fk_ref, fv_ref,
                   fl_ref, carry_sc, *, nblk):
    x16 = x16_ref[...].astype(_BF16)
    width = _FOX_HEADS * _FOX_DH
    for n, dst in enumerate((fq_ref, fk_ref, fv_ref)):
        dst[...] = _dot(x16, w_ref[:, n * width:(n + 1) * width]).astype(_BF16)
    fl_ref[...] = _dot(x16, w_ref[:, 3 * width:3 * width + _LANES])

    r = lax.broadcasted_iota(jnp.int32, (_BLOCK, _LANES), 0)
    lane = lax.broadcasted_iota(jnp.int32, (_BLOCK, _LANES), 1)
    tri = (r >= lane).astype(_BF16)
    half = _LANES // 2
    q_scale = jnp.asarray(_FOX_DH ** -0.5, _BF16)
    carry = carry_sc[...]
    packed = []
    for blk in range(nblk):
        rows = slice(blk * _BLOCK, (blk + 1) * _BLOCK)
        z = fl_ref[rows, :] + b_ref[...]
        log_f = jnp.minimum(z, 0.0) - jnp.log1p(jnp.exp(-jnp.abs(z)))
        pieces = []
        rest = log_f
        for _ in range(_C_PARTS):
            piece = rest.astype(_BF16)
            pieces.append(piece)
            rest = rest - piece.astype(_F32)
        sums = _dot(tri, jnp.concatenate(pieces, axis=1))
        cum = sums[:, (_C_PARTS - 1) * _LANES:]
        for n in reversed(range(_C_PARTS - 1)):
            cum = cum + sums[:, n * _LANES:(n + 1) * _LANES]
        cum = cum + carry
        carry = cum[_BLOCK - 1:_BLOCK, :]
        row = jnp.where(lane == _C_PARTS * _FOX_HEADS, 1.0, 0.0)
        rest = cum
        for n in range(_C_PARTS):
            piece = rest.astype(_BF16).astype(_F32)
            rest = rest - piece
            shifted = piece if n == 0 else pltpu.roll(piece, n * _FOX_HEADS, 1)
            row = jnp.where((lane >= n * _FOX_HEADS) & (lane < (n + 1) * _FOX_HEADS), shifted, row)
        packed.append(row.astype(_BF16))
    carry_sc[...] = carry
    packed = jnp.concatenate(packed, axis=0)
    lane_all = lax.broadcasted_iota(jnp.int32, (nblk * _BLOCK, _LANES), 1)
    for h in range(_FOX_HEADS):
        pair = slice((h // 2) * _LANES, (h // 2 + 1) * _LANES)
        in_head = (lane_all < half) if h % 2 == 0 else (lane_all >= half)
        extra = _dot(packed, sel_ref[h]).astype(_BF16)
        qa_ref[h] = jnp.where(in_head, fq_ref[:, pair] * q_scale, extra[:, :_LANES])
        ka_ref[h] = jnp.where(in_head, fk_ref[:, pair], extra[:, _LANES:])
        va_ref[h] = jnp.where(in_head, fv_ref[:, pair], jnp.ones((), _BF16))


def _fox_kernel(qa_ref, ka_ref, va_ref, o_ref, *, tq, nq):
    half = _LANES // 2
    low = lax.broadcasted_iota(jnp.int32, (tq, _LANES), 1) < half
    keep = (lax.broadcasted_iota(jnp.int32, (tq, tq), 0)
            >= lax.broadcasted_iota(jnp.int32, (tq, tq), 1))
    order = list(reversed(range(nq)))
    logits = {n: [_dot_nt(qa_ref[s, n * tq:(n + 1) * tq, :], ka_ref[s, 0:(n + 1) * tq, :])
                  for s in range(2)] for n in order}
    for n in order:
        rows = slice(n * tq, (n + 1) * tq)
        width = (n + 1) * tq
        pv = []
        for s in range(2):
            diag = jnp.where(keep, logits[n][s][:, width - tq:], -jnp.inf)
            lg = diag if n == 0 else jnp.concatenate([logits[n][s][:, :width - tq], diag], axis=1)
            pexp = jnp.exp(lg - jnp.max(lg, axis=1, keepdims=True))
            pv.append(_dot(pexp.astype(_BF16), va_ref[s, 0:width, :]))
        out = jnp.where(low, pv[0] / pv[0][:, half:half + 1], pv[1] / pv[1][:, 0:1])
        o_ref[rows, :] = out.astype(o_ref.dtype)


def _fox(qa, ka, va, bsz, seq):
    tq = min(_TQ, seq)
    nq = seq // tq
    npair = _FOX_HEADS // 2
    logit_bytes = 2 * tq * tq * (nq * (nq + 1) // 2) * 4
    assert logit_bytes <= _VMEM_LIMIT_BYTES // 2, "sequence too long for the single-pass attention step"
    return pl.pallas_call(
        functools.partial(_fox_kernel, tq=tq, nq=nq),
        out_shape=jax.ShapeDtypeStruct((bsz * seq, _FOX_HEADS * _FOX_DH), _BF16),
        grid=(bsz, npair),
        in_specs=[
            pl.BlockSpec((None, 2, seq, _LANES), lambda b, p: (b, p, 0, 0)),
            pl.BlockSpec((None, 2, seq, _LANES), lambda b, p: (b, p, 0, 0)),
            pl.BlockSpec((None, 2, seq, _LANES), lambda b, p: (b, p, 0, 0)),
        ],
        out_specs=pl.BlockSpec((seq, _LANES), lambda b, p: (b, p)),
        compiler_params=_params(2),
        name="fox_attention",
    )(qa, ka, va)


def _gmlp_tile(gu, gv, w_ref, bst_ref, lng_ref, lnb_ref, yc_sc):
    u = _gelu_tanh(gu)
    v = _layer_norm(_gelu_tanh(gv), lng_ref[...], lnb_ref[...], _LN_EPS).astype(_BF16)
    r = lax.broadcasted_iota(jnp.int32, (_BLOCK, _BLOCK), 0)
    c = lax.broadcasted_iota(jnp.int32, (_BLOCK, _BLOCK), 1)
    causal = r >= c
    nchunk = gu.shape[0] // _BLOCK
    for g in range(_GM_GROUPS):
        cols = slice(g * _GM_DG, (g + 1) * _GM_DG)
        wg = jnp.where(causal, w_ref[g], 0.0).astype(_BF16)
        bias = bst_ref[:, g:g + 1]
        v_chunks = jnp.concatenate(
            [v[ch * _BLOCK:(ch + 1) * _BLOCK, cols] for ch in range(nchunk)], axis=1)
        sp = _dot(wg, v_chunks) + bias
        for ch in range(nchunk):
            rows = slice(ch * _BLOCK, (ch + 1) * _BLOCK)
            yc_sc[rows, cols] = (u[rows, cols] * sp[:, ch * _GM_DG:(ch + 1) * _GM_DG]).astype(yc_sc.dtype)


def _mixer_out_kernel(x16_ref, ya_ref, yb_ref, wc_ref, gmw_ref, bst_ref, gmg_ref, gmb_ref, gbias_ref,
                      wbr_ref, wout_ref, x_ref, lng_ref, lnb_ref, *rest, alpha, d_model, with_router):
    if with_router:
        wr_ref, xo_ref, xtok_ref, route_ref, yc_sc = rest
    else:
        xo_ref, xo16_ref, yc_sc = rest
    gw = _GM_GROUPS * _GM_DG
    sub = d_model // _LANES

    def front(rows):
        x16 = x16_ref[rows, :].astype(_BF16)
        gu = _dot(x16, wc_ref[:, 0:gw])
        gv = _dot(x16, wc_ref[:, gw:2 * gw])

        def gate(n):
            cols = slice(2 * gw + n * d_model, 2 * gw + (n + 1) * d_model)
            z = _dot(x16, wc_ref[:, cols]) + gbias_ref[:, n * d_model:(n + 1) * d_model]
            return 1.0 / (1.0 + jnp.exp(-z))

        merged = (gate(0) * _dot(ya_ref[rows, :], wbr_ref[0])
                  + gate(1) * _dot(yb_ref[rows, :], wbr_ref[1]))
        return gu, gv, gate(2), merged

    def back(rows, gu, gv, gate_c, merged):
        yc = yc_sc.at[rows, :]
        _gmlp_tile(gu, gv, gmw_ref, bst_ref, gmg_ref, gmb_ref, yc)
        merged = merged + gate_c * _dot(yc[...], wbr_ref[2])
        h = _dot(merged.astype(_BF16), wout_ref[...])
        out = _layer_norm(alpha * x_ref[rows, :] + h, lng_ref[...], lnb_ref[...], _LN_EPS)
        xo_ref[rows, :] = out
        if with_router:
            _store_token_rows(xtok_ref.at[pl.ds(rows.start * sub, (rows.stop - rows.start) * sub), :], out)
            route_ref[rows, :] = _route_top2(out, wr_ref)
        else:
            xo16_ref[rows, :] = out.astype(_BF16)

    rows = slice(0, x16_ref.shape[0])
    back(rows, *front(rows))


def _mixer_out(x16, y_a, y_b, w_c, gm_w_s, gm_b_s, gm_ln_g, gm_ln_b, gate_b, w_branch, w_out, x,
               ln_g, ln_b, w_router, *, alpha, tm):
    tokens, d_model = x.shape
    bw = y_a.shape[1]
    gw = _GM_GROUPS * _GM_DG
    row = lambda i: (i, 0)
    const2 = lambda i: (0, 0)
    const3 = lambda i: (0, 0, 0)
    with_router = w_router is not None
    in_specs = [
        pl.BlockSpec((tm, d_model), row),
        pl.BlockSpec((tm, bw), row), pl.BlockSpec((tm, bw), row),
        pl.BlockSpec(w_c.shape, const2),
        pl.BlockSpec((_GM_GROUPS, _BLOCK, _BLOCK), const3),
        pl.BlockSpec((_BLOCK, _GM_GROUPS), const2),
        pl.BlockSpec((1, gw), const2), pl.BlockSpec((1, gw), const2),
        pl.BlockSpec((1, 3 * d_model), const2),
        pl.BlockSpec((3, bw, d_model), const3),
        pl.BlockSpec((d_model, d_model), const2),
        pl.BlockSpec((tm, d_model), row),
        pl.BlockSpec((1, d_model), const2), pl.BlockSpec((1, d_model), const2),
    ]
    args = [x16, y_a, y_b, w_c, gm_w_s, gm_b_s.T, gm_ln_g.reshape(1, gw), gm_ln_b.reshape(1, gw),
            gate_b.reshape(1, -1), w_branch.astype(_BF16), w_out.astype(_BF16), x,
            ln_g.reshape(1, -1), ln_b.reshape(1, -1)]
    out_shape = [jax.ShapeDtypeStruct((tokens, d_model), _F32)]
    out_specs = [pl.BlockSpec((tm, d_model), row)]
    if with_router:
        sub = d_model // _LANES
        in_specs.append(pl.BlockSpec((d_model, 2 * _LANES), const2))
        args.append(_router_weights(w_router))
        out_shape += [jax.ShapeDtypeStruct((tokens * sub, _LANES), _F32),
                      jax.ShapeDtypeStruct((tokens, _LANES), _F32)]
        out_specs += [pl.BlockSpec((tm * sub, _LANES), row), pl.BlockSpec((tm, _LANES), row)]
    else:
        out_shape.append(jax.ShapeDtypeStruct((tokens, d_model), _BF16))
        out_specs.append(pl.BlockSpec((tm, d_model), row))
    outs = pl.pallas_call(
        functools.partial(_mixer_out_kernel, alpha=alpha, d_model=d_model, with_router=with_router),
        out_shape=tuple(out_shape),
        grid=(tokens // tm,),
        in_specs=in_specs,
        out_specs=tuple(out_specs),
        scratch_shapes=[pltpu.VMEM((tm, gw), _BF16)],
        compiler_params=_params(1),
        name="mixer_out",
    )(*args)
    return outs if with_router else (*outs, None)


def _tile_groups(te):
    change = te[1:] != te[:-1]
    one = jnp.ones((1,), jnp.bool_)
    first = jnp.concatenate([one, change]).astype(jnp.int32)
    last = jnp.concatenate([change, one]).astype(jnp.int32)
    later = jnp.where(te[None, :] > te[:, None], te[None, :], _N_EXPERTS)
    nexte = jnp.min(later, axis=1)
    wrap = (nexte == _N_EXPERTS).astype(jnp.int32)
    return first, last, jnp.where(wrap == 1, te[0], nexte), wrap


def _load_token_rows(ref, n):
    s = ref.shape[0] // n
    return jnp.concatenate([ref[pl.ds(c, n, stride=s), :] for c in range(s)], axis=1)


def _store_token_rows(ref, x):
    n = x.shape[0]
    s = x.shape[1] // _LANES
    for c in range(s):
        ref[pl.ds(c, n, stride=s), :] = x[:, c * _LANES:(c + 1) * _LANES]


def _up_kernel(te_ref, nused_ref, first_ref, last_ref, nexte_ref, wrap_ref, x_ref, w_hbm, h_ref,
               stage_a, stage_b, wa_sc, wb_sc, sem, slot_sc, *, tn, nb, d_ff):
    j = pl.program_id(0)
    i = pl.program_id(1)

    def weight_copies(e, jj):
        col = pl.multiple_of(jj * tn, tn)
        return (pltpu.make_async_copy(w_hbm.at[e, :, pl.ds(col, tn)], stage_a, sem.at[0]),
                pltpu.make_async_copy(w_hbm.at[e, :, pl.ds(d_ff + col, tn)], stage_b, sem.at[1]))

    def round_into(slot):
        wa_sc[slot] = stage_a[...].astype(_BF16)
        wb_sc[slot] = stage_b[...].astype(_BF16)

    @pl.when((j == 0) & (i == 0))
    def _():
        for c in weight_copies(te_ref[0], 0):
            c.start()
        for c in weight_copies(te_ref[0], 0):
            c.wait()
        round_into(0)
        slot_sc[0] = 0

    has_next = (wrap_ref[i] == 0) | (j + 1 < nb)
    next_j = jnp.where(wrap_ref[i] == 1, j + 1, j)

    @pl.when((first_ref[i] == 1) & has_next)
    def _():
        for c in weight_copies(nexte_ref[i], next_j):
            c.start()

    slot = slot_sc[0]

    @pl.when(i < nused_ref[0])
    def _():
        x = _load_token_rows(x_ref, h_ref.shape[0]).astype(_BF16)
        a = _dot(x, wa_sc[slot])
        b = _dot(x, wb_sc[slot])
        h_ref[...] = (_silu(a) * b).astype(h_ref.dtype)

    @pl.when(i >= nused_ref[0])
    def _():
        h_ref[...] = jnp.zeros_like(h_ref)

    @pl.when((last_ref[i] == 1) & has_next)
    def _():
        for c in weight_copies(nexte_ref[i], next_j):
            c.wait()
        round_into(1 - slot)
        slot_sc[0] = 1 - slot


def _swiglu_up(xs, w_up, te, nused, *, tm, tn):
    k = w_up.shape[1]
    sub = k // _LANES
    rows = xs.shape[0] // sub
    d_ff = w_up.shape[2] // 2
    nb = d_ff // tn
    idx = lambda j, i, *prefetch: (i, 0)
    grid_spec = pltpu.PrefetchScalarGridSpec(
        num_scalar_prefetch=6,
        grid=(nb, rows // tm),
        in_specs=[pl.BlockSpec((tm * sub, _LANES), idx), pl.BlockSpec(memory_space=pl.ANY)],
        out_specs=pl.BlockSpec((tm, tn), lambda j, i, *prefetch: (i, j)),
        scratch_shapes=[pltpu.VMEM((k, tn), _F32), pltpu.VMEM((k, tn), _F32),
                        pltpu.VMEM((2, k, tn), _BF16), pltpu.VMEM((2, k, tn), _BF16),
                        pltpu.SemaphoreType.DMA((2,)), pltpu.SMEM((1,), jnp.int32)],
    )
    return pl.pallas_call(
        functools.partial(_up_kernel, tn=tn, nb=nb, d_ff=d_ff),
        out_shape=jax.ShapeDtypeStruct((rows, d_ff), _BF16),
        grid_spec=grid_spec,
        compiler_params=_params(2),
        name="swiglu_up",
    )(te, nused, *_tile_groups(te), xs, w_up)


def _down_kernel(te_ref, nused_ref, first_ref, last_ref, nexte_ref, wrap_ref, h_ref, w_hbm, y_ref,
                 stage, w_sc, sem, slot_sc):
    i = pl.program_id(0)

    def weight_copy(e):
        return pltpu.make_async_copy(w_hbm.at[e], stage, sem)

    @pl.when(i == 0)
    def _():
        weight_copy(te_ref[0]).start()
        weight_copy(te_ref[0]).wait()
        w_sc[0] = stage[...].astype(_BF16)
        slot_sc[0] = 0

    has_next = wrap_ref[i] == 0

    @pl.when((first_ref[i] == 1) & has_next)
    def _():
        weight_copy(nexte_ref[i]).start()

    slot = slot_sc[0]

    @pl.when(i < nused_ref[0])
    def _():
        _store_token_rows(y_ref, _dot(h_ref[...], w_sc[slot]))

    @pl.when(i >= nused_ref[0])
    def _():
        y_ref[...] = jnp.zeros_like(y_ref)

    @pl.when((last_ref[i] == 1) & has_next)
    def _():
        weight_copy(nexte_ref[i]).wait()
        w_sc[1 - slot] = stage[...].astype(_BF16)
        slot_sc[0] = 1 - slot


def _expert_down(h16, w_down, te, nused, *, tm):
    rows, k = h16.shape
    n = w_down.shape[2]
    sub = n // _LANES
    grid_spec = pltpu.PrefetchScalarGridSpec(
        num_scalar_prefetch=6,
        grid=(rows // tm,),
        in_specs=[pl.BlockSpec((tm, k), lambda i, *prefetch: (i, 0)),
                  pl.BlockSpec(memory_space=pl.ANY)],
        out_specs=pl.BlockSpec((tm * sub, _LANES), lambda i, *prefetch: (i, 0)),
        scratch_shapes=[pltpu.VMEM((k, n), _F32), pltpu.VMEM((2, k, n), _BF16),
                        pltpu.SemaphoreType.DMA, pltpu.SMEM((1,), jnp.int32)],
    )
    return pl.pallas_call(
        _down_kernel,
        out_shape=jax.ShapeDtypeStruct((rows * sub, _LANES), _F32),
        grid_spec=grid_spec,
        compiler_params=_params(1),
        name="expert_down",
    )(te, nused, *_tile_groups(te), h16, w_down)


def _dense_ffn_kernel(x16_ref, wup_ref, wdn_ref, x_ref, lng_ref, lnb_ref, xo_ref, xo16_ref,
                      *, alpha, d_ff, chunks):
    x16 = x16_ref[...]
    f = None
    for c0, c1 in chunks:
        a = _dot(x16, wup_ref[:, c0:c1])
        b = _dot(x16, wup_ref[:, d_ff + c0:d_ff + c1])
        part = _dot((_silu(a) * b).astype(_BF16), wdn_ref[c0:c1, :])
        f = part if f is None else f + part
    out = _layer_norm(alpha * x_ref[...] + f, lng_ref[...], lnb_ref[...], _LN_EPS)
    xo_ref[...] = out
    xo16_ref[...] = out.astype(_BF16)


def _dense_ffn_ln(x, x16, w_up, w_down, ln_g, ln_b, *, alpha, tm):
    tokens, d_model = x.shape
    d_ff = w_down.shape[0]
    split = (d_ff // (2 * _MXU_COLS)) * _MXU_COLS
    chunks = ((0, split), (split, d_ff)) if 0 < split < d_ff else ((0, d_ff),)
    row = lambda i: (i, 0)
    const2 = lambda i: (0, 0)
    return pl.pallas_call(
        functools.partial(_dense_ffn_kernel, alpha=alpha, d_ff=d_ff, chunks=chunks),
        out_shape=(jax.ShapeDtypeStruct((tokens, d_model), _F32),
                   jax.ShapeDtypeStruct((tokens, d_model), _BF16)),
        grid=(tokens // tm,),
        in_specs=[pl.BlockSpec((tm, d_model), row),
                  pl.BlockSpec((d_model, 2 * d_ff), const2),
                  pl.BlockSpec((d_ff, d_model), const2),
                  pl.BlockSpec((tm, d_model), row),
                  pl.BlockSpec((1, d_model), const2), pl.BlockSpec((1, d_model), const2)],
        out_specs=(pl.BlockSpec((tm, d_model), row), pl.BlockSpec((tm, d_model), row)),
        compiler_params=_params(1),
        name="dense_ffn_ln",
    )(x16, w_up.astype(_BF16), w_down.astype(_BF16), x, ln_g.reshape(1, -1), ln_b.reshape(1, -1))


def _route_top2(x, wr_ref):
    x_hi = x.astype(_BF16)
    x_lo = (x - x_hi.astype(_F32)).astype(_BF16)
    both = _dot(x_hi, wr_ref[...])
    logits = both[:, :_LANES] + (both[:, _LANES:] + _dot(x_lo, wr_ref[:, :_LANES]))
    lane = lax.broadcasted_iota(jnp.int32, logits.shape, 1).astype(_F32)
    lg = jnp.where(lane < _N_EXPERTS, logits, -jnp.inf)
    m1 = jnp.max(lg, axis=1, keepdims=True)
    i1 = jnp.min(jnp.where(lg == m1, lane, float(_LANES)), axis=1, keepdims=True)
    lg2 = jnp.where(lane == i1, -jnp.inf, lg)
    m2 = jnp.max(lg2, axis=1, keepdims=True)
    i2 = jnp.min(jnp.where(lg2 == m2, lane, float(_LANES)), axis=1, keepdims=True)
    e2 = jnp.exp(m2 - m1)
    den = 1.0 + e2
    w1 = 1.0 / den
    w2 = e2 / den
    return jnp.where(lane == 0, i1, jnp.where(lane == 1, i2,
                     jnp.where(lane == 2, w1, jnp.where(lane == 3, w2, 0.0))))


def _router_weights(w_router):
    d_model = w_router.shape[0]
    wr = jnp.zeros((d_model, _LANES), _F32).at[:, :_N_EXPERTS].set(w_router)
    wr_hi = wr.astype(_BF16)
    wr_lo = (wr - wr_hi.astype(_F32)).astype(_BF16)
    return jnp.concatenate([wr_hi, wr_lo], axis=1)


_ROW_UNROLL = 8


def _dispatch_kernel(zstart_ref, zvalid_ref, d0_ref, d1_ref, x_ref, xs_hbm, zero_sc, stage_sc, sem, zsem,
                     *, tt, tm, sub, nsteps):
    @pl.when(pl.program_id(0) == 0)
    def _():
        zero_sc[...] = jnp.zeros_like(zero_sc)

        def zero_copy(n):
            start = pl.multiple_of(zstart_ref[n] * sub, tm * sub)
            return pltpu.make_async_copy(zero_sc, xs_hbm.at[pl.ds(start, tm * sub), :], zsem)

        for n in range(2 * _N_EXPERTS):
            @pl.when(zvalid_ref[n] > 0)
            def _():
                zero_copy(n).start()
        for n in range(2 * _N_EXPERTS):
            @pl.when(zvalid_ref[n] > 0)
            def _():
                zero_copy(n).wait()

    i = pl.program_id(0)
    slot = i % 2
    stage_sc[slot] = x_ref[...]

    def issue(g, carry):
        for u in range(_ROW_UNROLL):
            r = g * _ROW_UNROLL + u
            src = stage_sc.at[slot, pl.ds(pl.multiple_of(r * sub, sub), sub), :]
            for d_ref in (d0_ref, d1_ref):
                dst = xs_hbm.at[pl.ds(pl.multiple_of(d_ref[0, 0, r] * sub, sub), sub), :]
                pltpu.make_async_copy(src, dst, sem.at[slot]).start(priority=u % 2)
        return carry

    def drain(s):
        for _ in range(2):
            pltpu.make_async_copy(stage_sc.at[s], xs_hbm.at[pl.ds(0, tt * sub), :], sem.at[s]).wait()

    lax.fori_loop(0, tt // _ROW_UNROLL, issue, 0)

    @pl.when(i > 0)
    def _():
        drain(1 - slot)

    @pl.when(i == nsteps - 1)
    def _():
        drain(slot)


def _dispatch_rows(x_tok, d0, d1, zstart, zvalid, *, tokens, rows, tt, tm):
    sub = x_tok.shape[0] // tokens
    smem = lambda: pl.BlockSpec((1, 1, tt), lambda i, zs, zv: (i, 0, 0), memory_space=pltpu.SMEM)
    grid_spec = pltpu.PrefetchScalarGridSpec(
        num_scalar_prefetch=2,
        grid=(tokens // tt,),
        in_specs=[smem(), smem(), pl.BlockSpec((tt * sub, _LANES), lambda i, zs, zv: (i, 0))],
        out_specs=pl.BlockSpec(memory_space=pl.ANY),
        scratch_shapes=[pltpu.VMEM((tm * sub, _LANES), _F32), pltpu.VMEM((2, tt * sub, _LANES), _F32),
                        pltpu.SemaphoreType.DMA((2,)), pltpu.SemaphoreType.DMA],
    )
    return pl.pallas_call(
        functools.partial(_dispatch_kernel, tt=tt, tm=tm, sub=sub, nsteps=tokens // tt),
        out_shape=jax.ShapeDtypeStruct((rows * sub, _LANES), _F32),
        grid_spec=grid_spec,
        compiler_params=_params(1),
        name="moe_dispatch",
    )(zstart, zvalid, d0, d1, x_tok)


def _combine_kernel(d0_ref, d1_ref, n0_ref, n1_ref, y_hbm, route_ref, x_ref, lng_ref, lnb_ref,
                    xo_ref, xo16_ref, buf, sem, *, alpha, tm, sub, nsteps):
    i = pl.program_id(0)
    slot = i % 2

    def gather(dst_slot, a_ref, b_ref):
        def issue(g, carry):
            for u in range(_ROW_UNROLL):
                r = g * _ROW_UNROLL + u
                for s, d_ref in enumerate((a_ref, b_ref)):
                    src = y_hbm.at[pl.ds(pl.multiple_of(d_ref[0, 0, r] * sub, sub), sub), :]
                    dst = buf.at[dst_slot, s, pl.ds(pl.multiple_of(r * sub, sub), sub), :]
                    pltpu.make_async_copy(src, dst, sem.at[dst_slot]).start(priority=u % 2)
            return carry

        lax.fori_loop(0, tm // _ROW_UNROLL, issue, 0)

    @pl.when(i == 0)
    def _():
        gather(0, d0_ref, d1_ref)

    @pl.when(i + 1 < nsteps)
    def _():
        gather(1 - slot, n0_ref, n1_ref)

    for s in range(2):
        pltpu.make_async_copy(y_hbm.at[pl.ds(0, tm * sub), :], buf.at[slot, s], sem.at[slot]).wait()
    route = route_ref[...]
    f = (route[:, 2:3] * _load_token_rows(buf.at[slot, 0], tm)
         + route[:, 3:4] * _load_token_rows(buf.at[slot, 1], tm))
    out = _layer_norm(alpha * x_ref[...] + f, lng_ref[...], lnb_ref[...], _LN_EPS)
    xo_ref[...] = out
    xo16_ref[...] = out.astype(_BF16)


def _combine_ln(ys, d0, d1, route, x, ln_g, ln_b, *, alpha, tm):
    tokens, d_model = x.shape
    sub = d_model // _LANES
    nsteps = tokens // tm
    row = lambda i: (i, 0)
    const2 = lambda i: (0, 0)
    cur = lambda: pl.BlockSpec((1, 1, tm), lambda i: (i, 0, 0), memory_space=pltpu.SMEM)
    nxt = lambda: pl.BlockSpec((1, 1, tm), lambda i: (jnp.minimum(i + 1, nsteps - 1), 0, 0),
                               memory_space=pltpu.SMEM)
    return pl.pallas_call(
        functools.partial(_combine_kernel, alpha=alpha, tm=tm, sub=sub, nsteps=nsteps),
        out_shape=(jax.ShapeDtypeStruct((tokens, d_model), _F32),
                   jax.ShapeDtypeStruct((tokens, d_model), _BF16)),
        grid=(nsteps,),
        in_specs=[cur(), cur(), nxt(), nxt(), pl.BlockSpec(memory_space=pl.ANY),
                  pl.BlockSpec((tm, _LANES), row), pl.BlockSpec((tm, d_model), row),
                  pl.BlockSpec((1, d_model), const2), pl.BlockSpec((1, d_model), const2)],
        out_specs=(pl.BlockSpec((tm, d_model), row), pl.BlockSpec((tm, d_model), row)),
        scratch_shapes=[pltpu.VMEM((2, 2, tm * sub, _LANES), _F32), pltpu.SemaphoreType.DMA((2,))],
        compiler_params=_params(1),
        name="moe_combine_ln",
    )(d0, d1, d0, d1, ys, route, x, ln_g.reshape(1, -1), ln_b.reshape(1, -1))


def _routing_plan(route, *, tm):
    tokens = route.shape[0]
    ntiles = (2 * tokens) // tm + _N_EXPERTS
    choice = route[:, 0:2].T.astype(jnp.int32)
    experts = jnp.arange(_N_EXPERTS, dtype=jnp.int32)[:, None]
    onehot = [(choice[k][None, :] == experts).astype(jnp.int32) for k in range(2)]
    csum = [jnp.cumsum(oh, axis=1) for oh in onehot]
    first_counts = csum[0][:, -1]
    counts = first_counts + csum[1][:, -1]
    padded = ((counts + tm - 1) // tm) * tm
    ends = jnp.cumsum(padded)
    starts = ends - padded
    base = [starts, starts + first_counts]
    dest = [jnp.sum(onehot[k] * (base[k][:, None] + csum[k] - 1), axis=0) for k in range(2)]
    tile_start = jnp.arange(ntiles, dtype=jnp.int32) * tm
    te = jnp.minimum(jnp.sum((ends[None, :] <= tile_start[:, None]).astype(jnp.int32), axis=1),
                     _N_EXPERTS - 1)
    nused = (ends[-1] // tm).reshape(1)
    tail = ends[-1] + jnp.arange(_N_EXPERTS, dtype=jnp.int32) * tm
    zstart = jnp.concatenate([jnp.maximum(ends - tm, 0), jnp.minimum(tail, (ntiles - 1) * tm)])
    zvalid = jnp.concatenate([padded > 0, tail < ntiles * tm]).astype(jnp.int32)
    return ntiles * tm, te, nused, dest[0], dest[1], zstart, zvalid


def _moe_ffn_ln(x, x_tok, route, w_up, w_down, ln_g, ln_b, *, alpha, tm):
    tokens = x.shape[0]
    tt = min(2 * _TM, tokens)
    tc = min(_TM // 2, tokens)
    rows, te, nused, d0, d1, zstart, zvalid = _routing_plan(route, tm=tm)
    xs = _dispatch_rows(x_tok, d0.reshape(tokens // tt, 1, tt), d1.reshape(tokens // tt, 1, tt),
                        zstart, zvalid, tokens=tokens, rows=rows, tt=tt, tm=tm)
    h = _swiglu_up(xs, w_up, te, nused, tm=tm, tn=w_up.shape[2] // 4)
    ys = _expert_down(h, w_down, te, nused, tm=tm)
    return _combine_ln(ys, d0.reshape(tokens // tc, 1, tc), d1.reshape(tokens // tc, 1, tc),
                       route, x, ln_g, ln_b, alpha=alpha, tm=tc)


def _mixer_ln(x, x16, w_in, fox_b_f, gate_b, gm_w_s, gm_b_s, gm_ln_g, gm_ln_b, w_branch, w_out,
              ln_g, ln_b, w_router, *, bsz, seq, alpha, tm):
    d_model = x.shape[1]
    n_r = 2 * _RET_HEADS * _RET_DK + 2 * _RET_HEADS * _RET_DV
    n_f = 3 * _FOX_HEADS * _FOX_DH
    n_fl = _FOX_HEADS
    w_r = w_in[:, :n_r].astype(_BF16)
    w_f = jnp.concatenate([w_in[:, n_r:n_r + n_f + n_fl],
                           jnp.zeros((d_model, _LANES - n_fl), w_in.dtype)], axis=1).astype(_BF16)
    w_c = w_in[:, n_r + n_f + n_fl:].astype(_BF16)

    bias = jnp.zeros((1, _LANES), _F32).at[0, :n_fl].set(fox_b_f)
    y_a, qa, ka, va = _mixer_ab(x16, w_r, w_f, bias, bsz, seq)
    y_b = _fox(qa, ka, va, bsz, seq)
    return _mixer_out(x16, y_a, y_b, w_c, gm_w_s, gm_b_s, gm_ln_g, gm_ln_b, gate_b, w_branch, w_out,
                      x, ln_g, ln_b, w_router, alpha=alpha, tm=tm)


def kernel(x, w_in, fox_b_f, gate_b, gm_w_s, gm_b_s, gm_ln_g, gm_ln_b, w_branch, w_out, ln_g, ln_b,
           dense_w_up, dense_w_down, moe_router, moe_w_up, moe_w_down):
    bsz, seq, d_model = x.shape
    depth = w_in.shape[0]
    alpha = (2 * depth) ** 0.25
    tokens = bsz * seq
    tm = min(_TM, tokens)
    xf = x.reshape(tokens, d_model)
    x16 = xf
    for l in range(depth):
        dense = l % 2 == 0
        xf, x16, route = _mixer_ln(
            xf, x16, w_in[l], fox_b_f[l], gate_b[l], gm_w_s[l], gm_b_s[l], gm_ln_g[l], gm_ln_b[l],
            w_branch[l], w_out[l], ln_g[l, 0], ln_b[l, 0], None if dense else moe_router[l // 2],
            bsz=bsz, seq=seq, alpha=alpha, tm=tm)
        if dense:
            xf, x16 = _dense_ffn_ln(xf, x16, dense_w_up[l // 2], dense_w_down[l // 2],
                                    ln_g[l, 1], ln_b[l, 1], alpha=alpha, tm=tm)
        else:
            xf, x16 = _moe_ffn_ln(xf, x16, route, moe_w_up[l // 2], moe_w_down[l // 2],
                                  ln_g[l, 1], ln_b[l, 1], alpha=alpha, tm=tm)
    return xf.reshape(bsz, seq, d_model)
```

```python
import functools
import math

import jax
import jax.numpy as jnp
import numpy as np
from jax import lax
from jax.experimental import pallas as pl
from jax.experimental.pallas import tpu as pltpu

_BF16 = jnp.bfloat16
_F32 = jnp.float32

_BLOCK = 128
_RET_HEADS, _RET_DK, _RET_DV = 4, 64, 128
_FOX_HEADS, _FOX_DH = 8, 64
_GM_GROUPS, _GM_DG = 4, 128
_N_EXPERTS = 8
_ROPE_BASE = 10000.0
_LN_EPS = 1e-5
_GN_EPS = 1e-6

_LANES = 128
_MXU_COLS = 256
_VMEM_PHYSICAL_BYTES = 64 * 1024 * 1024
_VMEM_LIMIT_BYTES = (_VMEM_PHYSICAL_BYTES * 7) // 8

_TM = 512
_TQ = 128
_TR_RET = 512

def _params(n_axes):
    return pltpu.CompilerParams(
        dimension_semantics=("arbitrary",) * n_axes,
        vmem_limit_bytes=_VMEM_LIMIT_BYTES,
    )


def _layer_norm(x, g, b, eps):
    mu = jnp.mean(x, axis=-1, keepdims=True)
    xc = x - mu
    var = jnp.mean(xc * xc, axis=-1, keepdims=True)
    return xc * lax.rsqrt(var + eps) * g + b


def _silu(x):
    return x * (1.0 / (1.0 + jnp.exp(-x)))


def _gelu_tanh(x):
    c = math.sqrt(2.0 / math.pi)
    return x * (0.5 * (1.0 + jnp.tanh(c * (x + 0.044715 * (x * x * x)))))


def _dot(a, b):
    return jnp.dot(a, b, preferred_element_type=_F32)


def _dot_nt(a, b):
    return lax.dot_general(a, b, (((1,), (1,)), ((), ())), preferred_element_type=_F32)


def _dot_tn(a, b):
    return lax.dot_general(a, b, (((0,), (0,)), ((), ())), preferred_element_type=_F32)


def _retention_project(x16_ref, w_ref, v_sc, g_sc):
    width = _RET_HEADS * _RET_DK
    wv = _RET_HEADS * _RET_DV
    x16 = x16_ref[...].astype(_BF16)
    q = _dot(x16, w_ref[:, 0:width])
    k = _dot(x16, w_ref[:, width:2 * width])
    v_sc[...] = _dot(x16, w_ref[:, 2 * width:2 * width + wv]).astype(_BF16)
    g_sc[...] = _dot(x16, w_ref[:, 2 * width + wv:2 * width + 2 * wv])
    return q, k


def _retention_core(q_raw, k_raw, cos_ref, sin_ref, qdec_ref, kdec_ref, din_ref,
                    y_ref, v_sc, g_sc, state_sc, *, chunk_dec):
    width = _RET_HEADS * _RET_DK
    wv = _RET_HEADS * _RET_DV
    rows_total = q_raw.shape[0]
    lane_all = lax.broadcasted_iota(jnp.int32, (rows_total, width), 1)
    first_half = (lane_all % _RET_DK) < (_RET_DK // 2)

    def rotary(x):
        partner = jnp.where(first_half,
                            pltpu.roll(x, width - _RET_DK // 2, 1),
                            pltpu.roll(x, _RET_DK // 2, 1))
        return x * cos_ref[...] + partner * sin_ref[...]

    q_all = rotary(q_raw)
    k_all = rotary(k_raw) * (_RET_DK ** -0.5)
    lane = lax.broadcasted_iota(jnp.int32, (_BLOCK, width), 1)
    srow = lax.broadcasted_iota(jnp.int32, (width, wv), 0) // _RET_DK
    scol = lax.broadcasted_iota(jnp.int32, (width, wv), 1) // _RET_DV
    on_diag = srow == scol
    col_head = lax.broadcasted_iota(jnp.int32, (1, wv), 1) // _RET_DV
    decay_cols = jnp.zeros((1, wv), _F32)
    for h in range(_RET_HEADS):
        decay_cols = jnp.where(col_head == h, chunk_dec[h], decay_cols)
    for cc in range(rows_total // _BLOCK):
        rows = slice(cc * _BLOCK, (cc + 1) * _BLOCK)
        q = q_all[rows]
        k = k_all[rows]
        v = v_sc[rows, :]
        q_heads = jnp.concatenate(
            [jnp.where((lane // _RET_DK) == h, q, 0.0) for h in range(_RET_HEADS)], axis=0)
        scores = _dot_nt(q_heads.astype(_BF16), k.astype(_BF16))
        state = state_sc[...]
        cross = _dot((q * qdec_ref[...]).astype(_BF16), state.astype(_BF16))
        kv = _dot_tn((k * kdec_ref[...]).astype(_BF16), v)
        state_sc[...] = state * decay_cols + jnp.where(on_diag, kv, 0.0)
        for h in range(_RET_HEADS):
            cols = slice(h * _RET_DV, (h + 1) * _RET_DV)
            sh = scores[h * _BLOCK:(h + 1) * _BLOCK] * din_ref[h]
            out = _dot(sh.astype(_BF16), v[:, cols]) + cross[:, cols]
            mu = jnp.mean(out, axis=-1, keepdims=True)
            oc = out - mu
            var = jnp.mean(oc * oc, axis=-1, keepdims=True)
            normed = oc * lax.rsqrt(var + _GN_EPS)
            y_ref[rows, cols] = (_silu(g_sc[rows, cols]) * normed).astype(y_ref.dtype)


def _retention_tables(seq):
    half = _RET_DK // 2
    inv_freq = _ROPE_BASE ** (-jnp.arange(half, dtype=_F32) / half)
    ang = jnp.arange(seq, dtype=jnp.int32).astype(_F32)[:, None] * inv_freq[None, :]
    cos, sin = jnp.cos(ang), jnp.sin(ang)
    cos_t = jnp.tile(jnp.concatenate([cos, cos], axis=1), (1, _RET_HEADS))
    sin_t = jnp.tile(jnp.concatenate([-sin, sin], axis=1), (1, _RET_HEADS))
    log_gamma = jnp.log1p(-jnp.exp2(-5.0 - jnp.arange(_RET_HEADS, dtype=_F32)))
    idx = jnp.arange(_BLOCK, dtype=_F32)
    rel = idx[:, None] - idx[None, :]
    causal = rel >= 0
    decay_in = jnp.where(causal[None],
                         jnp.exp(log_gamma[:, None, None] * jnp.where(causal, rel, 0.0)[None]), 0.0)
    q_dec = jnp.exp(log_gamma[:, None] * (idx + 1.0))
    k_dec = jnp.exp(log_gamma[:, None] * (_BLOCK - 1.0 - idx))
    q_dec_t = jnp.repeat(q_dec.T, _RET_DK, axis=1)
    k_dec_t = jnp.repeat(k_dec.T, _RET_DK, axis=1)
    return cos_t, sin_t, q_dec_t, k_dec_t, decay_in


def _mixer_ab_kernel(x_ref, wr_ref, cos_ref, sin_ref, qdec_ref, kdec_ref, din_ref, wf_ref, b_ref, sel_ref,
                     y_ref, qa_ref, ka_ref, va_ref,
                     v_sc, g_sc, state_sc, fq_sc, fk_sc, fv_sc, fl_sc, carry_sc, *, chunk_dec, nblk):
    @pl.when(pl.program_id(1) == 0)
    def _():
        state_sc[...] = jnp.zeros_like(state_sc)
        carry_sc[...] = jnp.zeros_like(carry_sc)

    _fox_prep_body(x_ref, wf_ref, b_ref, sel_ref, qa_ref, ka_ref, va_ref, fq_sc, fk_sc, fv_sc, fl_sc,
                   carry_sc, nblk=nblk)
    q_raw, k_raw = _retention_project(x_ref, wr_ref, v_sc, g_sc)
    _retention_core(q_raw, k_raw, cos_ref, sin_ref, qdec_ref, kdec_ref, din_ref, y_ref, v_sc, g_sc,
                    state_sc, chunk_dec=chunk_dec)


def _mixer_ab(x16, w_r, w_f, bias, bsz, seq):
    tokens, d_model = x16.shape
    tr = min(_TR_RET, seq)
    nstep = seq // tr
    cos_t, sin_t, q_dec_t, k_dec_t, decay_in = _retention_tables(seq)
    chunk_dec = tuple(float((1.0 - 2.0 ** (-5.0 - h)) ** _BLOCK) for h in range(_RET_HEADS))
    qk = _RET_HEADS * _RET_DK
    wv = _RET_HEADS * _RET_DV
    fw = _FOX_HEADS * _FOX_DH
    sel = _fox_selectors()
    row = lambda b, c: (b * nstep + c, 0)
    const2 = lambda b, c: (0, 0)
    const3 = lambda b, c: (0, 0, 0)
    aug = jax.ShapeDtypeStruct((bsz, _FOX_HEADS, seq, _LANES), _BF16)
    aug_spec = pl.BlockSpec((None, _FOX_HEADS, tr, _LANES), lambda b, c: (b, 0, c, 0))
    return pl.pallas_call(
        functools.partial(_mixer_ab_kernel, chunk_dec=chunk_dec, nblk=tr // _BLOCK),
        out_shape=(jax.ShapeDtypeStruct((tokens, wv), _BF16), aug, aug, aug),
        grid=(bsz, nstep),
        in_specs=[
            pl.BlockSpec((tr, d_model), row),
            pl.BlockSpec(w_r.shape, const2),
            pl.BlockSpec((tr, qk), lambda b, c: (c, 0)),
            pl.BlockSpec((tr, qk), lambda b, c: (c, 0)),
            pl.BlockSpec((_BLOCK, qk), const2),
            pl.BlockSpec((_BLOCK, qk), const2),
            pl.BlockSpec((_RET_HEADS, _BLOCK, _BLOCK), const3),
            pl.BlockSpec(w_f.shape, const2),
            pl.BlockSpec((1, _LANES), const2),
            pl.BlockSpec(sel.shape, const3),
        ],
        out_specs=(pl.BlockSpec((tr, wv), row), aug_spec, aug_spec, aug_spec),
        scratch_shapes=[pltpu.VMEM((tr, wv), _BF16), pltpu.VMEM((tr, wv), _F32),
                        pltpu.VMEM((qk, wv), _F32),
                        pltpu.VMEM((tr, fw), _BF16), pltpu.VMEM((tr, fw), _BF16),
                        pltpu.VMEM((tr, fw), _BF16), pltpu.VMEM((tr, _LANES), _F32),
                        pltpu.VMEM((1, _LANES), _F32)],
        compiler_params=_params(2),
        name="mixer_ab",
    )(x16, w_r, cos_t, sin_t, q_dec_t, k_dec_t, decay_in, w_f, bias, sel)


_C_PARTS = 3


def _fox_selectors():
    half = _LANES // 2
    one_row = _C_PARTS * _FOX_HEADS
    sel = np.zeros((_FOX_HEADS, _LANES, 2 * _LANES), np.float32)
    for h in range(_FOX_HEADS):
        base = half if h % 2 == 0 else 0
        for n in range(_C_PARTS):
            sel[h, n * _FOX_HEADS + h, base + n] = 1.0
            sel[h, one_row, base + _C_PARTS + n] = 1.0
            sel[h, one_row, _LANES + base + n] = 1.0
            sel[h, n * _FOX_HEADS + h, _LANES + base + _C_PARTS + n] = -1.0
    return jnp.asarray(sel, _BF16)


def _fox_prep_body(x16_ref, w_ref, b_ref, sel_ref, qa_ref, ka_ref, va_ref, fq_ref, fk_ref, fv_ref,
                   fl_ref, carry_sc, *, nblk):
    x16 = x16_ref[...].astype(_BF16)
    width = _FOX_HEADS * _FOX_DH
    for n, dst in enumerate((fq_ref, fk_ref, fv_ref)):
        dst[...] = _dot(x16, w_ref[:, n * width:(n + 1) * width]).astype(_BF16)
    fl_ref[...] = _dot(x16, w_ref[:, 3 * width:3 * width + _LANES])

    r = lax.broadcasted_iota(jnp.int32, (_BLOCK, _LANES), 0)
    lane = lax.broadcasted_iota(jnp.int32, (_BLOCK, _LANES), 1)
    tri = (r >= lane).astype(_BF16)
    half = _LANES // 2
    q_scale = jnp.asarray(_FOX_DH ** -0.5, _BF16)
    carry = carry_sc[...]
    packed = []
    for blk in range(nblk):
        rows = slice(blk * _BLOCK, (blk + 1) * _BLOCK)
        z = fl_ref[rows, :] + b_ref[...]
        log_f = jnp.minimum(z, 0.0) - jnp.log1p(jnp.exp(-jnp.abs(z)))
        pieces = []
        rest = log_f
        for _ in range(_C_PARTS):
            piece = rest.astype(_BF16)
            pieces.append(piece)
            rest = rest - piece.astype(_F32)
        sums = _dot(tri, jnp.concatenate(pieces, axis=1))
        cum = sums[:, (_C_PARTS - 1) * _LANES:]
        for n in reversed(range(_C_PARTS - 1)):
            cum = cum + sums[:, n * _LANES:(n + 1) * _LANES]
        cum = cum + carry
        carry = cum[_BLOCK - 1:_BLOCK, :]
        row = jnp.where(lane == _C_PARTS * _FOX_HEADS, 1.0, 0.0)
        rest = cum
        for n in range(_C_PARTS):
            piece = rest.astype(_BF16).astype(_F32)
            rest = rest - piece
            shifted = piece if n == 0 else pltpu.roll(piece, n * _FOX_HEADS, 1)
            row = jnp.where((lane >= n * _FOX_HEADS) & (lane < (n + 1) * _FOX_HEADS), shifted, row)
        packed.append(row.astype(_BF16))
    carry_sc[...] = carry
    packed = jnp.concatenate(packed, axis=0)
    lane_all = lax.broadcasted_iota(jnp.int32, (nblk * _BLOCK, _LANES), 1)
    for h in range(_FOX_HEADS):
        pair = slice((h // 2) * _LANES, (h // 2 + 1) * _LANES)
        in_head = (lane_all < half) if h % 2 == 0 else (lane_all >= half)
        extra = _dot(packed, sel_ref[h]).astype(_BF16)
        qa_ref[h] = jnp.where(in_head, fq_ref[:, pair] * q_scale, extra[:, :_LANES])
        ka_ref[h] = jnp.where(in_head, fk_ref[:, pair], extra[:, _LANES:])
        va_ref[h] = jnp.where(in_head, fv_ref[:, pair], jnp.ones((), _BF16))


def _fox_kernel(qa_ref, ka_ref, va_ref, o_ref, *, tq, nq):
    half = _LANES // 2
    low = lax.broadcasted_iota(jnp.int32, (tq, _LANES), 1) < half
    keep = (lax.broadcasted_iota(jnp.int32, (tq, tq), 0)
            >= lax.broadcasted_iota(jnp.int32, (tq, tq), 1))
    order = list(reversed(range(nq)))
    logits = {n: [_dot_nt(qa_ref[s, n * tq:(n + 1) * tq, :], ka_ref[s, 0:(n + 1) * tq, :])
                  for s in range(2)] for n in order}
    for n in order:
        rows = slice(n * tq, (n + 1) * tq)
        width = (n + 1) * tq
        pv = []
        for s in range(2):
            diag = jnp.where(keep, logits[n][s][:, width - tq:], -jnp.inf)
            lg = diag if n == 0 else jnp.concatenate([logits[n][s][:, :width - tq], diag], axis=1)
            pexp = jnp.exp(lg - jnp.max(lg, axis=1, keepdims=True))
            pv.append(_dot(pexp.astype(_BF16), va_ref[s, 0:width, :]))
        out = jnp.where(low, pv[0] / pv[0][:, half:half + 1], pv[1] / pv[1][:, 0:1])
        o_ref[rows, :] = out.astype(o_ref.dtype)


def _fox(qa, ka, va, bsz, seq):
    tq = min(_TQ, seq)
    nq = seq // tq
    npair = _FOX_HEADS // 2
    logit_bytes = 2 * tq * tq * (nq * (nq + 1) // 2) * 4
    assert logit_bytes <= _VMEM_LIMIT_BYTES // 2, "sequence too long for the single-pass attention step"
    return pl.pallas_call(
        functools.partial(_fox_kernel, tq=tq, nq=nq),
        out_shape=jax.ShapeDtypeStruct((bsz * seq, _FOX_HEADS * _FOX_DH), _BF16),
        grid=(bsz, npair),
        in_specs=[
            pl.BlockSpec((None, 2, seq, _LANES), lambda b, p: (b, p, 0, 0)),
            pl.BlockSpec((None, 2, seq, _LANES), lambda b, p: (b, p, 0, 0)),
            pl.BlockSpec((None, 2, seq, _LANES), lambda b, p: (b, p, 0, 0)),
        ],
        out_specs=pl.BlockSpec((seq, _LANES), lambda b, p: (b, p)),
        compiler_params=_params(2),
        name="fox_attention",
    )(qa, ka, va)


def _gmlp_tile(gu, gv, w_ref, bst_ref, lng_ref, lnb_ref, yc_sc):
    u = _gelu_tanh(gu)
    v = _layer_norm(_gelu_tanh(gv), lng_ref[...], lnb_ref[...], _LN_EPS).astype(_BF16)
    r = lax.broadcasted_iota(jnp.int32, (_BLOCK, _BLOCK), 0)
    c = lax.broadcasted_iota(jnp.int32, (_BLOCK, _BLOCK), 1)
    causal = r >= c
    nchunk = gu.shape[0] // _BLOCK
    for g in range(_GM_GROUPS):
        cols = slice(g * _GM_DG, (g + 1) * _GM_DG)
        wg = jnp.where(causal, w_ref[g], 0.0).astype(_BF16)
        bias = bst_ref[:, g:g + 1]
        v_chunks = jnp.concatenate(
            [v[ch * _BLOCK:(ch + 1) * _BLOCK, cols] for ch in range(nchunk)], axis=1)
        sp = _dot(wg, v_chunks) + bias
        for ch in range(nchunk):
            rows = slice(ch * _BLOCK, (ch + 1) * _BLOCK)
            yc_sc[rows, cols] = (u[rows, cols] * sp[:, ch * _GM_DG:(ch + 1) * _GM_DG]).astype(yc_sc.dtype)


def _mixer_out_kernel(x16_ref, ya_ref, yb_ref, wc_ref, gmw_ref, bst_ref, gmg_ref, gmb_ref, gbias_ref,
                      wbr_ref, wout_ref, x_ref, lng_ref, lnb_ref, *rest, alpha, d_model, with_router):
    if with_router:
        wr_ref, xo_ref, xtok_ref, route_ref, yc_sc = rest
    else:
        xo_ref, xo16_ref, yc_sc = rest
    gw = _GM_GROUPS * _GM_DG
    sub = d_model // _LANES

    def front(rows):
        x16 = x16_ref[rows, :].astype(_BF16)
        gu = _dot(x16, wc_ref[:, 0:gw])
        gv = _dot(x16, wc_ref[:, gw:2 * gw])

        def gate(n):
            cols = slice(2 * gw + n * d_model, 2 * gw + (n + 1) * d_model)
            z = _dot(x16, wc_ref[:, cols]) + gbias_ref[:, n * d_model:(n + 1) * d_model]
            return 1.0 / (1.0 + jnp.exp(-z))

        merged = (gate(0) * _dot(ya_ref[rows, :], wbr_ref[0])
                  + gate(1) * _dot(yb_ref[rows, :], wbr_ref[1]))
        return gu, gv, gate(2), merged

    def back(rows, gu, gv, gate_c, merged):
        yc = yc_sc.at[rows, :]
        _gmlp_tile(gu, gv, gmw_ref, bst_ref, gmg_ref, gmb_ref, yc)
        merged = merged + gate_c * _dot(yc[...], wbr_ref[2])
        h = _dot(merged.astype(_BF16), wout_ref[...])
        out = _layer_norm(alpha * x_ref[rows, :] + h, lng_ref[...], lnb_ref[...], _LN_EPS)
        xo_ref[rows, :] = out
        if with_router:
            _store_token_rows(xtok_ref.at[pl.ds(rows.start * sub, (rows.stop - rows.start) * sub), :], out)
            route_ref[rows, :] = _route_top2(out, wr_ref)
        else:
            xo16_ref[rows, :] = out.astype(_BF16)

    rows = slice(0, x16_ref.shape[0])
    back(rows, *front(rows))


def _mixer_out(x16, y_a, y_b, w_c, gm_w_s, gm_b_s, gm_ln_g, gm_ln_b, gate_b, w_branch, w_out, x,
               ln_g, ln_b, w_router, *, alpha, tm):
    tokens, d_model = x.shape
    bw = y_a.shape[1]
    gw = _GM_GROUPS * _GM_DG
    row = lambda i: (i, 0)
    const2 = lambda i: (0, 0)
    const3 = lambda i: (0, 0, 0)
    with_router = w_router is not None
    in_specs = [
        pl.BlockSpec((tm, d_model), row),
        pl.BlockSpec((tm, bw), row), pl.BlockSpec((tm, bw), row),
        pl.BlockSpec(w_c.shape, const2),
        pl.BlockSpec((_GM_GROUPS, _BLOCK, _BLOCK), const3),
        pl.BlockSpec((_BLOCK, _GM_GROUPS), const2),
        pl.BlockSpec((1, gw), const2), pl.BlockSpec((1, gw), const2),
        pl.BlockSpec((1, 3 * d_model), const2),
        pl.BlockSpec((3, bw, d_model), const3),
        pl.BlockSpec((d_model, d_model), const2),
        pl.BlockSpec((tm, d_model), row),
        pl.BlockSpec((1, d_model), const2), pl.BlockSpec((1, d_model), const2),
    ]
    args = [x16, y_a, y_b, w_c, gm_w_s, gm_b_s.T, gm_ln_g.reshape(1, gw), gm_ln_b.reshape(1, gw),
            gate_b.reshape(1, -1), w_branch.astype(_BF16), w_out.astype(_BF16), x,
            ln_g.reshape(1, -1), ln_b.reshape(1, -1)]
    out_shape = [jax.ShapeDtypeStruct((tokens, d_model), _F32)]
    out_specs = [pl.BlockSpec((tm, d_model), row)]
    if with_router:
        sub = d_model // _LANES
        in_specs.append(pl.BlockSpec((d_model, 2 * _LANES), const2))
        args.append(_router_weights(w_router))
        out_shape += [jax.ShapeDtypeStruct((tokens * sub, _LANES), _F32),
                      jax.ShapeDtypeStruct((tokens, _LANES), _F32)]
        out_specs += [pl.BlockSpec((tm * sub, _LANES), row), pl.BlockSpec((tm, _LANES), row)]
    else:
        out_shape.append(jax.ShapeDtypeStruct((tokens, d_model), _BF16))
        out_specs.append(pl.BlockSpec((tm, d_model), row))
    outs = pl.pallas_call(
        functools.partial(_mixer_out_kernel, alpha=alpha, d_model=d_model, with_router=with_router),
        out_shape=tuple(out_shape),
        grid=(tokens // tm,),
        in_specs=in_specs,
        out_specs=tuple(out_specs),
        scratch_shapes=[pltpu.VMEM((tm, gw), _BF16)],
        compiler_params=_params(1),
        name="mixer_out",
    )(*args)
    return outs if with_router else (*outs, None)


def _tile_groups(te):
    change = te[1:] != te[:-1]
    one = jnp.ones((1,), jnp.bool_)
    first = jnp.concatenate([one, change]).astype(jnp.int32)
    last = jnp.concatenate([change, one]).astype(jnp.int32)
    later = jnp.where(te[None, :] > te[:, None], te[None, :], _N_EXPERTS)
    nexte = jnp.min(later, axis=1)
    wrap = (nexte == _N_EXPERTS).astype(jnp.int32)
    return first, last, jnp.where(wrap == 1, te[0], nexte), wrap


def _load_token_rows(ref, n):
    s = ref.shape[0] // n
    return jnp.concatenate([ref[pl.ds(c, n, stride=s), :] for c in range(s)], axis=1)


def _store_token_rows(ref, x):
    n = x.shape[0]
    s = x.shape[1] // _LANES
    for c in range(s):
        ref[pl.ds(c, n, stride=s), :] = x[:, c * _LANES:(c + 1) * _LANES]


def _up_kernel(te_ref, nused_ref, first_ref, last_ref, nexte_ref, wrap_ref, x_ref, w_hbm, h_ref,
               stage_a, stage_b, wa_sc, wb_sc, sem, slot_sc, *, tn, nb, d_ff):
    j = pl.program_id(0)
    i = pl.program_id(1)

    def weight_copies(e, jj):
        col = pl.multiple_of(jj * tn, tn)
        return (pltpu.make_async_copy(w_hbm.at[e, :, pl.ds(col, tn)], stage_a, sem.at[0]),
                pltpu.make_async_copy(w_hbm.at[e, :, pl.ds(d_ff + col, tn)], stage_b, sem.at[1]))

    def round_into(slot):
        wa_sc[slot] = stage_a[...].astype(_BF16)
        wb_sc[slot] = stage_b[...].astype(_BF16)

    @pl.when((j == 0) & (i == 0))
    def _():
        for c in weight_copies(te_ref[0], 0):
            c.start()
        for c in weight_copies(te_ref[0], 0):
            c.wait()
        round_into(0)
        slot_sc[0] = 0

    has_next = (wrap_ref[i] == 0) | (j + 1 < nb)
    next_j = jnp.where(wrap_ref[i] == 1, j + 1, j)

    @pl.when((first_ref[i] == 1) & has_next)
    def _():
        for c in weight_copies(nexte_ref[i], next_j):
            c.start()

    slot = slot_sc[0]

    @pl.when(i < nused_ref[0])
    def _():
        x = _load_token_rows(x_ref, h_ref.shape[0]).astype(_BF16)
        a = _dot(x, wa_sc[slot])
        b = _dot(x, wb_sc[slot])
        h_ref[...] = (_silu(a) * b).astype(h_ref.dtype)

    @pl.when(i >= nused_ref[0])
    def _():
        h_ref[...] = jnp.zeros_like(h_ref)

    @pl.when((last_ref[i] == 1) & has_next)
    def _():
        for c in weight_copies(nexte_ref[i], next_j):
            c.wait()
        round_into(1 - slot)
        slot_sc[0] = 1 - slot


def _swiglu_up(xs, w_up, te, nused, *, tm, tn):
    k = w_up.shape[1]
    sub = k // _LANES
    rows = xs.shape[0] // sub
    d_ff = w_up.shape[2] // 2
    nb = d_ff // tn
    idx = lambda j, i, *prefetch: (i, 0)
    grid_spec = pltpu.PrefetchScalarGridSpec(
        num_scalar_prefetch=6,
        grid=(nb, rows // tm),
        in_specs=[pl.BlockSpec((tm * sub, _LANES), idx), pl.BlockSpec(memory_space=pl.ANY)],
        out_specs=pl.BlockSpec((tm, tn), lambda j, i, *prefetch: (i, j)),
        scratch_shapes=[pltpu.VMEM((k, tn), _F32), pltpu.VMEM((k, tn), _F32),
                        pltpu.VMEM((2, k, tn), _BF16), pltpu.VMEM((2, k, tn), _BF16),
                        pltpu.SemaphoreType.DMA((2,)), pltpu.SMEM((1,), jnp.int32)],
    )
    return pl.pallas_call(
        functools.partial(_up_kernel, tn=tn, nb=nb, d_ff=d_ff),
        out_shape=jax.ShapeDtypeStruct((rows, d_ff), _BF16),
        grid_spec=grid_spec,
        compiler_params=_params(2),
        name="swiglu_up",
    )(te, nused, *_tile_groups(te), xs, w_up)


def _down_kernel(te_ref, nused_ref, first_ref, last_ref, nexte_ref, wrap_ref, h_ref, w_hbm, y_ref,
                 stage, w_sc, sem, slot_sc):
    i = pl.program_id(0)

    def weight_copy(e):
        return pltpu.make_async_copy(w_hbm.at[e], stage, sem)

    @pl.when(i == 0)
    def _():
        weight_copy(te_ref[0]).start()
        weight_copy(te_ref[0]).wait()
        w_sc[0] = stage[...].astype(_BF16)
        slot_sc[0] = 0

    has_next = wrap_ref[i] == 0

    @pl.when((first_ref[i] == 1) & has_next)
    def _():
        weight_copy(nexte_ref[i]).start()

    slot = slot_sc[0]

    @pl.when(i < nused_ref[0])
    def _():
        _store_token_rows(y_ref, _dot(h_ref[...], w_sc[slot]))

    @pl.when(i >= nused_ref[0])
    def _():
        y_ref[...] = jnp.zeros_like(y_ref)

    @pl.when((last_ref[i] == 1) & has_next)
    def _():
        weight_copy(nexte_ref[i]).wait()
        w_sc[1 - slot] = stage[...].astype(_BF16)
        slot_sc[0] = 1 - slot


def _expert_down(h16, w_down, te, nused, *, tm):
    rows, k = h16.shape
    n = w_down.shape[2]
    sub = n // _LANES
    grid_spec = pltpu.PrefetchScalarGridSpec(
        num_scalar_prefetch=6,
        grid=(rows // tm,),
        in_specs=[pl.BlockSpec((tm, k), lambda i, *prefetch: (i, 0)),
                  pl.BlockSpec(memory_space=pl.ANY)],
        out_specs=pl.BlockSpec((tm * sub, _LANES), lambda i, *prefetch: (i, 0)),
        scratch_shapes=[pltpu.VMEM((k, n), _F32), pltpu.VMEM((2, k, n), _BF16),
                        pltpu.SemaphoreType.DMA, pltpu.SMEM((1,), jnp.int32)],
    )
    return pl.pallas_call(
        _down_kernel,
        out_shape=jax.ShapeDtypeStruct((rows * sub, _LANES), _F32),
        grid_spec=grid_spec,
        compiler_params=_params(1),
        name="expert_down",
    )(te, nused, *_tile_groups(te), h16, w_down)


def _dense_ffn_kernel(x16_ref, wup_ref, wdn_ref, x_ref, lng_ref, lnb_ref, xo_ref, xo16_ref,
                      *, alpha, d_ff, chunks):
    x16 = x16_ref[...]
    f = None
    for c0, c1 in chunks:
        a = _dot(x16, wup_ref[:, c0:c1])
        b = _dot(x16, wup_ref[:, d_ff + c0:d_ff + c1])
        part = _dot((_silu(a) * b).astype(_BF16), wdn_ref[c0:c1, :])
        f = part if f is None else f + part
    out = _layer_norm(alpha * x_ref[...] + f, lng_ref[...], lnb_ref[...], _LN_EPS)
    xo_ref[...] = out
    xo16_ref[...] = out.astype(_BF16)


def _dense_ffn_ln(x, x16, w_up, w_down, ln_g, ln_b, *, alpha, tm):
    tokens, d_model = x.shape
    d_ff = w_down.shape[0]
    split = (d_ff // (2 * _MXU_COLS)) * _MXU_COLS
    chunks = ((0, split), (split, d_ff)) if 0 < split < d_ff else ((0, d_ff),)
    row = lambda i: (i, 0)
    const2 = lambda i: (0, 0)
    return pl.pallas_call(
        functools.partial(_dense_ffn_kernel, alpha=alpha, d_ff=d_ff, chunks=chunks),
        out_shape=(jax.ShapeDtypeStruct((tokens, d_model), _F32),
                   jax.ShapeDtypeStruct((tokens, d_model), _BF16)),
        grid=(tokens // tm,),
        in_specs=[pl.BlockSpec((tm, d_model), row),
                  pl.BlockSpec((d_model, 2 * d_ff), const2),
                  pl.BlockSpec((d_ff, d_model), const2),
                  pl.BlockSpec((tm, d_model), row),
                  pl.BlockSpec((1, d_model), const2), pl.BlockSpec((1, d_model), const2)],
        out_specs=(pl.BlockSpec((tm, d_model), row), pl.BlockSpec((tm, d_model), row)),
        compiler_params=_params(1),
        name="dense_ffn_ln",
    )(x16, w_up.astype(_BF16), w_down.astype(_BF16), x, ln_g.reshape(1, -1), ln_b.reshape(1, -1))


def _route_top2(x, wr_ref):
    x_hi = x.astype(_BF16)
    x_lo = (x - x_hi.astype(_F32)).astype(_BF16)
    both = _dot(x_hi, wr_ref[...])
    logits = both[:, :_LANES] + (both[:, _LANES:] + _dot(x_lo, wr_ref[:, :_LANES]))
    lane = lax.broadcasted_iota(jnp.int32, logits.shape, 1).astype(_F32)
    lg = jnp.where(lane < _N_EXPERTS, logits, -jnp.inf)
    m1 = jnp.max(lg, axis=1, keepdims=True)
    i1 = jnp.min(jnp.where(lg == m1, lane, float(_LANES)), axis=1, keepdims=True)
    lg2 = jnp.where(lane == i1, -jnp.inf, lg)
    m2 = jnp.max(lg2, axis=1, keepdims=True)
    i2 = jnp.min(jnp.where(lg2 == m2, lane, float(_LANES)), axis=1, keepdims=True)
    e2 = jnp.exp(m2 - m1)
    den = 1.0 + e2
    w1 = 1.0 / den
    w2 = e2 / den
    return jnp.where(lane == 0, i1, jnp.where(lane == 1, i2,
                     jnp.where(lane == 2, w1, jnp.where(lane == 3, w2, 0.0))))


def _router_weights(w_router):
    d_model = w_router.shape[0]
    wr = jnp.zeros((d_model, _LANES), _F32).at[:, :_N_EXPERTS].set(w_router)
    wr_hi = wr.astype(_BF16)
    wr_lo = (wr - wr_hi.astype(_F32)).astype(_BF16)
    return jnp.concatenate([wr_hi, wr_lo], axis=1)


_ROW_UNROLL = 8


def _dispatch_kernel(zstart_ref, zvalid_ref, d0_ref, d1_ref, x_ref, xs_hbm, zero_sc, stage_sc, sem, zsem,
                     *, tt, tm, sub, nsteps):
    @pl.when(pl.program_id(0) == 0)
    def _():
        zero_sc[...] = jnp.zeros_like(zero_sc)

        def zero_copy(n):
            start = pl.multiple_of(zstart_ref[n] * sub, tm * sub)
            return pltpu.make_async_copy(zero_sc, xs_hbm.at[pl.ds(start, tm * sub), :], zsem)

        for n in range(2 * _N_EXPERTS):
            @pl.when(zvalid_ref[n] > 0)
            def _():
                zero_copy(n).start()
        for n in range(2 * _N_EXPERTS):
            @pl.when(zvalid_ref[n] > 0)
            def _():
                zero_copy(n).wait()

    i = pl.program_id(0)
    slot = i % 2
    stage_sc[slot] = x_ref[...]

    def issue(g, carry):
        for u in range(_ROW_UNROLL):
            r = g * _ROW_UNROLL + u
            src = stage_sc.at[slot, pl.ds(pl.multiple_of(r * sub, sub), sub), :]
            for d_ref in (d0_ref, d1_ref):
                dst = xs_hbm.at[pl.ds(pl.multiple_of(d_ref[0, 0, r] * sub, sub), sub), :]
                pltpu.make_async_copy(src, dst, sem.at[slot]).start(priority=u % 2)
        return carry

    def drain(s):
        for _ in range(2):
            pltpu.make_async_copy(stage_sc.at[s], xs_hbm.at[pl.ds(0, tt * sub), :], sem.at[s]).wait()

    lax.fori_loop(0, tt // _ROW_UNROLL, issue, 0)

    @pl.when(i > 0)
    def _():
        drain(1 - slot)

    @pl.when(i == nsteps - 1)
    def _():
        drain(slot)


def _dispatch_rows(x_tok, d0, d1, zstart, zvalid, *, tokens, rows, tt, tm):
    sub = x_tok.shape[0] // tokens
    smem = lambda: pl.BlockSpec((1, 1, tt), lambda i, zs, zv: (i, 0, 0), memory_space=pltpu.SMEM)
    grid_spec = pltpu.PrefetchScalarGridSpec(
        num_scalar_prefetch=2,
        grid=(tokens // tt,),
        in_specs=[smem(), smem(), pl.BlockSpec((tt * sub, _LANES), lambda i, zs, zv: (i, 0))],
        out_specs=pl.BlockSpec(memory_space=pl.ANY),
        scratch_shapes=[pltpu.VMEM((tm * sub, _LANES), _F32), pltpu.VMEM((2, tt * sub, _LANES), _F32),
                        pltpu.SemaphoreType.DMA((2,)), pltpu.SemaphoreType.DMA],
    )
    return pl.pallas_call(
        functools.partial(_dispatch_kernel, tt=tt, tm=tm, sub=sub, nsteps=tokens // tt),
        out_shape=jax.ShapeDtypeStruct((rows * sub, _LANES), _F32),
        grid_spec=grid_spec,
        compiler_params=_params(1),
        name="moe_dispatch",
    )(zstart, zvalid, d0, d1, x_tok)


def _combine_kernel(d0_ref, d1_ref, n0_ref, n1_ref, y_hbm, route_ref, x_ref, lng_ref, lnb_ref,
                    xo_ref, xo16_ref, buf, sem, *, alpha, tm, sub, nsteps):
    i = pl.program_id(0)
    slot = i % 2

    def gather(dst_slot, a_ref, b_ref):
        def issue(g, carry):
            for u in range(_ROW_UNROLL):
                r = g * _ROW_UNROLL + u
                for s, d_ref in enumerate((a_ref, b_ref)):
                    src = y_hbm.at[pl.ds(pl.multiple_of(d_ref[0, 0, r] * sub, sub), sub), :]
                    dst = buf.at[dst_slot, s, pl.ds(pl.multiple_of(r * sub, sub), sub), :]
                    pltpu.make_async_copy(src, dst, sem.at[dst_slot]).start(priority=u % 2)
            return carry

        lax.fori_loop(0, tm // _ROW_UNROLL, issue, 0)

    @pl.when(i == 0)
    def _():
        gather(0, d0_ref, d1_ref)

    @pl.when(i + 1 < nsteps)
    def _():
        gather(1 - slot, n0_ref, n1_ref)

    for s in range(2):
        pltpu.make_async_copy(y_hbm.at[pl.ds(0, tm * sub), :], buf.at[slot, s], sem.at[slot]).wait()
    route = route_ref[...]
    f = (route[:, 2:3] * _load_token_rows(buf.at[slot, 0], tm)
         + route[:, 3:4] * _load_token_rows(buf.at[slot, 1], tm))
    out = _layer_norm(alpha * x_ref[...] + f, lng_ref[...], lnb_ref[...], _LN_EPS)
    xo_ref[...] = out
    xo16_ref[...] = out.astype(_BF16)


def _combine_ln(ys, d0, d1, route, x, ln_g, ln_b, *, alpha, tm):
    tokens, d_model = x.shape
    sub = d_model // _LANES
    nsteps = tokens // tm
    row = lambda i: (i, 0)
    const2 = lambda i: (0, 0)
    cur = lambda: pl.BlockSpec((1, 1, tm), lambda i: (i, 0, 0), memory_space=pltpu.SMEM)
    nxt = lambda: pl.BlockSpec((1, 1, tm), lambda i: (jnp.minimum(i + 1, nsteps - 1), 0, 0),
                               memory_space=pltpu.SMEM)
    return pl.pallas_call(
        functools.partial(_combine_kernel, alpha=alpha, tm=tm, sub=sub, nsteps=nsteps),
        out_shape=(jax.ShapeDtypeStruct((tokens, d_model), _F32),
                   jax.ShapeDtypeStruct((tokens, d_model), _BF16)),
        grid=(nsteps,),
        in_specs=[cur(), cur(), nxt(), nxt(), pl.BlockSpec(memory_space=pl.ANY),
                  pl.BlockSpec((tm, _LANES), row), pl.BlockSpec((tm, d_model), row),
                  pl.BlockSpec((1, d_model), const2), pl.BlockSpec((1, d_model), const2)],
        out_specs=(pl.BlockSpec((tm, d_model), row), pl.BlockSpec((tm, d_model), row)),
        scratch_shapes=[pltpu.VMEM((2, 2, tm * sub, _LANES), _F32), pltpu.SemaphoreType.DMA((2,))],
        compiler_params=_params(1),
        name="moe_combine_ln",
    )(d0, d1, d0, d1, ys, route, x, ln_g.reshape(1, -1), ln_b.reshape(1, -1))


def _routing_plan(route, *, tm):
    tokens = route.shape[0]
    ntiles = (2 * tokens) // tm + _N_EXPERTS
    choice = route[:, 0:2].T.astype(jnp.int32)
    experts = jnp.arange(_N_EXPERTS, dtype=jnp.int32)[:, None]
    onehot = [(choice[k][None, :] == experts).astype(jnp.int32) for k in range(2)]
    csum = [jnp.cumsum(oh, axis=1) for oh in onehot]
    first_counts = csum[0][:, -1]
    counts = first_counts + csum[1][:, -1]
    padded = ((counts + tm - 1) // tm) * tm
    ends = jnp.cumsum(padded)
    starts = ends - padded
    base = [starts, starts + first_counts]
    dest = [jnp.sum(onehot[k] * (base[k][:, None] + csum[k] - 1), axis=0) for k in range(2)]
    tile_start = jnp.arange(ntiles, dtype=jnp.int32) * tm
    te = jnp.minimum(jnp.sum((ends[None, :] <= tile_start[:, None]).astype(jnp.int32), axis=1),
                     _N_EXPERTS - 1)
    nused = (ends[-1] // tm).reshape(1)
    tail = ends[-1] + jnp.arange(_N_EXPERTS, dtype=jnp.int32) * tm
    zstart = jnp.concatenate([jnp.maximum(ends - tm, 0), jnp.minimum(tail, (ntiles - 1) * tm)])
    zvalid = jnp.concatenate([padded > 0, tail < ntiles * tm]).astype(jnp.int32)
    return ntiles * tm, te, nused, dest[0], dest[1], zstart, zvalid


def _moe_ffn_ln(x, x_tok, route, w_up, w_down, ln_g, ln_b, *, alpha, tm):
    tokens = x.shape[0]
    tt = min(2 * _TM, tokens)
    tc = min(_TM // 2, tokens)
    rows, te, nused, d0, d1, zstart, zvalid = _routing_plan(route, tm=tm)
    xs = _dispatch_rows(x_tok, d0.reshape(tokens // tt, 1, tt), d1.reshape(tokens // tt, 1, tt),
                        zstart, zvalid, tokens=tokens, rows=rows, tt=tt, tm=tm)
    h = _swiglu_up(xs, w_up, te, nused, tm=tm, tn=w_up.shape[2] // 4)
    ys = _expert_down(h, w_down, te, nused, tm=tm)
    return _combine_ln(ys, d0.reshape(tokens // tc, 1, tc), d1.reshape(tokens // tc, 1, tc),
                       route, x, ln_g, ln_b, alpha=alpha, tm=tc)


def _mixer_ln(x, x16, w_in, fox_b_f, gate_b, gm_w_s, gm_b_s, gm_ln_g, gm_ln_b, w_branch, w_out,
              ln_g, ln_b, w_router, *, bsz, seq, alpha, tm):
    d_model = x.shape[1]
    n_r = 2 * _RET_HEADS * _RET_DK + 2 * _RET_HEADS * _RET_DV
    n_f = 3 * _FOX_HEADS * _FOX_DH
    n_fl = _FOX_HEADS
    w_r = w_in[:, :n_r].astype(_BF16)
    w_f = jnp.concatenate([w_in[:, n_r:n_r + n_f + n_fl],
                           jnp.zeros((d_model, _LANES - n_fl), w_in.dtype)], axis=1).astype(_BF16)
    w_c = w_in[:, n_r + n_f + n_fl:].astype(_BF16)

    bias = jnp.zeros((1, _LANES), _F32).at[0, :n_fl].set(fox_b_f)
    y_a, qa, ka, va = _mixer_ab(x16, w_r, w_f, bias, bsz, seq)
    y_b = _fox(qa, ka, va, bsz, seq)
    return _mixer_out(x16, y_a, y_b, w_c, gm_w_s, gm_b_s, gm_ln_g, gm_ln_b, gate_b, w_branch, w_out,
                      x, ln_g, ln_b, w_router, alpha=alpha, tm=tm)


def kernel(x, w_in, fox_b_f, gate_b, gm_w_s, gm_b_s, gm_ln_g, gm_ln_b, w_branch, w_out, ln_g, ln_b,
           dense_w_up, dense_w_down, moe_router, moe_w_up, moe_w_down):
    bsz, seq, d_model = x.shape
    depth = w_in.shape[0]
    alpha = (2 * depth) ** 0.25
    tokens = bsz * seq
    tm = min(_TM, tokens)
    xf = x.reshape(tokens, d_model)
    x16 = xf
    for l in range(depth):
        dense = l % 2 == 0
        xf, x16, route = _mixer_ln(
            xf, x16, w_in[l], fox_b_f[l], gate_b[l], gm_w_s[l], gm_b_s[l], gm_ln_g[l], gm_ln_b[l],
            w_branch[l], w_out[l], ln_g[l, 0], ln_b[l, 0], None if dense else moe_router[l // 2],
            bsz=bsz, seq=seq, alpha=alpha, tm=tm)
        if dense:
            xf, x16 = _dense_ffn_ln(xf, x16, dense_w_up[l // 2], dense_w_down[l // 2],
                                    ln_g[l, 1], ln_b[l, 1], alpha=alpha, tm=tm)
        else:
            xf, x16 = _moe_ffn_ln(xf, x16, route, moe_w_up[l // 2], moe_w_down[l // 2],
                                  ln_g[l, 1], ln_b[l, 1], alpha=alpha, tm=tm)
    return xf.reshape(bsz, seq, d_model)
```

```python
import functools
import math

import jax
import jax.numpy as jnp
import numpy as np
from jax import lax
from jax.experimental import pallas as pl
from jax.experimental.pallas import tpu as pltpu

_BF16 = jnp.bfloat16
_F32 = jnp.float32

_BLOCK = 128
_RET_HEADS, _RET_DK, _RET_DV = 4, 64, 128
_FOX_HEADS, _FOX_DH = 8, 64
_GM_GROUPS, _GM_DG = 4, 128
_N_EXPERTS = 8
_ROPE_BASE = 10000.0
_LN_EPS = 1e-5
_GN_EPS = 1e-6
_LOG2_E = 1.4426950408889634

_LANES = 128
_MXU_COLS = 256
_VMEM_PHYSICAL_BYTES = 64 * 1024 * 1024
_VMEM_LIMIT_BYTES = (_VMEM_PHYSICAL_BYTES * 7) // 8

_TM = 512
_TQ = 128
_TR_RET = 512

def _params(n_axes):
    return pltpu.CompilerParams(
        dimension_semantics=("arbitrary",) * n_axes,
        vmem_limit_bytes=_VMEM_LIMIT_BYTES,
    )


def _layer_norm(x, g, b, eps):
    mu = jnp.mean(x, axis=-1, keepdims=True)
    xc = x - mu
    var = jnp.mean(xc * xc, axis=-1, keepdims=True)
    return xc * lax.rsqrt(var + eps) * g + b


def _silu(x):
    return x * (1.0 / (1.0 + jnp.exp(-x)))


def _gelu_tanh(x):
    c = math.sqrt(2.0 / math.pi)
    return x * (0.5 * (1.0 + jnp.tanh(c * (x + 0.044715 * (x * x * x)))))


def _dot(a, b):
    return jnp.dot(a, b, preferred_element_type=_F32)


def _dot_nt(a, b):
    return lax.dot_general(a, b, (((1,), (1,)), ((), ())), preferred_element_type=_F32)


def _dot_tn(a, b):
    return lax.dot_general(a, b, (((0,), (0,)), ((), ())), preferred_element_type=_F32)


def _retention_project(x16_ref, w_ref, v_sc, g_sc):
    width = _RET_HEADS * _RET_DK
    wv = _RET_HEADS * _RET_DV
    x16 = x16_ref[...].astype(_BF16)
    q = _dot(x16, w_ref[:, 0:width])
    k = _dot(x16, w_ref[:, width:2 * width])
    v_sc[...] = _dot(x16, w_ref[:, 2 * width:2 * width + wv]).astype(_BF16)
    g_sc[...] = _dot(x16, w_ref[:, 2 * width + wv:2 * width + 2 * wv])
    return q, k


def _retention_core(q_raw, k_raw, cos_ref, sin_ref, qdec_ref, kdec_ref, din_ref,
                    y_ref, v_sc, g_sc, state_sc, *, chunk_dec):
    width = _RET_HEADS * _RET_DK
    wv = _RET_HEADS * _RET_DV
    rows_total = q_raw.shape[0]
    lane_all = lax.broadcasted_iota(jnp.int32, (rows_total, width), 1)
    first_half = (lane_all % _RET_DK) < (_RET_DK // 2)

    def rotary(x):
        partner = jnp.where(first_half,
                            pltpu.roll(x, width - _RET_DK // 2, 1),
                            pltpu.roll(x, _RET_DK // 2, 1))
        return x * cos_ref[...] + partner * sin_ref[...]

    q_all = rotary(q_raw)
    k_all = rotary(k_raw) * (_RET_DK ** -0.5)
    lane = lax.broadcasted_iota(jnp.int32, (_BLOCK, width), 1)
    srow = lax.broadcasted_iota(jnp.int32, (width, wv), 0) // _RET_DK
    scol = lax.broadcasted_iota(jnp.int32, (width, wv), 1) // _RET_DV
    on_diag = srow == scol
    col_head = lax.broadcasted_iota(jnp.int32, (1, wv), 1) // _RET_DV
    decay_cols = jnp.zeros((1, wv), _F32)
    for h in range(_RET_HEADS):
        decay_cols = jnp.where(col_head == h, chunk_dec[h], decay_cols)
    for cc in range(rows_total // _BLOCK):
        rows = slice(cc * _BLOCK, (cc + 1) * _BLOCK)
        q = q_all[rows]
        k = k_all[rows]
        v = v_sc[rows, :]
        q_heads = jnp.concatenate(
            [jnp.where((lane // _RET_DK) == h, q, 0.0) for h in range(_RET_HEADS)], axis=0)
        scores = _dot_nt(q_heads.astype(_BF16), k.astype(_BF16))
        state = state_sc[...]
        cross = _dot((q * qdec_ref[...]).astype(_BF16), state.astype(_BF16))
        kv = _dot_tn((k * kdec_ref[...]).astype(_BF16), v)
        state_sc[...] = state * decay_cols + jnp.where(on_diag, kv, 0.0)
        for h in range(_RET_HEADS):
            cols = slice(h * _RET_DV, (h + 1) * _RET_DV)
            sh = scores[h * _BLOCK:(h + 1) * _BLOCK] * din_ref[h]
            out = _dot(sh.astype(_BF16), v[:, cols]) + cross[:, cols]
            mu = jnp.mean(out, axis=-1, keepdims=True)
            oc = out - mu
            var = jnp.mean(oc * oc, axis=-1, keepdims=True)
            normed = oc * lax.rsqrt(var + _GN_EPS)
            y_ref[rows, cols] = (_silu(g_sc[rows, cols]) * normed).astype(y_ref.dtype)


def _retention_tables(seq):
    half = _RET_DK // 2
    inv_freq = _ROPE_BASE ** (-jnp.arange(half, dtype=_F32) / half)
    ang = jnp.arange(seq, dtype=jnp.int32).astype(_F32)[:, None] * inv_freq[None, :]
    cos, sin = jnp.cos(ang), jnp.sin(ang)
    cos_t = jnp.tile(jnp.concatenate([cos, cos], axis=1), (1, _RET_HEADS))
    sin_t = jnp.tile(jnp.concatenate([-sin, sin], axis=1), (1, _RET_HEADS))
    log_gamma = jnp.log1p(-jnp.exp2(-5.0 - jnp.arange(_RET_HEADS, dtype=_F32)))
    idx = jnp.arange(_BLOCK, dtype=_F32)
    rel = idx[:, None] - idx[None, :]
    causal = rel >= 0
    decay_in = jnp.where(causal[None],
                         jnp.exp(log_gamma[:, None, None] * jnp.where(causal, rel, 0.0)[None]), 0.0)
    q_dec = jnp.exp(log_gamma[:, None] * (idx + 1.0))
    k_dec = jnp.exp(log_gamma[:, None] * (_BLOCK - 1.0 - idx))
    q_dec_t = jnp.repeat(q_dec.T, _RET_DK, axis=1)
    k_dec_t = jnp.repeat(k_dec.T, _RET_DK, axis=1)
    return cos_t, sin_t, q_dec_t, k_dec_t, decay_in


def _mixer_ab_kernel(x_ref, wr_ref, cos_ref, sin_ref, qdec_ref, kdec_ref, din_ref, wf_ref, b_ref, sel_ref,
                     y_ref, qa_ref, ka_ref, va_ref,
                     v_sc, g_sc, state_sc, fq_sc, fk_sc, fv_sc, fl_sc, carry_sc, *, chunk_dec, nblk):
    @pl.when(pl.program_id(1) == 0)
    def _():
        state_sc[...] = jnp.zeros_like(state_sc)
        carry_sc[...] = jnp.zeros_like(carry_sc)

    _fox_prep_body(x_ref, wf_ref, b_ref, sel_ref, qa_ref, ka_ref, va_ref, fq_sc, fk_sc, fv_sc, fl_sc,
                   carry_sc, nblk=nblk)
    q_raw, k_raw = _retention_project(x_ref, wr_ref, v_sc, g_sc)
    _retention_core(q_raw, k_raw, cos_ref, sin_ref, qdec_ref, kdec_ref, din_ref, y_ref, v_sc, g_sc,
                    state_sc, chunk_dec=chunk_dec)


def _mixer_ab(x16, w_r, w_f, bias, bsz, seq):
    tokens, d_model = x16.shape
    tr = min(_TR_RET, seq)
    nstep = seq // tr
    cos_t, sin_t, q_dec_t, k_dec_t, decay_in = _retention_tables(seq)
    chunk_dec = tuple(float((1.0 - 2.0 ** (-5.0 - h)) ** _BLOCK) for h in range(_RET_HEADS))
    qk = _RET_HEADS * _RET_DK
    wv = _RET_HEADS * _RET_DV
    fw = _FOX_HEADS * _FOX_DH
    sel = _fox_selectors()
    row = lambda b, c: (b * nstep + c, 0)
    const2 = lambda b, c: (0, 0)
    const3 = lambda b, c: (0, 0, 0)
    aug = jax.ShapeDtypeStruct((bsz, _FOX_HEADS, seq, _LANES), _BF16)
    aug_spec = pl.BlockSpec((None, _FOX_HEADS, tr, _LANES), lambda b, c: (b, 0, c, 0))
    return pl.pallas_call(
        functools.partial(_mixer_ab_kernel, chunk_dec=chunk_dec, nblk=tr // _BLOCK),
        out_shape=(jax.ShapeDtypeStruct((tokens, wv), _BF16), aug, aug, aug),
        grid=(bsz, nstep),
        in_specs=[
            pl.BlockSpec((tr, d_model), row),
            pl.BlockSpec(w_r.shape, const2),
            pl.BlockSpec((tr, qk), lambda b, c: (c, 0)),
            pl.BlockSpec((tr, qk), lambda b, c: (c, 0)),
            pl.BlockSpec((_BLOCK, qk), const2),
            pl.BlockSpec((_BLOCK, qk), const2),
            pl.BlockSpec((_RET_HEADS, _BLOCK, _BLOCK), const3),
            pl.BlockSpec(w_f.shape, const2),
            pl.BlockSpec((1, _LANES), const2),
            pl.BlockSpec(sel.shape, const3),
        ],
        out_specs=(pl.BlockSpec((tr, wv), row), aug_spec, aug_spec, aug_spec),
        scratch_shapes=[pltpu.VMEM((tr, wv), _BF16), pltpu.VMEM((tr, wv), _F32),
                        pltpu.VMEM((qk, wv), _F32),
                        pltpu.VMEM((tr, fw), _BF16), pltpu.VMEM((tr, fw), _BF16),
                        pltpu.VMEM((tr, fw), _BF16), pltpu.VMEM((tr, _LANES), _F32),
                        pltpu.VMEM((1, _LANES), _F32)],
        compiler_params=_params(2),
        name="mixer_ab",
    )(x16, w_r, cos_t, sin_t, q_dec_t, k_dec_t, decay_in, w_f, bias, sel)


_C_PARTS = 3


def _fox_selectors():
    half = _LANES // 2
    one_row = _C_PARTS * _FOX_HEADS
    sel = np.zeros((_FOX_HEADS, _LANES, 2 * _LANES), np.float32)
    for h in range(_FOX_HEADS):
        base = half if h % 2 == 0 else 0
        for n in range(_C_PARTS):
            sel[h, n * _FOX_HEADS + h, base + n] = 1.0
            sel[h, one_row, base + _C_PARTS + n] = 1.0
            sel[h, one_row, _LANES + base + n] = 1.0
            sel[h, n * _FOX_HEADS + h, _LANES + base + _C_PARTS + n] = -1.0
    return jnp.asarray(sel, _BF16)


def _fox_prep_body(x16_ref, w_ref, b_ref, sel_ref, qa_ref, ka_ref, va_ref, fq_ref, fk_ref, fv_ref,
                   fl_ref, carry_sc, *, nblk):
    x16 = x16_ref[...].astype(_BF16)
    width = _FOX_HEADS * _FOX_DH
    for n, dst in enumerate((fq_ref, fk_ref, fv_ref)):
        proj = _dot(x16, w_ref[:, n * width:(n + 1) * width])
        dst[...] = (proj * (_LOG2_E * _FOX_DH ** -0.5) if n == 0 else proj).astype(_BF16)
    fl_ref[...] = _dot(x16, w_ref[:, 3 * width:3 * width + _LANES])

    r = lax.broadcasted_iota(jnp.int32, (_BLOCK, _LANES), 0)
    lane = lax.broadcasted_iota(jnp.int32, (_BLOCK, _LANES), 1)
    tri = (r >= lane).astype(_BF16)
    half = _LANES // 2
    carry = carry_sc[...]
    packed = []
    for blk in range(nblk):
        rows = slice(blk * _BLOCK, (blk + 1) * _BLOCK)
        z = fl_ref[rows, :] + b_ref[...]
        log_f = jnp.minimum(z, 0.0) - jnp.log1p(jnp.exp(-jnp.abs(z)))
        pieces = []
        rest = log_f
        for _ in range(_C_PARTS):
            piece = rest.astype(_BF16)
            pieces.append(piece)
            rest = rest - piece.astype(_F32)
        sums = _dot(tri, jnp.concatenate(pieces, axis=1))
        cum = sums[:, (_C_PARTS - 1) * _LANES:]
        for n in reversed(range(_C_PARTS - 1)):
            cum = cum + sums[:, n * _LANES:(n + 1) * _LANES]
        cum = cum + carry
        carry = cum[_BLOCK - 1:_BLOCK, :]
        row = jnp.where(lane == _C_PARTS * _FOX_HEADS, 1.0, 0.0)
        rest = cum * _LOG2_E
        for n in range(_C_PARTS):
            piece = rest.astype(_BF16).astype(_F32)
            rest = rest - piece
            shifted = piece if n == 0 else pltpu.roll(piece, n * _FOX_HEADS, 1)
            row = jnp.where((lane >= n * _FOX_HEADS) & (lane < (n + 1) * _FOX_HEADS), shifted, row)
        packed.append(row.astype(_BF16))
    carry_sc[...] = carry
    packed = jnp.concatenate(packed, axis=0)
    lane_all = lax.broadcasted_iota(jnp.int32, (nblk * _BLOCK, _LANES), 1)
    for h in range(_FOX_HEADS):
        pair = slice((h // 2) * _LANES, (h // 2 + 1) * _LANES)
        in_head = (lane_all < half) if h % 2 == 0 else (lane_all >= half)
        extra = _dot(packed, sel_ref[h]).astype(_BF16)
        qa_ref[h] = jnp.where(in_head, fq_ref[:, pair], extra[:, :_LANES])
        ka_ref[h] = jnp.where(in_head, fk_ref[:, pair], extra[:, _LANES:])
        va_ref[h] = jnp.where(in_head, fv_ref[:, pair], jnp.ones((), _BF16))


def _fox_kernel(qa_ref, ka_ref, va_ref, o_ref, *, tq, nq):
    half = _LANES // 2
    low = lax.broadcasted_iota(jnp.int32, (tq, _LANES), 1) < half
    keep = (lax.broadcasted_iota(jnp.int32, (tq, tq), 0)
            >= lax.broadcasted_iota(jnp.int32, (tq, tq), 1))
    order = list(reversed(range(nq)))
    logits = {n: [_dot_nt(qa_ref[s, n * tq:(n + 1) * tq, :], ka_ref[s, 0:(n + 1) * tq, :])
                  for s in range(2)] for n in order}
    for n in order:
        rows = slice(n * tq, (n + 1) * tq)
        width = (n + 1) * tq
        pv = []
        for s in range(2):
            diag = jnp.where(keep, logits[n][s][:, width - tq:], -jnp.inf)
            lg = diag if n == 0 else jnp.concatenate([logits[n][s][:, :width - tq], diag], axis=1)
            pexp = jnp.exp2(lg - jnp.max(lg, axis=1, keepdims=True))
            pv.append(_dot(pexp.astype(_BF16), va_ref[s, 0:width, :]))
        out = jnp.where(low, pv[0] / pv[0][:, half:half + 1], pv[1] / pv[1][:, 0:1])
        o_ref[rows, :] = out.astype(o_ref.dtype)


def _fox(qa, ka, va, bsz, seq):
    tq = min(_TQ, seq)
    nq = seq // tq
    npair = _FOX_HEADS // 2
    logit_bytes = 2 * tq * tq * (nq * (nq + 1) // 2) * 4
    assert logit_bytes <= _VMEM_LIMIT_BYTES // 2, "sequence too long for the single-pass attention step"
    return pl.pallas_call(
        functools.partial(_fox_kernel, tq=tq, nq=nq),
        out_shape=jax.ShapeDtypeStruct((bsz * seq, _FOX_HEADS * _FOX_DH), _BF16),
        grid=(bsz, npair),
        in_specs=[
            pl.BlockSpec((None, 2, seq, _LANES), lambda b, p: (b, p, 0, 0)),
            pl.BlockSpec((None, 2, seq, _LANES), lambda b, p: (b, p, 0, 0)),
            pl.BlockSpec((None, 2, seq, _LANES), lambda b, p: (b, p, 0, 0)),
        ],
        out_specs=pl.BlockSpec((seq, _LANES), lambda b, p: (b, p)),
        compiler_params=_params(2),
        name="fox_attention",
    )(qa, ka, va)


def _gmlp_tile(gu, gv, w_ref, bst_ref, lng_ref, lnb_ref, yc_sc):
    u = _gelu_tanh(gu)
    v = _layer_norm(_gelu_tanh(gv), lng_ref[...], lnb_ref[...], _LN_EPS).astype(_BF16)
    r = lax.broadcasted_iota(jnp.int32, (_BLOCK, _BLOCK), 0)
    c = lax.broadcasted_iota(jnp.int32, (_BLOCK, _BLOCK), 1)
    causal = r >= c
    nchunk = gu.shape[0] // _BLOCK
    for g in range(_GM_GROUPS):
        cols = slice(g * _GM_DG, (g + 1) * _GM_DG)
        wg = jnp.where(causal, w_ref[g], 0.0).astype(_BF16)
        bias = bst_ref[:, g:g + 1]
        v_chunks = jnp.concatenate(
            [v[ch * _BLOCK:(ch + 1) * _BLOCK, cols] for ch in range(nchunk)], axis=1)
        sp = _dot(wg, v_chunks) + bias
        for ch in range(nchunk):
            rows = slice(ch * _BLOCK, (ch + 1) * _BLOCK)
            yc_sc[rows, cols] = (u[rows, cols] * sp[:, ch * _GM_DG:(ch + 1) * _GM_DG]).astype(yc_sc.dtype)


def _mixer_out_kernel(x16_ref, ya_ref, yb_ref, wc_ref, gmw_ref, bst_ref, gmg_ref, gmb_ref, gbias_ref,
                      wbr_ref, wout_ref, x_ref, lng_ref, lnb_ref, *rest, alpha, d_model, with_router):
    if with_router:
        wr_ref, xo_ref, xtok_ref, route_ref, yc_sc = rest
    else:
        xo_ref, xo16_ref, yc_sc = rest
    gw = _GM_GROUPS * _GM_DG
    sub = d_model // _LANES

    def front(rows):
        x16 = x16_ref[rows, :].astype(_BF16)
        gu = _dot(x16, wc_ref[:, 0:gw])
        gv = _dot(x16, wc_ref[:, gw:2 * gw])

        def gate(n):
            cols = slice(2 * gw + n * d_model, 2 * gw + (n + 1) * d_model)
            z = _dot(x16, wc_ref[:, cols]) + gbias_ref[:, n * d_model:(n + 1) * d_model]
            return 1.0 / (1.0 + jnp.exp(-z))

        merged = (gate(0) * _dot(ya_ref[rows, :], wbr_ref[0])
                  + gate(1) * _dot(yb_ref[rows, :], wbr_ref[1]))
        return gu, gv, gate(2), merged

    def back(rows, gu, gv, gate_c, merged):
        yc = yc_sc.at[rows, :]
        _gmlp_tile(gu, gv, gmw_ref, bst_ref, gmg_ref, gmb_ref, yc)
        merged = merged + gate_c * _dot(yc[...], wbr_ref[2])
        h = _dot(merged.astype(_BF16), wout_ref[...])
        out = _layer_norm(alpha * x_ref[rows, :] + h, lng_ref[...], lnb_ref[...], _LN_EPS)
        xo_ref[rows, :] = out
        if with_router:
            _store_token_rows(xtok_ref.at[pl.ds(rows.start * sub, (rows.stop - rows.start) * sub), :], out)
            route_ref[rows, :] = _route_top2(out, wr_ref)
        else:
            xo16_ref[rows, :] = out.astype(_BF16)

    rows = slice(0, x16_ref.shape[0])
    back(rows, *front(rows))


def _mixer_out(x16, y_a, y_b, w_c, gm_w_s, gm_b_s, gm_ln_g, gm_ln_b, gate_b, w_branch, w_out, x,
               ln_g, ln_b, w_router, *, alpha, tm):
    tokens, d_model = x.shape
    bw = y_a.shape[1]
    gw = _GM_GROUPS * _GM_DG
    row = lambda i: (i, 0)
    const2 = lambda i: (0, 0)
    const3 = lambda i: (0, 0, 0)
    with_router = w_router is not None
    in_specs = [
        pl.BlockSpec((tm, d_model), row),
        pl.BlockSpec((tm, bw), row), pl.BlockSpec((tm, bw), row),
        pl.BlockSpec(w_c.shape, const2),
        pl.BlockSpec((_GM_GROUPS, _BLOCK, _BLOCK), const3),
        pl.BlockSpec((_BLOCK, _GM_GROUPS), const2),
        pl.BlockSpec((1, gw), const2), pl.BlockSpec((1, gw), const2),
        pl.BlockSpec((1, 3 * d_model), const2),
        pl.BlockSpec((3, bw, d_model), const3),
        pl.BlockSpec((d_model, d_model), const2),
        pl.BlockSpec((tm, d_model), row),
        pl.BlockSpec((1, d_model), const2), pl.BlockSpec((1, d_model), const2),
    ]
    args = [x16, y_a, y_b, w_c, gm_w_s, gm_b_s.T, gm_ln_g.reshape(1, gw), gm_ln_b.reshape(1, gw),
            gate_b.reshape(1, -1), w_branch.astype(_BF16), w_out.astype(_BF16), x,
            ln_g.reshape(1, -1), ln_b.reshape(1, -1)]
    out_shape = [jax.ShapeDtypeStruct((tokens, d_model), _F32)]
    out_specs = [pl.BlockSpec((tm, d_model), row)]
    if with_router:
        sub = d_model // _LANES
        in_specs.append(pl.BlockSpec((d_model, 2 * _LANES), const2))
        args.append(_router_weights(w_router))
        out_shape += [jax.ShapeDtypeStruct((tokens * sub, _LANES), _F32),
                      jax.ShapeDtypeStruct((tokens, _LANES), _F32)]
        out_specs += [pl.BlockSpec((tm * sub, _LANES), row), pl.BlockSpec((tm, _LANES), row)]
    else:
        out_shape.append(jax.ShapeDtypeStruct((tokens, d_model), _BF16))
        out_specs.append(pl.BlockSpec((tm, d_model), row))
    outs = pl.pallas_call(
        functools.partial(_mixer_out_kernel, alpha=alpha, d_model=d_model, with_router=with_router),
        out_shape=tuple(out_shape),
        grid=(tokens // tm,),
        in_specs=in_specs,
        out_specs=tuple(out_specs),
        scratch_shapes=[pltpu.VMEM((tm, gw), _BF16)],
        compiler_params=_params(1),
        name="mixer_out",
    )(*args)
    return outs if with_router else (*outs, None)


def _tile_groups(te):
    change = te[1:] != te[:-1]
    one = jnp.ones((1,), jnp.bool_)
    first = jnp.concatenate([one, change]).astype(jnp.int32)
    last = jnp.concatenate([change, one]).astype(jnp.int32)
    later = jnp.where(te[None, :] > te[:, None], te[None, :], _N_EXPERTS)
    nexte = jnp.min(later, axis=1)
    wrap = (nexte == _N_EXPERTS).astype(jnp.int32)
    return first, last, jnp.where(wrap == 1, te[0], nexte), wrap


def _load_token_rows(ref, n):
    s = ref.shape[0] // n
    return jnp.concatenate([ref[pl.ds(c, n, stride=s), :] for c in range(s)], axis=1)


def _store_token_rows(ref, x):
    n = x.shape[0]
    s = x.shape[1] // _LANES
    for c in range(s):
        ref[pl.ds(c, n, stride=s), :] = x[:, c * _LANES:(c + 1) * _LANES]


def _up_kernel(te_ref, nused_ref, first_ref, last_ref, nexte_ref, wrap_ref, x_ref, w_hbm, h_ref,
               stage_a, stage_b, wa_sc, wb_sc, sem, slot_sc, *, tn, nb, d_ff):
    j = pl.program_id(0)
    i = pl.program_id(1)

    def weight_copies(e, jj):
        col = pl.multiple_of(jj * tn, tn)
        return (pltpu.make_async_copy(w_hbm.at[e, :, pl.ds(col, tn)], stage_a, sem.at[0]),
                pltpu.make_async_copy(w_hbm.at[e, :, pl.ds(d_ff + col, tn)], stage_b, sem.at[1]))

    def round_into(slot):
        wa_sc[slot] = stage_a[...].astype(_BF16)
        wb_sc[slot] = stage_b[...].astype(_BF16)

    @pl.when((j == 0) & (i == 0))
    def _():
        for c in weight_copies(te_ref[0], 0):
            c.start()
        for c in weight_copies(te_ref[0], 0):
            c.wait()
        round_into(0)
        slot_sc[0] = 0

    has_next = (wrap_ref[i] == 0) | (j + 1 < nb)
    next_j = jnp.where(wrap_ref[i] == 1, j + 1, j)

    @pl.when((first_ref[i] == 1) & has_next)
    def _():
        for c in weight_copies(nexte_ref[i], next_j):
            c.start()

    slot = slot_sc[0]

    @pl.when(i < nused_ref[0])
    def _():
        x = _load_token_rows(x_ref, h_ref.shape[0]).astype(_BF16)
        a = _dot(x, wa_sc[slot])
        b = _dot(x, wb_sc[slot])
        h_ref[...] = (_silu(a) * b).astype(h_ref.dtype)

    @pl.when(i >= nused_ref[0])
    def _():
        h_ref[...] = jnp.zeros_like(h_ref)

    @pl.when((last_ref[i] == 1) & has_next)
    def _():
        for c in weight_copies(nexte_ref[i], next_j):
            c.wait()
        round_into(1 - slot)
        slot_sc[0] = 1 - slot


def _swiglu_up(xs, w_up, te, nused, *, tm, tn):
    k = w_up.shape[1]
    sub = k // _LANES
    rows = xs.shape[0] // sub
    d_ff = w_up.shape[2] // 2
    nb = d_ff // tn
    idx = lambda j, i, *prefetch: (i, 0)
    grid_spec = pltpu.PrefetchScalarGridSpec(
        num_scalar_prefetch=6,
        grid=(nb, rows // tm),
        in_specs=[pl.BlockSpec((tm * sub, _LANES), idx), pl.BlockSpec(memory_space=pl.ANY)],
        out_specs=pl.BlockSpec((tm, tn), lambda j, i, *prefetch: (i, j)),
        scratch_shapes=[pltpu.VMEM((k, tn), _F32), pltpu.VMEM((k, tn), _F32),
                        pltpu.VMEM((2, k, tn), _BF16), pltpu.VMEM((2, k, tn), _BF16),
                        pltpu.SemaphoreType.DMA((2,)), pltpu.SMEM((1,), jnp.int32)],
    )
    return pl.pallas_call(
        functools.partial(_up_kernel, tn=tn, nb=nb, d_ff=d_ff),
        out_shape=jax.ShapeDtypeStruct((rows, d_ff), _BF16),
        grid_spec=grid_spec,
        compiler_params=_params(2),
        name="swiglu_up",
    )(te, nused, *_tile_groups(te), xs, w_up)


def _down_kernel(te_ref, nused_ref, first_ref, last_ref, nexte_ref, wrap_ref, h_ref, w_hbm, y_ref,
                 stage, w_sc, sem, slot_sc):
    i = pl.program_id(0)

    def weight_copy(e):
        return pltpu.make_async_copy(w_hbm.at[e], stage, sem)

    @pl.when(i == 0)
    def _():
        weight_copy(te_ref[0]).start()
        weight_copy(te_ref[0]).wait()
        w_sc[0] = stage[...].astype(_BF16)
        slot_sc[0] = 0

    has_next = wrap_ref[i] == 0

    @pl.when((first_ref[i] == 1) & has_next)
    def _():
        weight_copy(nexte_ref[i]).start()

    slot = slot_sc[0]

    @pl.when(i < nused_ref[0])
    def _():
        _store_token_rows(y_ref, _dot(h_ref[...], w_sc[slot]))

    @pl.when(i >= nused_ref[0])
    def _():
        y_ref[...] = jnp.zeros_like(y_ref)

    @pl.when((last_ref[i] == 1) & has_next)
    def _():
        weight_copy(nexte_ref[i]).wait()
        w_sc[1 - slot] = stage[...].astype(_BF16)
        slot_sc[0] = 1 - slot


def _expert_down(h16, w_down, te, nused, *, tm):
    rows, k = h16.shape
    n = w_down.shape[2]
    sub = n // _LANES
    grid_spec = pltpu.PrefetchScalarGridSpec(
        num_scalar_prefetch=6,
        grid=(rows // tm,),
        in_specs=[pl.BlockSpec((tm, k), lambda i, *prefetch: (i, 0)),
                  pl.BlockSpec(memory_space=pl.ANY)],
        out_specs=pl.BlockSpec((tm * sub, _LANES), lambda i, *prefetch: (i, 0)),
        scratch_shapes=[pltpu.VMEM((k, n), _F32), pltpu.VMEM((2, k, n), _BF16),
                        pltpu.SemaphoreType.DMA, pltpu.SMEM((1,), jnp.int32)],
    )
    return pl.pallas_call(
        _down_kernel,
        out_shape=jax.ShapeDtypeStruct((rows * sub, _LANES), _F32),
        grid_spec=grid_spec,
        compiler_params=_params(1),
        name="expert_down",
    )(te, nused, *_tile_groups(te), h16, w_down)


def _dense_ffn_kernel(x16_ref, wup_ref, wdn_ref, x_ref, lng_ref, lnb_ref, xo_ref, xo16_ref,
                      *, alpha, d_ff, chunks):
    x16 = x16_ref[...]
    f = None
    for c0, c1 in chunks:
        a = _dot(x16, wup_ref[:, c0:c1])
        b = _dot(x16, wup_ref[:, d_ff + c0:d_ff + c1])
        part = _dot((_silu(a) * b).astype(_BF16), wdn_ref[c0:c1, :])
        f = part if f is None else f + part
    out = _layer_norm(alpha * x_ref[...] + f, lng_ref[...], lnb_ref[...], _LN_EPS)
    xo_ref[...] = out
    xo16_ref[...] = out.astype(_BF16)


def _dense_ffn_ln(x, x16, w_up, w_down, ln_g, ln_b, *, alpha, tm):
    tokens, d_model = x.shape
    d_ff = w_down.shape[0]
    split = (d_ff // (2 * _MXU_COLS)) * _MXU_COLS
    chunks = ((0, split), (split, d_ff)) if 0 < split < d_ff else ((0, d_ff),)
    row = lambda i: (i, 0)
    const2 = lambda i: (0, 0)
    return pl.pallas_call(
        functools.partial(_dense_ffn_kernel, alpha=alpha, d_ff=d_ff, chunks=chunks),
        out_shape=(jax.ShapeDtypeStruct((tokens, d_model), _F32),
                   jax.ShapeDtypeStruct((tokens, d_model), _BF16)),
        grid=(tokens // tm,),
        in_specs=[pl.BlockSpec((tm, d_model), row),
                  pl.BlockSpec((d_model, 2 * d_ff), const2),
                  pl.BlockSpec((d_ff, d_model), const2),
                  pl.BlockSpec((tm, d_model), row),
                  pl.BlockSpec((1, d_model), const2), pl.BlockSpec((1, d_model), const2)],
        out_specs=(pl.BlockSpec((tm, d_model), row), pl.BlockSpec((tm, d_model), row)),
        compiler_params=_params(1),
        name="dense_ffn_ln",
    )(x16, w_up.astype(_BF16), w_down.astype(_BF16), x, ln_g.reshape(1, -1), ln_b.reshape(1, -1))


def _route_top2(x, wr_ref):
    x_hi = x.astype(_BF16)
    x_lo = (x - x_hi.astype(_F32)).astype(_BF16)
    both = _dot(x_hi, wr_ref[...])
    logits = both[:, :_LANES] + (both[:, _LANES:] + _dot(x_lo, wr_ref[:, :_LANES]))
    lane = lax.broadcasted_iota(jnp.int32, logits.shape, 1).astype(_F32)
    lg = jnp.where(lane < _N_EXPERTS, logits, -jnp.inf)
    m1 = jnp.max(lg, axis=1, keepdims=True)
    i1 = jnp.min(jnp.where(lg == m1, lane, float(_LANES)), axis=1, keepdims=True)
    lg2 = jnp.where(lane == i1, -jnp.inf, lg)
    m2 = jnp.max(lg2, axis=1, keepdims=True)
    i2 = jnp.min(jnp.where(lg2 == m2, lane, float(_LANES)), axis=1, keepdims=True)
    e2 = jnp.exp(m2 - m1)
    den = 1.0 + e2
    w1 = 1.0 / den
    w2 = e2 / den
    return jnp.where(lane == 0, i1, jnp.where(lane == 1, i2,
                     jnp.where(lane == 2, w1, jnp.where(lane == 3, w2, 0.0))))


def _router_weights(w_router):
    d_model = w_router.shape[0]
    wr = jnp.zeros((d_model, _LANES), _F32).at[:, :_N_EXPERTS].set(w_router)
    wr_hi = wr.astype(_BF16)
    wr_lo = (wr - wr_hi.astype(_F32)).astype(_BF16)
    return jnp.concatenate([wr_hi, wr_lo], axis=1)


_ROW_UNROLL = 8


def _dispatch_kernel(zstart_ref, zvalid_ref, d0_ref, d1_ref, x_ref, xs_hbm, zero_sc, stage_sc, sem, zsem,
                     *, tt, tm, sub, nsteps):
    @pl.when(pl.program_id(0) == 0)
    def _():
        zero_sc[...] = jnp.zeros_like(zero_sc)

        def zero_copy(n):
            start = pl.multiple_of(zstart_ref[n] * sub, tm * sub)
            return pltpu.make_async_copy(zero_sc, xs_hbm.at[pl.ds(start, tm * sub), :], zsem)

        for n in range(2 * _N_EXPERTS):
            @pl.when(zvalid_ref[n] > 0)
            def _():
                zero_copy(n).start()
        for n in range(2 * _N_EXPERTS):
            @pl.when(zvalid_ref[n] > 0)
            def _():
                zero_copy(n).wait()

    i = pl.program_id(0)
    slot = i % 2
    stage_sc[slot] = x_ref[...]

    def issue(g, carry):
        for u in range(_ROW_UNROLL):
            r = g * _ROW_UNROLL + u
            src = stage_sc.at[slot, pl.ds(pl.multiple_of(r * sub, sub), sub), :]
            for d_ref in (d0_ref, d1_ref):
                dst = xs_hbm.at[pl.ds(pl.multiple_of(d_ref[0, 0, r] * sub, sub), sub), :]
                pltpu.make_async_copy(src, dst, sem.at[slot]).start(priority=u % 2)
        return carry

    def drain(s):
        for _ in range(2):
            pltpu.make_async_copy(stage_sc.at[s], xs_hbm.at[pl.ds(0, tt * sub), :], sem.at[s]).wait()

    lax.fori_loop(0, tt // _ROW_UNROLL, issue, 0)

    @pl.when(i > 0)
    def _():
        drain(1 - slot)

    @pl.when(i == nsteps - 1)
    def _():
        drain(slot)


def _dispatch_rows(x_tok, d0, d1, zstart, zvalid, *, tokens, rows, tt, tm):
    sub = x_tok.shape[0] // tokens
    smem = lambda: pl.BlockSpec((1, 1, tt), lambda i, zs, zv: (i, 0, 0), memory_space=pltpu.SMEM)
    grid_spec = pltpu.PrefetchScalarGridSpec(
        num_scalar_prefetch=2,
        grid=(tokens // tt,),
        in_specs=[smem(), smem(), pl.BlockSpec((tt * sub, _LANES), lambda i, zs, zv: (i, 0))],
        out_specs=pl.BlockSpec(memory_space=pl.ANY),
        scratch_shapes=[pltpu.VMEM((tm * sub, _LANES), _F32), pltpu.VMEM((2, tt * sub, _LANES), _F32),
                        pltpu.SemaphoreType.DMA((2,)), pltpu.SemaphoreType.DMA],
    )
    return pl.pallas_call(
        functools.partial(_dispatch_kernel, tt=tt, tm=tm, sub=sub, nsteps=tokens // tt),
        out_shape=jax.ShapeDtypeStruct((rows * sub, _LANES), _F32),
        grid_spec=grid_spec,
        compiler_params=_params(1),
        name="moe_dispatch",
    )(zstart, zvalid, d0, d1, x_tok)


def _combine_kernel(d0_ref, d1_ref, n0_ref, n1_ref, y_hbm, route_ref, x_ref, lng_ref, lnb_ref,
                    xo_ref, xo16_ref, buf, sem, *, alpha, tm, sub, nsteps):
    i = pl.program_id(0)
    slot = i % 2

    def gather(dst_slot, a_ref, b_ref):
        def issue(g, carry):
            for u in range(_ROW_UNROLL):
                r = g * _ROW_UNROLL + u
                for s, d_ref in enumerate((a_ref, b_ref)):
                    src = y_hbm.at[pl.ds(pl.multiple_of(d_ref[0, 0, r] * sub, sub), sub), :]
                    dst = buf.at[dst_slot, s, pl.ds(pl.multiple_of(r * sub, sub), sub), :]
                    pltpu.make_async_copy(src, dst, sem.at[dst_slot]).start(priority=u % 2)
            return carry

        lax.fori_loop(0, tm // _ROW_UNROLL, issue, 0)

    @pl.when(i == 0)
    def _():
        gather(0, d0_ref, d1_ref)

    @pl.when(i + 1 < nsteps)
    def _():
        gather(1 - slot, n0_ref, n1_ref)

    for s in range(2):
        pltpu.make_async_copy(y_hbm.at[pl.ds(0, tm * sub), :], buf.at[slot, s], sem.at[slot]).wait()
    route = route_ref[...]
    f = (route[:, 2:3] * _load_token_rows(buf.at[slot, 0], tm)
         + route[:, 3:4] * _load_token_rows(buf.at[slot, 1], tm))
    out = _layer_norm(alpha * x_ref[...] + f, lng_ref[...], lnb_ref[...], _LN_EPS)
    xo_ref[...] = out
    xo16_ref[...] = out.astype(_BF16)


def _combine_ln(ys, d0, d1, route, x, ln_g, ln_b, *, alpha, tm):
    tokens, d_model = x.shape
    sub = d_model // _LANES
    nsteps = tokens // tm
    row = lambda i: (i, 0)
    const2 = lambda i: (0, 0)
    cur = lambda: pl.BlockSpec((1, 1, tm), lambda i: (i, 0, 0), memory_space=pltpu.SMEM)
    nxt = lambda: pl.BlockSpec((1, 1, tm), lambda i: (jnp.minimum(i + 1, nsteps - 1), 0, 0),
                               memory_space=pltpu.SMEM)
    return pl.pallas_call(
        functools.partial(_combine_kernel, alpha=alpha, tm=tm, sub=sub, nsteps=nsteps),
        out_shape=(jax.ShapeDtypeStruct((tokens, d_model), _F32),
                   jax.ShapeDtypeStruct((tokens, d_model), _BF16)),
        grid=(nsteps,),
        in_specs=[cur(), cur(), nxt(), nxt(), pl.BlockSpec(memory_space=pl.ANY),
                  pl.BlockSpec((tm, _LANES), row), pl.BlockSpec((tm, d_model), row),
                  pl.BlockSpec((1, d_model), const2), pl.BlockSpec((1, d_model), const2)],
        out_specs=(pl.BlockSpec((tm, d_model), row), pl.BlockSpec((tm, d_model), row)),
        scratch_shapes=[pltpu.VMEM((2, 2, tm * sub, _LANES), _F32), pltpu.SemaphoreType.DMA((2,))],
        compiler_params=_params(1),
        name="moe_combine_ln",
    )(d0, d1, d0, d1, ys, route, x, ln_g.reshape(1, -1), ln_b.reshape(1, -1))


def _routing_plan(route, *, tm):
    tokens = route.shape[0]
    ntiles = (2 * tokens) // tm + _N_EXPERTS
    choice = route[:, 0:2].T.astype(jnp.int32)
    experts = jnp.arange(_N_EXPERTS, dtype=jnp.int32)[:, None]
    onehot = [(choice[k][None, :] == experts).astype(jnp.int32) for k in range(2)]
    csum = [jnp.cumsum(oh, axis=1) for oh in onehot]
    first_counts = csum[0][:, -1]
    counts = first_counts + csum[1][:, -1]
    padded = ((counts + tm - 1) // tm) * tm
    ends = jnp.cumsum(padded)
    starts = ends - padded
    base = [starts, starts + first_counts]
    dest = [jnp.sum(onehot[k] * (base[k][:, None] + csum[k] - 1), axis=0) for k in range(2)]
    tile_start = jnp.arange(ntiles, dtype=jnp.int32) * tm
    te = jnp.minimum(jnp.sum((ends[None, :] <= tile_start[:, None]).astype(jnp.int32), axis=1),
                     _N_EXPERTS - 1)
    nused = (ends[-1] // tm).reshape(1)
    tail = ends[-1] + jnp.arange(_N_EXPERTS, dtype=jnp.int32) * tm
    zstart = jnp.concatenate([jnp.maximum(ends - tm, 0), jnp.minimum(tail, (ntiles - 1) * tm)])
    zvalid = jnp.concatenate([padded > 0, tail < ntiles * tm]).astype(jnp.int32)
    return ntiles * tm, te, nused, dest[0], dest[1], zstart, zvalid


def _moe_ffn_ln(x, x_tok, route, w_up, w_down, ln_g, ln_b, *, alpha, tm):
    tokens = x.shape[0]
    tt = min(2 * _TM, tokens)
    tc = min(_TM // 2, tokens)
    rows, te, nused, d0, d1, zstart, zvalid = _routing_plan(route, tm=tm)
    xs = _dispatch_rows(x_tok, d0.reshape(tokens // tt, 1, tt), d1.reshape(tokens // tt, 1, tt),
                        zstart, zvalid, tokens=tokens, rows=rows, tt=tt, tm=tm)
    h = _swiglu_up(xs, w_up, te, nused, tm=tm, tn=w_up.shape[2] // 4)
    ys = _expert_down(h, w_down, te, nused, tm=tm)
    return _combine_ln(ys, d0.reshape(tokens // tc, 1, tc), d1.reshape(tokens // tc, 1, tc),
                       route, x, ln_g, ln_b, alpha=alpha, tm=tc)


def _mixer_ln(x, x16, w_in, fox_b_f, gate_b, gm_w_s, gm_b_s, gm_ln_g, gm_ln_b, w_branch, w_out,
              ln_g, ln_b, w_router, *, bsz, seq, alpha, tm):
    d_model = x.shape[1]
    n_r = 2 * _RET_HEADS * _RET_DK + 2 * _RET_HEADS * _RET_DV
    n_f = 3 * _FOX_HEADS * _FOX_DH
    n_fl = _FOX_HEADS
    w_r = w_in[:, :n_r].astype(_BF16)
    w_f = jnp.concatenate([w_in[:, n_r:n_r + n_f + n_fl],
                           jnp.zeros((d_model, _LANES - n_fl), w_in.dtype)], axis=1).astype(_BF16)
    w_c = w_in[:, n_r + n_f + n_fl:].astype(_BF16)

    bias = jnp.zeros((1, _LANES), _F32).at[0, :n_fl].set(fox_b_f)
    y_a, qa, ka, va = _mixer_ab(x16, w_r, w_f, bias, bsz, seq)
    y_b = _fox(qa, ka, va, bsz, seq)
    return _mixer_out(x16, y_a, y_b, w_c, gm_w_s, gm_b_s, gm_ln_g, gm_ln_b, gate_b, w_branch, w_out,
                      x, ln_g, ln_b, w_router, alpha=alpha, tm=tm)


def kernel(x, w_in, fox_b_f, gate_b, gm_w_s, gm_b_s, gm_ln_g, gm_ln_b, w_branch, w_out, ln_g, ln_b,
           dense_w_up, dense_w_down, moe_router, moe_w_up, moe_w_down):
    bsz, seq, d_model = x.shape
    depth = w_in.shape[0]
    alpha = (2 * depth) ** 0.25
    tokens = bsz * seq
    tm = min(_TM, tokens)
    xf = x.reshape(tokens, d_model)
    x16 = xf
    for l in range(depth):
        dense = l % 2 == 0
        xf, x16, route = _mixer_ln(
            xf, x16, w_in[l], fox_b_f[l], gate_b[l], gm_w_s[l], gm_b_s[l], gm_ln_g[l], gm_ln_b[l],
            w_branch[l], w_out[l], ln_g[l, 0], ln_b[l, 0], None if dense else moe_router[l // 2],
            bsz=bsz, seq=seq, alpha=alpha, tm=tm)
        if dense:
            xf, x16 = _dense_ffn_ln(xf, x16, dense_w_up[l // 2], dense_w_down[l // 2],
                                    ln_g[l, 1], ln_b[l, 1], alpha=alpha, tm=tm)
        else:
            xf, x16 = _moe_ffn_ln(xf, x16, route, moe_w_up[l // 2], moe_w_down[l // 2],
                                  ln_g[l, 1], ln_b[l, 1], alpha=alpha, tm=tm)
    return xf.reshape(bsz, seq, d_model)
```
